```python
import jax, jax.numpy as jnp
from jax import lax
import numpy as np

D_MODEL = 2048
BATCH = 2
SEQ = 4096
DEPTH = 4
DEC_BATCH = 8
DEC_SEQ = 1
PAST_LEN = 16384
PAGE_SIZE = 128

N_A_LAYERS = DEPTH // 2
N_B_LAYERS = DEPTH - N_A_LAYERS
HEAD_DIM = 128
MIX_DIM = 3 * D_MODEL // 4
N_MIX_HEADS = MIX_DIM // HEAD_DIM
MEM_DIM = D_MODEL - MIX_DIM
N_MEM_HEADS = 4
MEM_HEAD_DIM = MEM_DIM // N_MEM_HEADS
N_MEM = 256
D_MIX = MIX_DIM + MEM_DIM
HGRN_CHUNK = 64
N_KV = 4
GROUP = N_MIX_HEADS // N_KV
N_BRANCH = 3
N_GATE = N_MIX_HEADS * N_BRANCH
CMP_STRIDE = 16
CMP_BLOCK = 2 * CMP_STRIDE
CMP_HID = 256
SLC_BLOCK = 64
N_SELECT = 16
WINDOW = 512
Q_BLOCK = 128
D_IN_A = 4 * MIX_DIM + MEM_DIM
D_IN_B = MIX_DIM + N_GATE + MEM_DIM
D_KV = 6 * N_KV * HEAD_DIM
D_FF = ((8 * D_MODEL // 3 + 127) // 128) * 128
CONV_W = 3
RMS_EPS = 1e-6
SEL_BONUS = 1e9

kernel_name = 'yoco_hgrn2_nsa_decoder_step'


def rmsnorm(x, g):
    xf = x.astype(jnp.float32)
    y = xf * lax.rsqrt(jnp.mean(xf * xf, axis=-1, keepdims=True) + RMS_EPS)
    return (y * g.astype(jnp.float32)).astype(x.dtype)


def masked_softmax(s, mask, axis):
    s = jnp.where(mask, s.astype(jnp.float32), -jnp.inf)
    m = jnp.max(s, axis=axis, keepdims=True)
    m = jnp.where(jnp.isfinite(m), m, 0.0)
    p = jnp.exp(s - m)
    return p / jnp.maximum(jnp.sum(p, axis=axis, keepdims=True), 1e-30)


def gla_chunked(q, k, v, log_f, s0):
    f32 = jnp.float32
    B, L, H, DK = q.shape
    DV = v.shape[-1]
    C = min(HGRN_CHUNK, L)
    n = -(-L // C)
    pad = n * C - L

    def prep(a):
        a = jnp.pad(a.astype(f32), ((0, 0), (0, pad), (0, 0), (0, 0)))
        return a.reshape(B, n, C, H, a.shape[-1]).swapaxes(0, 1)

    causal = jnp.tril(jnp.ones((C, C), dtype=bool))[None, :, :, None, None]

    def step(S, chunk):
        qc, kc, vc, gc = chunk
        b = jnp.cumsum(gc, axis=1)
        b_last = b[:, -1]
        o_inter = jnp.einsum('bthk,bhkv->bthv', qc * jnp.exp(b), S)
        decay = jnp.exp(jnp.where(causal, b[:, :, None] - b[:, None, :], -jnp.inf))
        att = jnp.einsum('bthk,bshk,btshk->bhts', qc, kc, decay)
        o_intra = jnp.einsum('bhts,bshv->bthv', att, vc)
        S_new = jnp.exp(b_last)[..., None] * S + jnp.einsum('bshk,bshv->bhkv', kc * jnp.exp(b_last[:, None] - b), vc)
        return S_new, o_inter + o_intra

    S_fin, o = lax.scan(step, s0.astype(f32), (prep(q), prep(k), prep(v), prep(log_f)))
    o = o.swapaxes(0, 1).reshape(B, n * C, H, DV)[:, :L]
    return o, S_fin


def hgrn2_mixer(z, lb, onorm, s0):
    B, L, _ = z.shape
    q, f, i, g = jnp.split(z, 4, axis=-1)
    heads = (B, L, N_MIX_HEADS, HEAD_DIM)
    log_f = jnp.logaddexp(jnp.log(lb), jnp.log1p(-lb) + jax.nn.log_sigmoid(f.astype(jnp.float32)))
    k = -jnp.expm1(log_f)
    o, s_new = gla_chunked(jax.nn.silu(q).reshape(heads), k.reshape(heads), i.reshape(heads),
                           log_f.reshape(heads), s0)
    o = rmsnorm(o, onorm.reshape(N_MIX_HEADS, HEAD_DIM)).reshape(B, L, MIX_DIM)
    return (o * jax.nn.silu(g.astype(jnp.float32))).astype(z.dtype), s_new


def mem_attention(q, mem_kv):
    B, L = q.shape[:2]
    s = jnp.einsum('blhd,bmhd->bhlm', q, mem_kv[:, :, 0]).astype(jnp.float32) * (MEM_HEAD_DIM ** -0.5)
    p = jax.nn.softmax(s, axis=-1).astype(mem_kv.dtype)
    return jnp.einsum('bhlm,bmhd->blhd', p, mem_kv[:, :, 1]).reshape(B, L, MEM_DIM).astype(q.dtype)


def compress_blocks(rows, pe, w1, w2):
    B, T = rows.shape[:2]
    n_sub = T // CMP_STRIDE
    half = CMP_STRIDE * HEAD_DIM
    sub = rows.reshape(B, n_sub, CMP_STRIDE, N_KV, HEAD_DIM).transpose(0, 1, 3, 2, 4).reshape(B, n_sub, N_KV, half)
    h = sub[:, :-1] @ w1[:half] + sub[:, 1:] @ w1[half:] + pe.reshape(-1) @ w1
    return jax.nn.gelu(h) @ w2


def to_sel_blocks(rows):
    B, T = rows.shape[:2]
    return rows.reshape(B, T // SLC_BLOCK, SLC_BLOCK, N_KV, HEAD_DIM).transpose(0, 3, 1, 2, 4)


def nsa_block(q, gate, q_pos, k_cmp, v_cmp, k_sb, v_sb, k_win, v_win, w_pos):
    B, Lq = q.shape[:2]
    NC = k_cmp.shape[1]
    NS = k_sb.shape[2]
    scale = HEAD_DIM ** -0.5
    qg = q.reshape(B, Lq, N_KV, GROUP, HEAD_DIM)
    qp = q_pos[:, None]

    cmp_end = CMP_STRIDE * jnp.arange(NC) + CMP_BLOCK - 1
    s_c = jnp.einsum('bqngd,bcnd->bngqc', qg, k_cmp) * scale
    p_c = masked_softmax(s_c, cmp_end[None, :] <= qp, axis=-1)
    o_cmp = jnp.einsum('bngqc,bcnd->bqngd', p_c.astype(v_cmp.dtype), v_cmp)

    R = SLC_BLOCK // CMP_STRIDE
    imp = jnp.pad(jnp.sum(p_c, axis=2), ((0, 0), (0, 0), (0, 0), (1, 1)))
    imp_s = imp[..., :R * NS].reshape(B, N_KV, Lq, NS, R).sum(-1) + imp[..., R::R]
    blk = jnp.arange(NS)[None, :]
    cur = (q_pos // SLC_BLOCK)[:, None]
    valid = blk <= cur
    forced = valid & ((blk == 0) | (blk == cur) | (blk == cur - 1))
    score = jnp.where(forced, SEL_BONUS, jnp.where(valid, imp_s, -jnp.inf))
    n_sel = min(N_SELECT, NS)
    _, idx = lax.top_k(score, n_sel)
    bi = jnp.arange(B)[:, None, None, None]
    ni = jnp.arange(N_KV)[None, :, None, None]
    kg = k_sb[bi, ni, idx]
    vg = v_sb[bi, ni, idx]
    s_s = jnp.einsum('bqngd,bnqksd->bngqks', qg, kg) * scale
    tok = idx[..., None] * SLC_BLOCK + jnp.arange(SLC_BLOCK)
    mask_s = (tok <= q_pos[None, None, :, None, None])[:, :, None]
    p_s = masked_softmax(s_s, mask_s, axis=(-2, -1))
    o_slc = jnp.einsum('bngqks,bnqksd->bqngd', p_s.astype(vg.dtype), vg)

    s_w = jnp.einsum('bqngd,bwnd->bngqw', qg, k_win) * scale
    dpos = qp - w_pos[None, :]
    mask_w = (dpos >= 0) & (dpos <= WINDOW) & (w_pos[None, :] >= 0)
    p_w = masked_softmax(s_w, mask_w, axis=-1)
    o_win = jnp.einsum('bngqw,bwnd->bqngd', p_w.astype(v_win.dtype), v_win)

    g = jax.nn.sigmoid(gate.astype(jnp.float32)).reshape(B, Lq, N_KV, GROUP, N_BRANCH)
    o = g[..., 0:1] * o_cmp + g[..., 1:2] * o_slc + g[..., 2:3] * o_win
    return o.reshape(B, Lq, MIX_DIM).astype(q.dtype)


def conv_ffn(h, buf, w_up_l, conv_w_l, conv_b_l, w_down_l):
    L = h.shape[1]
    a, u = jnp.split(h @ w_up_l, 2, axis=-1)
    a_ext = jnp.concatenate([buf.astype(a.dtype), a], axis=1)
    c = conv_b_l + conv_w_l[CONV_W - 1] * a_ext[:, CONV_W - 1:CONV_W - 1 + L]
    for j in range(CONV_W - 1):
        c = c + conv_w_l[j] * a_ext[:, j:j + L]
    return (jax.nn.gelu(c) * u) @ w_down_l, a_ext[:, L:]


def setup_inputs(seed: int = 0) -> dict:
    key = jax.random.key(seed)
    keys = iter(jax.random.split(key, 32))

    def nrm(shape, scale=1.0):
        return scale * jax.random.normal(next(keys), shape, jnp.float32)

    n_pages = PAST_LEN // PAGE_SIZE
    n_used = DEC_BATCH * n_pages
    n_pool = n_used + max(n_used // 4, 1)
    w_buf = min(WINDOW, PAST_LEN)
    page_table = jax.random.permutation(next(keys), n_pool)[:n_used].reshape(DEC_BATCH, n_pages).astype(jnp.int32)
    return {
        'x_prompt': nrm((BATCH, SEQ, D_MODEL)),
        'x_sample': nrm((DEC_BATCH, DEC_SEQ, D_MODEL)),
        'mem_prompt': nrm((BATCH, N_MEM, D_MODEL)),
        'cache_nsa_kv': nrm((n_pool, PAGE_SIZE, 4, N_KV, HEAD_DIM)),
        'page_table': page_table,
        'cache_win_kv': nrm((DEC_BATCH, w_buf, 2, N_KV, HEAD_DIM)),
        'state_hgrn': nrm((N_A_LAYERS, DEC_BATCH, N_MIX_HEADS, HEAD_DIM, HEAD_DIM), 0.5),
        'state_conv': nrm((DEPTH, DEC_BATCH, CONV_W - 1, D_FF)),
        'cache_mem_kv': nrm((DEPTH, DEC_BATCH, N_MEM, 2, N_MEM_HEADS, MEM_HEAD_DIM)),
        'norm_mix': 1.0 + nrm((DEPTH, D_MODEL), 0.02),
        'norm_ffn': 1.0 + nrm((DEPTH, D_MODEL), 0.02),
        'norm_mem': 1.0 + nrm((DEPTH, D_MODEL), 0.02),
        'norm_kv': 1.0 + nrm((D_MODEL,), 0.02),
        'norm_final': 1.0 + nrm((D_MODEL,), 0.02),
        'w_in_a': nrm((N_A_LAYERS, D_MODEL, D_IN_A), D_MODEL ** -0.5),
        'hgrn_lb': nrm((N_A_LAYERS, MIX_DIM), 0.5),
        'hgrn_onorm': 1.0 + nrm((N_A_LAYERS, MIX_DIM), 0.02),
        'w_in_b': nrm((N_B_LAYERS, D_MODEL, D_IN_B), D_MODEL ** -0.5),
        'b_gate': nrm((N_B_LAYERS, N_GATE), 0.1),
        'w_kv': nrm((D_MODEL, D_KV), D_MODEL ** -0.5),
        'cmp_pe': nrm((2, CMP_BLOCK, HEAD_DIM), 0.1),
        'cmp_w1': nrm((2, CMP_BLOCK * HEAD_DIM, CMP_HID), (CMP_BLOCK * HEAD_DIM) ** -0.5),
        'cmp_w2': nrm((2, CMP_HID, HEAD_DIM), CMP_HID ** -0.5),
        'w_mem_kv': nrm((DEPTH, D_MODEL, 2 * MEM_DIM), D_MODEL ** -0.5),
        'w_out': nrm((DEPTH, D_MIX, D_MODEL), D_MIX ** -0.5),
        'w_up': nrm((DEPTH, D_MODEL, 2 * D_FF), D_MODEL ** -0.5),
        'conv_w': nrm((DEPTH, CONV_W, D_FF), CONV_W ** -0.5),
        'conv_b': nrm((DEPTH, D_FF), 0.02),
        'w_down': nrm((DEPTH, D_FF, D_MODEL), D_FF ** -0.5),
    }


def reference(x_prompt, x_sample, mem_prompt, cache_nsa_kv, page_table, cache_win_kv, state_hgrn, state_conv,
              cache_mem_kv, norm_mix, norm_ffn, norm_mem, norm_kv, norm_final, w_in_a, hgrn_lb, hgrn_onorm,
              w_in_b, b_gate, w_kv, cmp_pe, cmp_w1, cmp_w2, w_mem_kv, w_out, w_up, conv_w, conv_b, w_down):
    lb_all = jnp.cumsum(jax.nn.softmax(hgrn_lb.astype(jnp.float32), axis=0), axis=0)
    lb_all = lb_all - lb_all[0]

    def project_kv(x):
        B, L = x.shape[:2]
        return (rmsnorm(x, norm_kv) @ w_kv).reshape(B, L, 6, N_KV, HEAD_DIM)

    def cmp_tokens(k_rows, v_rows):
        return (compress_blocks(k_rows, cmp_pe[0], cmp_w1[0], cmp_w2[0]),
                compress_blocks(v_rows, cmp_pe[1], cmp_w1[1], cmp_w2[1]))

    def make_nsa_prompt(x):
        B, L = x.shape[:2]
        rows = project_kv(x)
        k_cmp, v_cmp = cmp_tokens(rows[:, :, 0], rows[:, :, 1])
        k_sb, v_sb = to_sel_blocks(rows[:, :, 2]), to_sel_blocks(rows[:, :, 3])
        pad = ((0, 0), (WINDOW, 0), (0, 0), (0, 0))
        k_wp, v_wp = jnp.pad(rows[:, :, 4], pad), jnp.pad(rows[:, :, 5], pad)
        nb = L // Q_BLOCK

        def attend(q, gate):
            qb = q.reshape(B, nb, Q_BLOCK, N_MIX_HEADS, HEAD_DIM).swapaxes(0, 1)
            gb = gate.reshape(B, nb, Q_BLOCK, N_MIX_HEADS, N_BRANCH).swapaxes(0, 1)

            def one(args):
                i, q_i, g_i = args
                start = i * Q_BLOCK
                q_pos = start + jnp.arange(Q_BLOCK)
                kw = lax.dynamic_slice_in_dim(k_wp, start, Q_BLOCK + WINDOW, axis=1)
                vw = lax.dynamic_slice_in_dim(v_wp, start, Q_BLOCK + WINDOW, axis=1)
                w_pos = start - WINDOW + jnp.arange(Q_BLOCK + WINDOW)
                return nsa_block(q_i, g_i, q_pos, k_cmp, v_cmp, k_sb, v_sb, kw, vw, w_pos)

            o = lax.map(one, (jnp.arange(nb), qb, gb))
            return o.swapaxes(0, 1).reshape(B, L, MIX_DIM)

        w_keep = min(WINDOW, L)
        return attend, (rows[:, :, :4], rows[:, L - w_keep:, 4:])

    def make_nsa_sample(x):
        B, L = x.shape[:2]
        rows = project_kv(x)
        past_len = page_table.shape[1] * cache_nsa_kv.shape[1]
        past = cache_nsa_kv[page_table].reshape(B, past_len, 4, N_KV, HEAD_DIM)
        full = jnp.concatenate([past, rows[:, :, :4].astype(past.dtype)], axis=1)
        T = past_len + L
        T_pad = -(-T // SLC_BLOCK) * SLC_BLOCK
        full = jnp.pad(full, ((0, 0), (0, T_pad - T), (0, 0), (0, 0), (0, 0)))
        k_cmp, v_cmp = cmp_tokens(full[:, :, 0], full[:, :, 1])
        k_sb, v_sb = to_sel_blocks(full[:, :, 2]), to_sel_blocks(full[:, :, 3])
        w_buf = cache_win_kv.shape[1]
        win = jnp.concatenate([cache_win_kv, rows[:, :, 4:].astype(cache_win_kv.dtype)], axis=1)
        w_pos = past_len - w_buf + jnp.arange(w_buf + L)
        q_pos = past_len + jnp.arange(L)

        def attend(q, gate):
            return nsa_block(q, gate, q_pos, k_cmp, v_cmp, k_sb, v_sb, win[:, :, 0], win[:, :, 1], w_pos)

        return attend, (rows[:, :, :4], win[:, L:])

    def trunk(x, mem_kv, hgrn_s0, conv_s0, make_nsa):
        B, L = x.shape[:2]
        new_hgrn, new_conv = [], []
        attend, kv_new = None, None
        for l in range(DEPTH):
            h = rmsnorm(x, norm_mix[l])
            if l < N_A_LAYERS:
                z = h @ w_in_a[l]
                o_mix, s_new = hgrn2_mixer(z[..., :4 * MIX_DIM], lb_all[l], hgrn_onorm[l], hgrn_s0[l])
                new_hgrn.append(s_new)
                q_mem = z[..., 4 * MIX_DIM:]
            else:
                if l == N_A_LAYERS:
                    attend, kv_new = make_nsa(x)
                j = l - N_A_LAYERS
                z = h @ w_in_b[j]
                q = z[..., :MIX_DIM].reshape(B, L, N_MIX_HEADS, HEAD_DIM)
                gate = (z[..., MIX_DIM:MIX_DIM + N_GATE] + b_gate[j]).reshape(B, L, N_MIX_HEADS, N_BRANCH)
                o_mix = attend(q, gate)
                q_mem = z[..., MIX_DIM + N_GATE:]
            o_mem = mem_attention(q_mem.reshape(B, L, N_MEM_HEADS, MEM_HEAD_DIM), mem_kv[l])
            x = x + jnp.concatenate([o_mix, o_mem], axis=-1) @ w_out[l]
            y, buf = conv_ffn(rmsnorm(x, norm_ffn[l]), conv_s0[l], w_up[l], conv_w[l], conv_b[l], w_down[l])
            new_conv.append(buf)
            x = x + y
        return rmsnorm(x, norm_final), kv_new, jnp.stack(new_hgrn), jnp.stack(new_conv)

    Bp, Nm = mem_prompt.shape[:2]
    mem_kv_prompt = jnp.stack([
        (rmsnorm(mem_prompt, norm_mem[l]) @ w_mem_kv[l]).reshape(Bp, Nm, 2, N_MEM_HEADS, MEM_HEAD_DIM)
        for l in range(DEPTH)])
    hgrn0 = jnp.zeros((N_A_LAYERS, Bp, N_MIX_HEADS, HEAD_DIM, HEAD_DIM), jnp.float32)
    conv0 = jnp.zeros((DEPTH, Bp, CONV_W - 1, D_FF), x_prompt.dtype)
    y_prompt, kv_p, hgrn_prompt, conv_prompt = trunk(x_prompt, mem_kv_prompt, hgrn0, conv0, make_nsa_prompt)
    y_sample, kv_s, hgrn_sample, conv_sample = trunk(x_sample, cache_mem_kv, state_hgrn, state_conv, make_nsa_sample)
    nsa_rows_prompt, win_prompt = kv_p
    nsa_rows_sample, win_sample = kv_s
    return (y_prompt, y_sample, nsa_rows_prompt, nsa_rows_sample, win_prompt, win_sample,
            hgrn_prompt, hgrn_sample, conv_prompt, conv_sample, mem_kv_prompt)
```

```python
import functools

import jax
import jax.numpy as jnp
from jax import lax
from jax.experimental import pallas as pl
from jax.experimental.pallas import tpu as pltpu

F32 = jnp.float32
BF16 = jnp.bfloat16

D_MODEL = 2048
HEAD_DIM = 128
MIX_DIM = 1536
N_MIX_HEADS = 12
MEM_DIM = 512
N_MEM_HEADS = 4
N_KV = 4
GROUP = 3
N_BRANCH = 3
CMP_STRIDE = 16
CMP_HID = 256
SLC_BLOCK = 64
N_SELECT = 16
WINDOW = 512
PAGE_SIZE = 128
D_FF = 5504
D_FF_PAD = 5632
CONV_W = 3
RMS_EPS = 1e-6
SEL_BONUS = 1e9
NEG_BIG = -1e30

HGRN_T = 64
HGRN_SUB = 16
FFN_TF = 512
VMEM_LIMIT = 56 * 1024 * 1024


def _cp(sem, vmem=VMEM_LIMIT):
    return pltpu.CompilerParams(dimension_semantics=sem, vmem_limit_bytes=vmem)


def _dot(a, b):
    return jnp.dot(a, b, preferred_element_type=F32)


def _dot_nt(a, b):
    return lax.dot_general(a, b, (((1,), (1,)), ((), ())), preferred_element_type=F32)


def _dot_tn(a, b):
    return lax.dot_general(a, b, (((0,), (0,)), ((), ())), preferred_element_type=F32)


def _pick(n, cands):
    for c in cands:
        if n % c == 0:
            return c
    return n


def _rms_matmul_kernel(x_ref, g_ref, w_ref, b_ref, o_ref, xn_ref):
    @pl.when(pl.program_id(1) == 0)
    def _():
        x = x_ref[...]
        y = x * lax.rsqrt(jnp.mean(x * x, axis=-1, keepdims=True) + RMS_EPS)
        xn_ref[...] = (y * g_ref[...]).astype(BF16)

    o_ref[...] = _dot(xn_ref[...], w_ref[...]) + b_ref[...]


def rms_matmul(x, g, w, bias=None):
    M, K = x.shape
    N = w.shape[1]
    tm = _pick(M, (1024, 512, 256, 128))
    tn = _pick(N, (512, 256, 128))
    if bias is None:
        bias = jnp.zeros((N,), F32)
    return pl.pallas_call(
        _rms_matmul_kernel,
        grid=(M // tm, N // tn),
        in_specs=[
            pl.BlockSpec((tm, K), lambda i, j: (i, 0)),
            pl.BlockSpec((1, K), lambda i, j: (0, 0)),
            pl.BlockSpec((K, tn), lambda i, j: (0, j)),
            pl.BlockSpec((1, tn), lambda i, j: (0, j)),
        ],
        out_specs=pl.BlockSpec((tm, tn), lambda i, j: (i, j)),
        out_shape=jax.ShapeDtypeStruct((M, N), F32),
        scratch_shapes=[pltpu.VMEM((tm, K), BF16)],
        compiler_params=_cp(("parallel", "arbitrary")),
        name="rms_matmul",
    )(x, g.reshape(1, K), w, bias.reshape(1, N))


def _rmsnorm_kernel(x_ref, g_ref, o_ref):
    x = x_ref[...]
    y = x * lax.rsqrt(jnp.mean(x * x, axis=-1, keepdims=True) + RMS_EPS)
    o_ref[...] = y * g_ref[...]


def rmsnorm_rows(x, g):
    M, K = x.shape
    tm = _pick(M, (512, 256, 128))
    return pl.pallas_call(
        _rmsnorm_kernel,
        grid=(M // tm,),
        in_specs=[pl.BlockSpec((tm, K), lambda i: (i, 0)), pl.BlockSpec((1, K), lambda i: (0, 0))],
        out_specs=pl.BlockSpec((tm, K), lambda i: (i, 0)),
        out_shape=jax.ShapeDtypeStruct((M, K), F32),
        compiler_params=_cp(("parallel",)),
        name="final_rmsnorm",
    )(x, g.reshape(1, K))


def _outproj_kernel(om_ref, oe_ref, w1_ref, w2_ref, x_ref, y_ref):
    y_ref[...] = x_ref[...] + _dot(om_ref[...], w1_ref[...]) + _dot(oe_ref[...], w2_ref[...])


def out_proj(o_mix, o_mem, w_mix, w_mem, x):
    M = x.shape[0]
    tm = _pick(M, (1024, 512, 256, 128))
    tn = 512
    return pl.pallas_call(
        _outproj_kernel,
        grid=(M // tm, D_MODEL // tn),
        in_specs=[
            pl.BlockSpec((tm, MIX_DIM), lambda i, j: (i, 0)),
            pl.BlockSpec((tm, MEM_DIM), lambda i, j: (i, 0)),
            pl.BlockSpec((MIX_DIM, tn), lambda i, j: (0, j)),
            pl.BlockSpec((MEM_DIM, tn), lambda i, j: (0, j)),
            pl.BlockSpec((tm, tn), lambda i, j: (i, j)),
        ],
        out_specs=pl.BlockSpec((tm, tn), lambda i, j: (i, j)),
        out_shape=jax.ShapeDtypeStruct((M, D_MODEL), F32),
        compiler_params=_cp(("parallel", "parallel")),
        name="out_proj",
    )(o_mix, o_mem, w_mix, w_mem, x)


def _mem_attn_kernel(q_ref, kv_ref, o_ref):
    scale = HEAD_DIM ** -0.5
    for h in range(N_MEM_HEADS):
        q = q_ref[0, :, h * HEAD_DIM:(h + 1) * HEAD_DIM].astype(BF16)
        k = kv_ref[0, :, h * HEAD_DIM:(h + 1) * HEAD_DIM].astype(BF16)
        v = kv_ref[0, :, MEM_DIM + h * HEAD_DIM:MEM_DIM + (h + 1) * HEAD_DIM].astype(BF16)
        s = _dot_nt(q, k) * scale
        m = jnp.max(s, axis=-1, keepdims=True)
        p = jnp.exp(s - m)
        p = p / jnp.sum(p, axis=-1, keepdims=True)
        o_ref[0, :, h * HEAD_DIM:(h + 1) * HEAD_DIM] = _dot(p.astype(BF16), v).astype(o_ref.dtype)


def mem_attention(z3, col_block, kv):
    B, L, _ = z3.shape
    n_mem = kv.shape[1]
    tm = _pick(L, (1024, 512, 256, 128))
    return pl.pallas_call(
        _mem_attn_kernel,
        grid=(B, L // tm),
        in_specs=[
            pl.BlockSpec((1, tm, MEM_DIM), lambda b, i: (b, i, col_block)),
            pl.BlockSpec((1, n_mem, 2 * MEM_DIM), lambda b, i: (b, 0, 0)),
        ],
        out_specs=pl.BlockSpec((1, tm, MEM_DIM), lambda b, i: (b, i, 0)),
        out_shape=jax.ShapeDtypeStruct((B, L, MEM_DIM), BF16),
        compiler_params=_cp(("parallel", "parallel")),
        name="mem_attention",
    )(z3, kv)


def _ffn_kernel(*refs, blocks_per_seq, decode):
    if decode:
        (x_ref, g_ref, wa_ref, wu_ref, cw_ref, cb_ref, wd_ref, p1_ref, p2_ref,
         y_ref, at_ref, xn_ref, acc_ref) = refs
    else:
        (x_ref, g_ref, wa_ref, wu_ref, cw_ref, cb_ref, wd_ref,
         y_ref, at_ref, xn_ref, acc_ref, carry_ref) = refs
    i = pl.program_id(0)
    j = pl.program_id(1)

    @pl.when(j == 0)
    def _():
        x = x_ref[...]
        y = x * lax.rsqrt(jnp.mean(x * x, axis=-1, keepdims=True) + RMS_EPS)
        xn_ref[...] = (y * g_ref[...]).astype(BF16)
        acc_ref[...] = jnp.zeros_like(acc_ref)

    xn = xn_ref[...]
    a = _dot(xn, wa_ref[...])
    u = _dot(xn, wu_ref[...])
    tm = a.shape[0]
    if decode:
        a1 = p1_ref[...]
        a2 = p2_ref[...]
    else:
        @pl.when((i % blocks_per_seq) == 0)
        def _():
            carry_ref[j] = jnp.zeros((8, a.shape[1]), F32)

        prev = carry_ref[j]
        row = lax.broadcasted_iota(jnp.int32, a.shape, 0)
        a1 = jnp.where(row == 0, prev[7:8], pltpu.roll(a, 1, 0))
        a2 = jnp.where(row == 0, prev[6:7], jnp.where(row == 1, prev[7:8], pltpu.roll(a, 2, 0)))
        carry_ref[j] = a[tm - 8:tm]
    at_ref[0] = a[tm - at_ref.shape[1]:tm]
    c = cb_ref[...] + cw_ref[2:3] * a
    c = c + cw_ref[0:1] * a2
    c = c + cw_ref[1:2] * a1
    h = (jax.nn.gelu(c) * u).astype(BF16)
    acc_ref[...] += _dot(h, wd_ref[...])

    @pl.when(j == pl.num_programs(1) - 1)
    def _():
        y_ref[...] = x_ref[...] + acc_ref[...]


def conv_ffn(x, g, wa, wu, cw, cb, wd, *, seq_len=None, prev=None):
    M = x.shape[0]
    decode = prev is not None
    tm = M if decode else _pick(seq_len, (512, 256, 128))
    n_tail = tm if decode else 8
    nf = D_FF_PAD // FFN_TF
    in_specs = [
        pl.BlockSpec((tm, D_MODEL), lambda i, j: (i, 0)),
        pl.BlockSpec((1, D_MODEL), lambda i, j: (0, 0)),
        pl.BlockSpec((D_MODEL, FFN_TF), lambda i, j: (0, j)),
        pl.BlockSpec((D_MODEL, FFN_TF), lambda i, j: (0, j)),
        pl.BlockSpec((8, FFN_TF), lambda i, j: (0, j)),
        pl.BlockSpec((1, FFN_TF), lambda i, j: (0, j)),
        pl.BlockSpec((FFN_TF, D_MODEL), lambda i, j: (j, 0)),
    ]
    args = [x, g.reshape(1, D_MODEL), wa, wu, cw, cb, wd]
    scratch = [pltpu.VMEM((tm, D_MODEL), BF16), pltpu.VMEM((tm, D_MODEL), F32)]
    if decode:
        in_specs += [pl.BlockSpec((tm, FFN_TF), lambda i, j: (i, j))] * 2
        args += list(prev)
    else:
        scratch.append(pltpu.VMEM((nf, 8, FFN_TF), F32))
    return pl.pallas_call(
        functools.partial(_ffn_kernel, blocks_per_seq=(1 if decode else seq_len // tm), decode=decode),
        grid=(M // tm, nf),
        in_specs=in_specs,
        out_specs=[
            pl.BlockSpec((tm, D_MODEL), lambda i, j: (i, 0)),
            pl.BlockSpec((1, n_tail, FFN_TF), lambda i, j: (i, 0, j)),
        ],
        out_shape=[
            jax.ShapeDtypeStruct((M, D_MODEL), F32),
            jax.ShapeDtypeStruct((M // tm, n_tail, D_FF_PAD), F32),
        ],
        scratch_shapes=scratch,
        compiler_params=_cp(("arbitrary", "arbitrary")),
        name="conv_ffn",
    )(*args)


def _hgrn_gates(q, f, log_lb, log1m_lb):
    qs = jax.nn.silu(q)
    log_f = jnp.logaddexp(log_lb, log1m_lb + jax.nn.log_sigmoid(f))
    k = 1.0 - jnp.exp(log_f)
    return qs, k, log_f


def _hgrn_out(o, g, onorm):
    y = o * lax.rsqrt(jnp.mean(o * o, axis=-1, keepdims=True) + RMS_EPS)
    return (y * onorm) * jax.nn.silu(g)


def _hgrn_kernel(q_ref, f_ref, i_ref, g_ref, llb_ref, l1m_ref, on_ref, s0_ref, o_ref, s_ref, st_ref,
                 *, n_chunks):
    c = pl.program_id(2)
    T = HGRN_T

    @pl.when(c == 0)
    def _():
        st_ref[...] = s0_ref[0, 0].T

    row = lax.broadcasted_iota(jnp.int32, (T, HEAD_DIM), 0)
    rl = row % HGRN_SUB
    log_lb = llb_ref[...]
    log1m_lb = l1m_ref[...]
    onorm = on_ref[...]

    def chunk(ci, carry):
        sl = pl.ds(pl.multiple_of(ci * T, T), T)
        q, k, lf = _hgrn_gates(q_ref[0, sl, :], f_ref[0, sl, :], log_lb, log1m_lb)
        v = i_ref[0, sl, :]
        b = lf
        s = 1
        while s < T:
            b = b + jnp.where(row >= s, pltpu.roll(b, s, 0), 0.0)
            s *= 2
        st = st_ref[...]
        o = _dot_nt((q * jnp.exp(b)).astype(BF16), st.astype(BF16))
        o = o + jnp.sum(q * k, axis=-1, keepdims=True) * v
        for d in range(1, HGRN_SUB):
            w = jnp.exp(jnp.where(rl >= d, b - pltpu.roll(b, d, 0), -jnp.inf))
            a = jnp.sum(q * pltpu.roll(k, d, 0) * w, axis=-1, keepdims=True)
            o = o + a * pltpu.roll(v, d, 0)
        parts = [jnp.zeros((HGRN_SUB, HEAD_DIM), F32)]
        for blk in range(1, T // HGRN_SUB):
            lo = blk * HGRN_SUB
            r = b[lo - 1:lo]
            qt = q[lo:lo + HGRN_SUB] * jnp.exp(b[lo:lo + HGRN_SUB] - r)
            kt = k[0:lo] * jnp.exp(r - b[0:lo])
            att = _dot_nt(qt.astype(BF16), kt.astype(BF16))
            parts.append(_dot(att.astype(BF16), v[0:lo].astype(BF16)))
        o = o + jnp.concatenate(parts, axis=0)
        bl = b[T - 1:T]
        kt = k * jnp.exp(bl - b)
        st_ref[...] = st * jnp.exp(bl) + _dot_tn(v.astype(BF16), kt.astype(BF16))
        o_ref[0, sl, :] = _hgrn_out(o, g_ref[0, sl, :], onorm).astype(o_ref.dtype)
        return carry

    lax.fori_loop(0, n_chunks, chunk, 0)

    @pl.when(c == pl.num_programs(2) - 1)
    def _():
        s_ref[0, 0] = st_ref[...].T


def hgrn_prompt(z3, log_lb, log1m_lb, onorm, s0):
    B, L, _ = z3.shape
    H = N_MIX_HEADS
    tc = _pick(L, (512, 256, 128, 64))
    zspec = lambda k: pl.BlockSpec((1, tc, HEAD_DIM), lambda b, h, c: (b, c, k * H + h))
    vspec = pl.BlockSpec((1, HEAD_DIM), lambda b, h, c: (0, h))
    sspec = pl.BlockSpec((1, 1, HEAD_DIM, HEAD_DIM), lambda b, h, c: (b, h, 0, 0))
    return pl.pallas_call(
        functools.partial(_hgrn_kernel, n_chunks=tc // HGRN_T),
        grid=(B, H, L // tc),
        in_specs=[zspec(0), zspec(1), zspec(2), zspec(3), vspec, vspec, vspec, sspec],
        out_specs=[pl.BlockSpec((1, tc, HEAD_DIM), lambda b, h, c: (b, c, h)), sspec],
        out_shape=[
            jax.ShapeDtypeStruct((B, L, MIX_DIM), BF16),
            jax.ShapeDtypeStruct((B, H, HEAD_DIM, HEAD_DIM), F32),
        ],
        scratch_shapes=[pltpu.VMEM((HEAD_DIM, HEAD_DIM), F32)],
        compiler_params=_cp(("parallel", "parallel", "arbitrary")),
        name="hgrn_chunked",
    )(z3, z3, z3, z3, log_lb, log1m_lb, onorm, s0)


def _hgrn_step_kernel(z_ref, llb_ref, l1m_ref, on_ref, s0_ref, o_ref, s_ref):
    for h in range(N_MIX_HEADS):
        col = lambda k: slice((k * N_MIX_HEADS + h) * HEAD_DIM, (k * N_MIX_HEADS + h + 1) * HEAD_DIM)
        hs = slice(h * HEAD_DIM, (h + 1) * HEAD_DIM)
        q, k, lf = _hgrn_gates(z_ref[0, :, col(0)], z_ref[0, :, col(1)], llb_ref[:, hs], l1m_ref[:, hs])
        v = z_ref[0, :, col(2)]
        g = z_ref[0, :, col(3)]
        rows = jnp.concatenate([q, k, jnp.exp(lf), jnp.zeros((5, HEAD_DIM), F32)], axis=0)
        cols = rows.T
        s_new = cols[:, 2:3] * s0_ref[0, h] + cols[:, 1:2] * v
        s_ref[0, h] = s_new
        o = jnp.sum(cols[:, 0:1] * s_new, axis=0, keepdims=True)
        o_ref[0, :, hs] = _hgrn_out(o, g, on_ref[:, hs]).astype(o_ref.dtype)


def hgrn_step(z3, log_lb, log1m_lb, onorm, s0):
    B = z3.shape[0]
    W = z3.shape[2]
    H = N_MIX_HEADS
    vspec = pl.BlockSpec((1, MIX_DIM), lambda b: (0, 0))
    sspec = pl.BlockSpec((1, H, HEAD_DIM, HEAD_DIM), lambda b: (b, 0, 0, 0))
    return pl.pallas_call(
        _hgrn_step_kernel,
        grid=(B,),
        in_specs=[pl.BlockSpec((1, 1, W), lambda b: (b, 0, 0)), vspec, vspec, vspec, sspec],
        out_specs=[pl.BlockSpec((1, 1, MIX_DIM), lambda b: (b, 0, 0)), sspec],
        out_shape=[
            jax.ShapeDtypeStruct((B, 1, MIX_DIM), BF16),
            jax.ShapeDtypeStruct((B, H, HEAD_DIM, HEAD_DIM), F32),
        ],
        compiler_params=_cp(("parallel",)),
        name="hgrn_step",
    )(z3, log_lb, log1m_lb, onorm, s0)


def _pe_proj_kernel(pe_ref, w_ref, o_ref):
    pe = jnp.broadcast_to(pe_ref[0], (8, pe_ref.shape[2])).astype(BF16)
    o_ref[0] = _dot(pe, w_ref[0])


def pe_proj(pe, w1):
    K = pe.shape[2]
    return pl.pallas_call(
        _pe_proj_kernel,
        grid=(2,),
        in_specs=[pl.BlockSpec((1, 1, K), lambda i: (i, 0, 0)), pl.BlockSpec((1, K, CMP_HID), lambda i: (i, 0, 0))],
        out_specs=pl.BlockSpec((1, 8, CMP_HID), lambda i: (i, 0, 0)),
        out_shape=jax.ShapeDtypeStruct((2, 8, CMP_HID), F32),
        compiler_params=_cp(("parallel",)),
        name="cmp_pe_proj",
    )(pe, w1)


def _cmp_mlp_kernel(x_ref, w1_ref, w2_ref, pe_ref, o_ref):
    half = x_ref.shape[3]
    x = x_ref[0, 0].astype(BF16)
    a = _dot(x, w1_ref[0, 0:half])
    bm = _dot(x, w1_ref[0, half:2 * half])
    n = a.shape[0]
    h = a + pltpu.roll(bm, n - 1, 0) + pe_ref[0, 0:1]
    o_ref[0, 0] = _dot(jax.nn.gelu(h).astype(BF16), w2_ref[0])


def cmp_mlp(xsub, w1, w2, pe_h):
    _, G, n_sub, half = xsub.shape
    return pl.pallas_call(
        _cmp_mlp_kernel,
        grid=(2, G),
        in_specs=[
            pl.BlockSpec((1, 1, n_sub, half), lambda k, g: (k, g, 0, 0)),
            pl.BlockSpec((1, 2 * half, CMP_HID), lambda k, g: (k, 0, 0)),
            pl.BlockSpec((1, CMP_HID, HEAD_DIM), lambda k, g: (k, 0, 0)),
            pl.BlockSpec((1, 8, CMP_HID), lambda k, g: (k, 0, 0)),
        ],
        out_specs=pl.BlockSpec((1, 1, n_sub, HEAD_DIM), lambda k, g: (k, g, 0, 0)),
        out_shape=jax.ShapeDtypeStruct((2, G, n_sub, HEAD_DIM), F32),
        compiler_params=_cp(("parallel", "parallel")),
        name="cmp_mlp",
    )(xsub, w1, w2, pe_h)


def _masked_softmax_rows(s, mask):
    s = jnp.where(mask, s, -jnp.inf)
    m = jnp.max(s, axis=-1, keepdims=True)
    m = jnp.where(m == -jnp.inf, 0.0, m)
    p = jnp.exp(s - m)
    return p / jnp.maximum(jnp.sum(p, axis=-1, keepdims=True), 1e-30)


def _split3(p):
    hi = p.astype(BF16)
    r = p - hi.astype(F32)
    mid = r.astype(BF16)
    lo = (r - mid.astype(F32)).astype(BF16)
    return hi, mid, lo


def _nsa_prompt_kernel(q_ref, gt_ref, kc_ref, vc_ref, ks_ref, vs_ref, kw_ref, vw_ref, o_ref,
                       kaug_ref, vsb_ref, kwb_ref, vwb_ref, *, tq, tk, seq_len):
    qi = pl.program_id(2)
    n_blk = seq_len // SLC_BLOCK
    scale = HEAD_DIM ** -0.5

    @pl.when(qi == 0)
    def _():
        r = lax.broadcasted_iota(jnp.int32, (seq_len, HEAD_DIM), 0) // SLC_BLOCK
        col = lax.broadcasted_iota(jnp.int32, (seq_len, HEAD_DIM), 1)
        kaug_ref[:, 0:HEAD_DIM] = ks_ref[0].astype(BF16)
        kaug_ref[:, HEAD_DIM:2 * HEAD_DIM] = jnp.where(r == col, 1.0, 0.0).astype(BF16)
        vsb_ref[...] = vs_ref[0].astype(BF16)
        kwb_ref[...] = kw_ref[0].astype(BF16)
        vwb_ref[...] = vw_ref[0].astype(BF16)

    q0 = qi * tq
    qs = q_ref[0] * scale
    qf = jnp.concatenate([qs[:, g * HEAD_DIM:(g + 1) * HEAD_DIM] for g in range(GROUP)], axis=0)
    qb = qf.astype(BF16)
    rows = GROUP * tq

    nc = kc_ref.shape[1]
    kc = kc_ref[0].astype(BF16)
    vc = vc_ref[0].astype(BF16)
    qpos_c = q0 + lax.broadcasted_iota(jnp.int32, (rows, nc), 0) % tq
    cend = CMP_STRIDE * lax.broadcasted_iota(jnp.int32, (rows, nc), 1) + (2 * CMP_STRIDE - 1)
    p_c = _masked_softmax_rows(_dot_nt(qb, kc), cend <= qpos_c)
    o_cmp = _dot(p_c.astype(BF16), vc)

    psum = p_c[0:tq]
    for g in range(1, GROUP):
        psum = psum + p_c[g * tq:(g + 1) * tq]
    ci = lax.broadcasted_iota(jnp.int32, (nc, HEAD_DIM), 0)
    ji = lax.broadcasted_iota(jnp.int32, (nc, HEAD_DIM), 1)
    ratio = SLC_BLOCK // CMP_STRIDE
    member = ((ci >= ratio * ji - 1) & (ci <= ratio * ji + ratio - 1)).astype(BF16)
    imp = sum(_dot(part, member) for part in _split3(psum))
    imp_t = imp.T
    blk = lax.broadcasted_iota(jnp.int32, (HEAD_DIM, tq), 0)
    cur = (q0 + lax.broadcasted_iota(jnp.int32, (HEAD_DIM, tq), 1)) // SLC_BLOCK
    valid = (blk <= cur) & (blk < n_blk)
    forced = valid & ((blk == 0) | (blk == cur) | (blk == cur - 1))
    score = jnp.where(forced, SEL_BONUS, jnp.where(valid, imp_t, -jnp.inf))
    score = score[0:n_blk]
    blk = lax.broadcasted_iota(jnp.int32, (n_blk, tq), 0)
    rank = jnp.zeros((n_blk, tq), jnp.int32)
    for i in range(n_blk):
        si = score[i:i + 1]
        beats = (si > score) | ((si == score) & (blk > i))
        rank = rank + beats.astype(jnp.int32)
    bias_t = jnp.where(rank < N_SELECT, 0.0, NEG_BIG)
    if n_blk < HEAD_DIM:
        bias_t = jnp.concatenate([bias_t, jnp.zeros((HEAD_DIM - n_blk, tq), F32)], axis=0)
    bias = bias_t.T.astype(BF16)
    qaug = jnp.concatenate([qb, jnp.concatenate([bias] * GROUP, axis=0)], axis=1)

    qpos_k = q0 + lax.broadcasted_iota(jnp.int32, (rows, tk), 0) % tq
    kcol = lax.broadcasted_iota(jnp.int32, (rows, tk), 1)

    def slc_body(kt, carry):
        m, l, acc = carry
        ksl = pl.ds(pl.multiple_of(kt * tk, tk), tk)
        s = _dot_nt(qaug, kaug_ref[ksl, :])
        s = jnp.where(kt * tk + kcol <= qpos_k, s, -jnp.inf)
        m_new = jnp.maximum(m, jnp.max(s, axis=-1, keepdims=True))
        alpha = jnp.exp(m - m_new)
        p = jnp.exp(s - m_new)
        l = alpha * l + jnp.sum(p, axis=-1, keepdims=True)
        acc = alpha * acc + _dot(p.astype(BF16), vsb_ref[ksl, :])
        return m_new, l, acc

    init = (jnp.full((rows, 1), -jnp.inf, F32), jnp.zeros((rows, 1), F32), jnp.zeros((rows, HEAD_DIM), F32))
    n_kt = (q0 + tq - 1) // tk + 1
    _, l_s, acc_s = lax.fori_loop(0, n_kt, slc_body, init)
    o_slc = acc_s / jnp.maximum(l_s, 1e-30)

    qpos_w = q0 + lax.broadcasted_iota(jnp.int32, (rows, tq), 0) % tq
    wcol = lax.broadcasted_iota(jnp.int32, (rows, tq), 1)

    def win_body(it, carry):
        m, l, acc = carry
        k0 = (qi - it) * tq
        ksl = pl.ds(pl.multiple_of(k0, tq), tq)
        s = _dot_nt(qb, kwb_ref[ksl, :])
        d = qpos_w - (k0 + wcol)
        s = jnp.where((d >= 0) & (d <= WINDOW), s, -jnp.inf)
        m_new = jnp.maximum(m, jnp.max(s, axis=-1, keepdims=True))
        alpha = jnp.exp(m - m_new)
        p = jnp.exp(s - m_new)
        l = alpha * l + jnp.sum(p, axis=-1, keepdims=True)
        acc = alpha * acc + _dot(p.astype(BF16), vwb_ref[ksl, :])
        return m_new, l, acc

    n_wt = jnp.minimum(qi, (WINDOW + tq - 1) // tq) + 1
    _, l_w, acc_w = lax.fori_loop(0, n_wt, win_body, init)
    o_win = acc_w / jnp.maximum(l_w, 1e-30)

    gates = jax.nn.sigmoid(gt_ref[0])
    for g in range(GROUP):
        rs = slice(g * tq, (g + 1) * tq)
        c0 = g * N_BRANCH
        o = (gates[:, c0:c0 + 1] * o_cmp[rs] + gates[:, c0 + 1:c0 + 2] * o_slc[rs]
             + gates[:, c0 + 2:c0 + 3] * o_win[rs])
        o_ref[0, :, g * HEAD_DIM:(g + 1) * HEAD_DIM] = o.astype(o_ref.dtype)


def nsa_prompt(zq3, gates3, k_cmp, v_cmp, rows3):
    B, L, _ = zq3.shape
    tq = _pick(L, (256, 128))
    tk = _pick(L, (512, 256, 128))
    n_sub = k_cmp.shape[1]
    qw = GROUP * HEAD_DIM
    rspec = lambda kind: pl.BlockSpec((1, L, HEAD_DIM), lambda b, n, i: (b, 0, kind * N_KV + n))
    cspec = pl.BlockSpec((1, n_sub, HEAD_DIM), lambda b, n, i: (b * N_KV + n, 0, 0))
    return pl.pallas_call(
        functools.partial(_nsa_prompt_kernel, tq=tq, tk=tk, seq_len=L),
        grid=(B, N_KV, L // tq),
        in_specs=[
            pl.BlockSpec((1, tq, qw), lambda b, n, i: (b, i, n)),
            pl.BlockSpec((1, tq, HEAD_DIM), lambda b, n, i: (b, i, n)),
            cspec, cspec, rspec(2), rspec(3), rspec(4), rspec(5),
        ],
        out_specs=pl.BlockSpec((1, tq, qw), lambda b, n, i: (b, i, n)),
        out_shape=jax.ShapeDtypeStruct((B, L, MIX_DIM), BF16),
        scratch_shapes=[
            pltpu.VMEM((L, 2 * HEAD_DIM), BF16),
            pltpu.VMEM((L, HEAD_DIM), BF16),
            pltpu.VMEM((L, HEAD_DIM), BF16),
            pltpu.VMEM((L, HEAD_DIM), BF16),
        ],
        compiler_params=_cp(("parallel", "parallel", "arbitrary")),
        name="nsa_prompt",
    )(zq3, gates3, k_cmp, v_cmp, rows3, rows3, rows3, rows3)


def _dec_cmp_partial_kernel(pt_ref, *refs, n_pg, n_groups):
    del pt_ref
    pg_refs = refs[:n_pg]
    new_ref, w_ref, o_ref, lhs_ref = refs[n_pg:]
    g = pl.program_id(1)
    sub_pg = PAGE_SIZE // CMP_STRIDE

    @pl.when(g < n_groups)
    def _():
        for r in range(n_pg):
            for p in range(CMP_STRIDE):
                x = pg_refs[r][0, pl.ds(p, sub_pg, stride=CMP_STRIDE), :, :]
                lhs_ref[r * 8 * sub_pg:(r + 1) * 8 * sub_pg, p * HEAD_DIM:(p + 1) * HEAD_DIM] = (
                    x.reshape(8 * sub_pg, HEAD_DIM).astype(BF16))

    @pl.when(g == n_groups)
    def _():
        lhs_ref[...] = jnp.zeros_like(lhs_ref)
        lhs_ref[0:16, 0:HEAD_DIM] = jnp.concatenate(
            [new_ref[0], jnp.zeros((8, HEAD_DIM), F32)], axis=0).astype(BF16)

    y = _dot(lhs_ref[...], w_ref[...])
    is_k = lax.broadcasted_iota(jnp.int32, (y.shape[0], CMP_HID), 0) % 8 < N_KV
    a = jnp.where(is_k, y[:, 0:CMP_HID], y[:, 2 * CMP_HID:3 * CMP_HID])
    bm = jnp.where(is_k, y[:, CMP_HID:2 * CMP_HID], y[:, 3 * CMP_HID:4 * CMP_HID])
    o_ref[0] = jnp.concatenate([a, bm], axis=1).reshape(n_pg * sub_pg, 8, 2 * CMP_HID)


def dec_cmp_partial(cache4, page_table, new_cmp, w1cat):
    B, n_pages = page_table.shape
    n_pg = _pick(n_pages, (8, 4, 2, 1))
    n_groups = n_pages // n_pg
    sub_pg = PAGE_SIZE // CMP_STRIDE

    def page_spec(r):
        def imap(b, g, pt):
            return (pt[b * n_pages + jnp.minimum(g * n_pg + r, n_pages - 1)], 0, 0, 0)
        return pl.BlockSpec((1, PAGE_SIZE, 8, HEAD_DIM), imap)

    grid_spec = pltpu.PrefetchScalarGridSpec(
        num_scalar_prefetch=1,
        grid=(B, n_groups + 1),
        in_specs=[page_spec(r) for r in range(n_pg)] + [
            pl.BlockSpec((1, 8, HEAD_DIM), lambda b, g, pt: (b, 0, 0)),
            pl.BlockSpec((CMP_STRIDE * HEAD_DIM, 4 * CMP_HID), lambda b, g, pt: (0, 0)),
        ],
        out_specs=pl.BlockSpec((1, n_pg * sub_pg, 8, 2 * CMP_HID), lambda b, g, pt: (b, g, 0, 0)),
        scratch_shapes=[pltpu.VMEM((n_pg * sub_pg * 8, CMP_STRIDE * HEAD_DIM), BF16)],
    )
    return pl.pallas_call(
        functools.partial(_dec_cmp_partial_kernel, n_pg=n_pg, n_groups=n_groups),
        grid_spec=grid_spec,
        out_shape=jax.ShapeDtypeStruct((B, (n_groups + 1) * n_pg * sub_pg, 8, 2 * CMP_HID), F32),
        compiler_params=_cp(("parallel", "arbitrary")),
        name="dec_cmp_partial",
    )(page_table.reshape(-1), *([cache4] * n_pg), new_cmp, w1cat)


def _dec_cmp_final_kernel(ab_ref, nx_ref, pe_ref, w2_ref, o_ref):
    ch = ab_ref.shape[1]
    a = ab_ref[0, :, :, 0:CMP_HID]
    bn = jnp.concatenate([ab_ref[0, 1:ch, :, CMP_HID:2 * CMP_HID], nx_ref[0, :, :, CMP_HID:2 * CMP_HID]], axis=0)
    h = a + bn + pe_ref[...][None]
    y = _dot(jax.nn.gelu(h).reshape(ch * 8, CMP_HID).astype(BF16), w2_ref[...])
    is_k = lax.broadcasted_iota(jnp.int32, (ch * 8, HEAD_DIM), 0) % 8 < N_KV
    o_ref[0] = jnp.where(is_k, y[:, 0:HEAD_DIM], y[:, HEAD_DIM:2 * HEAD_DIM]).reshape(ch, 8, HEAD_DIM)


def dec_cmp_final(ab, pe_sel, w2cat):
    B, S = ab.shape[:2]
    ch = _pick(S, (64, 32, 16, 8))
    return pl.pallas_call(
        _dec_cmp_final_kernel,
        grid=(B, S // ch),
        in_specs=[
            pl.BlockSpec((1, ch, 8, 2 * CMP_HID), lambda b, c: (b, c, 0, 0)),
            pl.BlockSpec((1, 1, 8, 2 * CMP_HID), lambda b, c: (b, jnp.minimum((c + 1) * ch, S - 1), 0, 0)),
            pl.BlockSpec((8, CMP_HID), lambda b, c: (0, 0)),
            pl.BlockSpec((CMP_HID, 2 * HEAD_DIM), lambda b, c: (0, 0)),
        ],
        out_specs=pl.BlockSpec((1, ch, 8, HEAD_DIM), lambda b, c: (b, c, 0, 0)),
        out_shape=jax.ShapeDtypeStruct((B, S, 8, HEAD_DIM), F32),
        compiler_params=_cp(("parallel", "parallel")),
        name="dec_cmp_final",
    )(ab, ab, pe_sel, w2cat)


def _dec_select_kernel(q_ref, kv_ref, mem_ref, ocmp_ref, idx_ref, *, q_pos, nc, n_blk, n_sel):
    R = kv_ref.shape[1]
    JB = mem_ref.shape[1]
    scale = HEAD_DIM ** -0.5
    kv = kv_ref[0]
    kb = kv.astype(BF16)
    qb = (q_ref[0] * scale).astype(BF16)
    s = _dot_nt(qb, kb)
    r_i = lax.broadcasted_iota(jnp.int32, (16, R), 1)
    h_i = lax.broadcasted_iota(jnp.int32, (16, R), 0)
    c_i = r_i >> 3
    slot = r_i & 7
    ok = ((CMP_STRIDE * c_i + (2 * CMP_STRIDE - 1) <= q_pos) & (c_i < nc) & (slot < N_KV)
          & (h_i >= GROUP * slot) & (h_i < GROUP * slot + GROUP))
    p = _masked_softmax_rows(s, ok)
    vb = pltpu.roll(kv, R - N_KV, 0).astype(BF16)
    ocmp_ref[0] = _dot(p.astype(BF16), vb)

    n_r = lax.broadcasted_iota(jnp.int32, (8, 16), 0)
    h_r = lax.broadcasted_iota(jnp.int32, (8, 16), 1)
    gsel = ((h_r >= GROUP * n_r) & (h_r < GROUP * n_r + GROUP)).astype(BF16)
    psum = sum(_dot(gsel, part) for part in _split3(p))
    imp = sum(_dot(part, mem_ref[...]) for part in _split3(psum))
    j_i = lax.broadcasted_iota(jnp.int32, (8, JB), 1)
    cur = q_pos // SLC_BLOCK
    valid = (j_i <= cur) & (j_i < n_blk)
    forced = valid & ((j_i == 0) | (j_i == cur) | (j_i == cur - 1))
    score = jnp.where(forced, SEL_BONUS, jnp.where(valid, imp, -jnp.inf))
    j_f = j_i.astype(F32)
    taken = j_i >= n_blk
    lane_o = lax.broadcasted_iota(jnp.int32, (8, HEAD_DIM), 1)
    out = jnp.zeros((8, HEAD_DIM), F32)
    for r in range(n_sel):
        m = jnp.max(jnp.where(taken, -jnp.inf, score), axis=-1, keepdims=True)
        cand = jnp.logical_not(taken) & (score == m)
        idx = jnp.min(jnp.where(cand, j_f, float(JB)), axis=-1, keepdims=True)
        out = jnp.where(lane_o == r, idx, out)
        taken = taken | (j_f == idx)
    idx_ref[0] = out.astype(jnp.int32)


def dec_select(q16, cmp_tok, member, *, q_pos, nc, n_blk, n_sel):
    B, R = cmp_tok.shape[:2]
    JB = member.shape[1]
    return pl.pallas_call(
        functools.partial(_dec_select_kernel, q_pos=q_pos, nc=nc, n_blk=n_blk, n_sel=n_sel),
        grid=(B,),
        in_specs=[
            pl.BlockSpec((1, 16, HEAD_DIM), lambda b: (b, 0, 0)),
            pl.BlockSpec((1, R, HEAD_DIM), lambda b: (b, 0, 0)),
            pl.BlockSpec((R, JB), lambda b: (0, 0)),
        ],
        out_specs=[
            pl.BlockSpec((1, 16, HEAD_DIM), lambda b: (b, 0, 0)),
            pl.BlockSpec((1, 8, HEAD_DIM), lambda b: (b, 0, 0)),
        ],
        out_shape=[
            jax.ShapeDtypeStruct((B, 16, HEAD_DIM), F32),
            jax.ShapeDtypeStruct((B, 8, HEAD_DIM), jnp.int32),
        ],
        compiler_params=_cp(("parallel",)),
        name="dec_select",
    )(q16, cmp_tok, member)


def _dec_attend_kernel(idx_ref, pt_ref, q_ref, gl_ref, ocmp_ref, nslc_ref, win_ref, nwin_ref, *rest,
                       q_pos, past_len, n_sel):
    del pt_ref
    blk_refs = rest[:n_sel]
    o_ref = rest[n_sel]
    b = pl.program_id(0)
    n = pl.program_id(1)
    base = (b * N_KV + n) * n_sel
    scale = HEAD_DIM ** -0.5
    past_blocks = past_len // SLC_BLOCK
    rb = SLC_BLOCK * 8
    rw = win_ref.shape[1]
    w_buf = rw // 8
    qf = q_ref[0, 0] * scale
    qb = qf.astype(BF16)
    js = [idx_ref[base + s] for s in range(n_sel)]
    has_new = js[0] == past_blocks
    for s in range(1, n_sel):
        has_new = has_new | (js[s] == past_blocks)
    new_ok = has_new & (past_len <= q_pos)

    def head_rows(x):
        slot = lax.broadcasted_iota(jnp.int32, (8, HEAD_DIM), 0)
        k = jnp.sum(jnp.where(slot == n, x, 0.0), axis=0, keepdims=True)
        v = jnp.sum(jnp.where(slot == n + N_KV, x, 0.0), axis=0, keepdims=True)
        return k, v

    def finish(m):
        return jnp.where(m == -jnp.inf, 0.0, m)

    r_s = lax.broadcasted_iota(jnp.int32, (8, rb), 1)
    t_s = r_s >> 3
    mine_s = (r_s & 7) == n

    def slc_scores(s):
        blk = blk_refs[s][0, 0].reshape(rb, HEAD_DIM)
        sc = _dot_nt(qb, blk.astype(BF16))
        ok = mine_s & (js[s] * SLC_BLOCK + t_s <= q_pos) & (js[s] < past_blocks)
        return jnp.where(ok, sc, -jnp.inf), blk

    k_new, v_new = head_rows(nslc_ref[0])
    s_new = jnp.where(new_ok, jnp.sum(qf * k_new, axis=-1, keepdims=True), -jnp.inf)
    m = s_new
    for s in range(n_sel):
        m = jnp.maximum(m, jnp.max(slc_scores(s)[0], axis=-1, keepdims=True))
    m = finish(m)
    p_new = jnp.exp(s_new - m)
    l = p_new
    acc = p_new * v_new
    for s in range(n_sel):
        sc, blk = slc_scores(s)
        p = jnp.exp(sc - m)
        l = l + jnp.sum(p, axis=-1, keepdims=True)
        acc = acc + _dot(p.astype(BF16), pltpu.roll(blk, rb - N_KV, 0).astype(BF16))
    o_slc = acc / jnp.maximum(l, 1e-30)

    r_w = lax.broadcasted_iota(jnp.int32, (8, rw), 1)
    w_pos = past_len - w_buf + (r_w >> 3)
    w_ok = ((r_w & 7) == n) & (q_pos - w_pos >= 0) & (q_pos - w_pos <= WINDOW) & (w_pos >= 0)
    win = win_ref[0]
    s_w = jnp.where(w_ok, _dot_nt(qb, win.astype(BF16)), -jnp.inf)
    k_nw, v_nw = head_rows(nwin_ref[0])
    nw_ok = (q_pos - past_len >= 0) and (q_pos - past_len <= WINDOW)
    s_nw = jnp.sum(qf * k_nw, axis=-1, keepdims=True) if nw_ok else jnp.full((8, 1), -jnp.inf, F32)
    m = finish(jnp.maximum(s_nw, jnp.max(s_w, axis=-1, keepdims=True)))
    p_w = jnp.exp(s_w - m)
    p_nw = jnp.exp(s_nw - m)
    l = p_nw + jnp.sum(p_w, axis=-1, keepdims=True)
    acc = p_nw * v_nw + _dot(p_w.astype(BF16), pltpu.roll(win, rw - N_KV, 0).astype(BF16))
    o_win = acc / jnp.maximum(l, 1e-30)

    gates = jax.nn.sigmoid(gl_ref[0, 0])
    o_ref[0, 0] = gates[:, 0:1] * ocmp_ref[0, 0] + gates[:, 1:2] * o_slc + gates[:, 2:3] * o_win


def dec_attend(idx_flat, page_table, q8, gate8, ocmp8, new_slc, win2, new_win, cache5, *, q_pos, past_len, n_sel):
    B, n_pages = page_table.shape
    rw = win2.shape[1]
    past_blocks = past_len // SLC_BLOCK
    per_page = PAGE_SIZE // SLC_BLOCK

    def blk_spec(s):
        def imap(b, n, idx, pt):
            j = jnp.minimum(idx[(b * N_KV + n) * n_sel + s], past_blocks - 1)
            return (pt[b * n_pages + j // per_page], j % per_page, 0, 1, 0)
        return pl.BlockSpec((1, 1, SLC_BLOCK, 8, HEAD_DIM), imap)

    head_spec = pl.BlockSpec((1, 1, 8, HEAD_DIM), lambda b, n, idx, pt: (b, n, 0, 0))
    tok_spec = pl.BlockSpec((1, 8, HEAD_DIM), lambda b, n, idx, pt: (b, 0, 0))
    grid_spec = pltpu.PrefetchScalarGridSpec(
        num_scalar_prefetch=2,
        grid=(B, N_KV),
        in_specs=[head_spec, head_spec, head_spec, tok_spec,
                  pl.BlockSpec((1, rw, HEAD_DIM), lambda b, n, idx, pt: (b, 0, 0)), tok_spec]
        + [blk_spec(s) for s in range(n_sel)],
        out_specs=head_spec,
    )
    return pl.pallas_call(
        functools.partial(_dec_attend_kernel, q_pos=q_pos, past_len=past_len, n_sel=n_sel),
        grid_spec=grid_spec,
        out_shape=jax.ShapeDtypeStruct((B, N_KV, 8, HEAD_DIM), F32),
        compiler_params=_cp(("parallel", "arbitrary")),
        name="dec_attend",
    )(idx_flat, page_table.reshape(-1), q8, gate8, ocmp8, new_slc, win2, new_win, *([cache5] * n_sel))


def _pad_to(a, axis, size):
    pad = [(0, 0)] * a.ndim
    pad[axis] = (0, size - a.shape[axis])
    return jnp.pad(a, pad)


def _prep_weights(w_in_a, w_in_b, b_gate, w_kv, cmp_pe, cmp_w1, cmp_w2, w_mem_kv, w_out, w_up, conv_w, conv_b,
                  w_down, hgrn_lb):
    depth = w_out.shape[0]
    n_b = w_in_b.shape[0]
    n_gate = N_MIX_HEADS * N_BRANCH
    per_kv = GROUP * N_BRANCH
    w_q = w_in_b[:, :, :MIX_DIM]
    w_g = w_in_b[:, :, MIX_DIM:MIX_DIM + n_gate].reshape(n_b, D_MODEL, N_KV, per_kv)
    w_m = w_in_b[:, :, MIX_DIM + n_gate:]
    half = CMP_STRIDE * HEAD_DIM
    w1 = cmp_w1.astype(BF16)
    lb = jnp.cumsum(jax.nn.softmax(hgrn_lb.astype(F32), axis=0), axis=0)
    lb = lb - lb[0]
    return dict(
        in_a=w_in_a.astype(BF16),
        in_b_qm=jnp.concatenate([w_q, w_m], axis=-1).astype(BF16),
        in_b_gate=_pad_to(w_g, 3, HEAD_DIM).reshape(n_b, D_MODEL, N_KV * HEAD_DIM).astype(BF16),
        b_gate=_pad_to(b_gate.reshape(n_b, N_KV, per_kv), 2, HEAD_DIM).reshape(n_b, N_KV * HEAD_DIM),
        kv=w_kv.astype(BF16),
        mem_kv=w_mem_kv.astype(BF16),
        out_mix=w_out[:, :MIX_DIM].astype(BF16),
        out_mem=w_out[:, MIX_DIM:].astype(BF16),
        up_a=_pad_to(w_up[:, :, :D_FF], 2, D_FF_PAD).astype(BF16),
        up_u=_pad_to(w_up[:, :, D_FF:], 2, D_FF_PAD).astype(BF16),
        conv_w=_pad_to(_pad_to(conv_w, 2, D_FF_PAD), 1, 8),
        conv_b=_pad_to(conv_b, 1, D_FF_PAD).reshape(depth, 1, D_FF_PAD),
        down=_pad_to(w_down, 1, D_FF_PAD).astype(BF16),
        cmp_w1=w1,
        cmp_w1cat=jnp.concatenate([w1[0, :half], w1[0, half:], w1[1, :half], w1[1, half:]], axis=1),
        cmp_w2=cmp_w2.astype(BF16),
        cmp_w2cat=jnp.concatenate([cmp_w2[0], cmp_w2[1]], axis=1).astype(BF16),
        cmp_pe=cmp_pe.reshape(2, 1, 2 * half),
        log_lb=jnp.log(lb),
        log1m_lb=jnp.log1p(-lb),
    )


def kernel(x_prompt, x_sample, mem_prompt, cache_nsa_kv, page_table, cache_win_kv, state_hgrn, state_conv,
           cache_mem_kv, norm_mix, norm_ffn, norm_mem, norm_kv, norm_final, w_in_a, hgrn_lb, hgrn_onorm,
           w_in_b, b_gate, w_kv, cmp_pe, cmp_w1, cmp_w2, w_mem_kv, w_out, w_up, conv_w, conv_b, w_down):
    B, L, _ = x_prompt.shape
    Bs = x_sample.shape[0]
    depth = w_out.shape[0]
    n_a = w_in_a.shape[0]
    n_mem = mem_prompt.shape[1]
    n_pool, page, _, _, _ = cache_nsa_kv.shape
    n_pages = page_table.shape[1]
    past_len = n_pages * page
    w_buf = cache_win_kv.shape[1]
    assert x_sample.shape[1] == 1 and page == PAGE_SIZE and L % 128 == 0 and L // SLC_BLOCK <= HEAD_DIM
    W = _prep_weights(w_in_a, w_in_b, b_gate, w_kv, cmp_pe, cmp_w1, cmp_w2, w_mem_kv, w_out, w_up, conv_w,
                      conv_b, w_down, hgrn_lb)
    pe_h = pe_proj(W["cmp_pe"], W["cmp_w1"])
    onorm = hgrn_onorm.reshape(n_a, 1, MIX_DIM)
    log_lb = W["log_lb"].reshape(n_a, 1, MIX_DIM)
    log1m_lb = W["log1m_lb"].reshape(n_a, 1, MIX_DIM)

    def ffn(x, l, **kw):
        return conv_ffn(x, norm_ffn[l], W["up_a"][l], W["up_u"][l], W["conv_w"][l], W["conv_b"][l], W["down"][l], **kw)

    M = B * L
    mem_flat = mem_prompt.reshape(B * n_mem, D_MODEL)
    mem_kv_p = [rms_matmul(mem_flat, norm_mem[l], W["mem_kv"][l]).reshape(B, n_mem, 2 * MEM_DIM) for l in range(depth)]
    x = x_prompt.reshape(M, D_MODEL)
    hgrn_p, conv_p = [], []
    tm_ffn = _pick(L, (512, 256, 128))
    for l in range(depth):
        if l < n_a:
            z3 = rms_matmul(x, norm_mix[l], W["in_a"][l]).reshape(B, L, -1)
            s0 = jnp.zeros((B, N_MIX_HEADS, HEAD_DIM, HEAD_DIM), F32)
            o_mix, s_new = hgrn_prompt(z3, log_lb[l], log1m_lb[l], onorm[l], s0)
            hgrn_p.append(s_new)
            o_mem = mem_attention(z3, 4 * MIX_DIM // MEM_DIM, mem_kv_p[l])
        else:
            j = l - n_a
            if j == 0:
                rows_p = rms_matmul(x, norm_kv, W["kv"])
                rows3 = rows_p.reshape(B, L, -1)
                n_sub = L // CMP_STRIDE
                xsub = rows_p[:, :2 * N_KV * HEAD_DIM].reshape(B, n_sub, CMP_STRIDE, 2, N_KV, HEAD_DIM)
                xsub = xsub.transpose(3, 0, 4, 1, 2, 5).reshape(2, B * N_KV, n_sub, CMP_STRIDE * HEAD_DIM)
                cmp_p = cmp_mlp(xsub, W["cmp_w1"], W["cmp_w2"], pe_h)
            zq3 = rms_matmul(x, norm_mix[l], W["in_b_qm"][j]).reshape(B, L, -1)
            gt3 = rms_matmul(x, norm_mix[l], W["in_b_gate"][j], W["b_gate"][j]).reshape(B, L, -1)
            o_mix = nsa_prompt(zq3, gt3, cmp_p[0], cmp_p[1], rows3)
            o_mem = mem_attention(zq3, MIX_DIM // MEM_DIM, mem_kv_p[l])
        x = out_proj(o_mix.reshape(M, MIX_DIM), o_mem.reshape(M, MEM_DIM), W["out_mix"][l], W["out_mem"][l], x)
        x, a_tail = ffn(x, l, seq_len=L)
        conv_p.append(a_tail.reshape(B, L // tm_ffn, 8, D_FF_PAD)[:, -1, 8 - (CONV_W - 1):, :D_FF])
    y_prompt = rmsnorm_rows(x, norm_final).reshape(B, L, D_MODEL)
    n_row4 = 4 * N_KV * HEAD_DIM
    w_keep = min(WINDOW, L)
    nsa_rows_prompt = rows3[:, :, :n_row4].reshape(B, L, 4, N_KV, HEAD_DIM)
    win_prompt = rows3[:, L - w_keep:, n_row4:].reshape(B, w_keep, 2, N_KV, HEAD_DIM)
    mem_kv_prompt = jnp.stack(mem_kv_p).reshape(depth, B, n_mem, 2, N_MEM_HEADS, HEAD_DIM)

    Ms = 16
    pad_rows = lambda a: _pad_to(a, 0, Ms)
    xs = pad_rows(x_sample.reshape(Bs, D_MODEL))
    hgrn_s, conv_s = [], []
    q_pos = past_len
    t_pad = -(-(past_len + 1) // SLC_BLOCK) * SLC_BLOCK
    n_blk = t_pad // SLC_BLOCK
    nc = t_pad // CMP_STRIDE - 1
    n_sel = min(N_SELECT, n_blk)
    for l in range(depth):
        if l < n_a:
            z = rms_matmul(xs, norm_mix[l], W["in_a"][l])
            z3 = z[:Bs].reshape(Bs, 1, -1)
            o_mix, s_new = hgrn_step(z3, log_lb[l], log1m_lb[l], onorm[l], state_hgrn[l])
            hgrn_s.append(s_new)
            o_mix = o_mix.reshape(Bs, MIX_DIM)
            o_mem = mem_attention(z3, 4 * MIX_DIM // MEM_DIM, cache_mem_kv[l].reshape(Bs, n_mem, 2 * MEM_DIM))
        else:
            j = l - n_a
            if j == 0:
                rows_s = rms_matmul(xs, norm_kv, W["kv"])[:Bs]
                new_cmp = rows_s[:, 0:1024].reshape(Bs, 8, HEAD_DIM)
                new_slc = rows_s[:, 1024:2048].reshape(Bs, 8, HEAD_DIM)
                new_win = rows_s[:, 2048:3072].reshape(Bs, 8, HEAD_DIM)
                cache4 = cache_nsa_kv.reshape(n_pool, PAGE_SIZE, 4 * N_KV, HEAD_DIM)
                cache5 = cache_nsa_kv.reshape(n_pool, PAGE_SIZE // SLC_BLOCK, SLC_BLOCK, 4 * N_KV, HEAD_DIM)
                win2 = cache_win_kv.reshape(Bs, w_buf * 2 * N_KV, HEAD_DIM)
                ab = dec_cmp_partial(cache4, page_table, new_cmp, W["cmp_w1cat"])
                pe_sel = jnp.concatenate([jnp.broadcast_to(pe_h[0, 0:1], (N_KV, CMP_HID)),
                                          jnp.broadcast_to(pe_h[1, 0:1], (N_KV, CMP_HID))], axis=0)
                cmp_s = dec_cmp_final(ab, pe_sel, W["cmp_w2cat"])
                assert cmp_s.shape[1] > nc
                cmp_s = cmp_s.reshape(Bs, -1, HEAD_DIM)
                ratio = SLC_BLOCK // CMP_STRIDE
                c_of_row = jnp.arange(cmp_s.shape[1])[:, None] // 8
                j_of_col = jnp.arange(-(-n_blk // HEAD_DIM) * HEAD_DIM)[None, :]
                member = ((c_of_row >= ratio * j_of_col - 1) & (c_of_row <= ratio * j_of_col + ratio - 1)).astype(BF16)
            zq = rms_matmul(xs, norm_mix[l], W["in_b_qm"][j])
            gt = rms_matmul(xs, norm_mix[l], W["in_b_gate"][j], W["b_gate"][j])[:Bs]
            z3 = zq[:Bs].reshape(Bs, 1, -1)
            q12 = zq[:Bs, :MIX_DIM].reshape(Bs, N_MIX_HEADS, HEAD_DIM)
            o_cmp, idx = dec_select(_pad_to(q12, 1, 16), cmp_s, member, q_pos=q_pos, nc=nc, n_blk=n_blk, n_sel=n_sel)
            idx_flat = idx[:, :N_KV, :n_sel].reshape(-1)
            per_head = lambda a: _pad_to(a.reshape(Bs, N_KV, GROUP, -1), 2, 8)
            gate9 = gt.reshape(Bs, N_KV, HEAD_DIM)[:, :, :GROUP * N_BRANCH]
            o_all = dec_attend(idx_flat, page_table, per_head(q12), _pad_to(per_head(gate9), 3, HEAD_DIM),
                               per_head(o_cmp[:, :N_MIX_HEADS]), new_slc, win2, new_win, cache5,
                               q_pos=q_pos, past_len=past_len, n_sel=n_sel)
            o_mix = o_all[:, :, :GROUP].reshape(Bs, MIX_DIM).astype(BF16)
            o_mem = mem_attention(z3, MIX_DIM // MEM_DIM, cache_mem_kv[l].reshape(Bs, n_mem, 2 * MEM_DIM))
        xs = out_proj(pad_rows(o_mix), pad_rows(o_mem.reshape(Bs, MEM_DIM)), W["out_mix"][l], W["out_mem"][l], xs)
        prev = (pad_rows(_pad_to(state_conv[l][:, 1], 1, D_FF_PAD)), pad_rows(_pad_to(state_conv[l][:, 0], 1, D_FF_PAD)))
        xs, a_new = ffn(xs, l, prev=prev)
        conv_s.append(jnp.stack([state_conv[l][:, 1], a_new[0, :Bs, :D_FF]], axis=1))
    y_sample = rmsnorm_rows(xs, norm_final)[:Bs].reshape(Bs, 1, D_MODEL)
    nsa_rows_sample = rows_s[:, :n_row4].reshape(Bs, 1, 4, N_KV, HEAD_DIM)
    win_new = rows_s[:, n_row4:].reshape(Bs, 1, 2, N_KV, HEAD_DIM).astype(cache_win_kv.dtype)
    win_sample = jnp.concatenate([cache_win_kv, win_new], axis=1)[:, 1:]

    return (y_prompt, y_sample, nsa_rows_prompt, nsa_rows_sample, win_prompt, win_sample,
            jnp.stack(hgrn_p), jnp.stack(hgrn_s), jnp.stack(conv_p), jnp.stack(conv_s), mem_kv_prompt)
```

```python
import functools

import jax
import jax.numpy as jnp
from jax import lax
from jax.experimental import pallas as pl
from jax.experimental.pallas import tpu as pltpu

F32 = jnp.float32
BF16 = jnp.bfloat16

D_MODEL = 2048
HEAD_DIM = 128
MIX_DIM = 1536
N_MIX_HEADS = 12
MEM_DIM = 512
N_MEM_HEADS = 4
N_KV = 4
GROUP = 3
N_BRANCH = 3
CMP_STRIDE = 16
CMP_HID = 256
SLC_BLOCK = 64
N_SELECT = 16
WINDOW = 512
PAGE_SIZE = 128
D_FF = 5504
D_FF_PAD = 5632
CONV_W = 3
RMS_EPS = 1e-6
SEL_BONUS = 1e9
NEG_BIG = -1e30

HGRN_T = 64
HGRN_SUB = 16
HGRN_HEADS_PER_STEP = 4
FFN_TF = 512
VMEM_LIMIT = 56 * 1024 * 1024


def _cp(sem, vmem=VMEM_LIMIT):
    return pltpu.CompilerParams(dimension_semantics=sem, vmem_limit_bytes=vmem)


def _dot(a, b):
    return jnp.dot(a, b, preferred_element_type=F32)


def _dot_nt(a, b):
    return lax.dot_general(a, b, (((1,), (1,)), ((), ())), preferred_element_type=F32)


def _dot_tn(a, b):
    return lax.dot_general(a, b, (((0,), (0,)), ((), ())), preferred_element_type=F32)


def _pick(n, cands):
    for c in cands:
        if n % c == 0:
            return c
    return n


def _rms_matmul_kernel(x_ref, g_ref, w_ref, b_ref, o_ref, xn_ref):
    @pl.when(pl.program_id(1) == 0)
    def _():
        x = x_ref[...]
        y = x * lax.rsqrt(jnp.mean(x * x, axis=-1, keepdims=True) + RMS_EPS)
        xn_ref[...] = (y * g_ref[...]).astype(BF16)

    o_ref[...] = _dot(xn_ref[...], w_ref[...]) + b_ref[...]


def rms_matmul(x, g, w, bias=None):
    M, K = x.shape
    N = w.shape[1]
    tm = _pick(M, (1024, 512, 256, 128))
    tn = _pick(N, (512, 256, 128))
    if bias is None:
        bias = jnp.zeros((N,), F32)
    return pl.pallas_call(
        _rms_matmul_kernel,
        grid=(M // tm, N // tn),
        in_specs=[
            pl.BlockSpec((tm, K), lambda i, j: (i, 0)),
            pl.BlockSpec((1, K), lambda i, j: (0, 0)),
            pl.BlockSpec((K, tn), lambda i, j: (0, j)),
            pl.BlockSpec((1, tn), lambda i, j: (0, j)),
        ],
        out_specs=pl.BlockSpec((tm, tn), lambda i, j: (i, j)),
        out_shape=jax.ShapeDtypeStruct((M, N), F32),
        scratch_shapes=[pltpu.VMEM((tm, K), BF16)],
        compiler_params=_cp(("parallel", "arbitrary")),
        name="rms_matmul",
    )(x, g.reshape(1, K), w, bias.reshape(1, N))


def _rmsnorm_kernel(x_ref, g_ref, o_ref):
    x = x_ref[...]
    y = x * lax.rsqrt(jnp.mean(x * x, axis=-1, keepdims=True) + RMS_EPS)
    o_ref[...] = y * g_ref[...]


def rmsnorm_rows(x, g):
    M, K = x.shape
    tm = _pick(M, (512, 256, 128))
    return pl.pallas_call(
        _rmsnorm_kernel,
        grid=(M // tm,),
        in_specs=[pl.BlockSpec((tm, K), lambda i: (i, 0)), pl.BlockSpec((1, K), lambda i: (0, 0))],
        out_specs=pl.BlockSpec((tm, K), lambda i: (i, 0)),
        out_shape=jax.ShapeDtypeStruct((M, K), F32),
        compiler_params=_cp(("parallel",)),
        name="final_rmsnorm",
    )(x, g.reshape(1, K))


def _outproj_kernel(om_ref, oe_ref, w1_ref, w2_ref, x_ref, y_ref):
    y_ref[...] = x_ref[...] + _dot(om_ref[...], w1_ref[...]) + _dot(oe_ref[...], w2_ref[...])


def out_proj(o_mix, o_mem, w_mix, w_mem, x):
    M = x.shape[0]
    tm = _pick(M, (1024, 512, 256, 128))
    tn = 512
    return pl.pallas_call(
        _outproj_kernel,
        grid=(M // tm, D_MODEL // tn),
        in_specs=[
            pl.BlockSpec((tm, MIX_DIM), lambda i, j: (i, 0)),
            pl.BlockSpec((tm, MEM_DIM), lambda i, j: (i, 0)),
            pl.BlockSpec((MIX_DIM, tn), lambda i, j: (0, j)),
            pl.BlockSpec((MEM_DIM, tn), lambda i, j: (0, j)),
            pl.BlockSpec((tm, tn), lambda i, j: (i, j)),
        ],
        out_specs=pl.BlockSpec((tm, tn), lambda i, j: (i, j)),
        out_shape=jax.ShapeDtypeStruct((M, D_MODEL), F32),
        compiler_params=_cp(("parallel", "parallel")),
        name="out_proj",
    )(o_mix, o_mem, w_mix, w_mem, x)


def _mem_attn_kernel(q_ref, kv_ref, o_ref):
    scale = HEAD_DIM ** -0.5
    for h in range(N_MEM_HEADS):
        q = q_ref[0, :, h * HEAD_DIM:(h + 1) * HEAD_DIM].astype(BF16)
        k = kv_ref[0, :, h * HEAD_DIM:(h + 1) * HEAD_DIM].astype(BF16)
        v = kv_ref[0, :, MEM_DIM + h * HEAD_DIM:MEM_DIM + (h + 1) * HEAD_DIM].astype(BF16)
        s = _dot_nt(q, k) * scale
        m = jnp.max(s, axis=-1, keepdims=True)
        p = jnp.exp(s - m)
        p = p / jnp.sum(p, axis=-1, keepdims=True)
        o_ref[0, :, h * HEAD_DIM:(h + 1) * HEAD_DIM] = _dot(p.astype(BF16), v).astype(o_ref.dtype)


def mem_attention(z3, col_block, kv):
    B, L, _ = z3.shape
    n_mem = kv.shape[1]
    tm = _pick(L, (1024, 512, 256, 128))
    return pl.pallas_call(
        _mem_attn_kernel,
        grid=(B, L // tm),
        in_specs=[
            pl.BlockSpec((1, tm, MEM_DIM), lambda b, i: (b, i, col_block)),
            pl.BlockSpec((1, n_mem, 2 * MEM_DIM), lambda b, i: (b, 0, 0)),
        ],
        out_specs=pl.BlockSpec((1, tm, MEM_DIM), lambda b, i: (b, i, 0)),
        out_shape=jax.ShapeDtypeStruct((B, L, MEM_DIM), BF16),
        compiler_params=_cp(("parallel", "parallel")),
        name="mem_attention",
    )(z3, kv)


def _ffn_kernel(*refs, blocks_per_seq, decode):
    if decode:
        (x_ref, g_ref, wa_ref, wu_ref, cw_ref, cb_ref, wd_ref, p1_ref, p2_ref,
         y_ref, at_ref, xn_ref, acc_ref) = refs
    else:
        (x_ref, g_ref, wa_ref, wu_ref, cw_ref, cb_ref, wd_ref,
         y_ref, at_ref, xn_ref, acc_ref, carry_ref) = refs
    i = pl.program_id(0)
    j = pl.program_id(1)

    @pl.when(j == 0)
    def _():
        x = x_ref[...]
        y = x * lax.rsqrt(jnp.mean(x * x, axis=-1, keepdims=True) + RMS_EPS)
        xn_ref[...] = (y * g_ref[...]).astype(BF16)
        acc_ref[...] = jnp.zeros_like(acc_ref)

    xn = xn_ref[...]
    a = _dot(xn, wa_ref[...])
    u = _dot(xn, wu_ref[...])
    tm = a.shape[0]
    if decode:
        a1 = p1_ref[...]
        a2 = p2_ref[...]
    else:
        @pl.when((i % blocks_per_seq) == 0)
        def _():
            carry_ref[j] = jnp.zeros((8, a.shape[1]), F32)

        prev = carry_ref[j]
        row = lax.broadcasted_iota(jnp.int32, a.shape, 0)
        a1 = jnp.where(row == 0, prev[7:8], pltpu.roll(a, 1, 0))
        a2 = jnp.where(row == 0, prev[6:7], jnp.where(row == 1, prev[7:8], pltpu.roll(a, 2, 0)))
        carry_ref[j] = a[tm - 8:tm]
    at_ref[0] = a[tm - at_ref.shape[1]:tm]
    c = cb_ref[...] + cw_ref[2:3] * a
    c = c + cw_ref[0:1] * a2
    c = c + cw_ref[1:2] * a1
    h = (jax.nn.gelu(c) * u).astype(BF16)
    acc_ref[...] += _dot(h, wd_ref[...])

    @pl.when(j == pl.num_programs(1) - 1)
    def _():
        y_ref[...] = x_ref[...] + acc_ref[...]


def conv_ffn(x, g, wa, wu, cw, cb, wd, *, seq_len=None, prev=None):
    M = x.shape[0]
    decode = prev is not None
    tm = M if decode else _pick(seq_len, (512, 256, 128))
    n_tail = tm if decode else 8
    nf = D_FF_PAD // FFN_TF
    in_specs = [
        pl.BlockSpec((tm, D_MODEL), lambda i, j: (i, 0)),
        pl.BlockSpec((1, D_MODEL), lambda i, j: (0, 0)),
        pl.BlockSpec((D_MODEL, FFN_TF), lambda i, j: (0, j)),
        pl.BlockSpec((D_MODEL, FFN_TF), lambda i, j: (0, j)),
        pl.BlockSpec((8, FFN_TF), lambda i, j: (0, j)),
        pl.BlockSpec((1, FFN_TF), lambda i, j: (0, j)),
        pl.BlockSpec((FFN_TF, D_MODEL), lambda i, j: (j, 0)),
    ]
    args = [x, g.reshape(1, D_MODEL), wa, wu, cw, cb, wd]
    scratch = [pltpu.VMEM((tm, D_MODEL), BF16), pltpu.VMEM((tm, D_MODEL), F32)]
    if decode:
        in_specs += [pl.BlockSpec((tm, FFN_TF), lambda i, j: (i, j))] * 2
        args += list(prev)
    else:
        scratch.append(pltpu.VMEM((nf, 8, FFN_TF), F32))
    return pl.pallas_call(
        functools.partial(_ffn_kernel, blocks_per_seq=(1 if decode else seq_len // tm), decode=decode),
        grid=(M // tm, nf),
        in_specs=in_specs,
        out_specs=[
            pl.BlockSpec((tm, D_MODEL), lambda i, j: (i, 0)),
            pl.BlockSpec((1, n_tail, FFN_TF), lambda i, j: (i, 0, j)),
        ],
        out_shape=[
            jax.ShapeDtypeStruct((M, D_MODEL), F32),
            jax.ShapeDtypeStruct((M // tm, n_tail, D_FF_PAD), F32),
        ],
        scratch_shapes=scratch,
        compiler_params=_cp(("arbitrary", "arbitrary")),
        name="conv_ffn",
    )(*args)


def _hgrn_gates(q, f, log_lb, log1m_lb):
    qs = jax.nn.silu(q)
    log_f = jnp.logaddexp(log_lb, log1m_lb + jax.nn.log_sigmoid(f))
    k = 1.0 - jnp.exp(log_f)
    return qs, k, log_f


def _hgrn_out(o, g, onorm):
    y = o * lax.rsqrt(jnp.mean(o * o, axis=-1, keepdims=True) + RMS_EPS)
    return (y * onorm) * jax.nn.silu(g)


def _hgrn_kernel(q_ref, f_ref, i_ref, g_ref, llb_ref, l1m_ref, on_ref, s0_ref, o_ref, s_ref, st_ref,
                 *, n_chunks):
    c = pl.program_id(2)
    T = HGRN_T
    n_heads = st_ref.shape[0]

    @pl.when(c == 0)
    def _():
        for h in range(n_heads):
            st_ref[h] = s0_ref[0, h].T

    row = lax.broadcasted_iota(jnp.int32, (T, HEAD_DIM), 0)
    rl = row % HGRN_SUB

    def one_head(h, sl):
        hs = slice(h * HEAD_DIM, (h + 1) * HEAD_DIM)
        q, k, lf = _hgrn_gates(q_ref[0, sl, hs], f_ref[0, sl, hs], llb_ref[:, hs], l1m_ref[:, hs])
        v = i_ref[0, sl, hs]
        b = lf
        s = 1
        while s < T:
            b = b + jnp.where(row >= s, pltpu.roll(b, s, 0), 0.0)
            s *= 2
        st = st_ref[h]
        o = _dot_nt((q * jnp.exp(b)).astype(BF16), st.astype(BF16))
        o = o + jnp.sum(q * k, axis=-1, keepdims=True) * v
        for d in range(1, HGRN_SUB):
            w = jnp.exp(jnp.where(rl >= d, b - pltpu.roll(b, d, 0), -jnp.inf))
            a = jnp.sum(q * pltpu.roll(k, d, 0) * w, axis=-1, keepdims=True)
            o = o + a * pltpu.roll(v, d, 0)
        parts = [jnp.zeros((HGRN_SUB, HEAD_DIM), F32)]
        for blk in range(1, T // HGRN_SUB):
            lo = blk * HGRN_SUB
            r = b[lo - 1:lo]
            qt = q[lo:lo + HGRN_SUB] * jnp.exp(b[lo:lo + HGRN_SUB] - r)
            kt = k[0:lo] * jnp.exp(r - b[0:lo])
            att = _dot_nt(qt.astype(BF16), kt.astype(BF16))
            parts.append(_dot(att.astype(BF16), v[0:lo].astype(BF16)))
        o = o + jnp.concatenate(parts, axis=0)
        bl = b[T - 1:T]
        kt = k * jnp.exp(bl - b)
        st_ref[h] = st * jnp.exp(bl) + _dot_tn(v.astype(BF16), kt.astype(BF16))
        o_ref[0, sl, hs] = _hgrn_out(o, g_ref[0, sl, hs], on_ref[:, hs]).astype(o_ref.dtype)

    def chunk(ci, carry):
        sl = pl.ds(pl.multiple_of(ci * T, T), T)
        for h in range(n_heads):
            one_head(h, sl)
        return carry

    lax.fori_loop(0, n_chunks, chunk, 0)

    @pl.when(c == pl.num_programs(2) - 1)
    def _():
        for h in range(n_heads):
            s_ref[0, h] = st_ref[h].T


def hgrn_prompt(z3, log_lb, log1m_lb, onorm, s0):
    B, L, _ = z3.shape
    H = N_MIX_HEADS
    hb = HGRN_HEADS_PER_STEP
    wb = hb * HEAD_DIM
    tc = _pick(L, (512, 256, 128, 64))
    zspec = lambda k: pl.BlockSpec((1, tc, wb), lambda b, h, c: (b, c, k * (H // hb) + h))
    vspec = pl.BlockSpec((1, wb), lambda b, h, c: (0, h))
    sspec = pl.BlockSpec((1, hb, HEAD_DIM, HEAD_DIM), lambda b, h, c: (b, h, 0, 0))
    return pl.pallas_call(
        functools.partial(_hgrn_kernel, n_chunks=tc // HGRN_T),
        grid=(B, H // hb, L // tc),
        in_specs=[zspec(0), zspec(1), zspec(2), zspec(3), vspec, vspec, vspec, sspec],
        out_specs=[pl.BlockSpec((1, tc, wb), lambda b, h, c: (b, c, h)), sspec],
        out_shape=[
            jax.ShapeDtypeStruct((B, L, MIX_DIM), BF16),
            jax.ShapeDtypeStruct((B, H, HEAD_DIM, HEAD_DIM), F32),
        ],
        scratch_shapes=[pltpu.VMEM((hb, HEAD_DIM, HEAD_DIM), F32)],
        compiler_params=_cp(("parallel", "parallel", "arbitrary")),
        name="hgrn_chunked",
    )(z3, z3, z3, z3, log_lb, log1m_lb, onorm, s0)


def _hgrn_step_kernel(z_ref, llb_ref, l1m_ref, on_ref, s0_ref, o_ref, s_ref):
    for h in range(N_MIX_HEADS):
        col = lambda k: slice((k * N_MIX_HEADS + h) * HEAD_DIM, (k * N_MIX_HEADS + h + 1) * HEAD_DIM)
        hs = slice(h * HEAD_DIM, (h + 1) * HEAD_DIM)
        q, k, lf = _hgrn_gates(z_ref[0, :, col(0)], z_ref[0, :, col(1)], llb_ref[:, hs], l1m_ref[:, hs])
        v = z_ref[0, :, col(2)]
        g = z_ref[0, :, col(3)]
        rows = jnp.concatenate([q, k, jnp.exp(lf), jnp.zeros((5, HEAD_DIM), F32)], axis=0)
        cols = rows.T
        s_new = cols[:, 2:3] * s0_ref[0, h] + cols[:, 1:2] * v
        s_ref[0, h] = s_new
        o = jnp.sum(cols[:, 0:1] * s_new, axis=0, keepdims=True)
        o_ref[0, :, hs] = _hgrn_out(o, g, on_ref[:, hs]).astype(o_ref.dtype)


def hgrn_step(z3, log_lb, log1m_lb, onorm, s0):
    B = z3.shape[0]
    W = z3.shape[2]
    H = N_MIX_HEADS
    vspec = pl.BlockSpec((1, MIX_DIM), lambda b: (0, 0))
    sspec = pl.BlockSpec((1, H, HEAD_DIM, HEAD_DIM), lambda b: (b, 0, 0, 0))
    return pl.pallas_call(
        _hgrn_step_kernel,
        grid=(B,),
        in_specs=[pl.BlockSpec((1, 1, W), lambda b: (b, 0, 0)), vspec, vspec, vspec, sspec],
        out_specs=[pl.BlockSpec((1, 1, MIX_DIM), lambda b: (b, 0, 0)), sspec],
        out_shape=[
            jax.ShapeDtypeStruct((B, 1, MIX_DIM), BF16),
            jax.ShapeDtypeStruct((B, H, HEAD_DIM, HEAD_DIM), F32),
        ],
        compiler_params=_cp(("parallel",)),
        name="hgrn_step",
    )(z3, log_lb, log1m_lb, onorm, s0)


def _pe_proj_kernel(pe_ref, w_ref, o_ref):
    pe = jnp.broadcast_to(pe_ref[0], (8, pe_ref.shape[2])).astype(BF16)
    o_ref[0] = _dot(pe, w_ref[0])


def pe_proj(pe, w1):
    K = pe.shape[2]
    return pl.pallas_call(
        _pe_proj_kernel,
        grid=(2,),
        in_specs=[pl.BlockSpec((1, 1, K), lambda i: (i, 0, 0)), pl.BlockSpec((1, K, CMP_HID), lambda i: (i, 0, 0))],
        out_specs=pl.BlockSpec((1, 8, CMP_HID), lambda i: (i, 0, 0)),
        out_shape=jax.ShapeDtypeStruct((2, 8, CMP_HID), F32),
        compiler_params=_cp(("parallel",)),
        name="cmp_pe_proj",
    )(pe, w1)


def _cmp_mlp_kernel(x_ref, w1_ref, w2_ref, pe_ref, o_ref):
    half = x_ref.shape[3]
    x = x_ref[0, 0].astype(BF16)
    a = _dot(x, w1_ref[0, 0:half])
    bm = _dot(x, w1_ref[0, half:2 * half])
    n = a.shape[0]
    h = a + pltpu.roll(bm, n - 1, 0) + pe_ref[0, 0:1]
    o_ref[0, 0] = _dot(jax.nn.gelu(h).astype(BF16), w2_ref[0])


def cmp_mlp(xsub, w1, w2, pe_h):
    _, G, n_sub, half = xsub.shape
    return pl.pallas_call(
        _cmp_mlp_kernel,
        grid=(2, G),
        in_specs=[
            pl.BlockSpec((1, 1, n_sub, half), lambda k, g: (k, g, 0, 0)),
            pl.BlockSpec((1, 2 * half, CMP_HID), lambda k, g: (k, 0, 0)),
            pl.BlockSpec((1, CMP_HID, HEAD_DIM), lambda k, g: (k, 0, 0)),
            pl.BlockSpec((1, 8, CMP_HID), lambda k, g: (k, 0, 0)),
        ],
        out_specs=pl.BlockSpec((1, 1, n_sub, HEAD_DIM), lambda k, g: (k, g, 0, 0)),
        out_shape=jax.ShapeDtypeStruct((2, G, n_sub, HEAD_DIM), F32),
        compiler_params=_cp(("parallel", "parallel")),
        name="cmp_mlp",
    )(xsub, w1, w2, pe_h)


LOG2E = 1.4426950408889634


def _masked_softmax_rows(s, mask, exp_fn=jnp.exp):
    s = jnp.where(mask, s, -jnp.inf)
    m = jnp.max(s, axis=-1, keepdims=True)
    m = jnp.where(m == -jnp.inf, 0.0, m)
    p = exp_fn(s - m)
    return p / jnp.maximum(jnp.sum(p, axis=-1, keepdims=True), 1e-30)


def _split3(p):
    hi = p.astype(BF16)
    r = p - hi.astype(F32)
    mid = r.astype(BF16)
    lo = (r - mid.astype(F32)).astype(BF16)
    return hi, mid, lo


def _nsa_prompt_kernel(q_ref, gt_ref, kc_ref, vc_ref, ks_ref, vs_ref, kw_ref, vw_ref, o_ref,
                       kaug_ref, vsb_ref, kwb_ref, vwb_ref, score_ref, *, tq, tk, seq_len):
    qi = pl.program_id(2)
    n_blk = seq_len // SLC_BLOCK
    scale = HEAD_DIM ** -0.5 * LOG2E

    @pl.when(qi == 0)
    def _():
        r = lax.broadcasted_iota(jnp.int32, (seq_len, HEAD_DIM), 0) // SLC_BLOCK
        col = lax.broadcasted_iota(jnp.int32, (seq_len, HEAD_DIM), 1)
        kaug_ref[:, 0:HEAD_DIM] = ks_ref[0].astype(BF16)
        kaug_ref[:, HEAD_DIM:2 * HEAD_DIM] = jnp.where(r == col, 1.0, 0.0).astype(BF16)
        vsb_ref[...] = vs_ref[0].astype(BF16)
        kwb_ref[...] = kw_ref[0].astype(BF16)
        vwb_ref[...] = vw_ref[0].astype(BF16)

    q0 = qi * tq
    qs = q_ref[0] * scale
    qf = jnp.concatenate([qs[:, g * HEAD_DIM:(g + 1) * HEAD_DIM] for g in range(GROUP)], axis=0)
    qb = qf.astype(BF16)
    rows = GROUP * tq

    nc = kc_ref.shape[1]
    kc = kc_ref[0].astype(BF16)
    vc = vc_ref[0].astype(BF16)
    qpos_c = q0 + lax.broadcasted_iota(jnp.int32, (rows, nc), 0) % tq
    cend = CMP_STRIDE * lax.broadcasted_iota(jnp.int32, (rows, nc), 1) + (2 * CMP_STRIDE - 1)
    p_c = _masked_softmax_rows(_dot_nt(qb, kc), cend <= qpos_c, jnp.exp2)
    o_cmp = _dot(p_c.astype(BF16), vc)

    psum = p_c[0:tq]
    for g in range(1, GROUP):
        psum = psum + p_c[g * tq:(g + 1) * tq]
    ci = lax.broadcasted_iota(jnp.int32, (nc, HEAD_DIM), 0)
    ji = lax.broadcasted_iota(jnp.int32, (nc, HEAD_DIM), 1)
    ratio = SLC_BLOCK // CMP_STRIDE
    member = ((ci >= ratio * ji - 1) & (ci <= ratio * ji + ratio - 1)).astype(BF16)
    imp = sum(_dot(part, member) for part in _split3(psum))
    imp_t = imp.T
    blk = lax.broadcasted_iota(jnp.int32, (HEAD_DIM, tq), 0)
    cur = (q0 + lax.broadcasted_iota(jnp.int32, (HEAD_DIM, tq), 1)) // SLC_BLOCK
    valid = (blk <= cur) & (blk < n_blk)
    forced = valid & ((blk == 0) | (blk == cur) | (blk == cur - 1))
    score = jnp.where(forced, SEL_BONUS, jnp.where(valid, imp_t, -jnp.inf))
    score = score[0:n_blk]
    score_ref[...] = score
    blk = lax.broadcasted_iota(jnp.int32, (n_blk, tq), 0)
    per_tile = tq // SLC_BLOCK

    def rank_body(it, rank):
        for u in range(per_tile):
            i = it * per_tile + u
            si = score_ref[pl.ds(i, 1), :]
            beats = (si > score) | ((si == score) & (blk > i))
            rank = rank + beats.astype(jnp.int32)
        return rank

    rank = lax.fori_loop(0, qi + 1, rank_body, jnp.zeros((n_blk, tq), jnp.int32))
    bias_t = jnp.where(rank < N_SELECT, 0.0, NEG_BIG)
    if n_blk < HEAD_DIM:
        bias_t = jnp.concatenate([bias_t, jnp.zeros((HEAD_DIM - n_blk, tq), F32)], axis=0)
    bias = bias_t.T.astype(BF16)
    qaug = jnp.concatenate([qb, jnp.concatenate([bias] * GROUP, axis=0)], axis=1)

    qpos_k = q0 + lax.broadcasted_iota(jnp.int32, (rows, tk), 0) % tq
    kcol = lax.broadcasted_iota(jnp.int32, (rows, tk), 1)

    def slc_body(kt, carry, causal):
        m, l, acc = carry
        ksl = pl.ds(pl.multiple_of(kt * tk, tk), tk)
        s = _dot_nt(qaug, kaug_ref[ksl, :])
        if causal:
            s = jnp.where(kt * tk + kcol <= qpos_k, s, -jnp.inf)
        m_new = jnp.maximum(m, jnp.max(s, axis=-1, keepdims=True))
        alpha = jnp.exp2(m - m_new)
        p = jnp.exp2(s - m_new)
        l = alpha * l + jnp.sum(p, axis=-1, keepdims=True)
        acc = alpha * acc + _dot(p.astype(BF16), vsb_ref[ksl, :])
        return m_new, l, acc

    init = (jnp.full((rows, 1), -jnp.inf, F32), jnp.zeros((rows, 1), F32), jnp.zeros((rows, HEAD_DIM), F32))
    n_full = q0 // tk
    n_kt = (q0 + tq - 1) // tk + 1
    carry = lax.fori_loop(0, n_full, functools.partial(slc_body, causal=False), init)
    _, l_s, acc_s = lax.fori_loop(n_full, n_kt, functools.partial(slc_body, causal=True), carry)
    o_slc = acc_s / jnp.maximum(l_s, 1e-30)

    qpos_w = q0 + lax.broadcasted_iota(jnp.int32, (rows, tq), 0) % tq
    wcol = lax.broadcasted_iota(jnp.int32, (rows, tq), 1)

    def win_body(it, carry):
        m, l, acc = carry
        k0 = (qi - it) * tq
        ksl = pl.ds(pl.multiple_of(k0, tq), tq)
        s = _dot_nt(qb, kwb_ref[ksl, :])
        d = qpos_w - (k0 + wcol)
        s = jnp.where((d >= 0) & (d <= WINDOW), s, -jnp.inf)
        m_new = jnp.maximum(m, jnp.max(s, axis=-1, keepdims=True))
        alpha = jnp.exp2(m - m_new)
        p = jnp.exp2(s - m_new)
        l = alpha * l + jnp.sum(p, axis=-1, keepdims=True)
        acc = alpha * acc + _dot(p.astype(BF16), vwb_ref[ksl, :])
        return m_new, l, acc

    n_wt = jnp.minimum(qi, (WINDOW + tq - 1) // tq) + 1
    _, l_w, acc_w = lax.fori_loop(0, n_wt, win_body, init)
    o_win = acc_w / jnp.maximum(l_w, 1e-30)

    gates = jax.nn.sigmoid(gt_ref[0])
    for g in range(GROUP):
        rs = slice(g * tq, (g + 1) * tq)
        c0 = g * N_BRANCH
        o = (gates[:, c0:c0 + 1] * o_cmp[rs] + gates[:, c0 + 1:c0 + 2] * o_slc[rs]
             + gates[:, c0 + 2:c0 + 3] * o_win[rs])
        o_ref[0, :, g * HEAD_DIM:(g + 1) * HEAD_DIM] = o.astype(o_ref.dtype)


def nsa_prompt(zq3, gates3, k_cmp, v_cmp, rows3):
    B, L, _ = zq3.shape
    tq = _pick(L, (256, 128))
    tk = _pick(L, (512, 256, 128))
    n_sub = k_cmp.shape[1]
    qw = GROUP * HEAD_DIM
    rspec = lambda kind: pl.BlockSpec((1, L, HEAD_DIM), lambda b, n, i: (b, 0, kind * N_KV + n))
    cspec = pl.BlockSpec((1, n_sub, HEAD_DIM), lambda b, n, i: (b * N_KV + n, 0, 0))
    return pl.pallas_call(
        functools.partial(_nsa_prompt_kernel, tq=tq, tk=tk, seq_len=L),
        grid=(B, N_KV, L // tq),
        in_specs=[
            pl.BlockSpec((1, tq, qw), lambda b, n, i: (b, i, n)),
            pl.BlockSpec((1, tq, HEAD_DIM), lambda b, n, i: (b, i, n)),
            cspec, cspec, rspec(2), rspec(3), rspec(4), rspec(5),
        ],
        out_specs=pl.BlockSpec((1, tq, qw), lambda b, n, i: (b, i, n)),
        out_shape=jax.ShapeDtypeStruct((B, L, MIX_DIM), BF16),
        scratch_shapes=[
            pltpu.VMEM((L, 2 * HEAD_DIM), BF16),
            pltpu.VMEM((L, HEAD_DIM), BF16),
            pltpu.VMEM((L, HEAD_DIM), BF16),
            pltpu.VMEM((L, HEAD_DIM), BF16),
            pltpu.VMEM((L // SLC_BLOCK, tq), F32),
        ],
        compiler_params=_cp(("parallel", "parallel", "arbitrary")),
        name="nsa_prompt",
    )(zq3, gates3, k_cmp, v_cmp, rows3, rows3, rows3, rows3)


def _dec_cmp_partial_kernel(pt_ref, *refs, n_pg, n_groups):
    del pt_ref
    pg_refs = refs[:n_pg]
    new_ref, w_ref, o_ref, lhs_ref = refs[n_pg:]
    g = pl.program_id(1)
    sub_pg = PAGE_SIZE // CMP_STRIDE

    @pl.when(g < n_groups)
    def _():
        for r in range(n_pg):
            for p in range(CMP_STRIDE):
                x = pg_refs[r][0, pl.ds(p, sub_pg, stride=CMP_STRIDE), :, :]
                lhs_ref[r * 8 * sub_pg:(r + 1) * 8 * sub_pg, p * HEAD_DIM:(p + 1) * HEAD_DIM] = (
                    x.reshape(8 * sub_pg, HEAD_DIM).astype(BF16))

    @pl.when(g == n_groups)
    def _():
        lhs_ref[...] = jnp.zeros_like(lhs_ref)
        lhs_ref[0:16, 0:HEAD_DIM] = jnp.concatenate(
            [new_ref[0], jnp.zeros((8, HEAD_DIM), F32)], axis=0).astype(BF16)

    y = _dot(lhs_ref[...], w_ref[...])
    is_k = lax.broadcasted_iota(jnp.int32, (y.shape[0], CMP_HID), 0) % 8 < N_KV
    a = jnp.where(is_k, y[:, 0:CMP_HID], y[:, 2 * CMP_HID:3 * CMP_HID])
    bm = jnp.where(is_k, y[:, CMP_HID:2 * CMP_HID], y[:, 3 * CMP_HID:4 * CMP_HID])
    o_ref[0] = jnp.concatenate([a, bm], axis=1).reshape(n_pg * sub_pg, 8, 2 * CMP_HID)


def dec_cmp_partial(cache4, page_table, new_cmp, w1cat):
    B, n_pages = page_table.shape
    n_pg = _pick(n_pages, (8, 4, 2, 1))
    n_groups = n_pages // n_pg
    sub_pg = PAGE_SIZE // CMP_STRIDE

    def page_spec(r):
        def imap(b, g, pt):
            return (pt[b * n_pages + jnp.minimum(g * n_pg + r, n_pages - 1)], 0, 0, 0)
        return pl.BlockSpec((1, PAGE_SIZE, 8, HEAD_DIM), imap)

    grid_spec = pltpu.PrefetchScalarGridSpec(
        num_scalar_prefetch=1,
        grid=(B, n_groups + 1),
        in_specs=[page_spec(r) for r in range(n_pg)] + [
            pl.BlockSpec((1, 8, HEAD_DIM), lambda b, g, pt: (b, 0, 0)),
            pl.BlockSpec((CMP_STRIDE * HEAD_DIM, 4 * CMP_HID), lambda b, g, pt: (0, 0)),
        ],
        out_specs=pl.BlockSpec((1, n_pg * sub_pg, 8, 2 * CMP_HID), lambda b, g, pt: (b, g, 0, 0)),
        scratch_shapes=[pltpu.VMEM((n_pg * sub_pg * 8, CMP_STRIDE * HEAD_DIM), BF16)],
    )
    return pl.pallas_call(
        functools.partial(_dec_cmp_partial_kernel, n_pg=n_pg, n_groups=n_groups),
        grid_spec=grid_spec,
        out_shape=jax.ShapeDtypeStruct((B, (n_groups + 1) * n_pg * sub_pg, 8, 2 * CMP_HID), F32),
        compiler_params=_cp(("parallel", "arbitrary")),
        name="dec_cmp_partial",
    )(page_table.reshape(-1), *([cache4] * n_pg), new_cmp, w1cat)


def _dec_cmp_final_kernel(ab_ref, nx_ref, pe_ref, w2_ref, o_ref):
    ch = ab_ref.shape[1]
    a = ab_ref[0, :, :, 0:CMP_HID]
    bn = jnp.concatenate([ab_ref[0, 1:ch, :, CMP_HID:2 * CMP_HID], nx_ref[0, :, :, CMP_HID:2 * CMP_HID]], axis=0)
    h = a + bn + pe_ref[...][None]
    y = _dot(jax.nn.gelu(h).reshape(ch * 8, CMP_HID).astype(BF16), w2_ref[...])
    is_k = lax.broadcasted_iota(jnp.int32, (ch * 8, HEAD_DIM), 0) % 8 < N_KV
    o_ref[0] = jnp.where(is_k, y[:, 0:HEAD_DIM], y[:, HEAD_DIM:2 * HEAD_DIM]).reshape(ch, 8, HEAD_DIM)


def dec_cmp_final(ab, pe_sel, w2cat):
    B, S = ab.shape[:2]
    ch = _pick(S, (64, 32, 16, 8))
    return pl.pallas_call(
        _dec_cmp_final_kernel,
        grid=(B, S // ch),
        in_specs=[
            pl.BlockSpec((1, ch, 8, 2 * CMP_HID), lambda b, c: (b, c, 0, 0)),
            pl.BlockSpec((1, 1, 8, 2 * CMP_HID), lambda b, c: (b, jnp.minimum((c + 1) * ch, S - 1), 0, 0)),
            pl.BlockSpec((8, CMP_HID), lambda b, c: (0, 0)),
            pl.BlockSpec((CMP_HID, 2 * HEAD_DIM), lambda b, c: (0, 0)),
        ],
        out_specs=pl.BlockSpec((1, ch, 8, HEAD_DIM), lambda b, c: (b, c, 0, 0)),
        out_shape=jax.ShapeDtypeStruct((B, S, 8, HEAD_DIM), F32),
        compiler_params=_cp(("parallel", "parallel")),
        name="dec_cmp_final",
    )(ab, ab, pe_sel, w2cat)


def _dec_select_kernel(q_ref, kv_ref, mem_ref, ocmp_ref, idx_ref, *, q_pos, nc, n_blk, n_sel):
    R = kv_ref.shape[1]
    JB = mem_ref.shape[1]
    scale = HEAD_DIM ** -0.5
    kv = kv_ref[0]
    kb = kv.astype(BF16)
    qb = (q_ref[0] * scale).astype(BF16)
    s = _dot_nt(qb, kb)
    r_i = lax.broadcasted_iota(jnp.int32, (16, R), 1)
    h_i = lax.broadcasted_iota(jnp.int32, (16, R), 0)
    c_i = r_i >> 3
    slot = r_i & 7
    ok = ((CMP_STRIDE * c_i + (2 * CMP_STRIDE - 1) <= q_pos) & (c_i < nc) & (slot < N_KV)
          & (h_i >= GROUP * slot) & (h_i < GROUP * slot + GROUP))
    p = _masked_softmax_rows(s, ok)
    vb = pltpu.roll(kv, R - N_KV, 0).astype(BF16)
    ocmp_ref[0] = _dot(p.astype(BF16), vb)

    n_r = lax.broadcasted_iota(jnp.int32, (8, 16), 0)
    h_r = lax.broadcasted_iota(jnp.int32, (8, 16), 1)
    gsel = ((h_r >= GROUP * n_r) & (h_r < GROUP * n_r + GROUP)).astype(BF16)
    psum = sum(_dot(gsel, part) for part in _split3(p))
    imp = sum(_dot(part, mem_ref[...]) for part in _split3(psum))
    j_i = lax.broadcasted_iota(jnp.int32, (8, JB), 1)
    cur = q_pos // SLC_BLOCK
    valid = (j_i <= cur) & (j_i < n_blk)
    forced = valid & ((j_i == 0) | (j_i == cur) | (j_i == cur - 1))
    score = jnp.where(forced, SEL_BONUS, jnp.where(valid, imp, -jnp.inf))
    j_f = j_i.astype(F32)
    taken = j_i >= n_blk
    lane_o = lax.broadcasted_iota(jnp.int32, (8, HEAD_DIM), 1)
    out = jnp.zeros((8, HEAD_DIM), F32)
    for r in range(n_sel):
        m = jnp.max(jnp.where(taken, -jnp.inf, score), axis=-1, keepdims=True)
        cand = jnp.logical_not(taken) & (score == m)
        idx = jnp.min(jnp.where(cand, j_f, float(JB)), axis=-1, keepdims=True)
        out = jnp.where(lane_o == r, idx, out)
        taken = taken | (j_f == idx)
    idx_ref[0] = out.astype(jnp.int32)


def dec_select(q16, cmp_tok, member, *, q_pos, nc, n_blk, n_sel):
    B, R = cmp_tok.shape[:2]
    JB = member.shape[1]
    return pl.pallas_call(
        functools.partial(_dec_select_kernel, q_pos=q_pos, nc=nc, n_blk=n_blk, n_sel=n_sel),
        grid=(B,),
        in_specs=[
            pl.BlockSpec((1, 16, HEAD_DIM), lambda b: (b, 0, 0)),
            pl.BlockSpec((1, R, HEAD_DIM), lambda b: (b, 0, 0)),
            pl.BlockSpec((R, JB), lambda b: (0, 0)),
        ],
        out_specs=[
            pl.BlockSpec((1, 16, HEAD_DIM), lambda b: (b, 0, 0)),
            pl.BlockSpec((1, 8, HEAD_DIM), lambda b: (b, 0, 0)),
        ],
        out_shape=[
            jax.ShapeDtypeStruct((B, 16, HEAD_DIM), F32),
            jax.ShapeDtypeStruct((B, 8, HEAD_DIM), jnp.int32),
        ],
        compiler_params=_cp(("parallel",)),
        name="dec_select",
    )(q16, cmp_tok, member)


def _dec_attend_kernel(idx_ref, pt_ref, q_ref, gl_ref, ocmp_ref, nslc_ref, win_ref, nwin_ref, *rest,
                       q_pos, past_len, n_sel):
    del pt_ref
    blk_refs = rest[:n_sel]
    o_ref = rest[n_sel]
    b = pl.program_id(0)
    n = pl.program_id(1)
    base = (b * N_KV + n) * n_sel
    scale = HEAD_DIM ** -0.5
    past_blocks = past_len // SLC_BLOCK
    rb = SLC_BLOCK * 8
    rw = win_ref.shape[1]
    w_buf = rw // 8
    qf = q_ref[0, 0] * scale
    qb = qf.astype(BF16)
    js = [idx_ref[base + s] for s in range(n_sel)]
    has_new = js[0] == past_blocks
    for s in range(1, n_sel):
        has_new = has_new | (js[s] == past_blocks)
    new_ok = has_new & (past_len <= q_pos)

    def head_rows(x):
        slot = lax.broadcasted_iota(jnp.int32, (8, HEAD_DIM), 0)
        k = jnp.sum(jnp.where(slot == n, x, 0.0), axis=0, keepdims=True)
        v = jnp.sum(jnp.where(slot == n + N_KV, x, 0.0), axis=0, keepdims=True)
        return k, v

    def finish(m):
        return jnp.where(m == -jnp.inf, 0.0, m)

    r_s = lax.broadcasted_iota(jnp.int32, (8, rb), 1)
    t_s = r_s >> 3
    mine_s = (r_s & 7) == n

    def slc_scores(s):
        blk = blk_refs[s][0, 0].reshape(rb, HEAD_DIM)
        sc = _dot_nt(qb, blk.astype(BF16))
        ok = mine_s & (js[s] * SLC_BLOCK + t_s <= q_pos) & (js[s] < past_blocks)
        return jnp.where(ok, sc, -jnp.inf), blk

    k_new, v_new = head_rows(nslc_ref[0])
    s_new = jnp.where(new_ok, jnp.sum(qf * k_new, axis=-1, keepdims=True), -jnp.inf)
    m = s_new
    for s in range(n_sel):
        m = jnp.maximum(m, jnp.max(slc_scores(s)[0], axis=-1, keepdims=True))
    m = finish(m)
    p_new = jnp.exp(s_new - m)
    l = p_new
    acc = p_new * v_new
    for s in range(n_sel):
        sc, blk = slc_scores(s)
        p = jnp.exp(sc - m)
        l = l + jnp.sum(p, axis=-1, keepdims=True)
        acc = acc + _dot(p.astype(BF16), pltpu.roll(blk, rb - N_KV, 0).astype(BF16))
    o_slc = acc / jnp.maximum(l, 1e-30)

    r_w = lax.broadcasted_iota(jnp.int32, (8, rw), 1)
    w_pos = past_len - w_buf + (r_w >> 3)
    w_ok = ((r_w & 7) == n) & (q_pos - w_pos >= 0) & (q_pos - w_pos <= WINDOW) & (w_pos >= 0)
    win = win_ref[0]
    s_w = jnp.where(w_ok, _dot_nt(qb, win.astype(BF16)), -jnp.inf)
    k_nw, v_nw = head_rows(nwin_ref[0])
    nw_ok = (q_pos - past_len >= 0) and (q_pos - past_len <= WINDOW)
    s_nw = jnp.sum(qf * k_nw, axis=-1, keepdims=True) if nw_ok else jnp.full((8, 1), -jnp.inf, F32)
    m = finish(jnp.maximum(s_nw, jnp.max(s_w, axis=-1, keepdims=True)))
    p_w = jnp.exp(s_w - m)
    p_nw = jnp.exp(s_nw - m)
    l = p_nw + jnp.sum(p_w, axis=-1, keepdims=True)
    acc = p_nw * v_nw + _dot(p_w.astype(BF16), pltpu.roll(win, rw - N_KV, 0).astype(BF16))
    o_win = acc / jnp.maximum(l, 1e-30)

    gates = jax.nn.sigmoid(gl_ref[0, 0])
    o_ref[0, 0] = gates[:, 0:1] * ocmp_ref[0, 0] + gates[:, 1:2] * o_slc + gates[:, 2:3] * o_win


def dec_attend(idx_flat, page_table, q8, gate8, ocmp8, new_slc, win2, new_win, cache5, *, q_pos, past_len, n_sel):
    B, n_pages = page_table.shape
    rw = win2.shape[1]
    past_blocks = past_len // SLC_BLOCK
    per_page = PAGE_SIZE // SLC_BLOCK

    def blk_spec(s):
        def imap(b, n, idx, pt):
            j = jnp.minimum(idx[(b * N_KV + n) * n_sel + s], past_blocks - 1)
            return (pt[b * n_pages + j // per_page], j % per_page, 0, 1, 0)
        return pl.BlockSpec((1, 1, SLC_BLOCK, 8, HEAD_DIM), imap)

    head_spec = pl.BlockSpec((1, 1, 8, HEAD_DIM), lambda b, n, idx, pt: (b, n, 0, 0))
    tok_spec = pl.BlockSpec((1, 8, HEAD_DIM), lambda b, n, idx, pt: (b, 0, 0))
    grid_spec = pltpu.PrefetchScalarGridSpec(
        num_scalar_prefetch=2,
        grid=(B, N_KV),
        in_specs=[head_spec, head_spec, head_spec, tok_spec,
                  pl.BlockSpec((1, rw, HEAD_DIM), lambda b, n, idx, pt: (b, 0, 0)), tok_spec]
        + [blk_spec(s) for s in range(n_sel)],
        out_specs=head_spec,
    )
    return pl.pallas_call(
        functools.partial(_dec_attend_kernel, q_pos=q_pos, past_len=past_len, n_sel=n_sel),
        grid_spec=grid_spec,
        out_shape=jax.ShapeDtypeStruct((B, N_KV, 8, HEAD_DIM), F32),
        compiler_params=_cp(("parallel", "arbitrary")),
        name="dec_attend",
    )(idx_flat, page_table.reshape(-1), q8, gate8, ocmp8, new_slc, win2, new_win, *([cache5] * n_sel))


def _pad_to(a, axis, size):
    pad = [(0, 0)] * a.ndim
    pad[axis] = (0, size - a.shape[axis])
    return jnp.pad(a, pad)


def _prep_weights(w_in_a, w_in_b, b_gate, w_kv, cmp_pe, cmp_w1, cmp_w2, w_mem_kv, w_out, w_up, conv_w, conv_b,
                  w_down, hgrn_lb):
    depth = w_out.shape[0]
    n_b = w_in_b.shape[0]
    n_gate = N_MIX_HEADS * N_BRANCH
    per_kv = GROUP * N_BRANCH
    half = CMP_STRIDE * HEAD_DIM
    w1 = cmp_w1.astype(BF16)
    lb = jnp.cumsum(jax.nn.softmax(hgrn_lb.astype(F32), axis=0), axis=0)
    lb = lb - lb[0]
    layers = lambda f, n=depth: [f(l) for l in range(n)]
    gate_w = lambda j: w_in_b[j][:, MIX_DIM:MIX_DIM + n_gate].reshape(D_MODEL, N_KV, per_kv)
    return dict(
        in_a=layers(lambda l: w_in_a[l].astype(BF16), w_in_a.shape[0]),
        in_b_qm=layers(lambda j: jnp.concatenate([w_in_b[j][:, :MIX_DIM], w_in_b[j][:, MIX_DIM + n_gate:]],
                                                 axis=-1).astype(BF16), n_b),
        in_b_gate=layers(lambda j: _pad_to(gate_w(j), 2, HEAD_DIM).reshape(D_MODEL, N_KV * HEAD_DIM).astype(BF16), n_b),
        b_gate=layers(lambda j: _pad_to(b_gate[j].reshape(N_KV, per_kv), 1, HEAD_DIM).reshape(N_KV * HEAD_DIM), n_b),
        kv=w_kv.astype(BF16),
        mem_kv=layers(lambda l: w_mem_kv[l].astype(BF16)),
        out_mix=layers(lambda l: w_out[l][:MIX_DIM].astype(BF16)),
        out_mem=layers(lambda l: w_out[l][MIX_DIM:].astype(BF16)),
        up_a=layers(lambda l: _pad_to(w_up[l][:, :D_FF], 1, D_FF_PAD).astype(BF16)),
        up_u=layers(lambda l: _pad_to(w_up[l][:, D_FF:], 1, D_FF_PAD).astype(BF16)),
        conv_w=layers(lambda l: _pad_to(_pad_to(conv_w[l], 1, D_FF_PAD), 0, 8)),
        conv_b=layers(lambda l: _pad_to(conv_b[l], 0, D_FF_PAD).reshape(1, D_FF_PAD)),
        down=layers(lambda l: _pad_to(w_down[l], 0, D_FF_PAD).astype(BF16)),
        cmp_w1=w1,
        cmp_w1cat=jnp.concatenate([w1[0, :half], w1[0, half:], w1[1, :half], w1[1, half:]], axis=1),
        cmp_w2=cmp_w2.astype(BF16),
        cmp_w2cat=jnp.concatenate([cmp_w2[0], cmp_w2[1]], axis=1).astype(BF16),
        cmp_pe=cmp_pe.reshape(2, 1, 2 * half),
        log_lb=jnp.log(lb),
        log1m_lb=jnp.log1p(-lb),
    )


def kernel(x_prompt, x_sample, mem_prompt, cache_nsa_kv, page_table, cache_win_kv, state_hgrn, state_conv,
           cache_mem_kv, norm_mix, norm_ffn, norm_mem, norm_kv, norm_final, w_in_a, hgrn_lb, hgrn_onorm,
           w_in_b, b_gate, w_kv, cmp_pe, cmp_w1, cmp_w2, w_mem_kv, w_out, w_up, conv_w, conv_b, w_down):
    B, L, _ = x_prompt.shape
    Bs = x_sample.shape[0]
    depth = w_out.shape[0]
    n_a = w_in_a.shape[0]
    n_mem = mem_prompt.shape[1]
    n_pool, page, _, _, _ = cache_nsa_kv.shape
    n_pages = page_table.shape[1]
    past_len = n_pages * page
    w_buf = cache_win_kv.shape[1]
    assert x_sample.shape[1] == 1 and page == PAGE_SIZE and L % 128 == 0 and L // SLC_BLOCK <= HEAD_DIM
    W = _prep_weights(w_in_a, w_in_b, b_gate, w_kv, cmp_pe, cmp_w1, cmp_w2, w_mem_kv, w_out, w_up, conv_w,
                      conv_b, w_down, hgrn_lb)
    pe_h = pe_proj(W["cmp_pe"], W["cmp_w1"])
    onorm = hgrn_onorm.reshape(n_a, 1, MIX_DIM)
    log_lb = W["log_lb"].reshape(n_a, 1, MIX_DIM)
    log1m_lb = W["log1m_lb"].reshape(n_a, 1, MIX_DIM)

    def ffn(x, l, **kw):
        return conv_ffn(x, norm_ffn[l], W["up_a"][l], W["up_u"][l], W["conv_w"][l], W["conv_b"][l], W["down"][l], **kw)

    M = B * L
    mem_flat = mem_prompt.reshape(B * n_mem, D_MODEL)
    mem_kv_p = [rms_matmul(mem_flat, norm_mem[l], W["mem_kv"][l]).reshape(B, n_mem, 2 * MEM_DIM) for l in range(depth)]
    x = x_prompt.reshape(M, D_MODEL)
    hgrn_p, conv_p = [], []
    tm_ffn = _pick(L, (512, 256, 128))
    for l in range(depth):
        if l < n_a:
            z3 = rms_matmul(x, norm_mix[l], W["in_a"][l]).reshape(B, L, -1)
            s0 = jnp.zeros((B, N_MIX_HEADS, HEAD_DIM, HEAD_DIM), F32)
            o_mix, s_new = hgrn_prompt(z3, log_lb[l], log1m_lb[l], onorm[l], s0)
            hgrn_p.append(s_new)
            o_mem = mem_attention(z3, 4 * MIX_DIM // MEM_DIM, mem_kv_p[l])
        else:
            j = l - n_a
            if j == 0:
                rows_p = rms_matmul(x, norm_kv, W["kv"])
                rows3 = rows_p.reshape(B, L, -1)
                n_sub = L // CMP_STRIDE
                xsub = rows_p[:, :2 * N_KV * HEAD_DIM].reshape(B, n_sub, CMP_STRIDE, 2, N_KV, HEAD_DIM)
                xsub = xsub.transpose(3, 0, 4, 1, 2, 5).reshape(2, B * N_KV, n_sub, CMP_STRIDE * HEAD_DIM)
                cmp_p = cmp_mlp(xsub, W["cmp_w1"], W["cmp_w2"], pe_h)
            zq3 = rms_matmul(x, norm_mix[l], W["in_b_qm"][j]).reshape(B, L, -1)
            gt3 = rms_matmul(x, norm_mix[l], W["in_b_gate"][j], W["b_gate"][j]).reshape(B, L, -1)
            o_mix = nsa_prompt(zq3, gt3, cmp_p[0], cmp_p[1], rows3)
            o_mem = mem_attention(zq3, MIX_DIM // MEM_DIM, mem_kv_p[l])
        x = out_proj(o_mix.reshape(M, MIX_DIM), o_mem.reshape(M, MEM_DIM), W["out_mix"][l], W["out_mem"][l], x)
        x, a_tail = ffn(x, l, seq_len=L)
        conv_p.append(a_tail.reshape(B, L // tm_ffn, 8, D_FF_PAD)[:, -1, 8 - (CONV_W - 1):, :D_FF])
    y_prompt = rmsnorm_rows(x, norm_final).reshape(B, L, D_MODEL)
    n_row4 = 4 * N_KV * HEAD_DIM
    w_keep = min(WINDOW, L)
    nsa_rows_prompt = rows3[:, :, :n_row4].reshape(B, L, 4, N_KV, HEAD_DIM)
    win_prompt = rows3[:, L - w_keep:, n_row4:].reshape(B, w_keep, 2, N_KV, HEAD_DIM)
    mem_kv_prompt = jnp.stack(mem_kv_p).reshape(depth, B, n_mem, 2, N_MEM_HEADS, HEAD_DIM)

    Ms = 16
    pad_rows = lambda a: _pad_to(a, 0, Ms)
    xs = pad_rows(x_sample.reshape(Bs, D_MODEL))
    hgrn_s, conv_s = [], []
    q_pos = past_len
    t_pad = -(-(past_len + 1) // SLC_BLOCK) * SLC_BLOCK
    n_blk = t_pad // SLC_BLOCK
    nc = t_pad // CMP_STRIDE - 1
    n_sel = min(N_SELECT, n_blk)
    for l in range(depth):
        if l < n_a:
            z = rms_matmul(xs, norm_mix[l], W["in_a"][l])
            z3 = z[:Bs].reshape(Bs, 1, -1)
            o_mix, s_new = hgrn_step(z3, log_lb[l], log1m_lb[l], onorm[l], state_hgrn[l])
            hgrn_s.append(s_new)
            o_mix = o_mix.reshape(Bs, MIX_DIM)
            o_mem = mem_attention(z3, 4 * MIX_DIM // MEM_DIM, cache_mem_kv[l].reshape(Bs, n_mem, 2 * MEM_DIM))
        else:
            j = l - n_a
            if j == 0:
                rows_s = rms_matmul(xs, norm_kv, W["kv"])[:Bs]
                new_cmp = rows_s[:, 0:1024].reshape(Bs, 8, HEAD_DIM)
                new_slc = rows_s[:, 1024:2048].reshape(Bs, 8, HEAD_DIM)
                new_win = rows_s[:, 2048:3072].reshape(Bs, 8, HEAD_DIM)
                cache4 = cache_nsa_kv.reshape(n_pool, PAGE_SIZE, 4 * N_KV, HEAD_DIM)
                cache5 = cache_nsa_kv.reshape(n_pool, PAGE_SIZE // SLC_BLOCK, SLC_BLOCK, 4 * N_KV, HEAD_DIM)
                win2 = cache_win_kv.reshape(Bs, w_buf * 2 * N_KV, HEAD_DIM)
                ab = dec_cmp_partial(cache4, page_table, new_cmp, W["cmp_w1cat"])
                pe_sel = jnp.concatenate([jnp.broadcast_to(pe_h[0, 0:1], (N_KV, CMP_HID)),
                                          jnp.broadcast_to(pe_h[1, 0:1], (N_KV, CMP_HID))], axis=0)
                cmp_s = dec_cmp_final(ab, pe_sel, W["cmp_w2cat"])
                assert cmp_s.shape[1] > nc
                cmp_s = cmp_s.reshape(Bs, -1, HEAD_DIM)
                ratio = SLC_BLOCK // CMP_STRIDE
                c_of_row = jnp.arange(cmp_s.shape[1])[:, None] // 8
                j_of_col = jnp.arange(-(-n_blk // HEAD_DIM) * HEAD_DIM)[None, :]
                member = ((c_of_row >= ratio * j_of_col - 1) & (c_of_row <= ratio * j_of_col + ratio - 1)).astype(BF16)
            zq = rms_matmul(xs, norm_mix[l], W["in_b_qm"][j])
            gt = rms_matmul(xs, norm_mix[l], W["in_b_gate"][j], W["b_gate"][j])[:Bs]
            z3 = zq[:Bs].reshape(Bs, 1, -1)
            q12 = zq[:Bs, :MIX_DIM].reshape(Bs, N_MIX_HEADS, HEAD_DIM)
            o_cmp, idx = dec_select(_pad_to(q12, 1, 16), cmp_s, member, q_pos=q_pos, nc=nc, n_blk=n_blk, n_sel=n_sel)
            idx_flat = idx[:, :N_KV, :n_sel].reshape(-1)
            per_head = lambda a: _pad_to(a.reshape(Bs, N_KV, GROUP, -1), 2, 8)
            gate9 = gt.reshape(Bs, N_KV, HEAD_DIM)[:, :, :GROUP * N_BRANCH]
            o_all = dec_attend(idx_flat, page_table, per_head(q12), _pad_to(per_head(gate9), 3, HEAD_DIM),
                               per_head(o_cmp[:, :N_MIX_HEADS]), new_slc, win2, new_win, cache5,
                               q_pos=q_pos, past_len=past_len, n_sel=n_sel)
            o_mix = o_all[:, :, :GROUP].reshape(Bs, MIX_DIM).astype(BF16)
            o_mem = mem_attention(z3, MIX_DIM // MEM_DIM, cache_mem_kv[l].reshape(Bs, n_mem, 2 * MEM_DIM))
        xs = out_proj(pad_rows(o_mix), pad_rows(o_mem.reshape(Bs, MEM_DIM)), W["out_mix"][l], W["out_mem"][l], xs)
        prev = (pad_rows(_pad_to(state_conv[l][:, 1], 1, D_FF_PAD)), pad_rows(_pad_to(state_conv[l][:, 0], 1, D_FF_PAD)))
        xs, a_new = ffn(xs, l, prev=prev)
        conv_s.append(jnp.stack([state_conv[l][:, 1], a_new[0, :Bs, :D_FF]], axis=1))
    y_sample = rmsnorm_rows(xs, norm_final)[:Bs].reshape(Bs, 1, D_MODEL)
    nsa_rows_sample = rows_s[:, :n_row4].reshape(Bs, 1, 4, N_KV, HEAD_DIM)
    win_new = rows_s[:, n_row4:].reshape(Bs, 1, 2, N_KV, HEAD_DIM).astype(cache_win_kv.dtype)
    win_sample = jnp.concatenate([cache_win_kv, win_new], axis=1)[:, 1:]

    return (y_prompt, y_sample, nsa_rows_prompt, nsa_rows_sample, win_prompt, win_sample,
            jnp.stack(hgrn_p), jnp.stack(hgrn_s), jnp.stack(conv_p), jnp.stack(conv_s), mem_kv_prompt)
```

```python
import functools

import jax
import jax.numpy as jnp
from jax import lax
from jax.experimental import pallas as pl
from jax.experimental.pallas import tpu as pltpu

F32 = jnp.float32
BF16 = jnp.bfloat16

D_MODEL = 2048
HEAD_DIM = 128
MIX_DIM = 1536
N_MIX_HEADS = 12
MEM_DIM = 512
N_MEM_HEADS = 4
N_KV = 4
GROUP = 3
N_BRANCH = 3
CMP_STRIDE = 16
CMP_HID = 256
SLC_BLOCK = 64
N_SELECT = 16
WINDOW = 512
PAGE_SIZE = 128
D_FF = 5504
D_FF_PAD = 5632
CONV_W = 3
RMS_EPS = 1e-6
SEL_BONUS = 1e9
NEG_BIG = -1e30

HGRN_T = 64
HGRN_SUB = 8
HGRN_HEADS_PER_STEP = 4
FFN_TF = 512
VMEM_LIMIT = 56 * 1024 * 1024


def _cp(sem, vmem=VMEM_LIMIT):
    return pltpu.CompilerParams(dimension_semantics=sem, vmem_limit_bytes=vmem)


def _dot(a, b):
    return jnp.dot(a, b, preferred_element_type=F32)


def _dot_nt(a, b):
    return lax.dot_general(a, b, (((1,), (1,)), ((), ())), preferred_element_type=F32)


def _dot_tn(a, b):
    return lax.dot_general(a, b, (((0,), (0,)), ((), ())), preferred_element_type=F32)


def _pick(n, cands):
    for c in cands:
        if n % c == 0:
            return c
    return n


def _rms_matmul_kernel(x_ref, g_ref, w_ref, b_ref, o_ref, xn_ref):
    @pl.when(pl.program_id(1) == 0)
    def _():
        x = x_ref[...]
        y = x * lax.rsqrt(jnp.mean(x * x, axis=-1, keepdims=True) + RMS_EPS)
        xn_ref[...] = (y * g_ref[...]).astype(BF16)

    o_ref[...] = _dot(xn_ref[...], w_ref[...]) + b_ref[...]


def _layer_spec(lead, block, imap):
    return pl.BlockSpec((None,) * len(lead) + tuple(block), lambda *a: tuple(lead) + tuple(imap(*a)))


def rms_matmul(x, g, w, bias=None, lead=()):
    M, K = x.shape
    N = w.shape[-1]
    tm = _pick(M, (1024, 512, 256, 128))
    tn = _pick(N, (512, 256, 128))
    if bias is None:
        bias = jnp.zeros((N,), F32)
    return pl.pallas_call(
        _rms_matmul_kernel,
        grid=(M // tm, N // tn),
        in_specs=[
            pl.BlockSpec((tm, K), lambda i, j: (i, 0)),
            pl.BlockSpec((1, K), lambda i, j: (0, 0)),
            _layer_spec(lead, (K, tn), lambda i, j: (0, j)),
            pl.BlockSpec((1, tn), lambda i, j: (0, j)),
        ],
        out_specs=pl.BlockSpec((tm, tn), lambda i, j: (i, j)),
        out_shape=jax.ShapeDtypeStruct((M, N), F32),
        scratch_shapes=[pltpu.VMEM((tm, K), BF16)],
        compiler_params=_cp(("parallel", "arbitrary")),
        name="rms_matmul",
    )(x, g.reshape(1, K), w, bias.reshape(1, N))


def _rmsnorm_kernel(x_ref, g_ref, o_ref):
    x = x_ref[...]
    y = x * lax.rsqrt(jnp.mean(x * x, axis=-1, keepdims=True) + RMS_EPS)
    o_ref[...] = y * g_ref[...]


def rmsnorm_rows(x, g):
    M, K = x.shape
    tm = _pick(M, (512, 256, 128))
    return pl.pallas_call(
        _rmsnorm_kernel,
        grid=(M // tm,),
        in_specs=[pl.BlockSpec((tm, K), lambda i: (i, 0)), pl.BlockSpec((1, K), lambda i: (0, 0))],
        out_specs=pl.BlockSpec((tm, K), lambda i: (i, 0)),
        out_shape=jax.ShapeDtypeStruct((M, K), F32),
        compiler_params=_cp(("parallel",)),
        name="final_rmsnorm",
    )(x, g.reshape(1, K))


def _cast_kernel(x_ref, o_ref):
    o_ref[...] = x_ref[...].astype(o_ref.dtype)


def cast_bf16(w):
    G, R, C = w.shape
    tr = _pick(R, (256, 128))
    return pl.pallas_call(
        _cast_kernel,
        grid=(G, R // tr),
        in_specs=[pl.BlockSpec((1, tr, C), lambda g, r: (g, r, 0))],
        out_specs=pl.BlockSpec((1, tr, C), lambda g, r: (g, r, 0)),
        out_shape=jax.ShapeDtypeStruct((G, R, C), BF16),
        compiler_params=_cp(("parallel", "parallel")),
        name="cast_bf16",
    )(w)


def _cast_halves_kernel(x_ref, o_ref):
    c = x_ref.shape[2]
    o_ref[0, 0, :, 0:c] = x_ref[0].astype(o_ref.dtype)
    o_ref[0, 0, :, c:] = jnp.zeros((o_ref.shape[2], o_ref.shape[3] - c), o_ref.dtype)


def cast_split_halves(w, cols_pad):
    G, R, C2 = w.shape
    C = C2 // 2
    tr = _pick(R, (256, 128))
    return pl.pallas_call(
        _cast_halves_kernel,
        grid=(G, 2, R // tr),
        in_specs=[pl.BlockSpec((1, tr, C), lambda g, h, r: (g, r, h))],
        out_specs=pl.BlockSpec((1, 1, tr, cols_pad), lambda g, h, r: (g, h, r, 0)),
        out_shape=jax.ShapeDtypeStruct((G, 2, R, cols_pad), BF16),
        compiler_params=_cp(("parallel", "parallel", "parallel")),
        name="cast_split_halves",
    )(w)


def _cast_pad_rows_kernel(x_ref, o_ref, *, n_in):
    x = x_ref[...].astype(o_ref.dtype)
    o_ref[...] = jnp.where(pl.program_id(1) < n_in, x, jnp.zeros_like(x))


def cast_pad_rows(w, rows_pad):
    G, R, C = w.shape
    tr = 128
    n_in = R // tr
    assert R % tr == 0 and rows_pad % tr == 0
    return pl.pallas_call(
        functools.partial(_cast_pad_rows_kernel, n_in=n_in),
        grid=(G, rows_pad // tr),
        in_specs=[pl.BlockSpec((1, tr, C), lambda g, r: (g, jnp.minimum(r, n_in - 1), 0))],
        out_specs=pl.BlockSpec((1, tr, C), lambda g, r: (g, r, 0)),
        out_shape=jax.ShapeDtypeStruct((G, rows_pad, C), BF16),
        compiler_params=_cp(("parallel", "parallel")),
        name="cast_pad_rows",
    )(w)


def _outproj_kernel(om_ref, oe_ref, w1_ref, w2_ref, x_ref, y_ref):
    y_ref[...] = x_ref[...] + _dot(om_ref[...], w1_ref[...]) + _dot(oe_ref[...], w2_ref[...])


def out_proj(o_mix, o_mem, w_out, layer, x):
    M = x.shape[0]
    tm = _pick(M, (1024, 512, 256, 128))
    tn = 512
    assert MIX_DIM % MEM_DIM == 0
    return pl.pallas_call(
        _outproj_kernel,
        grid=(M // tm, D_MODEL // tn),
        in_specs=[
            pl.BlockSpec((tm, MIX_DIM), lambda i, j: (i, 0)),
            pl.BlockSpec((tm, MEM_DIM), lambda i, j: (i, 0)),
            _layer_spec((layer,), (MIX_DIM, tn), lambda i, j: (0, j)),
            _layer_spec((layer,), (MEM_DIM, tn), lambda i, j: (MIX_DIM // MEM_DIM, j)),
            pl.BlockSpec((tm, tn), lambda i, j: (i, j)),
        ],
        out_specs=pl.BlockSpec((tm, tn), lambda i, j: (i, j)),
        out_shape=jax.ShapeDtypeStruct((M, D_MODEL), F32),
        compiler_params=_cp(("parallel", "parallel")),
        name="out_proj",
    )(o_mix, o_mem, w_out, w_out, x)


def _mem_attn_kernel(q_ref, kv_ref, o_ref):
    scale = HEAD_DIM ** -0.5
    for h in range(N_MEM_HEADS):
        q = q_ref[0, :, h * HEAD_DIM:(h + 1) * HEAD_DIM].astype(BF16)
        k = kv_ref[0, :, h * HEAD_DIM:(h + 1) * HEAD_DIM].astype(BF16)
        v = kv_ref[0, :, MEM_DIM + h * HEAD_DIM:MEM_DIM + (h + 1) * HEAD_DIM].astype(BF16)
        s = _dot_nt(q, k) * scale
        m = jnp.max(s, axis=-1, keepdims=True)
        p = jnp.exp(s - m)
        p = p / jnp.sum(p, axis=-1, keepdims=True)
        o_ref[0, :, h * HEAD_DIM:(h + 1) * HEAD_DIM] = _dot(p.astype(BF16), v).astype(o_ref.dtype)


def mem_attention(z3, col_block, kv):
    B, L, _ = z3.shape
    n_mem = kv.shape[1]
    tm = _pick(L, (1024, 512, 256, 128))
    return pl.pallas_call(
        _mem_attn_kernel,
        grid=(B, L // tm),
        in_specs=[
            pl.BlockSpec((1, tm, MEM_DIM), lambda b, i: (b, i, col_block)),
            pl.BlockSpec((1, n_mem, 2 * MEM_DIM), lambda b, i: (b, 0, 0)),
        ],
        out_specs=pl.BlockSpec((1, tm, MEM_DIM), lambda b, i: (b, i, 0)),
        out_shape=jax.ShapeDtypeStruct((B, L, MEM_DIM), BF16),
        compiler_params=_cp(("parallel", "parallel")),
        name="mem_attention",
    )(z3, kv)


def _ffn_kernel(*refs, blocks_per_seq, decode):
    if decode:
        (x_ref, g_ref, wa_ref, wu_ref, cw_ref, cb_ref, wd_ref, p1_ref, p2_ref,
         y_ref, at_ref, xn_ref, acc_ref) = refs
    else:
        (x_ref, g_ref, wa_ref, wu_ref, cw_ref, cb_ref, wd_ref,
         y_ref, at_ref, xn_ref, acc_ref, carry_ref) = refs
    i = pl.program_id(0)
    j = pl.program_id(1)

    @pl.when(j == 0)
    def _():
        x = x_ref[...]
        y = x * lax.rsqrt(jnp.mean(x * x, axis=-1, keepdims=True) + RMS_EPS)
        xn_ref[...] = (y * g_ref[...]).astype(BF16)
        acc_ref[...] = jnp.zeros_like(acc_ref)

    xn = xn_ref[...]
    a = _dot(xn, wa_ref[...])
    u = _dot(xn, wu_ref[...])
    tm = a.shape[0]
    if decode:
        a1 = p1_ref[...]
        a2 = p2_ref[...]
    else:
        @pl.when((i % blocks_per_seq) == 0)
        def _():
            carry_ref[j] = jnp.zeros((8, a.shape[1]), F32)

        prev = carry_ref[j]
        row = lax.broadcasted_iota(jnp.int32, a.shape, 0)
        a1 = jnp.where(row == 0, prev[7:8], pltpu.roll(a, 1, 0))
        a2 = jnp.where(row == 0, prev[6:7], jnp.where(row == 1, prev[7:8], pltpu.roll(a, 2, 0)))
        carry_ref[j] = a[tm - 8:tm]
    at_ref[0] = a[tm - at_ref.shape[1]:tm]
    c = cb_ref[...] + cw_ref[2:3] * a
    c = c + cw_ref[0:1] * a2
    c = c + cw_ref[1:2] * a1
    h = (jax.nn.gelu(c) * u).astype(BF16)
    acc_ref[...] += _dot(h, wd_ref[...])

    @pl.when(j == pl.num_programs(1) - 1)
    def _():
        y_ref[...] = x_ref[...] + acc_ref[...]


def conv_ffn(x, g, w_up, cw, cb, w_down, layer, *, seq_len=None, prev=None):
    M = x.shape[0]
    decode = prev is not None
    tm = M if decode else _pick(seq_len, (512, 256, 128))
    n_tail = tm if decode else 8
    nf = D_FF_PAD // FFN_TF
    in_specs = [
        pl.BlockSpec((tm, D_MODEL), lambda i, j: (i, 0)),
        pl.BlockSpec((1, D_MODEL), lambda i, j: (0, 0)),
        _layer_spec((layer, 0), (D_MODEL, FFN_TF), lambda i, j: (0, j)),
        _layer_spec((layer, 1), (D_MODEL, FFN_TF), lambda i, j: (0, j)),
        pl.BlockSpec((8, FFN_TF), lambda i, j: (0, j)),
        pl.BlockSpec((1, FFN_TF), lambda i, j: (0, j)),
        _layer_spec((layer,), (FFN_TF, D_MODEL), lambda i, j: (j, 0)),
    ]
    args = [x, g.reshape(1, D_MODEL), w_up, w_up, cw, cb, w_down]
    scratch = [pltpu.VMEM((tm, D_MODEL), BF16), pltpu.VMEM((tm, D_MODEL), F32)]
    if decode:
        in_specs += [pl.BlockSpec((tm, FFN_TF), lambda i, j: (i, j))] * 2
        args += list(prev)
    else:
        scratch.append(pltpu.VMEM((nf, 8, FFN_TF), F32))
    return pl.pallas_call(
        functools.partial(_ffn_kernel, blocks_per_seq=(1 if decode else seq_len // tm), decode=decode),
        grid=(M // tm, nf),
        in_specs=in_specs,
        out_specs=[
            pl.BlockSpec((tm, D_MODEL), lambda i, j: (i, 0)),
            pl.BlockSpec((1, n_tail, FFN_TF), lambda i, j: (i, 0, j)),
        ],
        out_shape=[
            jax.ShapeDtypeStruct((M, D_MODEL), F32),
            jax.ShapeDtypeStruct((M // tm, n_tail, D_FF_PAD), F32),
        ],
        scratch_shapes=scratch,
        compiler_params=_cp(("arbitrary", "arbitrary")),
        name="conv_ffn",
    )(*args)


def _hgrn_gates(q, f, log_lb, log1m_lb):
    qs = jax.nn.silu(q)
    log_f = jnp.logaddexp(log_lb, log1m_lb + jax.nn.log_sigmoid(f))
    k = 1.0 - jnp.exp(log_f)
    return qs, k, log_f


def _hgrn_out(o, g, onorm):
    y = o * lax.rsqrt(jnp.mean(o * o, axis=-1, keepdims=True) + RMS_EPS)
    return (y * onorm) * jax.nn.silu(g)


def _hgrn_kernel(q_ref, f_ref, i_ref, g_ref, llb_ref, l1m_ref, on_ref, s0_ref, o_ref, s_ref, st_ref,
                 *, n_chunks):
    c = pl.program_id(2)
    T = HGRN_T
    n_heads = st_ref.shape[0]

    @pl.when(c == 0)
    def _():
        for h in range(n_heads):
            st_ref[h] = s0_ref[0, h].T

    assert HGRN_SUB == 8
    rl1 = lax.broadcasted_iota(jnp.int32, (T, 1), 0) % HGRN_SUB
    t_i = lax.broadcasted_iota(jnp.int32, (T, T), 0)
    s_i = lax.broadcasted_iota(jnp.int32, (T, T), 1)
    tril = jnp.where(s_i <= t_i, 1.0, 0.0).astype(BF16)
    pair_masks = []
    size = 2 * HGRN_SUB
    while size <= T:
        same = (t_i & -size) == (s_i & -size)
        pair_masks.append((size, same & ((t_i & (size - 1)) >= size // 2) & ((s_i & (size - 1)) < size // 2)))
        size *= 2

    def shift_rows(x, d):
        return pltpu.roll(x.reshape(T // HGRN_SUB, HGRN_SUB, HEAD_DIM), d, 1).reshape(T, HEAD_DIM)

    def one_head(q, k, b, v, h, sl):
        hs = slice(h * HEAD_DIM, (h + 1) * HEAD_DIM)
        st = st_ref[h]
        o = _dot_nt((q * jnp.exp(b)).astype(BF16), st.astype(BF16))
        o = o + jnp.sum(q * k, axis=-1, keepdims=True) * v
        for d in range(1, HGRN_SUB):
            w = jnp.exp(b - shift_rows(b, d))
            a = jnp.sum(q * shift_rows(k, d) * w, axis=-1, keepdims=True)
            o = o + jnp.where(rl1 >= d, a, 0.0) * shift_rows(v, d)
        att = jnp.zeros((T, T), F32)
        for size, keep in pair_masks:
            half = size // 2
            refs = [jnp.broadcast_to(b[j * size + half - 1:j * size + half], (size, HEAD_DIM))
                    for j in range(T // size)]
            r = jnp.concatenate(refs, axis=0) if len(refs) > 1 else refs[0]
            qt = q * jnp.exp(jnp.minimum(b - r, 0.0))
            kt = k * jnp.exp(jnp.minimum(r - b, 0.0))
            att = att + jnp.where(keep, _dot_nt(qt.astype(BF16), kt.astype(BF16)), 0.0)
        o = o + _dot(att.astype(BF16), v.astype(BF16))
        bl = b[T - 1:T]
        kt = k * jnp.exp(bl - b)
        st_ref[h] = st * jnp.exp(bl) + _dot_tn(v.astype(BF16), kt.astype(BF16))
        o_ref[0, sl, hs] = _hgrn_out(o, g_ref[0, sl, hs], on_ref[:, hs]).astype(o_ref.dtype)

    def chunk(ci, carry):
        sl = pl.ds(pl.multiple_of(ci * T, T), T)
        q, k, lf = _hgrn_gates(q_ref[0, sl, :], f_ref[0, sl, :], llb_ref[...], l1m_ref[...])
        b = sum(_dot(tril, part) for part in _split3(lf))
        for h in range(n_heads):
            hs = slice(h * HEAD_DIM, (h + 1) * HEAD_DIM)
            one_head(q[:, hs], k[:, hs], b[:, hs], i_ref[0, sl, hs], h, sl)
        return carry

    lax.fori_loop(0, n_chunks, chunk, 0)

    @pl.when(c == pl.num_programs(2) - 1)
    def _():
        for h in range(n_heads):
            s_ref[0, h] = st_ref[h].T


def hgrn_prompt(z3, log_lb, log1m_lb, onorm, s0):
    B, L, _ = z3.shape
    H = N_MIX_HEADS
    hb = HGRN_HEADS_PER_STEP
    wb = hb * HEAD_DIM
    tc = _pick(L, (512, 256, 128, 64))
    zspec = lambda k: pl.BlockSpec((1, tc, wb), lambda b, h, c: (b, c, k * (H // hb) + h))
    vspec = pl.BlockSpec((1, wb), lambda b, h, c: (0, h))
    sspec = pl.BlockSpec((1, hb, HEAD_DIM, HEAD_DIM), lambda b, h, c: (b, h, 0, 0))
    return pl.pallas_call(
        functools.partial(_hgrn_kernel, n_chunks=tc // HGRN_T),
        grid=(B, H // hb, L // tc),
        in_specs=[zspec(0), zspec(1), zspec(2), zspec(3), vspec, vspec, vspec, sspec],
        out_specs=[pl.BlockSpec((1, tc, wb), lambda b, h, c: (b, c, h)), sspec],
        out_shape=[
            jax.ShapeDtypeStruct((B, L, MIX_DIM), BF16),
            jax.ShapeDtypeStruct((B, H, HEAD_DIM, HEAD_DIM), F32),
        ],
        scratch_shapes=[pltpu.VMEM((hb, HEAD_DIM, HEAD_DIM), F32)],
        compiler_params=_cp(("parallel", "parallel", "arbitrary")),
        name="hgrn_chunked",
    )(z3, z3, z3, z3, log_lb, log1m_lb, onorm, s0)


def _hgrn_step_kernel(z_ref, llb_ref, l1m_ref, on_ref, s0_ref, o_ref, s_ref):
    for h in range(N_MIX_HEADS):
        col = lambda k: slice((k * N_MIX_HEADS + h) * HEAD_DIM, (k * N_MIX_HEADS + h + 1) * HEAD_DIM)
        hs = slice(h * HEAD_DIM, (h + 1) * HEAD_DIM)
        q, k, lf = _hgrn_gates(z_ref[0, :, col(0)], z_ref[0, :, col(1)], llb_ref[:, hs], l1m_ref[:, hs])
        v = z_ref[0, :, col(2)]
        g = z_ref[0, :, col(3)]
        rows = jnp.concatenate([q, k, jnp.exp(lf), jnp.zeros((5, HEAD_DIM), F32)], axis=0)
        cols = rows.T
        s_new = cols[:, 2:3] * s0_ref[0, h] + cols[:, 1:2] * v
        s_ref[0, h] = s_new
        o = jnp.sum(cols[:, 0:1] * s_new, axis=0, keepdims=True)
        o_ref[0, :, hs] = _hgrn_out(o, g, on_ref[:, hs]).astype(o_ref.dtype)


def hgrn_step(z3, log_lb, log1m_lb, onorm, s0):
    B = z3.shape[0]
    W = z3.shape[2]
    H = N_MIX_HEADS
    vspec = pl.BlockSpec((1, MIX_DIM), lambda b: (0, 0))
    sspec = pl.BlockSpec((1, H, HEAD_DIM, HEAD_DIM), lambda b: (b, 0, 0, 0))
    return pl.pallas_call(
        _hgrn_step_kernel,
        grid=(B,),
        in_specs=[pl.BlockSpec((1, 1, W), lambda b: (b, 0, 0)), vspec, vspec, vspec, sspec],
        out_specs=[pl.BlockSpec((1, 1, MIX_DIM), lambda b: (b, 0, 0)), sspec],
        out_shape=[
            jax.ShapeDtypeStruct((B, 1, MIX_DIM), BF16),
            jax.ShapeDtypeStruct((B, H, HEAD_DIM, HEAD_DIM), F32),
        ],
        compiler_params=_cp(("parallel",)),
        name="hgrn_step",
    )(z3, log_lb, log1m_lb, onorm, s0)


def _pe_proj_kernel(pe_ref, w_ref, o_ref):
    pe = jnp.broadcast_to(pe_ref[0], (8, pe_ref.shape[2])).astype(BF16)
    o_ref[0] = _dot(pe, w_ref[0])


def pe_proj(pe, w1):
    K = pe.shape[2]
    return pl.pallas_call(
        _pe_proj_kernel,
        grid=(2,),
        in_specs=[pl.BlockSpec((1, 1, K), lambda i: (i, 0, 0)), pl.BlockSpec((1, K, CMP_HID), lambda i: (i, 0, 0))],
        out_specs=pl.BlockSpec((1, 8, CMP_HID), lambda i: (i, 0, 0)),
        out_shape=jax.ShapeDtypeStruct((2, 8, CMP_HID), F32),
        compiler_params=_cp(("parallel",)),
        name="cmp_pe_proj",
    )(pe, w1)


def _cmp_mlp_kernel(x_ref, w1_ref, w2_ref, pe_ref, o_ref):
    half = x_ref.shape[3]
    x = x_ref[0, 0].astype(BF16)
    a = _dot(x, w1_ref[0, 0:half])
    bm = _dot(x, w1_ref[0, half:2 * half])
    n = a.shape[0]
    h = a + pltpu.roll(bm, n - 1, 0) + pe_ref[0, 0:1]
    o_ref[0, 0] = _dot(jax.nn.gelu(h).astype(BF16), w2_ref[0])


def cmp_mlp(xsub, w1, w2, pe_h):
    _, G, n_sub, half = xsub.shape
    return pl.pallas_call(
        _cmp_mlp_kernel,
        grid=(2, G),
        in_specs=[
            pl.BlockSpec((1, 1, n_sub, half), lambda k, g: (k, g, 0, 0)),
            pl.BlockSpec((1, 2 * half, CMP_HID), lambda k, g: (k, 0, 0)),
            pl.BlockSpec((1, CMP_HID, HEAD_DIM), lambda k, g: (k, 0, 0)),
            pl.BlockSpec((1, 8, CMP_HID), lambda k, g: (k, 0, 0)),
        ],
        out_specs=pl.BlockSpec((1, 1, n_sub, HEAD_DIM), lambda k, g: (k, g, 0, 0)),
        out_shape=jax.ShapeDtypeStruct((2, G, n_sub, HEAD_DIM), F32),
        compiler_params=_cp(("parallel", "parallel")),
        name="cmp_mlp",
    )(xsub, w1, w2, pe_h)


LOG2E = 1.4426950408889634


def _masked_softmax_rows(s, mask, exp_fn=jnp.exp):
    s = jnp.where(mask, s, -jnp.inf)
    m = jnp.max(s, axis=-1, keepdims=True)
    m = jnp.where(m == -jnp.inf, 0.0, m)
    p = exp_fn(s - m)
    return p / jnp.maximum(jnp.sum(p, axis=-1, keepdims=True), 1e-30)


def _split3(p):
    hi = p.astype(BF16)
    r = p - hi.astype(F32)
    mid = r.astype(BF16)
    lo = (r - mid.astype(F32)).astype(BF16)
    return hi, mid, lo


def _nsa_prompt_kernel(q_ref, gt_ref, kc_ref, vc_ref, ks_ref, vs_ref, kw_ref, vw_ref, o_ref,
                       kaug_ref, vsb_ref, kwb_ref, vwb_ref, score_ref, *, tq, tk, seq_len):
    qi = pl.program_id(2)
    n_blk = seq_len // SLC_BLOCK
    scale = HEAD_DIM ** -0.5 * LOG2E

    @pl.when(qi == 0)
    def _():
        r = lax.broadcasted_iota(jnp.int32, (seq_len, HEAD_DIM), 0) // SLC_BLOCK
        col = lax.broadcasted_iota(jnp.int32, (seq_len, HEAD_DIM), 1)
        kaug_ref[:, 0:HEAD_DIM] = ks_ref[0].astype(BF16)
        kaug_ref[:, HEAD_DIM:2 * HEAD_DIM] = jnp.where(r == col, 1.0, 0.0).astype(BF16)
        vsb_ref[...] = vs_ref[0].astype(BF16)
        kwb_ref[...] = kw_ref[0].astype(BF16)
        vwb_ref[...] = vw_ref[0].astype(BF16)

    q0 = qi * tq
    qs = q_ref[0] * scale
    qf = jnp.concatenate([qs[:, g * HEAD_DIM:(g + 1) * HEAD_DIM] for g in range(GROUP)], axis=0)
    qb = qf.astype(BF16)
    rows = GROUP * tq

    nc = kc_ref.shape[1]
    kc = kc_ref[0].astype(BF16)
    vc = vc_ref[0].astype(BF16)
    qpos_c = q0 + lax.broadcasted_iota(jnp.int32, (rows, nc), 0) % tq
    cend = CMP_STRIDE * lax.broadcasted_iota(jnp.int32, (rows, nc), 1) + (2 * CMP_STRIDE - 1)
    p_c = _masked_softmax_rows(_dot_nt(qb, kc), cend <= qpos_c, jnp.exp2)
    o_cmp = _dot(p_c.astype(BF16), vc)

    psum = p_c[0:tq]
    for g in range(1, GROUP):
        psum = psum + p_c[g * tq:(g + 1) * tq]
    ci = lax.broadcasted_iota(jnp.int32, (nc, HEAD_DIM), 0)
    ji = lax.broadcasted_iota(jnp.int32, (nc, HEAD_DIM), 1)
    ratio = SLC_BLOCK // CMP_STRIDE
    member = ((ci >= ratio * ji - 1) & (ci <= ratio * ji + ratio - 1)).astype(BF16)
    imp = sum(_dot(part, member) for part in _split3(psum))
    imp_t = imp.T
    blk = lax.broadcasted_iota(jnp.int32, (HEAD_DIM, tq), 0)
    cur = (q0 + lax.broadcasted_iota(jnp.int32, (HEAD_DIM, tq), 1)) // SLC_BLOCK
    valid = (blk <= cur) & (blk < n_blk)
    forced = valid & ((blk == 0) | (blk == cur) | (blk == cur - 1))
    score = jnp.where(forced, SEL_BONUS, jnp.where(valid, imp_t, -jnp.inf))
    score = score[0:n_blk]
    score_ref[...] = score
    blk = lax.broadcasted_iota(jnp.int32, (n_blk, tq), 0)
    per_tile = tq // SLC_BLOCK

    def rank_body(it, rank):
        for u in range(per_tile):
            i = it * per_tile + u
            si = score_ref[pl.ds(i, 1), :]
            beats = (si > score) | ((si == score) & (blk > i))
            rank = rank + beats.astype(jnp.int32)
        return rank

    rank = lax.fori_loop(0, qi + 1, rank_body, jnp.zeros((n_blk, tq), jnp.int32))
    bias_t = jnp.where(rank < N_SELECT, 0.0, NEG_BIG)
    if n_blk < HEAD_DIM:
        bias_t = jnp.concatenate([bias_t, jnp.zeros((HEAD_DIM - n_blk, tq), F32)], axis=0)
    bias = bias_t.T.astype(BF16)
    qaug = jnp.concatenate([qb, jnp.concatenate([bias] * GROUP, axis=0)], axis=1)

    qpos_k = q0 + lax.broadcasted_iota(jnp.int32, (rows, tk), 0) % tq
    kcol = lax.broadcasted_iota(jnp.int32, (rows, tk), 1)

    def slc_body(kt, carry, causal):
        m, l, acc = carry
        ksl = pl.ds(pl.multiple_of(kt * tk, tk), tk)
        s = _dot_nt(qaug, kaug_ref[ksl, :])
        if causal:
            s = jnp.where(kt * tk + kcol <= qpos_k, s, -jnp.inf)
        m_new = jnp.maximum(m, jnp.max(s, axis=-1, keepdims=True))
        alpha = jnp.exp2(m - m_new)
        p = jnp.exp2(s - m_new)
        l = alpha * l + jnp.sum(p, axis=-1, keepdims=True)
        acc = alpha * acc + _dot(p.astype(BF16), vsb_ref[ksl, :])
        return m_new, l, acc

    init = (jnp.full((rows, 1), -jnp.inf, F32), jnp.zeros((rows, 1), F32), jnp.zeros((rows, HEAD_DIM), F32))
    n_full = q0 // tk
    n_kt = (q0 + tq - 1) // tk + 1
    carry = lax.fori_loop(0, n_full, functools.partial(slc_body, causal=False), init)
    _, l_s, acc_s = lax.fori_loop(n_full, n_kt, functools.partial(slc_body, causal=True), carry)
    o_slc = acc_s / jnp.maximum(l_s, 1e-30)

    qpos_w = q0 + lax.broadcasted_iota(jnp.int32, (rows, tq), 0) % tq
    wcol = lax.broadcasted_iota(jnp.int32, (rows, tq), 1)

    def win_body(it, carry):
        m, l, acc = carry
        k0 = (qi - it) * tq
        ksl = pl.ds(pl.multiple_of(k0, tq), tq)
        s = _dot_nt(qb, kwb_ref[ksl, :])
        d = qpos_w - (k0 + wcol)
        s = jnp.where((d >= 0) & (d <= WINDOW), s, -jnp.inf)
        m_new = jnp.maximum(m, jnp.max(s, axis=-1, keepdims=True))
        alpha = jnp.exp2(m - m_new)
        p = jnp.exp2(s - m_new)
        l = alpha * l + jnp.sum(p, axis=-1, keepdims=True)
        acc = alpha * acc + _dot(p.astype(BF16), vwb_ref[ksl, :])
        return m_new, l, acc

    n_wt = jnp.minimum(qi, (WINDOW + tq - 1) // tq) + 1
    _, l_w, acc_w = lax.fori_loop(0, n_wt, win_body, init)
    o_win = acc_w / jnp.maximum(l_w, 1e-30)

    gates = jax.nn.sigmoid(gt_ref[0])
    for g in range(GROUP):
        rs = slice(g * tq, (g + 1) * tq)
        c0 = g * N_BRANCH
        o = (gates[:, c0:c0 + 1] * o_cmp[rs] + gates[:, c0 + 1:c0 + 2] * o_slc[rs]
             + gates[:, c0 + 2:c0 + 3] * o_win[rs])
        o_ref[0, :, g * HEAD_DIM:(g + 1) * HEAD_DIM] = o.astype(o_ref.dtype)


def nsa_prompt(zq3, gates3, k_cmp, v_cmp, rows3):
    B, L, _ = zq3.shape
    tq = _pick(L, (256, 128))
    tk = _pick(L, (512, 256, 128))
    n_sub = k_cmp.shape[1]
    qw = GROUP * HEAD_DIM
    rspec = lambda kind: pl.BlockSpec((1, L, HEAD_DIM), lambda b, n, i: (b, 0, kind * N_KV + n))
    cspec = pl.BlockSpec((1, n_sub, HEAD_DIM), lambda b, n, i: (b * N_KV + n, 0, 0))
    return pl.pallas_call(
        functools.partial(_nsa_prompt_kernel, tq=tq, tk=tk, seq_len=L),
        grid=(B, N_KV, L // tq),
        in_specs=[
            pl.BlockSpec((1, tq, qw), lambda b, n, i: (b, i, n)),
            pl.BlockSpec((1, tq, HEAD_DIM), lambda b, n, i: (b, i, n)),
            cspec, cspec, rspec(2), rspec(3), rspec(4), rspec(5),
        ],
        out_specs=pl.BlockSpec((1, tq, qw), lambda b, n, i: (b, i, n)),
        out_shape=jax.ShapeDtypeStruct((B, L, MIX_DIM), BF16),
        scratch_shapes=[
            pltpu.VMEM((L, 2 * HEAD_DIM), BF16),
            pltpu.VMEM((L, HEAD_DIM), BF16),
            pltpu.VMEM((L, HEAD_DIM), BF16),
            pltpu.VMEM((L, HEAD_DIM), BF16),
            pltpu.VMEM((L // SLC_BLOCK, tq), F32),
        ],
        compiler_params=_cp(("parallel", "parallel", "arbitrary")),
        name="nsa_prompt",
    )(zq3, gates3, k_cmp, v_cmp, rows3, rows3, rows3, rows3)


def _dec_cmp_kernel(pt_ref, *refs, n_pg, n_groups):
    del pt_ref
    pg_refs = refs[:n_pg]
    new_ref, w1_ref, w2_ref, pe_ref, kc_ref, vc_ref, lhs_ref, carry_ref = refs[n_pg:]
    g = pl.program_id(1)
    sub_pg = PAGE_SIZE // CMP_STRIDE
    n_sub = n_pg * sub_pg

    @pl.when(g == 0)
    def _():
        carry_ref[...] = jnp.zeros_like(carry_ref)

    @pl.when(g < n_groups)
    def _():
        for r in range(0, n_pg, 2):
            for p in range(CMP_STRIDE):
                pair = [jnp.swapaxes(pg_refs[r + e][0, pl.ds(p, sub_pg, stride=CMP_STRIDE), :, :], 0, 1)
                        for e in range(2)]
                lhs_ref[:, r * sub_pg:(r + 2) * sub_pg, p * HEAD_DIM:(p + 1) * HEAD_DIM] = (
                    jnp.concatenate(pair, axis=1).astype(BF16))

    @pl.when(g >= n_groups)
    def _():
        lhs_ref[...] = jnp.zeros_like(lhs_ref)

    @pl.when(g == n_groups)
    def _():
        first = lax.broadcasted_iota(jnp.int32, (8, 16, HEAD_DIM), 1) == 0
        lhs_ref[:, 0:16, 0:HEAD_DIM] = jnp.where(first, new_ref[0][:, None, :], 0.0).astype(BF16)

    row = lax.broadcasted_iota(jnp.int32, (n_sub, CMP_HID), 0)
    for kind, out_ref in enumerate((kc_ref, vc_ref)):
        x = lhs_ref[kind * N_KV:(kind + 1) * N_KV].reshape(N_KV * n_sub, CMP_STRIDE * HEAD_DIM)
        y = _dot(x, w1_ref[kind])
        for n in range(N_KV):
            slot = kind * N_KV + n
            a = y[n * n_sub:(n + 1) * n_sub, 0:CMP_HID]
            bm = y[n * n_sub:(n + 1) * n_sub, CMP_HID:2 * CMP_HID]
            a_prev = jnp.where(row == 0, carry_ref[slot:slot + 1, :], pltpu.roll(a, 1, 0))
            carry_ref[slot:slot + 1, :] = a[n_sub - 1:n_sub]
            h = a_prev + bm + pe_ref[kind, 0:1]
            out_ref[0, n] = _dot(jax.nn.gelu(h).astype(BF16), w2_ref[kind])


def dec_cmp_tokens(cache4, page_table, new_cmp, w1ab, w2, pe_h):
    B, n_pages = page_table.shape
    n_pg = _pick(n_pages, (8, 4, 2))
    assert n_pages % n_pg == 0
    n_groups = n_pages // n_pg
    sub_pg = PAGE_SIZE // CMP_STRIDE
    n_sub = n_pg * sub_pg
    n_steps = n_groups + 1
    while (n_steps * n_sub) % HEAD_DIM:
        n_steps += 1
    half = CMP_STRIDE * HEAD_DIM

    def page_spec(r):
        def imap(b, g, pt):
            return (pt[b * n_pages + jnp.minimum(g * n_pg + r, n_pages - 1)], 0, 0, 0)
        return pl.BlockSpec((1, PAGE_SIZE, 8, HEAD_DIM), imap)

    out_spec = pl.BlockSpec((1, N_KV, n_sub, HEAD_DIM), lambda b, g, pt: (b, 0, g, 0))
    grid_spec = pltpu.PrefetchScalarGridSpec(
        num_scalar_prefetch=1,
        grid=(B, n_steps),
        in_specs=[page_spec(r) for r in range(n_pg)] + [
            pl.BlockSpec((1, 8, HEAD_DIM), lambda b, g, pt: (b, 0, 0)),
            pl.BlockSpec((2, half, 2 * CMP_HID), lambda b, g, pt: (0, 0, 0)),
            pl.BlockSpec((2, CMP_HID, HEAD_DIM), lambda b, g, pt: (0, 0, 0)),
            pl.BlockSpec((2, 8, CMP_HID), lambda b, g, pt: (0, 0, 0)),
        ],
        out_specs=[out_spec, out_spec],
        scratch_shapes=[pltpu.VMEM((8, n_sub, half), BF16), pltpu.VMEM((8, CMP_HID), F32)],
    )
    tok = jax.ShapeDtypeStruct((B, N_KV, n_steps * n_sub, HEAD_DIM), F32)
    return pl.pallas_call(
        functools.partial(_dec_cmp_kernel, n_pg=n_pg, n_groups=n_groups),
        grid_spec=grid_spec,
        out_shape=[tok, tok],
        compiler_params=_cp(("parallel", "arbitrary")),
        name="dec_cmp_tokens",
    )(page_table.reshape(-1), *([cache4] * n_pg), new_cmp, w1ab, w2, pe_h)


def _dec_select_kernel(q_ref, kc_ref, vc_ref, mem_ref, ocmp_ref, idx_ref, *, q_pos, nc, n_blk, n_sel):
    C = kc_ref.shape[2]
    JB = mem_ref.shape[1]
    scale = HEAD_DIM ** -0.5
    c_i = lax.broadcasted_iota(jnp.int32, (8, C), 1) - 1
    g_i = lax.broadcasted_iota(jnp.int32, (8, C), 0)
    ok = (c_i >= 0) & (c_i < nc) & (CMP_STRIDE * c_i + (2 * CMP_STRIDE - 1) <= q_pos)
    psums = []
    for n in range(N_KV):
        qb = (q_ref[0, n] * scale).astype(BF16)
        p = _masked_softmax_rows(_dot_nt(qb, kc_ref[0, n].astype(BF16)), ok)
        ocmp_ref[0, n] = _dot(p.astype(BF16), vc_ref[0, n].astype(BF16))
        psums.append(jnp.sum(jnp.where(g_i < GROUP, p, 0.0), axis=0, keepdims=True))
    psum = jnp.concatenate(psums + [jnp.zeros((8 - N_KV, C), F32)], axis=0)
    imp = sum(_dot(part, mem_ref[...]) for part in _split3(psum))
    j_i = lax.broadcasted_iota(jnp.int32, (8, JB), 1)
    cur = q_pos // SLC_BLOCK
    valid = (j_i <= cur) & (j_i < n_blk)
    forced = valid & ((j_i == 0) | (j_i == cur) | (j_i == cur - 1))
    score = jnp.where(forced, SEL_BONUS, jnp.where(valid, imp, -jnp.inf))
    j_f = j_i.astype(F32)
    taken = j_i >= n_blk
    lane_o = lax.broadcasted_iota(jnp.int32, (8, HEAD_DIM), 1)
    out = jnp.zeros((8, HEAD_DIM), F32)
    for r in range(n_sel):
        m = jnp.max(jnp.where(taken, -jnp.inf, score), axis=-1, keepdims=True)
        cand = jnp.logical_not(taken) & (score == m)
        idx = jnp.min(jnp.where(cand, j_f, float(JB)), axis=-1, keepdims=True)
        out = jnp.where(lane_o == r, idx, out)
        taken = taken | (j_f == idx)
    idx_ref[0] = out.astype(jnp.int32)


def dec_select(q8, k_cmp, v_cmp, member, *, q_pos, nc, n_blk, n_sel):
    B, _, C, _ = k_cmp.shape
    JB = member.shape[1]
    head_spec = pl.BlockSpec((1, N_KV, 8, HEAD_DIM), lambda b: (b, 0, 0, 0))
    tok_spec = pl.BlockSpec((1, N_KV, C, HEAD_DIM), lambda b: (b, 0, 0, 0))
    return pl.pallas_call(
        functools.partial(_dec_select_kernel, q_pos=q_pos, nc=nc, n_blk=n_blk, n_sel=n_sel),
        grid=(B,),
        in_specs=[head_spec, tok_spec, tok_spec, pl.BlockSpec((C, JB), lambda b: (0, 0))],
        out_specs=[head_spec, pl.BlockSpec((1, 8, HEAD_DIM), lambda b: (b, 0, 0))],
        out_shape=[
            jax.ShapeDtypeStruct((B, N_KV, 8, HEAD_DIM), F32),
            jax.ShapeDtypeStruct((B, 8, HEAD_DIM), jnp.int32),
        ],
        compiler_params=_cp(("parallel",)),
        name="dec_select",
    )(q8, k_cmp, v_cmp, member)


def _dec_attend_kernel(idx_ref, pt_ref, q_ref, gl_ref, ocmp_ref, nslc_ref, win_ref, nwin_ref, *rest,
                       q_pos, past_len, n_sel):
    del pt_ref
    blk_refs = rest[:n_sel]
    o_ref = rest[n_sel]
    b = pl.program_id(0)
    n = pl.program_id(1)
    base = (b * N_KV + n) * n_sel
    scale = HEAD_DIM ** -0.5
    past_blocks = past_len // SLC_BLOCK
    rb = SLC_BLOCK * 8
    rw = win_ref.shape[1]
    w_buf = rw // 8
    qf = q_ref[0, 0] * scale
    qb = qf.astype(BF16)
    js = [idx_ref[base + s] for s in range(n_sel)]
    has_new = js[0] == past_blocks
    for s in range(1, n_sel):
        has_new = has_new | (js[s] == past_blocks)
    new_ok = has_new & (past_len <= q_pos)

    def head_rows(x):
        slot = lax.broadcasted_iota(jnp.int32, (8, HEAD_DIM), 0)
        k = jnp.sum(jnp.where(slot == n, x, 0.0), axis=0, keepdims=True)
        v = jnp.sum(jnp.where(slot == n + N_KV, x, 0.0), axis=0, keepdims=True)
        return k, v

    def finish(m):
        return jnp.where(m == -jnp.inf, 0.0, m)

    r_s = lax.broadcasted_iota(jnp.int32, (8, rb), 1)
    t_s = r_s >> 3
    mine_s = (r_s & 7) == n

    def slc_scores(s):
        blk = blk_refs[s][0, 0].reshape(rb, HEAD_DIM)
        sc = _dot_nt(qb, blk.astype(BF16))
        ok = mine_s & (js[s] * SLC_BLOCK + t_s <= q_pos) & (js[s] < past_blocks)
        return jnp.where(ok, sc, -jnp.inf)

    k_new, v_new = head_rows(nslc_ref[0])
    s_new = jnp.where(new_ok, jnp.sum(qf * k_new, axis=-1, keepdims=True), -jnp.inf)
    scores = [slc_scores(s) for s in range(n_sel)]
    m = s_new
    for sc in scores:
        m = jnp.maximum(m, jnp.max(sc, axis=-1, keepdims=True))
    m = finish(m)
    p_new = jnp.exp(s_new - m)
    l = p_new
    acc = p_new * v_new
    for s, sc in enumerate(scores):
        p = jnp.exp(sc - m)
        l = l + jnp.sum(p, axis=-1, keepdims=True)
        vals = pltpu.roll(blk_refs[s][0, 0].reshape(rb, HEAD_DIM), rb - N_KV, 0)
        acc = acc + _dot(p.astype(BF16), vals.astype(BF16))
    o_slc = acc / jnp.maximum(l, 1e-30)

    r_w = lax.broadcasted_iota(jnp.int32, (8, rw), 1)
    w_pos = past_len - w_buf + (r_w >> 3)
    w_ok = ((r_w & 7) == n) & (q_pos - w_pos >= 0) & (q_pos - w_pos <= WINDOW) & (w_pos >= 0)
    win = win_ref[0]
    s_w = jnp.where(w_ok, _dot_nt(qb, win.astype(BF16)), -jnp.inf)
    k_nw, v_nw = head_rows(nwin_ref[0])
    nw_ok = (q_pos - past_len >= 0) and (q_pos - past_len <= WINDOW)
    s_nw = jnp.sum(qf * k_nw, axis=-1, keepdims=True) if nw_ok else jnp.full((8, 1), -jnp.inf, F32)
    m = finish(jnp.maximum(s_nw, jnp.max(s_w, axis=-1, keepdims=True)))
    p_w = jnp.exp(s_w - m)
    p_nw = jnp.exp(s_nw - m)
    l = p_nw + jnp.sum(p_w, axis=-1, keepdims=True)
    acc = p_nw * v_nw + _dot(p_w.astype(BF16), pltpu.roll(win, rw - N_KV, 0).astype(BF16))
    o_win = acc / jnp.maximum(l, 1e-30)

    gates = jax.nn.sigmoid(gl_ref[0, 0])
    o_ref[0, 0] = gates[:, 0:1] * ocmp_ref[0, 0] + gates[:, 1:2] * o_slc + gates[:, 2:3] * o_win


def dec_attend(idx_flat, page_table, q8, gate8, ocmp8, new_slc, win2, new_win, cache5, *, q_pos, past_len, n_sel):
    B, n_pages = page_table.shape
    rw = win2.shape[1]
    past_blocks = past_len // SLC_BLOCK
    per_page = PAGE_SIZE // SLC_BLOCK

    def blk_spec(s):
        def imap(b, n, idx, pt):
            j = jnp.minimum(idx[(b * N_KV + n) * n_sel + s], past_blocks - 1)
            return (pt[b * n_pages + j // per_page], j % per_page, 0, 1, 0)
        return pl.BlockSpec((1, 1, SLC_BLOCK, 8, HEAD_DIM), imap)

    head_spec = pl.BlockSpec((1, 1, 8, HEAD_DIM), lambda b, n, idx, pt: (b, n, 0, 0))
    tok_spec = pl.BlockSpec((1, 8, HEAD_DIM), lambda b, n, idx, pt: (b, 0, 0))
    grid_spec = pltpu.PrefetchScalarGridSpec(
        num_scalar_prefetch=2,
        grid=(B, N_KV),
        in_specs=[head_spec, head_spec, head_spec, tok_spec,
                  pl.BlockSpec((1, rw, HEAD_DIM), lambda b, n, idx, pt: (b, 0, 0)), tok_spec]
        + [blk_spec(s) for s in range(n_sel)],
        out_specs=head_spec,
    )
    return pl.pallas_call(
        functools.partial(_dec_attend_kernel, q_pos=q_pos, past_len=past_len, n_sel=n_sel),
        grid_spec=grid_spec,
        out_shape=jax.ShapeDtypeStruct((B, N_KV, 8, HEAD_DIM), F32),
        compiler_params=_cp(("parallel", "arbitrary")),
        name="dec_attend",
    )(idx_flat, page_table.reshape(-1), q8, gate8, ocmp8, new_slc, win2, new_win, *([cache5] * n_sel))


def _pad_to(a, axis, size):
    pad = [(0, 0)] * a.ndim
    pad[axis] = (0, size - a.shape[axis])
    return jnp.pad(a, pad)


def _prep_weights(w_in_a, w_in_b, b_gate, w_kv, cmp_pe, cmp_w1, cmp_w2, w_mem_kv, w_out, w_up, conv_w, conv_b,
                  w_down, hgrn_lb):
    depth = w_out.shape[0]
    n_b = w_in_b.shape[0]
    n_gate = N_MIX_HEADS * N_BRANCH
    per_kv = GROUP * N_BRANCH
    half = CMP_STRIDE * HEAD_DIM
    w1 = cmp_w1.astype(BF16)
    lb = jnp.cumsum(jax.nn.softmax(hgrn_lb.astype(F32), axis=0), axis=0)
    lb = lb - lb[0]
    layers = lambda f, n=depth: [f(l) for l in range(n)]
    gate_w = lambda j: w_in_b[j][:, MIX_DIM:MIX_DIM + n_gate].reshape(D_MODEL, N_KV, per_kv)
    return dict(
        in_a=cast_bf16(w_in_a),
        in_b_qm=layers(lambda j: jnp.concatenate([w_in_b[j][:, :MIX_DIM], w_in_b[j][:, MIX_DIM + n_gate:]],
                                                 axis=-1).astype(BF16), n_b),
        in_b_gate=layers(lambda j: _pad_to(gate_w(j), 2, HEAD_DIM).reshape(D_MODEL, N_KV * HEAD_DIM).astype(BF16), n_b),
        b_gate=layers(lambda j: _pad_to(b_gate[j].reshape(N_KV, per_kv), 1, HEAD_DIM).reshape(N_KV * HEAD_DIM), n_b),
        kv=cast_bf16(w_kv[None]),
        mem_kv=cast_bf16(w_mem_kv),
        out=cast_bf16(w_out),
        up=cast_split_halves(w_up, D_FF_PAD),
        conv_w=layers(lambda l: _pad_to(_pad_to(conv_w[l], 1, D_FF_PAD), 0, 8)),
        conv_b=layers(lambda l: _pad_to(conv_b[l], 0, D_FF_PAD).reshape(1, D_FF_PAD)),
        down=cast_pad_rows(w_down, D_FF_PAD),
        cmp_w1=w1,
        cmp_w1ab=jnp.concatenate([w1[:, :half], w1[:, half:]], axis=2),
        cmp_w2=cmp_w2.astype(BF16),
        cmp_pe=cmp_pe.reshape(2, 1, 2 * half),
        log_lb=jnp.log(lb),
        log1m_lb=jnp.log1p(-lb),
    )


def kernel(x_prompt, x_sample, mem_prompt, cache_nsa_kv, page_table, cache_win_kv, state_hgrn, state_conv,
           cache_mem_kv, norm_mix, norm_ffn, norm_mem, norm_kv, norm_final, w_in_a, hgrn_lb, hgrn_onorm,
           w_in_b, b_gate, w_kv, cmp_pe, cmp_w1, cmp_w2, w_mem_kv, w_out, w_up, conv_w, conv_b, w_down):
    B, L, _ = x_prompt.shape
    Bs = x_sample.shape[0]
    depth = w_out.shape[0]
    n_a = w_in_a.shape[0]
    n_mem = mem_prompt.shape[1]
    n_pool, page, _, _, _ = cache_nsa_kv.shape
    n_pages = page_table.shape[1]
    past_len = n_pages * page
    w_buf = cache_win_kv.shape[1]
    assert x_sample.shape[1] == 1 and page == PAGE_SIZE and L % 128 == 0 and L // SLC_BLOCK <= HEAD_DIM
    W = _prep_weights(w_in_a, w_in_b, b_gate, w_kv, cmp_pe, cmp_w1, cmp_w2, w_mem_kv, w_out, w_up, conv_w,
                      conv_b, w_down, hgrn_lb)
    pe_h = pe_proj(W["cmp_pe"], W["cmp_w1"])
    onorm = hgrn_onorm.reshape(n_a, 1, MIX_DIM)
    log_lb = W["log_lb"].reshape(n_a, 1, MIX_DIM)
    log1m_lb = W["log1m_lb"].reshape(n_a, 1, MIX_DIM)

    def ffn(x, l, **kw):
        return conv_ffn(x, norm_ffn[l], W["up"], W["conv_w"][l], W["conv_b"][l], W["down"], l, **kw)

    M = B * L
    mem_flat = mem_prompt.reshape(B * n_mem, D_MODEL)
    mem_kv_p = [rms_matmul(mem_flat, norm_mem[l], W["mem_kv"], lead=(l,)).reshape(B, n_mem, 2 * MEM_DIM)
                for l in range(depth)]
    x = x_prompt.reshape(M, D_MODEL)
    hgrn_p, conv_p = [], []
    tm_ffn = _pick(L, (512, 256, 128))
    for l in range(depth):
        if l < n_a:
            z3 = rms_matmul(x, norm_mix[l], W["in_a"], lead=(l,)).reshape(B, L, -1)
            s0 = jnp.zeros((B, N_MIX_HEADS, HEAD_DIM, HEAD_DIM), F32)
            o_mix, s_new = hgrn_prompt(z3, log_lb[l], log1m_lb[l], onorm[l], s0)
            hgrn_p.append(s_new)
            o_mem = mem_attention(z3, 4 * MIX_DIM // MEM_DIM, mem_kv_p[l])
        else:
            j = l - n_a
            if j == 0:
                rows_p = rms_matmul(x, norm_kv, W["kv"], lead=(0,))
                rows3 = rows_p.reshape(B, L, -1)
                n_sub = L // CMP_STRIDE
                xsub = rows_p[:, :2 * N_KV * HEAD_DIM].reshape(B, n_sub, CMP_STRIDE, 2, N_KV, HEAD_DIM)
                xsub = xsub.transpose(3, 0, 4, 1, 2, 5).reshape(2, B * N_KV, n_sub, CMP_STRIDE * HEAD_DIM)
                cmp_p = cmp_mlp(xsub, W["cmp_w1"], W["cmp_w2"], pe_h)
            zq3 = rms_matmul(x, norm_mix[l], W["in_b_qm"][j]).reshape(B, L, -1)
            gt3 = rms_matmul(x, norm_mix[l], W["in_b_gate"][j], W["b_gate"][j]).reshape(B, L, -1)
            o_mix = nsa_prompt(zq3, gt3, cmp_p[0], cmp_p[1], rows3)
            o_mem = mem_attention(zq3, MIX_DIM // MEM_DIM, mem_kv_p[l])
        x = out_proj(o_mix.reshape(M, MIX_DIM), o_mem.reshape(M, MEM_DIM), W["out"], l, x)
        x, a_tail = ffn(x, l, seq_len=L)
        conv_p.append(a_tail.reshape(B, L // tm_ffn, 8, D_FF_PAD)[:, -1, 8 - (CONV_W - 1):, :D_FF])
    y_prompt = rmsnorm_rows(x, norm_final).reshape(B, L, D_MODEL)
    n_row4 = 4 * N_KV * HEAD_DIM
    w_keep = min(WINDOW, L)
    nsa_rows_prompt = rows3[:, :, :n_row4].reshape(B, L, 4, N_KV, HEAD_DIM)
    win_prompt = rows3[:, L - w_keep:, n_row4:].reshape(B, w_keep, 2, N_KV, HEAD_DIM)
    mem_kv_prompt = jnp.stack(mem_kv_p).reshape(depth, B, n_mem, 2, N_MEM_HEADS, HEAD_DIM)

    Ms = 16
    pad_rows = lambda a: _pad_to(a, 0, Ms)
    xs = pad_rows(x_sample.reshape(Bs, D_MODEL))
    hgrn_s, conv_s = [], []
    q_pos = past_len
    t_pad = -(-(past_len + 1) // SLC_BLOCK) * SLC_BLOCK
    n_blk = t_pad // SLC_BLOCK
    nc = t_pad // CMP_STRIDE - 1
    n_sel = min(N_SELECT, n_blk)
    for l in range(depth):
        if l < n_a:
            z = rms_matmul(xs, norm_mix[l], W["in_a"], lead=(l,))
            z3 = z[:Bs].reshape(Bs, 1, -1)
            o_mix, s_new = hgrn_step(z3, log_lb[l], log1m_lb[l], onorm[l], state_hgrn[l])
            hgrn_s.append(s_new)
            o_mix = o_mix.reshape(Bs, MIX_DIM)
            o_mem = mem_attention(z3, 4 * MIX_DIM // MEM_DIM, cache_mem_kv[l].reshape(Bs, n_mem, 2 * MEM_DIM))
        else:
            j = l - n_a
            if j == 0:
                rows_s = rms_matmul(xs, norm_kv, W["kv"], lead=(0,))[:Bs]
                new_cmp = rows_s[:, 0:1024].reshape(Bs, 8, HEAD_DIM)
                new_slc = rows_s[:, 1024:2048].reshape(Bs, 8, HEAD_DIM)
                new_win = rows_s[:, 2048:3072].reshape(Bs, 8, HEAD_DIM)
                cache4 = cache_nsa_kv.reshape(n_pool, PAGE_SIZE, 4 * N_KV, HEAD_DIM)
                cache5 = cache_nsa_kv.reshape(n_pool, PAGE_SIZE // SLC_BLOCK, SLC_BLOCK, 4 * N_KV, HEAD_DIM)
                win2 = cache_win_kv.reshape(Bs, w_buf * 2 * N_KV, HEAD_DIM)
                k_cmp_s, v_cmp_s = dec_cmp_tokens(cache4, page_table, new_cmp, W["cmp_w1ab"], W["cmp_w2"], pe_h)
                assert k_cmp_s.shape[2] > nc + 1
                ratio = SLC_BLOCK // CMP_STRIDE
                c_of_row = jnp.arange(k_cmp_s.shape[2])[:, None] - 1
                j_of_col = jnp.arange(-(-n_blk // HEAD_DIM) * HEAD_DIM)[None, :]
                member = ((c_of_row >= ratio * j_of_col - 1) & (c_of_row <= ratio * j_of_col + ratio - 1)).astype(BF16)
            zq = rms_matmul(xs, norm_mix[l], W["in_b_qm"][j])
            gt = rms_matmul(xs, norm_mix[l], W["in_b_gate"][j], W["b_gate"][j])[:Bs]
            z3 = zq[:Bs].reshape(Bs, 1, -1)
            per_head = lambda a: _pad_to(a.reshape(Bs, N_KV, GROUP, -1), 2, 8)
            q8 = per_head(zq[:Bs, :MIX_DIM])
            o_cmp, idx = dec_select(q8, k_cmp_s, v_cmp_s, member, q_pos=q_pos, nc=nc, n_blk=n_blk, n_sel=n_sel)
            idx_flat = idx[:, :N_KV, :n_sel].reshape(-1)
            gate9 = gt.reshape(Bs, N_KV, HEAD_DIM)[:, :, :GROUP * N_BRANCH]
            o_all = dec_attend(idx_flat, page_table, q8, _pad_to(per_head(gate9), 3, HEAD_DIM), o_cmp,
                               new_slc, win2, new_win, cache5, q_pos=q_pos, past_len=past_len, n_sel=n_sel)
            o_mix = o_all[:, :, :GROUP].reshape(Bs, MIX_DIM).astype(BF16)
            o_mem = mem_attention(z3, MIX_DIM // MEM_DIM, cache_mem_kv[l].reshape(Bs, n_mem, 2 * MEM_DIM))
        xs = out_proj(pad_rows(o_mix), pad_rows(o_mem.reshape(Bs, MEM_DIM)), W["out"], l, xs)
        prev = (pad_rows(_pad_to(state_conv[l][:, 1], 1, D_FF_PAD)), pad_rows(_pad_to(state_conv[l][:, 0], 1, D_FF_PAD)))
        xs, a_new = ffn(xs, l, prev=prev)
        conv_s.append(jnp.stack([state_conv[l][:, 1], a_new[0, :Bs, :D_FF]], axis=1))
    y_sample = rmsnorm_rows(xs, norm_final)[:Bs].reshape(Bs, 1, D_MODEL)
    nsa_rows_sample = rows_s[:, :n_row4].reshape(Bs, 1, 4, N_KV, HEAD_DIM)
    win_new = rows_s[:, n_row4:].reshape(Bs, 1, 2, N_KV, HEAD_DIM).astype(cache_win_kv.dtype)
    win_sample = jnp.concatenate([cache_win_kv, win_new], axis=1)[:, 1:]

    return (y_prompt, y_sample, nsa_rows_prompt, nsa_rows_sample, win_prompt, win_sample,
            jnp.stack(hgrn_p), jnp.stack(hgrn_s), jnp.stack(conv_p), jnp.stack(conv_s), mem_kv_prompt)
```

```python
import functools

import jax
import jax.numpy as jnp
from jax import lax
from jax.experimental import pallas as pl
from jax.experimental.pallas import tpu as pltpu

F32 = jnp.float32
BF16 = jnp.bfloat16

D_MODEL = 2048
HEAD_DIM = 128
MIX_DIM = 1536
N_MIX_HEADS = 12
MEM_DIM = 512
N_MEM_HEADS = 4
N_KV = 4
GROUP = 3
N_BRANCH = 3
CMP_STRIDE = 16
CMP_HID = 256
SLC_BLOCK = 64
N_SELECT = 16
WINDOW = 512
PAGE_SIZE = 128
D_FF = 5504
D_FF_PAD = 5632
CONV_W = 3
RMS_EPS = 1e-6
SEL_BONUS = 1e9
NEG_BIG = -1e30

HGRN_T = 64
HGRN_SUB = 8
HGRN_HEADS_PER_STEP = 4
FFN_TF = 512
VMEM_LIMIT = 56 * 1024 * 1024


def _cp(sem, vmem=VMEM_LIMIT):
    return pltpu.CompilerParams(dimension_semantics=sem, vmem_limit_bytes=vmem)


def _dot(a, b):
    return jnp.dot(a, b, preferred_element_type=F32)


def _dot_nt(a, b):
    return lax.dot_general(a, b, (((1,), (1,)), ((), ())), preferred_element_type=F32)


def _dot_tn(a, b):
    return lax.dot_general(a, b, (((0,), (0,)), ((), ())), preferred_element_type=F32)


def _pick(n, cands):
    for c in cands:
        if n % c == 0:
            return c
    return n


def _rms_matmul_kernel(x_ref, g_ref, w_ref, b_ref, o_ref, xn_ref):
    @pl.when(pl.program_id(1) == 0)
    def _():
        x = x_ref[...]
        y = x * lax.rsqrt(jnp.mean(x * x, axis=-1, keepdims=True) + RMS_EPS)
        xn_ref[...] = (y * g_ref[...]).astype(BF16)

    o_ref[...] = _dot(xn_ref[...], w_ref[...]) + b_ref[...]


def _layer_spec(lead, block, imap):
    return pl.BlockSpec((None,) * len(lead) + tuple(block), lambda *a: tuple(lead) + tuple(imap(*a)))


def rms_matmul(x, g, w, bias=None, lead=()):
    M, K = x.shape
    N = w.shape[-1]
    tm = _pick(M, (1024, 512, 256, 128))
    tn = _pick(N, (512, 256, 128))
    if bias is None:
        bias = jnp.zeros((N,), F32)
    return pl.pallas_call(
        _rms_matmul_kernel,
        grid=(M // tm, N // tn),
        in_specs=[
            pl.BlockSpec((tm, K), lambda i, j: (i, 0)),
            pl.BlockSpec((1, K), lambda i, j: (0, 0)),
            _layer_spec(lead, (K, tn), lambda i, j: (0, j)),
            pl.BlockSpec((1, tn), lambda i, j: (0, j)),
        ],
        out_specs=pl.BlockSpec((tm, tn), lambda i, j: (i, j)),
        out_shape=jax.ShapeDtypeStruct((M, N), F32),
        scratch_shapes=[pltpu.VMEM((tm, K), BF16)],
        compiler_params=_cp(("parallel", "arbitrary")),
        name="rms_matmul",
    )(x, g.reshape(1, K), w, bias.reshape(1, N))


def _rmsnorm_kernel(x_ref, g_ref, o_ref):
    x = x_ref[...]
    y = x * lax.rsqrt(jnp.mean(x * x, axis=-1, keepdims=True) + RMS_EPS)
    o_ref[...] = y * g_ref[...]


def rmsnorm_rows(x, g):
    M, K = x.shape
    tm = _pick(M, (512, 256, 128))
    return pl.pallas_call(
        _rmsnorm_kernel,
        grid=(M // tm,),
        in_specs=[pl.BlockSpec((tm, K), lambda i: (i, 0)), pl.BlockSpec((1, K), lambda i: (0, 0))],
        out_specs=pl.BlockSpec((tm, K), lambda i: (i, 0)),
        out_shape=jax.ShapeDtypeStruct((M, K), F32),
        compiler_params=_cp(("parallel",)),
        name="final_rmsnorm",
    )(x, g.reshape(1, K))


def _cast_kernel(x_ref, o_ref):
    o_ref[...] = x_ref[...].astype(o_ref.dtype)


def cast_bf16(w):
    G, R, C = w.shape
    tr = _pick(R, (256, 128))
    return pl.pallas_call(
        _cast_kernel,
        grid=(G, R // tr),
        in_specs=[pl.BlockSpec((1, tr, C), lambda g, r: (g, r, 0))],
        out_specs=pl.BlockSpec((1, tr, C), lambda g, r: (g, r, 0)),
        out_shape=jax.ShapeDtypeStruct((G, R, C), BF16),
        compiler_params=_cp(("parallel", "parallel")),
        name="cast_bf16",
    )(w)


def _cast_halves_kernel(x_ref, o_ref):
    c = x_ref.shape[2]
    o_ref[0, 0, :, 0:c] = x_ref[0].astype(o_ref.dtype)
    o_ref[0, 0, :, c:] = jnp.zeros((o_ref.shape[2], o_ref.shape[3] - c), o_ref.dtype)


def cast_split_halves(w, cols_pad):
    G, R, C2 = w.shape
    C = C2 // 2
    tr = _pick(R, (256, 128))
    return pl.pallas_call(
        _cast_halves_kernel,
        grid=(G, 2, R // tr),
        in_specs=[pl.BlockSpec((1, tr, C), lambda g, h, r: (g, r, h))],
        out_specs=pl.BlockSpec((1, 1, tr, cols_pad), lambda g, h, r: (g, h, r, 0)),
        out_shape=jax.ShapeDtypeStruct((G, 2, R, cols_pad), BF16),
        compiler_params=_cp(("parallel", "parallel", "parallel")),
        name="cast_split_halves",
    )(w)


def _cast_pad_rows_kernel(x_ref, o_ref):
    r = x_ref.shape[1]
    o_ref[0, 0:r, :] = x_ref[0].astype(o_ref.dtype)
    o_ref[0, r:, :] = jnp.zeros((o_ref.shape[1] - r, o_ref.shape[2]), o_ref.dtype)


def cast_pad_rows(w, rows_pad):
    G, R, C = w.shape
    tc = _pick(C, (256, 128))
    assert R % 16 == 0 and rows_pad % 16 == 0
    return pl.pallas_call(
        _cast_pad_rows_kernel,
        grid=(G, C // tc),
        in_specs=[pl.BlockSpec((1, R, tc), lambda g, c: (g, 0, c))],
        out_specs=pl.BlockSpec((1, rows_pad, tc), lambda g, c: (g, 0, c)),
        out_shape=jax.ShapeDtypeStruct((G, rows_pad, C), BF16),
        compiler_params=_cp(("parallel", "parallel")),
        name="cast_pad_rows",
    )(w)


def _outproj_kernel(om_ref, oe_ref, w1_ref, w2_ref, x_ref, y_ref):
    y_ref[...] = x_ref[...] + _dot(om_ref[...], w1_ref[...]) + _dot(oe_ref[...], w2_ref[...])


def out_proj(o_mix, o_mem, w_out, layer, x):
    M = x.shape[0]
    tm = _pick(M, (1024, 512, 256, 128))
    tn = 512
    assert MIX_DIM % MEM_DIM == 0
    return pl.pallas_call(
        _outproj_kernel,
        grid=(M // tm, D_MODEL // tn),
        in_specs=[
            pl.BlockSpec((tm, MIX_DIM), lambda i, j: (i, 0)),
            pl.BlockSpec((tm, MEM_DIM), lambda i, j: (i, 0)),
            _layer_spec((layer,), (MIX_DIM, tn), lambda i, j: (0, j)),
            _layer_spec((layer,), (MEM_DIM, tn), lambda i, j: (MIX_DIM // MEM_DIM, j)),
            pl.BlockSpec((tm, tn), lambda i, j: (i, j)),
        ],
        out_specs=pl.BlockSpec((tm, tn), lambda i, j: (i, j)),
        out_shape=jax.ShapeDtypeStruct((M, D_MODEL), F32),
        compiler_params=_cp(("parallel", "parallel")),
        name="out_proj",
    )(o_mix, o_mem, w_out, w_out, x)


def _mem_attn_kernel(q_ref, kv_ref, o_ref):
    scale = HEAD_DIM ** -0.5
    for h in range(N_MEM_HEADS):
        q = q_ref[0, :, h * HEAD_DIM:(h + 1) * HEAD_DIM].astype(BF16)
        k = kv_ref[0, :, h * HEAD_DIM:(h + 1) * HEAD_DIM].astype(BF16)
        v = kv_ref[0, :, MEM_DIM + h * HEAD_DIM:MEM_DIM + (h + 1) * HEAD_DIM].astype(BF16)
        s = _dot_nt(q, k) * scale
        m = jnp.max(s, axis=-1, keepdims=True)
        p = jnp.exp(s - m)
        p = p / jnp.sum(p, axis=-1, keepdims=True)
        o_ref[0, :, h * HEAD_DIM:(h + 1) * HEAD_DIM] = _dot(p.astype(BF16), v).astype(o_ref.dtype)


def mem_attention(z3, col_block, kv):
    B, L, _ = z3.shape
    n_mem = kv.shape[1]
    tm = _pick(L, (1024, 512, 256, 128))
    return pl.pallas_call(
        _mem_attn_kernel,
        grid=(B, L // tm),
        in_specs=[
            pl.BlockSpec((1, tm, MEM_DIM), lambda b, i: (b, i, col_block)),
            pl.BlockSpec((1, n_mem, 2 * MEM_DIM), lambda b, i: (b, 0, 0)),
        ],
        out_specs=pl.BlockSpec((1, tm, MEM_DIM), lambda b, i: (b, i, 0)),
        out_shape=jax.ShapeDtypeStruct((B, L, MEM_DIM), BF16),
        compiler_params=_cp(("parallel", "parallel")),
        name="mem_attention",
    )(z3, kv)


def _ffn_kernel(*refs, blocks_per_seq, decode):
    if decode:
        (x_ref, g_ref, wa_ref, wu_ref, cw_ref, cb_ref, wd_ref, p1_ref, p2_ref,
         y_ref, at_ref, xn_ref, acc_ref) = refs
    else:
        (x_ref, g_ref, wa_ref, wu_ref, cw_ref, cb_ref, wd_ref,
         y_ref, at_ref, xn_ref, acc_ref, carry_ref) = refs
    i = pl.program_id(0)
    j = pl.program_id(1)

    @pl.when(j == 0)
    def _():
        x = x_ref[...]
        y = x * lax.rsqrt(jnp.mean(x * x, axis=-1, keepdims=True) + RMS_EPS)
        xn_ref[...] = (y * g_ref[...]).astype(BF16)
        acc_ref[...] = jnp.zeros_like(acc_ref)

    if not decode:
        @pl.when((i % blocks_per_seq) == 0)
        def _():
            carry_ref[j] = jnp.zeros(carry_ref.shape[1:], F32)

    xn = xn_ref[...]
    tm = xn.shape[0]
    n_tail = at_ref.shape[1]
    half = wa_ref.shape[1] // 2
    down = None
    for cs in (slice(0, half), slice(half, 2 * half)):
        a = _dot(xn, wa_ref[:, cs])
        u = _dot(xn, wu_ref[:, cs])
        if decode:
            a1 = p1_ref[:, cs]
            a2 = p2_ref[:, cs]
        else:
            prev = carry_ref[j, :, cs]
            row = lax.broadcasted_iota(jnp.int32, a.shape, 0)
            a1 = jnp.where(row == 0, prev[7:8], pltpu.roll(a, 1, 0))
            a2 = jnp.where(row == 0, prev[6:7], jnp.where(row == 1, prev[7:8], pltpu.roll(a, 2, 0)))
            carry_ref[j, :, cs] = a[tm - 8:tm]
        at_ref[0, :, cs] = a[tm - n_tail:tm]
        c = cb_ref[:, cs] + cw_ref[2:3, cs] * a
        c = c + cw_ref[0:1, cs] * a2
        c = c + cw_ref[1:2, cs] * a1
        h = (jax.nn.gelu(c) * u).astype(BF16)
        d = _dot(h, wd_ref[cs, :])
        down = d if down is None else down + d
    acc_ref[...] += down

    @pl.when(j == pl.num_programs(1) - 1)
    def _():
        y_ref[...] = x_ref[...] + acc_ref[...]


def conv_ffn(x, g, w_up, cw, cb, w_down, layer, *, seq_len=None, prev=None):
    M = x.shape[0]
    decode = prev is not None
    tm = M if decode else _pick(seq_len, (512, 256, 128))
    n_tail = tm if decode else 8
    nf = D_FF_PAD // FFN_TF
    in_specs = [
        pl.BlockSpec((tm, D_MODEL), lambda i, j: (i, 0)),
        pl.BlockSpec((1, D_MODEL), lambda i, j: (0, 0)),
        _layer_spec((layer, 0), (D_MODEL, FFN_TF), lambda i, j: (0, j)),
        _layer_spec((layer, 1), (D_MODEL, FFN_TF), lambda i, j: (0, j)),
        pl.BlockSpec((8, FFN_TF), lambda i, j: (0, j)),
        pl.BlockSpec((1, FFN_TF), lambda i, j: (0, j)),
        _layer_spec((layer,), (FFN_TF, D_MODEL), lambda i, j: (j, 0)),
    ]
    args = [x, g.reshape(1, D_MODEL), w_up, w_up, cw, cb, w_down]
    scratch = [pltpu.VMEM((tm, D_MODEL), BF16), pltpu.VMEM((tm, D_MODEL), F32)]
    if decode:
        in_specs += [pl.BlockSpec((tm, FFN_TF), lambda i, j: (i, j))] * 2
        args += list(prev)
    else:
        scratch.append(pltpu.VMEM((nf, 8, FFN_TF), F32))
    return pl.pallas_call(
        functools.partial(_ffn_kernel, blocks_per_seq=(1 if decode else seq_len // tm), decode=decode),
        grid=(M // tm, nf),
        in_specs=in_specs,
        out_specs=[
            pl.BlockSpec((tm, D_MODEL), lambda i, j: (i, 0)),
            pl.BlockSpec((1, n_tail, FFN_TF), lambda i, j: (i, 0, j)),
        ],
        out_shape=[
            jax.ShapeDtypeStruct((M, D_MODEL), F32),
            jax.ShapeDtypeStruct((M // tm, n_tail, D_FF_PAD), F32),
        ],
        scratch_shapes=scratch,
        compiler_params=_cp(("arbitrary", "arbitrary")),
        name="conv_ffn",
    )(*args)


def _hgrn_gates(q, f, log_lb, log1m_lb):
    qs = jax.nn.silu(q)
    log_f = jnp.logaddexp(log_lb, log1m_lb + jax.nn.log_sigmoid(f))
    k = 1.0 - jnp.exp(log_f)
    return qs, k, log_f


def _hgrn_out(o, g, onorm):
    y = o * lax.rsqrt(jnp.mean(o * o, axis=-1, keepdims=True) + RMS_EPS)
    return (y * onorm) * jax.nn.silu(g)


def _hgrn_kernel(q_ref, f_ref, i_ref, g_ref, llb_ref, l1m_ref, on_ref, s0_ref, o_ref, s_ref, st_ref,
                 *, n_chunks):
    c = pl.program_id(2)
    T = HGRN_T
    n_heads = st_ref.shape[0]

    @pl.when(c == 0)
    def _():
        for h in range(n_heads):
            st_ref[h] = s0_ref[0, h].T

    assert HGRN_SUB == 8
    rl1 = lax.broadcasted_iota(jnp.int32, (T, 1), 0) % HGRN_SUB
    t_i = lax.broadcasted_iota(jnp.int32, (T, T), 0)
    s_i = lax.broadcasted_iota(jnp.int32, (T, T), 1)
    tril = jnp.where(s_i <= t_i, 1.0, 0.0).astype(BF16)
    pair_masks = []
    size = 2 * HGRN_SUB
    while size <= T:
        same = (t_i & -size) == (s_i & -size)
        pair_masks.append((size, same & ((t_i & (size - 1)) >= size // 2) & ((s_i & (size - 1)) < size // 2)))
        size *= 2

    def shift_rows(x, d):
        return pltpu.roll(x.reshape(T // HGRN_SUB, HGRN_SUB, HEAD_DIM), d, 1).reshape(T, HEAD_DIM)

    def one_head(q, k, b, v, h, sl):
        hs = slice(h * HEAD_DIM, (h + 1) * HEAD_DIM)
        st = st_ref[h]
        o = _dot_nt((q * jnp.exp(b)).astype(BF16), st.astype(BF16))
        o = o + jnp.sum(q * k, axis=-1, keepdims=True) * v
        for d in range(1, HGRN_SUB):
            w = jnp.exp(b - shift_rows(b, d))
            a = jnp.sum(q * shift_rows(k, d) * w, axis=-1, keepdims=True)
            o = o + jnp.where(rl1 >= d, a, 0.0) * shift_rows(v, d)
        att = jnp.zeros((T, T), F32)
        for size, keep in pair_masks:
            half = size // 2
            refs = [jnp.broadcast_to(b[j * size + half - 1:j * size + half], (size, HEAD_DIM))
                    for j in range(T // size)]
            r = jnp.concatenate(refs, axis=0) if len(refs) > 1 else refs[0]
            qt = q * jnp.exp(jnp.minimum(b - r, 0.0))
            kt = k * jnp.exp(jnp.minimum(r - b, 0.0))
            att = att + jnp.where(keep, _dot_nt(qt.astype(BF16), kt.astype(BF16)), 0.0)
        o = o + _dot(att.astype(BF16), v.astype(BF16))
        bl = b[T - 1:T]
        kt = k * jnp.exp(bl - b)
        st_ref[h] = st * jnp.exp(bl) + _dot_tn(v.astype(BF16), kt.astype(BF16))
        o_ref[0, sl, hs] = _hgrn_out(o, g_ref[0, sl, hs], on_ref[:, hs]).astype(o_ref.dtype)

    def chunk(ci, carry):
        sl = pl.ds(pl.multiple_of(ci * T, T), T)
        q, k, lf = _hgrn_gates(q_ref[0, sl, :], f_ref[0, sl, :], llb_ref[...], l1m_ref[...])
        b = sum(_dot(tril, part) for part in _split3(lf))
        for h in range(n_heads):
            hs = slice(h * HEAD_DIM, (h + 1) * HEAD_DIM)
            one_head(q[:, hs], k[:, hs], b[:, hs], i_ref[0, sl, hs], h, sl)
        return carry

    lax.fori_loop(0, n_chunks, chunk, 0)

    @pl.when(c == pl.num_programs(2) - 1)
    def _():
        for h in range(n_heads):
            s_ref[0, h] = st_ref[h].T


def hgrn_prompt(z3, log_lb, log1m_lb, onorm, s0):
    B, L, _ = z3.shape
    H = N_MIX_HEADS
    hb = HGRN_HEADS_PER_STEP
    wb = hb * HEAD_DIM
    tc = _pick(L, (512, 256, 128, 64))
    zspec = lambda k: pl.BlockSpec((1, tc, wb), lambda b, h, c: (b, c, k * (H // hb) + h))
    vspec = pl.BlockSpec((1, wb), lambda b, h, c: (0, h))
    sspec = pl.BlockSpec((1, hb, HEAD_DIM, HEAD_DIM), lambda b, h, c: (b, h, 0, 0))
    return pl.pallas_call(
        functools.partial(_hgrn_kernel, n_chunks=tc // HGRN_T),
        grid=(B, H // hb, L // tc),
        in_specs=[zspec(0), zspec(1), zspec(2), zspec(3), vspec, vspec, vspec, sspec],
        out_specs=[pl.BlockSpec((1, tc, wb), lambda b, h, c: (b, c, h)), sspec],
        out_shape=[
            jax.ShapeDtypeStruct((B, L, MIX_DIM), BF16),
            jax.ShapeDtypeStruct((B, H, HEAD_DIM, HEAD_DIM), F32),
        ],
        scratch_shapes=[pltpu.VMEM((hb, HEAD_DIM, HEAD_DIM), F32)],
        compiler_params=_cp(("parallel", "parallel", "arbitrary")),
        name="hgrn_chunked",
    )(z3, z3, z3, z3, log_lb, log1m_lb, onorm, s0)


def _hgrn_step_kernel(z_ref, llb_ref, l1m_ref, on_ref, s0_ref, o_ref, s_ref):
    for h in range(N_MIX_HEADS):
        col = lambda k: slice((k * N_MIX_HEADS + h) * HEAD_DIM, (k * N_MIX_HEADS + h + 1) * HEAD_DIM)
        hs = slice(h * HEAD_DIM, (h + 1) * HEAD_DIM)
        q, k, lf = _hgrn_gates(z_ref[0, :, col(0)], z_ref[0, :, col(1)], llb_ref[:, hs], l1m_ref[:, hs])
        v = z_ref[0, :, col(2)]
        g = z_ref[0, :, col(3)]
        rows = jnp.concatenate([q, k, jnp.exp(lf), jnp.zeros((5, HEAD_DIM), F32)], axis=0)
        cols = rows.T
        s_new = cols[:, 2:3] * s0_ref[0, h] + cols[:, 1:2] * v
        s_ref[0, h] = s_new
        o = jnp.sum(cols[:, 0:1] * s_new, axis=0, keepdims=True)
        o_ref[0, :, hs] = _hgrn_out(o, g, on_ref[:, hs]).astype(o_ref.dtype)


def hgrn_step(z3, log_lb, log1m_lb, onorm, s0):
    B = z3.shape[0]
    W = z3.shape[2]
    H = N_MIX_HEADS
    vspec = pl.BlockSpec((1, MIX_DIM), lambda b: (0, 0))
    sspec = pl.BlockSpec((1, H, HEAD_DIM, HEAD_DIM), lambda b: (b, 0, 0, 0))
    return pl.pallas_call(
        _hgrn_step_kernel,
        grid=(B,),
        in_specs=[pl.BlockSpec((1, 1, W), lambda b: (b, 0, 0)), vspec, vspec, vspec, sspec],
        out_specs=[pl.BlockSpec((1, 1, MIX_DIM), lambda b: (b, 0, 0)), sspec],
        out_shape=[
            jax.ShapeDtypeStruct((B, 1, MIX_DIM), BF16),
            jax.ShapeDtypeStruct((B, H, HEAD_DIM, HEAD_DIM), F32),
        ],
        compiler_params=_cp(("parallel",)),
        name="hgrn_step",
    )(z3, log_lb, log1m_lb, onorm, s0)


def _pe_proj_kernel(pe_ref, w_ref, o_ref):
    pe = jnp.broadcast_to(pe_ref[0], (8, pe_ref.shape[2])).astype(BF16)
    o_ref[0] = _dot(pe, w_ref[0])


def pe_proj(pe, w1):
    K = pe.shape[2]
    return pl.pallas_call(
        _pe_proj_kernel,
        grid=(2,),
        in_specs=[pl.BlockSpec((1, 1, K), lambda i: (i, 0, 0)), pl.BlockSpec((1, K, CMP_HID), lambda i: (i, 0, 0))],
        out_specs=pl.BlockSpec((1, 8, CMP_HID), lambda i: (i, 0, 0)),
        out_shape=jax.ShapeDtypeStruct((2, 8, CMP_HID), F32),
        compiler_params=_cp(("parallel",)),
        name="cmp_pe_proj",
    )(pe, w1)


def _cmp_mlp_kernel(x_ref, w1_ref, w2_ref, pe_ref, o_ref):
    half = x_ref.shape[3]
    x = x_ref[0, 0].astype(BF16)
    a = _dot(x, w1_ref[0, 0:half])
    bm = _dot(x, w1_ref[0, half:2 * half])
    n = a.shape[0]
    h = a + pltpu.roll(bm, n - 1, 0) + pe_ref[0, 0:1]
    o_ref[0, 0] = _dot(jax.nn.gelu(h).astype(BF16), w2_ref[0])


def cmp_mlp(xsub, w1, w2, pe_h):
    _, G, n_sub, half = xsub.shape
    return pl.pallas_call(
        _cmp_mlp_kernel,
        grid=(2, G),
        in_specs=[
            pl.BlockSpec((1, 1, n_sub, half), lambda k, g: (k, g, 0, 0)),
            pl.BlockSpec((1, 2 * half, CMP_HID), lambda k, g: (k, 0, 0)),
            pl.BlockSpec((1, CMP_HID, HEAD_DIM), lambda k, g: (k, 0, 0)),
            pl.BlockSpec((1, 8, CMP_HID), lambda k, g: (k, 0, 0)),
        ],
        out_specs=pl.BlockSpec((1, 1, n_sub, HEAD_DIM), lambda k, g: (k, g, 0, 0)),
        out_shape=jax.ShapeDtypeStruct((2, G, n_sub, HEAD_DIM), F32),
        compiler_params=_cp(("parallel", "parallel")),
        name="cmp_mlp",
    )(xsub, w1, w2, pe_h)


LOG2E = 1.4426950408889634


def _masked_softmax_rows(s, mask, exp_fn=jnp.exp):
    s = jnp.where(mask, s, -jnp.inf)
    m = jnp.max(s, axis=-1, keepdims=True)
    m = jnp.where(m == -jnp.inf, 0.0, m)
    p = exp_fn(s - m)
    return p / jnp.maximum(jnp.sum(p, axis=-1, keepdims=True), 1e-30)


def _split3(p):
    hi = p.astype(BF16)
    r = p - hi.astype(F32)
    mid = r.astype(BF16)
    lo = (r - mid.astype(F32)).astype(BF16)
    return hi, mid, lo


def _nsa_prompt_kernel(q_ref, gt_ref, kc_ref, vc_ref, ks_ref, vs_ref, kw_ref, vw_ref, o_ref,
                       kaug_ref, vsb_ref, kwb_ref, vwb_ref, score_ref, *, tq, tk, seq_len):
    qi = pl.program_id(2)
    n_blk = seq_len // SLC_BLOCK
    scale = HEAD_DIM ** -0.5 * LOG2E

    @pl.when(qi == 0)
    def _():
        r = lax.broadcasted_iota(jnp.int32, (seq_len, HEAD_DIM), 0) // SLC_BLOCK
        col = lax.broadcasted_iota(jnp.int32, (seq_len, HEAD_DIM), 1)
        kaug_ref[:, 0:HEAD_DIM] = ks_ref[0].astype(BF16)
        kaug_ref[:, HEAD_DIM:2 * HEAD_DIM] = jnp.where(r == col, 1.0, 0.0).astype(BF16)
        vsb_ref[...] = vs_ref[0].astype(BF16)
        kwb_ref[...] = kw_ref[0].astype(BF16)
        vwb_ref[...] = vw_ref[0].astype(BF16)

    q0 = qi * tq
    qs = q_ref[0] * scale
    qf = jnp.concatenate([qs[:, g * HEAD_DIM:(g + 1) * HEAD_DIM] for g in range(GROUP)], axis=0)
    qb = qf.astype(BF16)
    rows = GROUP * tq

    nc = kc_ref.shape[1]
    kc = kc_ref[0].astype(BF16)
    vc = vc_ref[0].astype(BF16)
    qpos_c = q0 + lax.broadcasted_iota(jnp.int32, (rows, nc), 0) % tq
    cend = CMP_STRIDE * lax.broadcasted_iota(jnp.int32, (rows, nc), 1) + (2 * CMP_STRIDE - 1)
    p_c = _masked_softmax_rows(_dot_nt(qb, kc), cend <= qpos_c, jnp.exp2)
    o_cmp = _dot(p_c.astype(BF16), vc)

    psum = p_c[0:tq]
    for g in range(1, GROUP):
        psum = psum + p_c[g * tq:(g + 1) * tq]
    ci = lax.broadcasted_iota(jnp.int32, (nc, HEAD_DIM), 0)
    ji = lax.broadcasted_iota(jnp.int32, (nc, HEAD_DIM), 1)
    ratio = SLC_BLOCK // CMP_STRIDE
    member = ((ci >= ratio * ji - 1) & (ci <= ratio * ji + ratio - 1)).astype(BF16)
    imp = sum(_dot(part, member) for part in _split3(psum))
    imp_t = imp.T
    blk = lax.broadcasted_iota(jnp.int32, (HEAD_DIM, tq), 0)
    cur = (q0 + lax.broadcasted_iota(jnp.int32, (HEAD_DIM, tq), 1)) // SLC_BLOCK
    valid = (blk <= cur) & (blk < n_blk)
    forced = valid & ((blk == 0) | (blk == cur) | (blk == cur - 1))
    score = jnp.where(forced, SEL_BONUS, jnp.where(valid, imp_t, -jnp.inf))
    score = score[0:n_blk]
    score_ref[...] = score
    blk = lax.broadcasted_iota(jnp.int32, (n_blk, tq), 0)
    per_tile = tq // SLC_BLOCK

    def rank_body(it, rank):
        for u in range(per_tile):
            i = it * per_tile + u
            si = score_ref[pl.ds(i, 1), :]
            later = jnp.where(blk > i, 1, 0)
            rank = rank + jnp.where(si > score, 1, jnp.where(si == score, later, 0))
        return rank

    rank = lax.fori_loop(0, qi + 1, rank_body, jnp.zeros((n_blk, tq), jnp.int32))
    bias_t = jnp.where(rank < N_SELECT, 0.0, NEG_BIG)
    if n_blk < HEAD_DIM:
        bias_t = jnp.concatenate([bias_t, jnp.zeros((HEAD_DIM - n_blk, tq), F32)], axis=0)
    bias = bias_t.T.astype(BF16)
    qaug = jnp.concatenate([qb, jnp.concatenate([bias] * GROUP, axis=0)], axis=1)

    qpos_k = q0 + lax.broadcasted_iota(jnp.int32, (rows, tk), 0) % tq
    kcol = lax.broadcasted_iota(jnp.int32, (rows, tk), 1)

    def slc_body(kt, carry, causal):
        m, l, acc = carry
        ksl = pl.ds(pl.multiple_of(kt * tk, tk), tk)
        s = _dot_nt(qaug, kaug_ref[ksl, :])
        if causal:
            s = jnp.where(kt * tk + kcol <= qpos_k, s, -jnp.inf)
        m_new = jnp.maximum(m, jnp.max(s, axis=-1, keepdims=True))
        alpha = jnp.exp2(m - m_new)
        p = jnp.exp2(s - m_new)
        l = alpha * l + jnp.sum(p, axis=-1, keepdims=True)
        acc = alpha * acc + _dot(p.astype(BF16), vsb_ref[ksl, :])
        return m_new, l, acc

    init = (jnp.full((rows, 1), -jnp.inf, F32), jnp.zeros((rows, 1), F32), jnp.zeros((rows, HEAD_DIM), F32))
    n_full = q0 // tk
    n_kt = (q0 + tq - 1) // tk + 1
    carry = lax.fori_loop(0, n_full, functools.partial(slc_body, causal=False), init)
    _, l_s, acc_s = lax.fori_loop(n_full, n_kt, functools.partial(slc_body, causal=True), carry)
    o_slc = acc_s / jnp.maximum(l_s, 1e-30)

    wk = min(WINDOW + tq, seq_len)
    k0 = jnp.minimum(jnp.maximum(q0 - WINDOW, 0), seq_len - wk)
    wsl = pl.ds(pl.multiple_of(k0, tq), wk)
    d = (q0 + lax.broadcasted_iota(jnp.int32, (rows, wk), 0) % tq) - (k0 + lax.broadcasted_iota(jnp.int32, (rows, wk), 1))
    s_w = jnp.where((d >= 0) & (d <= WINDOW), _dot_nt(qb, kwb_ref[wsl, :]), -jnp.inf)
    p_w = jnp.exp2(s_w - jnp.max(s_w, axis=-1, keepdims=True))
    l_w = jnp.sum(p_w, axis=-1, keepdims=True)
    o_win = _dot(p_w.astype(BF16), vwb_ref[wsl, :]) / jnp.maximum(l_w, 1e-30)

    gates = jax.nn.sigmoid(gt_ref[0])
    for g in range(GROUP):
        rs = slice(g * tq, (g + 1) * tq)
        c0 = g * N_BRANCH
        o = (gates[:, c0:c0 + 1] * o_cmp[rs] + gates[:, c0 + 1:c0 + 2] * o_slc[rs]
             + gates[:, c0 + 2:c0 + 3] * o_win[rs])
        o_ref[0, :, g * HEAD_DIM:(g + 1) * HEAD_DIM] = o.astype(o_ref.dtype)


def nsa_prompt(zq3, gates3, k_cmp, v_cmp, rows3):
    B, L, _ = zq3.shape
    tq = _pick(L, (256, 128))
    tk = _pick(L, (512, 256, 128))
    n_sub = k_cmp.shape[1]
    qw = GROUP * HEAD_DIM
    rspec = lambda kind: pl.BlockSpec((1, L, HEAD_DIM), lambda b, n, i: (b, 0, kind * N_KV + n))
    cspec = pl.BlockSpec((1, n_sub, HEAD_DIM), lambda b, n, i: (b * N_KV + n, 0, 0))
    return pl.pallas_call(
        functools.partial(_nsa_prompt_kernel, tq=tq, tk=tk, seq_len=L),
        grid=(B, N_KV, L // tq),
        in_specs=[
            pl.BlockSpec((1, tq, qw), lambda b, n, i: (b, i, n)),
            pl.BlockSpec((1, tq, HEAD_DIM), lambda b, n, i: (b, i, n)),
            cspec, cspec, rspec(2), rspec(3), rspec(4), rspec(5),
        ],
        out_specs=pl.BlockSpec((1, tq, qw), lambda b, n, i: (b, i, n)),
        out_shape=jax.ShapeDtypeStruct((B, L, MIX_DIM), BF16),
        scratch_shapes=[
            pltpu.VMEM((L, 2 * HEAD_DIM), BF16),
            pltpu.VMEM((L, HEAD_DIM), BF16),
            pltpu.VMEM((L, HEAD_DIM), BF16),
            pltpu.VMEM((L, HEAD_DIM), BF16),
            pltpu.VMEM((L // SLC_BLOCK, tq), F32),
        ],
        compiler_params=_cp(("parallel", "parallel", "arbitrary")),
        name="nsa_prompt",
    )(zq3, gates3, k_cmp, v_cmp, rows3, rows3, rows3, rows3)


def _dec_cmp_kernel(pt_ref, *refs, n_pg, n_groups):
    del pt_ref
    pg_refs = refs[:n_pg]
    new_ref, w1_ref, w2_ref, pe_ref, kc_ref, vc_ref, lhs_ref, carry_ref = refs[n_pg:]
    g = pl.program_id(1)
    sub_pg = PAGE_SIZE // CMP_STRIDE
    n_sub = n_pg * sub_pg

    @pl.when(g == 0)
    def _():
        carry_ref[...] = jnp.zeros_like(carry_ref)

    @pl.when(g < n_groups)
    def _():
        for r in range(0, n_pg, 2):
            for p in range(CMP_STRIDE):
                pair = [jnp.swapaxes(pg_refs[r + e][0, pl.ds(p, sub_pg, stride=CMP_STRIDE), :, :], 0, 1)
                        for e in range(2)]
                lhs_ref[:, r * sub_pg:(r + 2) * sub_pg, p * HEAD_DIM:(p + 1) * HEAD_DIM] = (
                    jnp.concatenate(pair, axis=1).astype(BF16))

    @pl.when(g >= n_groups)
    def _():
        lhs_ref[...] = jnp.zeros_like(lhs_ref)

    @pl.when(g == n_groups)
    def _():
        first = lax.broadcasted_iota(jnp.int32, (8, 16, HEAD_DIM), 1) == 0
        lhs_ref[:, 0:16, 0:HEAD_DIM] = jnp.where(first, new_ref[0][:, None, :], 0.0).astype(BF16)

    row = lax.broadcasted_iota(jnp.int32, (n_sub, CMP_HID), 0)
    for kind, out_ref in enumerate((kc_ref, vc_ref)):
        x = lhs_ref[kind * N_KV:(kind + 1) * N_KV].reshape(N_KV * n_sub, CMP_STRIDE * HEAD_DIM)
        y = _dot(x, w1_ref[kind])
        for n in range(N_KV):
            slot = kind * N_KV + n
            a = y[n * n_sub:(n + 1) * n_sub, 0:CMP_HID]
            bm = y[n * n_sub:(n + 1) * n_sub, CMP_HID:2 * CMP_HID]
            a_prev = jnp.where(row == 0, carry_ref[slot:slot + 1, :], pltpu.roll(a, 1, 0))
            carry_ref[slot:slot + 1, :] = a[n_sub - 1:n_sub]
            h = a_prev + bm + pe_ref[kind, 0:1]
            out_ref[0, n] = _dot(jax.nn.gelu(h).astype(BF16), w2_ref[kind])


def dec_cmp_tokens(cache4, page_table, new_cmp, w1ab, w2, pe_h):
    B, n_pages = page_table.shape
    n_pg = _pick(n_pages, (8, 4, 2))
    assert n_pages % n_pg == 0
    n_groups = n_pages // n_pg
    sub_pg = PAGE_SIZE // CMP_STRIDE
    n_sub = n_pg * sub_pg
    n_steps = n_groups + 1
    while (n_steps * n_sub) % HEAD_DIM:
        n_steps += 1
    half = CMP_STRIDE * HEAD_DIM

    def page_spec(r):
        def imap(b, g, pt):
            return (pt[b * n_pages + jnp.minimum(g * n_pg + r, n_pages - 1)], 0, 0, 0)
        return pl.BlockSpec((1, PAGE_SIZE, 8, HEAD_DIM), imap)

    out_spec = pl.BlockSpec((1, N_KV, n_sub, HEAD_DIM), lambda b, g, pt: (b, 0, g, 0))
    grid_spec = pltpu.PrefetchScalarGridSpec(
        num_scalar_prefetch=1,
        grid=(B, n_steps),
        in_specs=[page_spec(r) for r in range(n_pg)] + [
            pl.BlockSpec((1, 8, HEAD_DIM), lambda b, g, pt: (b, 0, 0)),
            pl.BlockSpec((2, half, 2 * CMP_HID), lambda b, g, pt: (0, 0, 0)),
            pl.BlockSpec((2, CMP_HID, HEAD_DIM), lambda b, g, pt: (0, 0, 0)),
            pl.BlockSpec((2, 8, CMP_HID), lambda b, g, pt: (0, 0, 0)),
        ],
        out_specs=[out_spec, out_spec],
        scratch_shapes=[pltpu.VMEM((8, n_sub, half), BF16), pltpu.VMEM((8, CMP_HID), F32)],
    )
    tok = jax.ShapeDtypeStruct((B, N_KV, n_steps * n_sub, HEAD_DIM), F32)
    return pl.pallas_call(
        functools.partial(_dec_cmp_kernel, n_pg=n_pg, n_groups=n_groups),
        grid_spec=grid_spec,
        out_shape=[tok, tok],
        compiler_params=_cp(("parallel", "arbitrary")),
        name="dec_cmp_tokens",
    )(page_table.reshape(-1), *([cache4] * n_pg), new_cmp, w1ab, w2, pe_h)


def _dec_select_kernel(q_ref, kc_ref, vc_ref, mem_ref, ocmp_ref, idx_ref, *, q_pos, nc, n_blk, n_sel):
    C = kc_ref.shape[2]
    JB = mem_ref.shape[1]
    scale = HEAD_DIM ** -0.5
    c_i = lax.broadcasted_iota(jnp.int32, (8, C), 1) - 1
    g_i = lax.broadcasted_iota(jnp.int32, (8, C), 0)
    ok = (c_i >= 0) & (c_i < nc) & (CMP_STRIDE * c_i + (2 * CMP_STRIDE - 1) <= q_pos)
    psums = []
    for n in range(N_KV):
        qb = (q_ref[0, n] * scale).astype(BF16)
        p = _masked_softmax_rows(_dot_nt(qb, kc_ref[0, n].astype(BF16)), ok)
        ocmp_ref[0, n] = _dot(p.astype(BF16), vc_ref[0, n].astype(BF16))
        psums.append(jnp.sum(jnp.where(g_i < GROUP, p, 0.0), axis=0, keepdims=True))
    psum = jnp.concatenate(psums + [jnp.zeros((8 - N_KV, C), F32)], axis=0)
    imp = sum(_dot(part, mem_ref[...]) for part in _split3(psum))
    j_i = lax.broadcasted_iota(jnp.int32, (8, JB), 1)
    cur = q_pos // SLC_BLOCK
    valid = (j_i <= cur) & (j_i < n_blk)
    forced = valid & ((j_i == 0) | (j_i == cur) | (j_i == cur - 1))
    score = jnp.where(forced, SEL_BONUS, jnp.where(valid, imp, -jnp.inf))
    j_f = j_i.astype(F32)
    taken = j_i >= n_blk
    lane_o = lax.broadcasted_iota(jnp.int32, (8, HEAD_DIM), 1)
    out = jnp.zeros((8, HEAD_DIM), F32)
    for r in range(n_sel):
        m = jnp.max(jnp.where(taken, -jnp.inf, score), axis=-1, keepdims=True)
        cand = jnp.logical_not(taken) & (score == m)
        idx = jnp.min(jnp.where(cand, j_f, float(JB)), axis=-1, keepdims=True)
        out = jnp.where(lane_o == r, idx, out)
        taken = taken | (j_f == idx)
    idx_ref[0] = out.astype(jnp.int32)


def dec_select(q8, k_cmp, v_cmp, member, *, q_pos, nc, n_blk, n_sel):
    B, _, C, _ = k_cmp.shape
    JB = member.shape[1]
    head_spec = pl.BlockSpec((1, N_KV, 8, HEAD_DIM), lambda b: (b, 0, 0, 0))
    tok_spec = pl.BlockSpec((1, N_KV, C, HEAD_DIM), lambda b: (b, 0, 0, 0))
    return pl.pallas_call(
        functools.partial(_dec_select_kernel, q_pos=q_pos, nc=nc, n_blk=n_blk, n_sel=n_sel),
        grid=(B,),
        in_specs=[head_spec, tok_spec, tok_spec, pl.BlockSpec((C, JB), lambda b: (0, 0))],
        out_specs=[head_spec, pl.BlockSpec((1, 8, HEAD_DIM), lambda b: (b, 0, 0))],
        out_shape=[
            jax.ShapeDtypeStruct((B, N_KV, 8, HEAD_DIM), F32),
            jax.ShapeDtypeStruct((B, 8, HEAD_DIM), jnp.int32),
        ],
        compiler_params=_cp(("parallel",)),
        name="dec_select",
    )(q8, k_cmp, v_cmp, member)


def _dec_attend_kernel(idx_ref, pt_ref, q_ref, gl_ref, ocmp_ref, nslc_ref, win_ref, nwin_ref, *rest,
                       q_pos, past_len, n_sel):
    del pt_ref
    blk_refs = rest[:n_sel]
    o_ref = rest[n_sel]
    b = pl.program_id(0)
    n = pl.program_id(1)
    base = (b * N_KV + n) * n_sel
    scale = HEAD_DIM ** -0.5
    past_blocks = past_len // SLC_BLOCK
    rb = SLC_BLOCK * 8
    rw = win_ref.shape[1]
    w_buf = rw // 8
    qf = q_ref[0, 0] * scale
    qb = qf.astype(BF16)
    js = [idx_ref[base + s] for s in range(n_sel)]
    has_new = js[0] == past_blocks
    for s in range(1, n_sel):
        has_new = has_new | (js[s] == past_blocks)
    new_ok = has_new & (past_len <= q_pos)

    def head_rows(x):
        slot = lax.broadcasted_iota(jnp.int32, (8, HEAD_DIM), 0)
        k = jnp.sum(jnp.where(slot == n, x, 0.0), axis=0, keepdims=True)
        v = jnp.sum(jnp.where(slot == n + N_KV, x, 0.0), axis=0, keepdims=True)
        return k, v

    def finish(m):
        return jnp.where(m == -jnp.inf, 0.0, m)

    r_s = lax.broadcasted_iota(jnp.int32, (8, rb), 1)
    t_s = r_s >> 3
    mine_s = (r_s & 7) == n

    def slc_scores(s):
        blk = blk_refs[s][0, 0].reshape(rb, HEAD_DIM)
        sc = _dot_nt(qb, blk.astype(BF16))
        ok = mine_s & (js[s] * SLC_BLOCK + t_s <= q_pos) & (js[s] < past_blocks)
        return jnp.where(ok, sc, -jnp.inf)

    k_new, v_new = head_rows(nslc_ref[0])
    s_new = jnp.where(new_ok, jnp.sum(qf * k_new, axis=-1, keepdims=True), -jnp.inf)
    scores = [slc_scores(s) for s in range(n_sel)]
    m = s_new
    for sc in scores:
        m = jnp.maximum(m, jnp.max(sc, axis=-1, keepdims=True))
    m = finish(m)
    p_new = jnp.exp(s_new - m)
    l = p_new
    acc = p_new * v_new
    for s, sc in enumerate(scores):
        p = jnp.exp(sc - m)
        l = l + jnp.sum(p, axis=-1, keepdims=True)
        vals = pltpu.roll(blk_refs[s][0, 0].reshape(rb, HEAD_DIM), rb - N_KV, 0)
        acc = acc + _dot(p.astype(BF16), vals.astype(BF16))
    o_slc = acc / jnp.maximum(l, 1e-30)

    r_w = lax.broadcasted_iota(jnp.int32, (8, rw), 1)
    w_pos = past_len - w_buf + (r_w >> 3)
    w_ok = ((r_w & 7) == n) & (q_pos - w_pos >= 0) & (q_pos - w_pos <= WINDOW) & (w_pos >= 0)
    win = win_ref[0]
    s_w = jnp.where(w_ok, _dot_nt(qb, win.astype(BF16)), -jnp.inf)
    k_nw, v_nw = head_rows(nwin_ref[0])
    nw_ok = (q_pos - past_len >= 0) and (q_pos - past_len <= WINDOW)
    s_nw = jnp.sum(qf * k_nw, axis=-1, keepdims=True) if nw_ok else jnp.full((8, 1), -jnp.inf, F32)
    m = finish(jnp.maximum(s_nw, jnp.max(s_w, axis=-1, keepdims=True)))
    p_w = jnp.exp(s_w - m)
    p_nw = jnp.exp(s_nw - m)
    l = p_nw + jnp.sum(p_w, axis=-1, keepdims=True)
    acc = p_nw * v_nw + _dot(p_w.astype(BF16), pltpu.roll(win, rw - N_KV, 0).astype(BF16))
    o_win = acc / jnp.maximum(l, 1e-30)

    gates = jax.nn.sigmoid(gl_ref[0, 0])
    o_ref[0, 0] = gates[:, 0:1] * ocmp_ref[0, 0] + gates[:, 1:2] * o_slc + gates[:, 2:3] * o_win


def dec_attend(idx_flat, page_table, q8, gate8, ocmp8, new_slc, win2, new_win, cache5, *, q_pos, past_len, n_sel):
    B, n_pages = page_table.shape
    rw = win2.shape[1]
    past_blocks = past_len // SLC_BLOCK
    per_page = PAGE_SIZE // SLC_BLOCK

    def blk_spec(s):
        def imap(b, n, idx, pt):
            j = jnp.minimum(idx[(b * N_KV + n) * n_sel + s], past_blocks - 1)
            return (pt[b * n_pages + j // per_page], j % per_page, 0, 1, 0)
        return pl.BlockSpec((1, 1, SLC_BLOCK, 8, HEAD_DIM), imap)

    head_spec = pl.BlockSpec((1, 1, 8, HEAD_DIM), lambda b, n, idx, pt: (b, n, 0, 0))
    tok_spec = pl.BlockSpec((1, 8, HEAD_DIM), lambda b, n, idx, pt: (b, 0, 0))
    grid_spec = pltpu.PrefetchScalarGridSpec(
        num_scalar_prefetch=2,
        grid=(B, N_KV),
        in_specs=[head_spec, head_spec, head_spec, tok_spec,
                  pl.BlockSpec((1, rw, HEAD_DIM), lambda b, n, idx, pt: (b, 0, 0)), tok_spec]
        + [blk_spec(s) for s in range(n_sel)],
        out_specs=head_spec,
    )
    return pl.pallas_call(
        functools.partial(_dec_attend_kernel, q_pos=q_pos, past_len=past_len, n_sel=n_sel),
        grid_spec=grid_spec,
        out_shape=jax.ShapeDtypeStruct((B, N_KV, 8, HEAD_DIM), F32),
        compiler_params=_cp(("parallel", "arbitrary")),
        name="dec_attend",
    )(idx_flat, page_table.reshape(-1), q8, gate8, ocmp8, new_slc, win2, new_win, *([cache5] * n_sel))


def _pad_to(a, axis, size):
    pad = [(0, 0)] * a.ndim
    pad[axis] = (0, size - a.shape[axis])
    return jnp.pad(a, pad)


def _prep_weights(w_in_a, w_in_b, b_gate, w_kv, cmp_pe, cmp_w1, cmp_w2, w_mem_kv, w_out, w_up, conv_w, conv_b,
                  w_down, hgrn_lb):
    depth = w_out.shape[0]
    n_b = w_in_b.shape[0]
    n_gate = N_MIX_HEADS * N_BRANCH
    per_kv = GROUP * N_BRANCH
    half = CMP_STRIDE * HEAD_DIM
    w1 = cmp_w1.astype(BF16)
    lb = jnp.cumsum(jax.nn.softmax(hgrn_lb.astype(F32), axis=0), axis=0)
    lb = lb - lb[0]
    layers = lambda f, n=depth: [f(l) for l in range(n)]
    gate_w = lambda j: w_in_b[j][:, MIX_DIM:MIX_DIM + n_gate].reshape(D_MODEL, N_KV, per_kv)
    return dict(
        in_a=cast_bf16(w_in_a),
        in_b_qm=layers(lambda j: jnp.concatenate([w_in_b[j][:, :MIX_DIM], w_in_b[j][:, MIX_DIM + n_gate:]],
                                                 axis=-1).astype(BF16), n_b),
        in_b_gate=layers(lambda j: _pad_to(gate_w(j), 2, HEAD_DIM).reshape(D_MODEL, N_KV * HEAD_DIM).astype(BF16), n_b),
        b_gate=layers(lambda j: _pad_to(b_gate[j].reshape(N_KV, per_kv), 1, HEAD_DIM).reshape(N_KV * HEAD_DIM), n_b),
        kv=cast_bf16(w_kv[None]),
        mem_kv=cast_bf16(w_mem_kv),
        out=cast_bf16(w_out),
        up=cast_split_halves(w_up, D_FF_PAD),
        conv_w=layers(lambda l: _pad_to(_pad_to(conv_w[l], 1, D_FF_PAD), 0, 8)),
        conv_b=layers(lambda l: _pad_to(conv_b[l], 0, D_FF_PAD).reshape(1, D_FF_PAD)),
        down=cast_pad_rows(w_down, D_FF_PAD),
        cmp_w1=w1,
        cmp_w1ab=jnp.concatenate([w1[:, :half], w1[:, half:]], axis=2),
        cmp_w2=cmp_w2.astype(BF16),
        cmp_pe=cmp_pe.reshape(2, 1, 2 * half),
        log_lb=jnp.log(lb),
        log1m_lb=jnp.log1p(-lb),
    )


def kernel(x_prompt, x_sample, mem_prompt, cache_nsa_kv, page_table, cache_win_kv, state_hgrn, state_conv,
           cache_mem_kv, norm_mix, norm_ffn, norm_mem, norm_kv, norm_final, w_in_a, hgrn_lb, hgrn_onorm,
           w_in_b, b_gate, w_kv, cmp_pe, cmp_w1, cmp_w2, w_mem_kv, w_out, w_up, conv_w, conv_b, w_down):
    B, L, _ = x_prompt.shape
    Bs = x_sample.shape[0]
    depth = w_out.shape[0]
    n_a = w_in_a.shape[0]
    n_mem = mem_prompt.shape[1]
    n_pool, page, _, _, _ = cache_nsa_kv.shape
    n_pages = page_table.shape[1]
    past_len = n_pages * page
    w_buf = cache_win_kv.shape[1]
    assert x_sample.shape[1] == 1 and page == PAGE_SIZE and L % 128 == 0 and L // SLC_BLOCK <= HEAD_DIM
    W = _prep_weights(w_in_a, w_in_b, b_gate, w_kv, cmp_pe, cmp_w1, cmp_w2, w_mem_kv, w_out, w_up, conv_w,
                      conv_b, w_down, hgrn_lb)
    pe_h = pe_proj(W["cmp_pe"], W["cmp_w1"])
    onorm = hgrn_onorm.reshape(n_a, 1, MIX_DIM)
    log_lb = W["log_lb"].reshape(n_a, 1, MIX_DIM)
    log1m_lb = W["log1m_lb"].reshape(n_a, 1, MIX_DIM)

    def ffn(x, l, **kw):
        return conv_ffn(x, norm_ffn[l], W["up"], W["conv_w"][l], W["conv_b"][l], W["down"], l, **kw)

    M = B * L
    mem_flat = mem_prompt.reshape(B * n_mem, D_MODEL)
    mem_kv_p = [rms_matmul(mem_flat, norm_mem[l], W["mem_kv"], lead=(l,)).reshape(B, n_mem, 2 * MEM_DIM)
                for l in range(depth)]
    x = x_prompt.reshape(M, D_MODEL)
    hgrn_p, conv_p = [], []
    tm_ffn = _pick(L, (512, 256, 128))
    for l in range(depth):
        if l < n_a:
            z3 = rms_matmul(x, norm_mix[l], W["in_a"], lead=(l,)).reshape(B, L, -1)
            s0 = jnp.zeros((B, N_MIX_HEADS, HEAD_DIM, HEAD_DIM), F32)
            o_mix, s_new = hgrn_prompt(z3, log_lb[l], log1m_lb[l], onorm[l], s0)
            hgrn_p.append(s_new)
            o_mem = mem_attention(z3, 4 * MIX_DIM // MEM_DIM, mem_kv_p[l])
        else:
            j = l - n_a
            if j == 0:
                rows_p = rms_matmul(x, norm_kv, W["kv"], lead=(0,))
                rows3 = rows_p.reshape(B, L, -1)
                n_sub = L // CMP_STRIDE
                xsub = rows_p[:, :2 * N_KV * HEAD_DIM].reshape(B, n_sub, CMP_STRIDE, 2, N_KV, HEAD_DIM)
                xsub = xsub.transpose(3, 0, 4, 1, 2, 5).reshape(2, B * N_KV, n_sub, CMP_STRIDE * HEAD_DIM)
                cmp_p = cmp_mlp(xsub, W["cmp_w1"], W["cmp_w2"], pe_h)
            zq3 = rms_matmul(x, norm_mix[l], W["in_b_qm"][j]).reshape(B, L, -1)
            gt3 = rms_matmul(x, norm_mix[l], W["in_b_gate"][j], W["b_gate"][j]).reshape(B, L, -1)
            o_mix = nsa_prompt(zq3, gt3, cmp_p[0], cmp_p[1], rows3)
            o_mem = mem_attention(zq3, MIX_DIM // MEM_DIM, mem_kv_p[l])
        x = out_proj(o_mix.reshape(M, MIX_DIM), o_mem.reshape(M, MEM_DIM), W["out"], l, x)
        x, a_tail = ffn(x, l, seq_len=L)
        conv_p.append(a_tail.reshape(B, L // tm_ffn, 8, D_FF_PAD)[:, -1, 8 - (CONV_W - 1):, :D_FF])
    y_prompt = rmsnorm_rows(x, norm_final).reshape(B, L, D_MODEL)
    n_row4 = 4 * N_KV * HEAD_DIM
    w_keep = min(WINDOW, L)
    nsa_rows_prompt = rows3[:, :, :n_row4].reshape(B, L, 4, N_KV, HEAD_DIM)
    win_prompt = rows3[:, L - w_keep:, n_row4:].reshape(B, w_keep, 2, N_KV, HEAD_DIM)
    mem_kv_prompt = jnp.stack(mem_kv_p).reshape(depth, B, n_mem, 2, N_MEM_HEADS, HEAD_DIM)

    Ms = 16
    pad_rows = lambda a: _pad_to(a, 0, Ms)
    xs = pad_rows(x_sample.reshape(Bs, D_MODEL))
    hgrn_s, conv_s = [], []
    q_pos = past_len
    t_pad = -(-(past_len + 1) // SLC_BLOCK) * SLC_BLOCK
    n_blk = t_pad // SLC_BLOCK
    nc = t_pad // CMP_STRIDE - 1
    n_sel = min(N_SELECT, n_blk)
    for l in range(depth):
        if l < n_a:
            z = rms_matmul(xs, norm_mix[l], W["in_a"], lead=(l,))
            z3 = z[:Bs].reshape(Bs, 1, -1)
            o_mix, s_new = hgrn_step(z3, log_lb[l], log1m_lb[l], onorm[l], state_hgrn[l])
            hgrn_s.append(s_new)
            o_mix = o_mix.reshape(Bs, MIX_DIM)
            o_mem = mem_attention(z3, 4 * MIX_DIM // MEM_DIM, cache_mem_kv[l].reshape(Bs, n_mem, 2 * MEM_DIM))
        else:
            j = l - n_a
            if j == 0:
                rows_s = rms_matmul(xs, norm_kv, W["kv"], lead=(0,))[:Bs]
                new_cmp = rows_s[:, 0:1024].reshape(Bs, 8, HEAD_DIM)
                new_slc = rows_s[:, 1024:2048].reshape(Bs, 8, HEAD_DIM)
                new_win = rows_s[:, 2048:3072].reshape(Bs, 8, HEAD_DIM)
                cache4 = cache_nsa_kv.reshape(n_pool, PAGE_SIZE, 4 * N_KV, HEAD_DIM)
                cache5 = cache_nsa_kv.reshape(n_pool, PAGE_SIZE // SLC_BLOCK, SLC_BLOCK, 4 * N_KV, HEAD_DIM)
                win2 = cache_win_kv.reshape(Bs, w_buf * 2 * N_KV, HEAD_DIM)
                k_cmp_s, v_cmp_s = dec_cmp_tokens(cache4, page_table, new_cmp, W["cmp_w1ab"], W["cmp_w2"], pe_h)
                assert k_cmp_s.shape[2] > nc + 1
                ratio = SLC_BLOCK // CMP_STRIDE
                c_of_row = jnp.arange(k_cmp_s.shape[2])[:, None] - 1
                j_of_col = jnp.arange(-(-n_blk // HEAD_DIM) * HEAD_DIM)[None, :]
                member = ((c_of_row >= ratio * j_of_col - 1) & (c_of_row <= ratio * j_of_col + ratio - 1)).astype(BF16)
            zq = rms_matmul(xs, norm_mix[l], W["in_b_qm"][j])
            gt = rms_matmul(xs, norm_mix[l], W["in_b_gate"][j], W["b_gate"][j])[:Bs]
            z3 = zq[:Bs].reshape(Bs, 1, -1)
            per_head = lambda a: _pad_to(a.reshape(Bs, N_KV, GROUP, -1), 2, 8)
            q8 = per_head(zq[:Bs, :MIX_DIM])
            o_cmp, idx = dec_select(q8, k_cmp_s, v_cmp_s, member, q_pos=q_pos, nc=nc, n_blk=n_blk, n_sel=n_sel)
            idx_flat = idx[:, :N_KV, :n_sel].reshape(-1)
            gate9 = gt.reshape(Bs, N_KV, HEAD_DIM)[:, :, :GROUP * N_BRANCH]
            o_all = dec_attend(idx_flat, page_table, q8, _pad_to(per_head(gate9), 3, HEAD_DIM), o_cmp,
                               new_slc, win2, new_win, cache5, q_pos=q_pos, past_len=past_len, n_sel=n_sel)
            o_mix = o_all[:, :, :GROUP].reshape(Bs, MIX_DIM).astype(BF16)
            o_mem = mem_attention(z3, MIX_DIM // MEM_DIM, cache_mem_kv[l].reshape(Bs, n_mem, 2 * MEM_DIM))
        xs = out_proj(pad_rows(o_mix), pad_rows(o_mem.reshape(Bs, MEM_DIM)), W["out"], l, xs)
        prev = (pad_rows(_pad_to(state_conv[l][:, 1], 1, D_FF_PAD)), pad_rows(_pad_to(state_conv[l][:, 0], 1, D_FF_PAD)))
        xs, a_new = ffn(xs, l, prev=prev)
        conv_s.append(jnp.stack([state_conv[l][:, 1], a_new[0, :Bs, :D_FF]], axis=1))
    y_sample = rmsnorm_rows(xs, norm_final)[:Bs].reshape(Bs, 1, D_MODEL)
    nsa_rows_sample = rows_s[:, :n_row4].reshape(Bs, 1, 4, N_KV, HEAD_DIM)
    win_new = rows_s[:, n_row4:].reshape(Bs, 1, 2, N_KV, HEAD_DIM).astype(cache_win_kv.dtype)
    win_sample = jnp.concatenate([cache_win_kv, win_new], axis=1)[:, 1:]

    return (y_prompt, y_sample, nsa_rows_prompt, nsa_rows_sample, win_prompt, win_sample,
            jnp.stack(hgrn_p), jnp.stack(hgrn_s), jnp.stack(conv_p), jnp.stack(conv_s), mem_kv_prompt)
```

```python
import functools

import jax
import jax.numpy as jnp
from jax import lax
from jax.experimental import pallas as pl
from jax.experimental.pallas import tpu as pltpu

F32 = jnp.float32
BF16 = jnp.bfloat16

D_MODEL = 2048
HEAD_DIM = 128
MIX_DIM = 1536
N_MIX_HEADS = 12
MEM_DIM = 512
N_MEM_HEADS = 4
N_KV = 4
GROUP = 3
N_BRANCH = 3
CMP_STRIDE = 16
CMP_HID = 256
SLC_BLOCK = 64
N_SELECT = 16
WINDOW = 512
PAGE_SIZE = 128
D_FF = 5504
D_FF_PAD = 5632
CONV_W = 3
RMS_EPS = 1e-6
SEL_BONUS = 1e9
NEG_BIG = -1e30

HGRN_T = 64
HGRN_SUB = 8
HGRN_HEADS_PER_STEP = 4
FFN_TF = 512
VMEM_LIMIT = 56 * 1024 * 1024


def _cp(sem, vmem=VMEM_LIMIT):
    return pltpu.CompilerParams(dimension_semantics=sem, vmem_limit_bytes=vmem)


def _dot(a, b):
    return jnp.dot(a, b, preferred_element_type=F32)


def _dot_nt(a, b):
    return lax.dot_general(a, b, (((1,), (1,)), ((), ())), preferred_element_type=F32)


def _dot_tn(a, b):
    return lax.dot_general(a, b, (((0,), (0,)), ((), ())), preferred_element_type=F32)


def _pick(n, cands):
    for c in cands:
        if n % c == 0:
            return c
    return n


def _rms_matmul_kernel(x_ref, g_ref, w_ref, b_ref, o_ref, xn_ref):
    @pl.when(pl.program_id(1) == 0)
    def _():
        x = x_ref[...]
        y = x * lax.rsqrt(jnp.mean(x * x, axis=-1, keepdims=True) + RMS_EPS)
        xn_ref[...] = (y * g_ref[...]).astype(BF16)

    o_ref[...] = _dot(xn_ref[...], w_ref[...]) + b_ref[...]


def _layer_spec(lead, block, imap):
    return pl.BlockSpec((None,) * len(lead) + tuple(block), lambda *a: tuple(lead) + tuple(imap(*a)))


def rms_matmul(x, g, w, bias=None, lead=()):
    M, K = x.shape
    N = w.shape[-1]
    tm = _pick(M, (1024, 512, 256, 128))
    tn = _pick(N, (512, 256, 128))
    if bias is None:
        bias = jnp.zeros((N,), F32)
    return pl.pallas_call(
        _rms_matmul_kernel,
        grid=(M // tm, N // tn),
        in_specs=[
            pl.BlockSpec((tm, K), lambda i, j: (i, 0)),
            pl.BlockSpec((1, K), lambda i, j: (0, 0)),
            _layer_spec(lead, (K, tn), lambda i, j: (0, j)),
            pl.BlockSpec((1, tn), lambda i, j: (0, j)),
        ],
        out_specs=pl.BlockSpec((tm, tn), lambda i, j: (i, j)),
        out_shape=jax.ShapeDtypeStruct((M, N), F32),
        scratch_shapes=[pltpu.VMEM((tm, K), BF16)],
        compiler_params=_cp(("parallel", "arbitrary")),
        name="rms_matmul",
    )(x, g.reshape(1, K), w, bias.reshape(1, N))


def _rmsnorm_kernel(x_ref, g_ref, o_ref):
    x = x_ref[...]
    y = x * lax.rsqrt(jnp.mean(x * x, axis=-1, keepdims=True) + RMS_EPS)
    o_ref[...] = y * g_ref[...]


def rmsnorm_rows(x, g):
    M, K = x.shape
    tm = _pick(M, (512, 256, 128))
    return pl.pallas_call(
        _rmsnorm_kernel,
        grid=(M // tm,),
        in_specs=[pl.BlockSpec((tm, K), lambda i: (i, 0)), pl.BlockSpec((1, K), lambda i: (0, 0))],
        out_specs=pl.BlockSpec((tm, K), lambda i: (i, 0)),
        out_shape=jax.ShapeDtypeStruct((M, K), F32),
        compiler_params=_cp(("parallel",)),
        name="final_rmsnorm",
    )(x, g.reshape(1, K))


def _cast_kernel(x_ref, o_ref):
    o_ref[...] = x_ref[...].astype(o_ref.dtype)


def cast_bf16(w):
    G, R, C = w.shape
    tr = _pick(R, (256, 128))
    return pl.pallas_call(
        _cast_kernel,
        grid=(G, R // tr),
        in_specs=[pl.BlockSpec((1, tr, C), lambda g, r: (g, r, 0))],
        out_specs=pl.BlockSpec((1, tr, C), lambda g, r: (g, r, 0)),
        out_shape=jax.ShapeDtypeStruct((G, R, C), BF16),
        compiler_params=_cp(("parallel", "parallel")),
        name="cast_bf16",
    )(w)


def _cast_halves_kernel(x_ref, o_ref):
    c = x_ref.shape[2]
    o_ref[0, 0, :, 0:c] = x_ref[0].astype(o_ref.dtype)
    o_ref[0, 0, :, c:] = jnp.zeros((o_ref.shape[2], o_ref.shape[3] - c), o_ref.dtype)


def cast_split_halves(w, cols_pad):
    G, R, C2 = w.shape
    C = C2 // 2
    tr = _pick(R, (256, 128))
    return pl.pallas_call(
        _cast_halves_kernel,
        grid=(G, 2, R // tr),
        in_specs=[pl.BlockSpec((1, tr, C), lambda g, h, r: (g, r, h))],
        out_specs=pl.BlockSpec((1, 1, tr, cols_pad), lambda g, h, r: (g, h, r, 0)),
        out_shape=jax.ShapeDtypeStruct((G, 2, R, cols_pad), BF16),
        compiler_params=_cp(("parallel", "parallel", "parallel")),
        name="cast_split_halves",
    )(w)


def _cast_pad_rows_kernel(x_ref, o_ref):
    r = x_ref.shape[1]
    o_ref[0, 0:r, :] = x_ref[0].astype(o_ref.dtype)
    o_ref[0, r:, :] = jnp.zeros((o_ref.shape[1] - r, o_ref.shape[2]), o_ref.dtype)


def cast_pad_rows(w, rows_pad):
    G, R, C = w.shape
    tc = _pick(C, (256, 128))
    assert R % 16 == 0 and rows_pad % 16 == 0
    return pl.pallas_call(
        _cast_pad_rows_kernel,
        grid=(G, C // tc),
        in_specs=[pl.BlockSpec((1, R, tc), lambda g, c: (g, 0, c))],
        out_specs=pl.BlockSpec((1, rows_pad, tc), lambda g, c: (g, 0, c)),
        out_shape=jax.ShapeDtypeStruct((G, rows_pad, C), BF16),
        compiler_params=_cp(("parallel", "parallel")),
        name="cast_pad_rows",
    )(w)


def _outproj_kernel(om_ref, oe_ref, w1_ref, w2_ref, x_ref, y_ref):
    y_ref[...] = x_ref[...] + _dot(om_ref[...], w1_ref[...]) + _dot(oe_ref[...], w2_ref[...])


def out_proj(o_mix, o_mem, w_out, layer, x):
    M = x.shape[0]
    tm = _pick(M, (1024, 512, 256, 128))
    tn = 512
    assert MIX_DIM % MEM_DIM == 0
    return pl.pallas_call(
        _outproj_kernel,
        grid=(M // tm, D_MODEL // tn),
        in_specs=[
            pl.BlockSpec((tm, MIX_DIM), lambda i, j: (i, 0)),
            pl.BlockSpec((tm, MEM_DIM), lambda i, j: (i, 0)),
            _layer_spec((layer,), (MIX_DIM, tn), lambda i, j: (0, j)),
            _layer_spec((layer,), (MEM_DIM, tn), lambda i, j: (MIX_DIM // MEM_DIM, j)),
            pl.BlockSpec((tm, tn), lambda i, j: (i, j)),
        ],
        out_specs=pl.BlockSpec((tm, tn), lambda i, j: (i, j)),
        out_shape=jax.ShapeDtypeStruct((M, D_MODEL), F32),
        compiler_params=_cp(("parallel", "parallel")),
        name="out_proj",
    )(o_mix, o_mem, w_out, w_out, x)


def _mem_attn_kernel(q_ref, kv_ref, o_ref):
    scale = HEAD_DIM ** -0.5
    for h in range(N_MEM_HEADS):
        q = q_ref[0, :, h * HEAD_DIM:(h + 1) * HEAD_DIM].astype(BF16)
        k = kv_ref[0, :, h * HEAD_DIM:(h + 1) * HEAD_DIM].astype(BF16)
        v = kv_ref[0, :, MEM_DIM + h * HEAD_DIM:MEM_DIM + (h + 1) * HEAD_DIM].astype(BF16)
        s = _dot_nt(q, k) * scale
        m = jnp.max(s, axis=-1, keepdims=True)
        p = jnp.exp(s - m)
        p = p / jnp.sum(p, axis=-1, keepdims=True)
        o_ref[0, :, h * HEAD_DIM:(h + 1) * HEAD_DIM] = _dot(p.astype(BF16), v).astype(o_ref.dtype)


def mem_attention(z3, col_block, kv):
    B, L, _ = z3.shape
    n_mem = kv.shape[1]
    tm = _pick(L, (1024, 512, 256, 128))
    return pl.pallas_call(
        _mem_attn_kernel,
        grid=(B, L // tm),
        in_specs=[
            pl.BlockSpec((1, tm, MEM_DIM), lambda b, i: (b, i, col_block)),
            pl.BlockSpec((1, n_mem, 2 * MEM_DIM), lambda b, i: (b, 0, 0)),
        ],
        out_specs=pl.BlockSpec((1, tm, MEM_DIM), lambda b, i: (b, i, 0)),
        out_shape=jax.ShapeDtypeStruct((B, L, MEM_DIM), BF16),
        compiler_params=_cp(("parallel", "parallel")),
        name="mem_attention",
    )(z3, kv)


def _ffn_kernel(*refs, blocks_per_seq, decode):
    if decode:
        (x_ref, g_ref, wa_ref, wu_ref, cw_ref, cb_ref, wd_ref, p1_ref, p2_ref,
         y_ref, at_ref, xn_ref, acc_ref) = refs
    else:
        (x_ref, g_ref, wa_ref, wu_ref, cw_ref, cb_ref, wd_ref,
         y_ref, at_ref, xn_ref, acc_ref, carry_ref) = refs
    i = pl.program_id(0)
    j = pl.program_id(1)

    @pl.when(j == 0)
    def _():
        x = x_ref[...]
        y = x * lax.rsqrt(jnp.mean(x * x, axis=-1, keepdims=True) + RMS_EPS)
        xn_ref[...] = (y * g_ref[...]).astype(BF16)
        acc_ref[...] = jnp.zeros_like(acc_ref)

    if not decode:
        @pl.when((i % blocks_per_seq) == 0)
        def _():
            carry_ref[j] = jnp.zeros(carry_ref.shape[1:], F32)

    xn = xn_ref[...]
    tm = xn.shape[0]
    n_tail = at_ref.shape[1]
    half = wa_ref.shape[1] // 2
    down = None
    for cs in (slice(0, half), slice(half, 2 * half)):
        a = _dot(xn, wa_ref[:, cs])
        u = _dot(xn, wu_ref[:, cs])
        if decode:
            a1 = p1_ref[:, cs]
            a2 = p2_ref[:, cs]
        else:
            prev = carry_ref[j, :, cs]
            row = lax.broadcasted_iota(jnp.int32, a.shape, 0)
            a1 = jnp.where(row == 0, prev[7:8], pltpu.roll(a, 1, 0))
            a2 = jnp.where(row == 0, prev[6:7], jnp.where(row == 1, prev[7:8], pltpu.roll(a, 2, 0)))
            carry_ref[j, :, cs] = a[tm - 8:tm]
        at_ref[0, :, cs] = a[tm - n_tail:tm]
        c = cb_ref[:, cs] + cw_ref[2:3, cs] * a
        c = c + cw_ref[0:1, cs] * a2
        c = c + cw_ref[1:2, cs] * a1
        h = (jax.nn.gelu(c) * u).astype(BF16)
        d = _dot(h, wd_ref[cs, :])
        down = d if down is None else down + d
    acc_ref[...] += down

    @pl.when(j == pl.num_programs(1) - 1)
    def _():
        y_ref[...] = x_ref[...] + acc_ref[...]


def conv_ffn(x, g, w_up, cw, cb, w_down, layer, *, seq_len=None, prev=None):
    M = x.shape[0]
    decode = prev is not None
    tm = M if decode else _pick(seq_len, (1024, 512, 256, 128))
    once = pl.Buffered(1)
    n_tail = tm if decode else 8
    nf = D_FF_PAD // FFN_TF
    in_specs = [
        pl.BlockSpec((tm, D_MODEL), lambda i, j: (i, 0), pipeline_mode=once),
        pl.BlockSpec((1, D_MODEL), lambda i, j: (0, 0)),
        _layer_spec((layer, 0), (D_MODEL, FFN_TF), lambda i, j: (0, j)),
        _layer_spec((layer, 1), (D_MODEL, FFN_TF), lambda i, j: (0, j)),
        pl.BlockSpec((8, FFN_TF), lambda i, j: (0, j)),
        pl.BlockSpec((1, FFN_TF), lambda i, j: (0, j)),
        _layer_spec((layer,), (FFN_TF, D_MODEL), lambda i, j: (j, 0)),
    ]
    args = [x, g.reshape(1, D_MODEL), w_up, w_up, cw, cb, w_down]
    scratch = [pltpu.VMEM((tm, D_MODEL), BF16), pltpu.VMEM((tm, D_MODEL), F32)]
    if decode:
        in_specs += [pl.BlockSpec((tm, FFN_TF), lambda i, j: (i, j))] * 2
        args += list(prev)
    else:
        scratch.append(pltpu.VMEM((nf, 8, FFN_TF), F32))
    return pl.pallas_call(
        functools.partial(_ffn_kernel, blocks_per_seq=(1 if decode else seq_len // tm), decode=decode),
        grid=(M // tm, nf),
        in_specs=in_specs,
        out_specs=[
            pl.BlockSpec((tm, D_MODEL), lambda i, j: (i, 0), pipeline_mode=once),
            pl.BlockSpec((1, n_tail, FFN_TF), lambda i, j: (i, 0, j)),
        ],
        out_shape=[
            jax.ShapeDtypeStruct((M, D_MODEL), F32),
            jax.ShapeDtypeStruct((M // tm, n_tail, D_FF_PAD), F32),
        ],
        scratch_shapes=scratch,
        compiler_params=_cp(("arbitrary", "arbitrary")),
        name="conv_ffn",
    )(*args)


def _hgrn_gates(q, f, log_lb, log1m_lb):
    qs = jax.nn.silu(q)
    log_f = jnp.logaddexp(log_lb, log1m_lb + jax.nn.log_sigmoid(f))
    k = 1.0 - jnp.exp(log_f)
    return qs, k, log_f


def _hgrn_out(o, g, onorm):
    y = o * lax.rsqrt(jnp.mean(o * o, axis=-1, keepdims=True) + RMS_EPS)
    return (y * onorm) * jax.nn.silu(g)


def _hgrn_kernel(q_ref, f_ref, i_ref, g_ref, llb_ref, l1m_ref, on_ref, s0_ref, o_ref, s_ref, st_ref,
                 *, n_chunks):
    c = pl.program_id(2)
    T = HGRN_T
    n_heads = st_ref.shape[0]

    @pl.when(c == 0)
    def _():
        for h in range(n_heads):
            st_ref[h] = s0_ref[0, h].T

    assert HGRN_SUB == 8
    rl1 = lax.broadcasted_iota(jnp.int32, (T, 1), 0) % HGRN_SUB
    t_i = lax.broadcasted_iota(jnp.int32, (T, T), 0)
    s_i = lax.broadcasted_iota(jnp.int32, (T, T), 1)
    tril = jnp.where(s_i <= t_i, 1.0, 0.0).astype(BF16)
    pair_masks = []
    size = 2 * HGRN_SUB
    while size <= T:
        same = (t_i & -size) == (s_i & -size)
        pair_masks.append((size, same & ((t_i & (size - 1)) >= size // 2) & ((s_i & (size - 1)) < size // 2)))
        size *= 2

    def shift_rows(x, d):
        return pltpu.roll(x.reshape(T // HGRN_SUB, HGRN_SUB, HEAD_DIM), d, 1).reshape(T, HEAD_DIM)

    def one_head(q, k, b, v, h, sl):
        hs = slice(h * HEAD_DIM, (h + 1) * HEAD_DIM)
        st = st_ref[h]
        o = _dot_nt((q * jnp.exp(b)).astype(BF16), st.astype(BF16))
        o = o + jnp.sum(q * k, axis=-1, keepdims=True) * v
        for d in range(1, HGRN_SUB):
            w = jnp.exp(b - shift_rows(b, d))
            a = jnp.sum(q * shift_rows(k, d) * w, axis=-1, keepdims=True)
            o = o + jnp.where(rl1 >= d, a, 0.0) * shift_rows(v, d)
        att = jnp.zeros((T, T), F32)
        for size, keep in pair_masks:
            half = size // 2
            refs = [jnp.broadcast_to(b[j * size + half - 1:j * size + half], (size, HEAD_DIM))
                    for j in range(T // size)]
            r = jnp.concatenate(refs, axis=0) if len(refs) > 1 else refs[0]
            qt = q * jnp.exp(jnp.minimum(b - r, 0.0))
            kt = k * jnp.exp(jnp.minimum(r - b, 0.0))
            att = att + jnp.where(keep, _dot_nt(qt.astype(BF16), kt.astype(BF16)), 0.0)
        o = o + _dot(att.astype(BF16), v.astype(BF16))
        bl = b[T - 1:T]
        kt = k * jnp.exp(bl - b)
        st_ref[h] = st * jnp.exp(bl) + _dot_tn(v.astype(BF16), kt.astype(BF16))
        o_ref[0, sl, hs] = _hgrn_out(o, g_ref[0, sl, hs], on_ref[:, hs]).astype(o_ref.dtype)

    def chunk(ci, carry):
        sl = pl.ds(pl.multiple_of(ci * T, T), T)
        q, k, lf = _hgrn_gates(q_ref[0, sl, :], f_ref[0, sl, :], llb_ref[...], l1m_ref[...])
        b = sum(_dot(tril, part) for part in _split3(lf))
        for h in range(n_heads):
            hs = slice(h * HEAD_DIM, (h + 1) * HEAD_DIM)
            one_head(q[:, hs], k[:, hs], b[:, hs], i_ref[0, sl, hs], h, sl)
        return carry

    lax.fori_loop(0, n_chunks, chunk, 0)

    @pl.when(c == pl.num_programs(2) - 1)
    def _():
        for h in range(n_heads):
            s_ref[0, h] = st_ref[h].T


def hgrn_prompt(z3, log_lb, log1m_lb, onorm, s0):
    B, L, _ = z3.shape
    H = N_MIX_HEADS
    hb = HGRN_HEADS_PER_STEP
    wb = hb * HEAD_DIM
    tc = _pick(L, (512, 256, 128, 64))
    zspec = lambda k: pl.BlockSpec((1, tc, wb), lambda b, h, c: (b, c, k * (H // hb) + h))
    vspec = pl.BlockSpec((1, wb), lambda b, h, c: (0, h))
    sspec = pl.BlockSpec((1, hb, HEAD_DIM, HEAD_DIM), lambda b, h, c: (b, h, 0, 0))
    return pl.pallas_call(
        functools.partial(_hgrn_kernel, n_chunks=tc // HGRN_T),
        grid=(B, H // hb, L // tc),
        in_specs=[zspec(0), zspec(1), zspec(2), zspec(3), vspec, vspec, vspec, sspec],
        out_specs=[pl.BlockSpec((1, tc, wb), lambda b, h, c: (b, c, h)), sspec],
        out_shape=[
            jax.ShapeDtypeStruct((B, L, MIX_DIM), BF16),
            jax.ShapeDtypeStruct((B, H, HEAD_DIM, HEAD_DIM), F32),
        ],
        scratch_shapes=[pltpu.VMEM((hb, HEAD_DIM, HEAD_DIM), F32)],
        compiler_params=_cp(("parallel", "parallel", "arbitrary")),
        name="hgrn_chunked",
    )(z3, z3, z3, z3, log_lb, log1m_lb, onorm, s0)


def _hgrn_step_kernel(z_ref, llb_ref, l1m_ref, on_ref, s0_ref, o_ref, s_ref):
    for h in range(N_MIX_HEADS):
        col = lambda k: slice((k * N_MIX_HEADS + h) * HEAD_DIM, (k * N_MIX_HEADS + h + 1) * HEAD_DIM)
        hs = slice(h * HEAD_DIM, (h + 1) * HEAD_DIM)
        q, k, lf = _hgrn_gates(z_ref[0, :, col(0)], z_ref[0, :, col(1)], llb_ref[:, hs], l1m_ref[:, hs])
        v = z_ref[0, :, col(2)]
        g = z_ref[0, :, col(3)]
        rows = jnp.concatenate([q, k, jnp.exp(lf), jnp.zeros((5, HEAD_DIM), F32)], axis=0)
        cols = rows.T
        s_new = cols[:, 2:3] * s0_ref[0, h] + cols[:, 1:2] * v
        s_ref[0, h] = s_new
        o = jnp.sum(cols[:, 0:1] * s_new, axis=0, keepdims=True)
        o_ref[0, :, hs] = _hgrn_out(o, g, on_ref[:, hs]).astype(o_ref.dtype)


def hgrn_step(z3, log_lb, log1m_lb, onorm, s0):
    B = z3.shape[0]
    W = z3.shape[2]
    H = N_MIX_HEADS
    vspec = pl.BlockSpec((1, MIX_DIM), lambda b: (0, 0))
    sspec = pl.BlockSpec((1, H, HEAD_DIM, HEAD_DIM), lambda b: (b, 0, 0, 0))
    return pl.pallas_call(
        _hgrn_step_kernel,
        grid=(B,),
        in_specs=[pl.BlockSpec((1, 1, W), lambda b: (b, 0, 0)), vspec, vspec, vspec, sspec],
        out_specs=[pl.BlockSpec((1, 1, MIX_DIM), lambda b: (b, 0, 0)), sspec],
        out_shape=[
            jax.ShapeDtypeStruct((B, 1, MIX_DIM), BF16),
            jax.ShapeDtypeStruct((B, H, HEAD_DIM, HEAD_DIM), F32),
        ],
        compiler_params=_cp(("parallel",)),
        name="hgrn_step",
    )(z3, log_lb, log1m_lb, onorm, s0)


def _pe_proj_kernel(pe_ref, w_ref, o_ref):
    pe = jnp.broadcast_to(pe_ref[0], (8, pe_ref.shape[2])).astype(BF16)
    o_ref[0] = _dot(pe, w_ref[0])


def pe_proj(pe, w1):
    K = pe.shape[2]
    return pl.pallas_call(
        _pe_proj_kernel,
        grid=(2,),
        in_specs=[pl.BlockSpec((1, 1, K), lambda i: (i, 0, 0)), pl.BlockSpec((1, K, CMP_HID), lambda i: (i, 0, 0))],
        out_specs=pl.BlockSpec((1, 8, CMP_HID), lambda i: (i, 0, 0)),
        out_shape=jax.ShapeDtypeStruct((2, 8, CMP_HID), F32),
        compiler_params=_cp(("parallel",)),
        name="cmp_pe_proj",
    )(pe, w1)


def _cmp_mlp_kernel(x_ref, w1_ref, w2_ref, pe_ref, o_ref):
    half = x_ref.shape[3]
    x = x_ref[0, 0].astype(BF16)
    a = _dot(x, w1_ref[0, 0:half])
    bm = _dot(x, w1_ref[0, half:2 * half])
    n = a.shape[0]
    h = a + pltpu.roll(bm, n - 1, 0) + pe_ref[0, 0:1]
    o_ref[0, 0] = _dot(jax.nn.gelu(h).astype(BF16), w2_ref[0])


def cmp_mlp(xsub, w1, w2, pe_h):
    _, G, n_sub, half = xsub.shape
    return pl.pallas_call(
        _cmp_mlp_kernel,
        grid=(2, G),
        in_specs=[
            pl.BlockSpec((1, 1, n_sub, half), lambda k, g: (k, g, 0, 0)),
            pl.BlockSpec((1, 2 * half, CMP_HID), lambda k, g: (k, 0, 0)),
            pl.BlockSpec((1, CMP_HID, HEAD_DIM), lambda k, g: (k, 0, 0)),
            pl.BlockSpec((1, 8, CMP_HID), lambda k, g: (k, 0, 0)),
        ],
        out_specs=pl.BlockSpec((1, 1, n_sub, HEAD_DIM), lambda k, g: (k, g, 0, 0)),
        out_shape=jax.ShapeDtypeStruct((2, G, n_sub, HEAD_DIM), F32),
        compiler_params=_cp(("parallel", "parallel")),
        name="cmp_mlp",
    )(xsub, w1, w2, pe_h)


LOG2E = 1.4426950408889634


def _masked_softmax_rows(s, mask, exp_fn=jnp.exp):
    s = jnp.where(mask, s, -jnp.inf)
    m = jnp.max(s, axis=-1, keepdims=True)
    m = jnp.where(m == -jnp.inf, 0.0, m)
    p = exp_fn(s - m)
    return p / jnp.maximum(jnp.sum(p, axis=-1, keepdims=True), 1e-30)


def _split3(p):
    hi = p.astype(BF16)
    r = p - hi.astype(F32)
    mid = r.astype(BF16)
    lo = (r - mid.astype(F32)).astype(BF16)
    return hi, mid, lo


def _masked_softmax_cols(s, mask, exp_fn):
    s = jnp.where(mask, s, -jnp.inf)
    m = jnp.max(s, axis=0, keepdims=True)
    m = jnp.where(m == -jnp.inf, 0.0, m)
    p = exp_fn(s - m)
    return p / jnp.maximum(jnp.sum(p, axis=0, keepdims=True), 1e-30)


def _nsa_prompt_t_kernel(q_ref, gt_ref, kc_ref, vc_ref, ks_ref, vs_ref, kw_ref, vw_ref, o_ref,
                         kaug_ref, vst_ref, kwb_ref, vwt_ref, vct_ref, score_ref, *, tq, tk, seq_len):
    qi = pl.program_id(2)
    n_blk = seq_len // SLC_BLOCK
    scale = HEAD_DIM ** -0.5 * LOG2E
    cols = GROUP * tq
    nc = kc_ref.shape[1]

    @pl.when(qi == 0)
    def _():
        r = lax.broadcasted_iota(jnp.int32, (seq_len, HEAD_DIM), 0) // SLC_BLOCK
        col = lax.broadcasted_iota(jnp.int32, (seq_len, HEAD_DIM), 1)
        kaug_ref[:, 0:HEAD_DIM] = ks_ref[0].astype(BF16)
        kaug_ref[:, HEAD_DIM:2 * HEAD_DIM] = jnp.where(r == col, 1.0, 0.0).astype(BF16)
        kwb_ref[...] = kw_ref[0].astype(BF16)
        for t in range(seq_len // tk):
            vst_ref[t] = vs_ref[0, t * tk:(t + 1) * tk, :].T.astype(BF16)
        for t in range(seq_len // tq):
            vwt_ref[t] = vw_ref[0, t * tq:(t + 1) * tq, :].T.astype(BF16)
        vct_ref[...] = vc_ref[0].T.astype(BF16)

    q0 = qi * tq
    qs = q_ref[0] * scale
    qb = jnp.concatenate([qs[:, g * HEAD_DIM:(g + 1) * HEAD_DIM] for g in range(GROUP)], axis=0).astype(BF16)

    def qpos(n_keys):
        return q0 + lax.broadcasted_iota(jnp.int32, (n_keys, cols), 1) % tq

    def kidx(n_keys):
        return lax.broadcasted_iota(jnp.int32, (n_keys, cols), 0)

    cend = CMP_STRIDE * kidx(nc) + (2 * CMP_STRIDE - 1)
    p_c = _masked_softmax_cols(_dot_nt(kc_ref[0].astype(BF16), qb), cend <= qpos(nc), jnp.exp2)
    o_cmp = _dot(vct_ref[...], p_c.astype(BF16))

    psum = p_c[:, 0:tq]
    for g in range(1, GROUP):
        psum = psum + p_c[:, g * tq:(g + 1) * tq]
    ji = lax.broadcasted_iota(jnp.int32, (HEAD_DIM, nc), 0)
    ci = lax.broadcasted_iota(jnp.int32, (HEAD_DIM, nc), 1)
    ratio = SLC_BLOCK // CMP_STRIDE
    member = ((ci >= ratio * ji - 1) & (ci <= ratio * ji + ratio - 1)).astype(BF16)
    imp = sum(_dot(member, part) for part in _split3(psum))
    blk = lax.broadcasted_iota(jnp.int32, (HEAD_DIM, tq), 0)
    cur = (q0 + lax.broadcasted_iota(jnp.int32, (HEAD_DIM, tq), 1)) // SLC_BLOCK
    valid = (blk <= cur) & (blk < n_blk)
    forced = valid & ((blk == 0) | (blk == cur) | (blk == cur - 1))
    score = jnp.where(forced, SEL_BONUS, jnp.where(valid, imp, -jnp.inf))
    score = score[0:n_blk]
    score_ref[...] = score
    blk = lax.broadcasted_iota(jnp.int32, (n_blk, tq), 0)
    per_tile = tq // SLC_BLOCK

    def rank_body(it, rank):
        for u in range(per_tile):
            i = it * per_tile + u
            si = score_ref[pl.ds(i, 1), :]
            later = jnp.where(blk > i, 1, 0)
            rank = rank + jnp.where(si > score, 1, jnp.where(si == score, later, 0))
        return rank

    rank = lax.fori_loop(0, qi + 1, rank_body, jnp.zeros((n_blk, tq), jnp.int32))
    bias_t = jnp.where(rank < N_SELECT, 0.0, NEG_BIG)
    if n_blk < HEAD_DIM:
        bias_t = jnp.concatenate([bias_t, jnp.zeros((HEAD_DIM - n_blk, tq), F32)], axis=0)
    bias = bias_t.T.astype(BF16)
    qaug = jnp.concatenate([qb, jnp.concatenate([bias] * GROUP, axis=0)], axis=1)

    qpos_k = qpos(tk)
    krow = kidx(tk)

    def slc_body(kt, carry, causal):
        m, l, acc = carry
        s = _dot_nt(kaug_ref[pl.ds(pl.multiple_of(kt * tk, tk), tk), :], qaug)
        if causal:
            s = jnp.where(kt * tk + krow <= qpos_k, s, -jnp.inf)
        m_new = jnp.maximum(m, jnp.max(s, axis=0, keepdims=True))
        alpha = jnp.exp2(m - m_new)
        p = jnp.exp2(s - m_new)
        l = alpha * l + jnp.sum(p, axis=0, keepdims=True)
        acc = alpha * acc + _dot(vst_ref[kt], p.astype(BF16))
        return m_new, l, acc

    init = (jnp.full((1, cols), -jnp.inf, F32), jnp.zeros((1, cols), F32), jnp.zeros((HEAD_DIM, cols), F32))
    n_full = q0 // tk

    def two_tiles(i, carry):
        return slc_body(2 * i + 1, slc_body(2 * i, carry, False), False)

    carry = lax.fori_loop(0, n_full // 2, two_tiles, init)
    carry = lax.fori_loop(n_full - n_full % 2, n_full, functools.partial(slc_body, causal=False), carry)
    _, l_s, acc_s = slc_body(n_full, carry, True)
    o_slc = acc_s / jnp.maximum(l_s, 1e-30)

    wk = min(WINDOW + tq, seq_len)
    k0 = jnp.minimum(jnp.maximum(q0 - WINDOW, 0), seq_len - wk)
    d = qpos(wk) - (k0 + kidx(wk))
    s_w = _dot_nt(kwb_ref[pl.ds(pl.multiple_of(k0, tq), wk), :], qb)
    s_w = jnp.where((d >= 0) & (d <= WINDOW), s_w, -jnp.inf)
    p_w = jnp.exp2(s_w - jnp.max(s_w, axis=0, keepdims=True))
    l_w = jnp.sum(p_w, axis=0, keepdims=True)
    p_wb = p_w.astype(BF16)
    t0 = k0 // tq
    o_win = sum(_dot(vwt_ref[t0 + i], p_wb[i * tq:(i + 1) * tq]) for i in range(wk // tq))
    o_win = o_win / jnp.maximum(l_w, 1e-30)

    gates = jax.nn.sigmoid(gt_ref[0]).T
    for g in range(GROUP):
        cs = slice(g * tq, (g + 1) * tq)
        c0 = g * N_BRANCH
        o = (gates[c0:c0 + 1] * o_cmp[:, cs] + gates[c0 + 1:c0 + 2] * o_slc[:, cs]
             + gates[c0 + 2:c0 + 3] * o_win[:, cs])
        o_ref[0, :, g * HEAD_DIM:(g + 1) * HEAD_DIM] = o.T.astype(o_ref.dtype)


def nsa_prompt_t(zq3, gates3, k_cmp, v_cmp, rows3):
    B, L, _ = zq3.shape
    tq = _pick(L, (256, 128))
    tk = _pick(L, (512, 256, 128))
    assert WINDOW % tq == 0 and tk % tq == 0
    n_sub = k_cmp.shape[1]
    qw = GROUP * HEAD_DIM
    rspec = lambda kind: pl.BlockSpec((1, L, HEAD_DIM), lambda b, n, i: (b, 0, kind * N_KV + n))
    cspec = pl.BlockSpec((1, n_sub, HEAD_DIM), lambda b, n, i: (b * N_KV + n, 0, 0))
    return pl.pallas_call(
        functools.partial(_nsa_prompt_t_kernel, tq=tq, tk=tk, seq_len=L),
        grid=(B, N_KV, L // tq),
        in_specs=[
            pl.BlockSpec((1, tq, qw), lambda b, n, i: (b, i, n)),
            pl.BlockSpec((1, tq, HEAD_DIM), lambda b, n, i: (b, i, n)),
            cspec, cspec, rspec(2), rspec(3), rspec(4), rspec(5),
        ],
        out_specs=pl.BlockSpec((1, tq, qw), lambda b, n, i: (b, i, n)),
        out_shape=jax.ShapeDtypeStruct((B, L, MIX_DIM), BF16),
        scratch_shapes=[
            pltpu.VMEM((L, 2 * HEAD_DIM), BF16),
            pltpu.VMEM((L // tk, HEAD_DIM, tk), BF16),
            pltpu.VMEM((L, HEAD_DIM), BF16),
            pltpu.VMEM((L // tq, HEAD_DIM, tq), BF16),
            pltpu.VMEM((HEAD_DIM, n_sub), BF16),
            pltpu.VMEM((L // SLC_BLOCK, tq), F32),
        ],
        compiler_params=_cp(("parallel", "parallel", "arbitrary")),
        name="nsa_prompt",
    )(zq3, gates3, k_cmp, v_cmp, rows3, rows3, rows3, rows3)


def _dec_cmp_kernel(pt_ref, *refs, n_pg, n_groups):
    del pt_ref
    pg_refs = refs[:n_pg]
    new_ref, w1_ref, w2_ref, pe_ref, kc_ref, vc_ref, lhs_ref, carry_ref = refs[n_pg:]
    g = pl.program_id(1)
    sub_pg = PAGE_SIZE // CMP_STRIDE
    n_sub = n_pg * sub_pg

    @pl.when(g == 0)
    def _():
        carry_ref[...] = jnp.zeros_like(carry_ref)

    @pl.when(g < n_groups)
    def _():
        for r in range(0, n_pg, 2):
            for p in range(CMP_STRIDE):
                pair = [jnp.swapaxes(pg_refs[r + e][0, pl.ds(p, sub_pg, stride=CMP_STRIDE), :, :], 0, 1)
                        for e in range(2)]
                lhs_ref[:, r * sub_pg:(r + 2) * sub_pg, p * HEAD_DIM:(p + 1) * HEAD_DIM] = (
                    jnp.concatenate(pair, axis=1).astype(BF16))

    @pl.when(g >= n_groups)
    def _():
        lhs_ref[...] = jnp.zeros_like(lhs_ref)

    @pl.when(g == n_groups)
    def _():
        first = lax.broadcasted_iota(jnp.int32, (8, 16, HEAD_DIM), 1) == 0
        lhs_ref[:, 0:16, 0:HEAD_DIM] = jnp.where(first, new_ref[0][:, None, :], 0.0).astype(BF16)

    row = lax.broadcasted_iota(jnp.int32, (n_sub, CMP_HID), 0)
    for kind, out_ref in enumerate((kc_ref, vc_ref)):
        x = lhs_ref[kind * N_KV:(kind + 1) * N_KV].reshape(N_KV * n_sub, CMP_STRIDE * HEAD_DIM)
        y = _dot(x, w1_ref[kind])
        for n in range(N_KV):
            slot = kind * N_KV + n
            a = y[n * n_sub:(n + 1) * n_sub, 0:CMP_HID]
            bm = y[n * n_sub:(n + 1) * n_sub, CMP_HID:2 * CMP_HID]
            a_prev = jnp.where(row == 0, carry_ref[slot:slot + 1, :], pltpu.roll(a, 1, 0))
            carry_ref[slot:slot + 1, :] = a[n_sub - 1:n_sub]
            h = a_prev + bm + pe_ref[kind, 0:1]
            out_ref[0, n] = _dot(jax.nn.gelu(h).astype(BF16), w2_ref[kind])


def dec_cmp_tokens(cache4, page_table, new_cmp, w1ab, w2, pe_h):
    B, n_pages = page_table.shape
    n_pg = _pick(n_pages, (8, 4, 2))
    assert n_pages % n_pg == 0
    n_groups = n_pages // n_pg
    sub_pg = PAGE_SIZE // CMP_STRIDE
    n_sub = n_pg * sub_pg
    n_steps = n_groups + 1
    while (n_steps * n_sub) % HEAD_DIM:
        n_steps += 1
    half = CMP_STRIDE * HEAD_DIM

    def page_spec(r):
        def imap(b, g, pt):
            return (pt[b * n_pages + jnp.minimum(g * n_pg + r, n_pages - 1)], 0, 0, 0)
        return pl.BlockSpec((1, PAGE_SIZE, 8, HEAD_DIM), imap)

    out_spec = pl.BlockSpec((1, N_KV, n_sub, HEAD_DIM), lambda b, g, pt: (b, 0, g, 0))
    grid_spec = pltpu.PrefetchScalarGridSpec(
        num_scalar_prefetch=1,
        grid=(B, n_steps),
        in_specs=[page_spec(r) for r in range(n_pg)] + [
            pl.BlockSpec((1, 8, HEAD_DIM), lambda b, g, pt: (b, 0, 0)),
            pl.BlockSpec((2, half, 2 * CMP_HID), lambda b, g, pt: (0, 0, 0)),
            pl.BlockSpec((2, CMP_HID, HEAD_DIM), lambda b, g, pt: (0, 0, 0)),
            pl.BlockSpec((2, 8, CMP_HID), lambda b, g, pt: (0, 0, 0)),
        ],
        out_specs=[out_spec, out_spec],
        scratch_shapes=[pltpu.VMEM((8, n_sub, half), BF16), pltpu.VMEM((8, CMP_HID), F32)],
    )
    tok = jax.ShapeDtypeStruct((B, N_KV, n_steps * n_sub, HEAD_DIM), F32)
    return pl.pallas_call(
        functools.partial(_dec_cmp_kernel, n_pg=n_pg, n_groups=n_groups),
        grid_spec=grid_spec,
        out_shape=[tok, tok],
        compiler_params=_cp(("parallel", "arbitrary")),
        name="dec_cmp_tokens",
    )(page_table.reshape(-1), *([cache4] * n_pg), new_cmp, w1ab, w2, pe_h)


def _dec_select_kernel(q_ref, kc_ref, vc_ref, mem_ref, ocmp_ref, idx_ref, *, q_pos, nc, n_blk, n_sel):
    C = kc_ref.shape[2]
    JB = mem_ref.shape[1]
    scale = HEAD_DIM ** -0.5
    c_i = lax.broadcasted_iota(jnp.int32, (8, C), 1) - 1
    g_i = lax.broadcasted_iota(jnp.int32, (8, C), 0)
    ok = (c_i >= 0) & (c_i < nc) & (CMP_STRIDE * c_i + (2 * CMP_STRIDE - 1) <= q_pos)
    psums = []
    for n in range(N_KV):
        qb = (q_ref[0, n] * scale).astype(BF16)
        p = _masked_softmax_rows(_dot_nt(qb, kc_ref[0, n].astype(BF16)), ok)
        ocmp_ref[0, n] = _dot(p.astype(BF16), vc_ref[0, n].astype(BF16))
        psums.append(jnp.sum(jnp.where(g_i < GROUP, p, 0.0), axis=0, keepdims=True))
    psum = jnp.concatenate(psums + [jnp.zeros((8 - N_KV, C), F32)], axis=0)
    imp = sum(_dot(part, mem_ref[...]) for part in _split3(psum))
    j_i = lax.broadcasted_iota(jnp.int32, (8, JB), 1)
    cur = q_pos // SLC_BLOCK
    valid = (j_i <= cur) & (j_i < n_blk)
    forced = valid & ((j_i == 0) | (j_i == cur) | (j_i == cur - 1))
    score = jnp.where(forced, SEL_BONUS, jnp.where(valid, imp, -jnp.inf))
    j_f = j_i.astype(F32)
    taken = j_i >= n_blk
    lane_o = lax.broadcasted_iota(jnp.int32, (8, HEAD_DIM), 1)
    out = jnp.zeros((8, HEAD_DIM), F32)
    for r in range(n_sel):
        m = jnp.max(jnp.where(taken, -jnp.inf, score), axis=-1, keepdims=True)
        cand = jnp.logical_not(taken) & (score == m)
        idx = jnp.min(jnp.where(cand, j_f, float(JB)), axis=-1, keepdims=True)
        out = jnp.where(lane_o == r, idx, out)
        taken = taken | (j_f == idx)
    idx_ref[0] = out.astype(jnp.int32)


def dec_select(q8, k_cmp, v_cmp, member, *, q_pos, nc, n_blk, n_sel):
    B, _, C, _ = k_cmp.shape
    JB = member.shape[1]
    head_spec = pl.BlockSpec((1, N_KV, 8, HEAD_DIM), lambda b: (b, 0, 0, 0))
    tok_spec = pl.BlockSpec((1, N_KV, C, HEAD_DIM), lambda b: (b, 0, 0, 0))
    return pl.pallas_call(
        functools.partial(_dec_select_kernel, q_pos=q_pos, nc=nc, n_blk=n_blk, n_sel=n_sel),
        grid=(B,),
        in_specs=[head_spec, tok_spec, tok_spec, pl.BlockSpec((C, JB), lambda b: (0, 0))],
        out_specs=[head_spec, pl.BlockSpec((1, 8, HEAD_DIM), lambda b: (b, 0, 0))],
        out_shape=[
            jax.ShapeDtypeStruct((B, N_KV, 8, HEAD_DIM), F32),
            jax.ShapeDtypeStruct((B, 8, HEAD_DIM), jnp.int32),
        ],
        compiler_params=_cp(("parallel",)),
        name="dec_select",
    )(q8, k_cmp, v_cmp, member)


def _dec_attend_kernel(idx_ref, pt_ref, q_ref, gl_ref, ocmp_ref, nslc_ref, win_ref, nwin_ref, *rest,
                       q_pos, past_len, n_sel):
    del pt_ref
    blk_refs = rest[:n_sel]
    o_ref = rest[n_sel]
    b = pl.program_id(0)
    n = pl.program_id(1)
    base = (b * N_KV + n) * n_sel
    scale = HEAD_DIM ** -0.5
    past_blocks = past_len // SLC_BLOCK
    rb = SLC_BLOCK * 8
    rw = win_ref.shape[1]
    w_buf = rw // 8
    qf = q_ref[0, 0] * scale
    qb = qf.astype(BF16)
    js = [idx_ref[base + s] for s in range(n_sel)]
    has_new = js[0] == past_blocks
    for s in range(1, n_sel):
        has_new = has_new | (js[s] == past_blocks)
    new_ok = has_new & (past_len <= q_pos)

    def head_rows(x):
        slot = lax.broadcasted_iota(jnp.int32, (8, HEAD_DIM), 0)
        k = jnp.sum(jnp.where(slot == n, x, 0.0), axis=0, keepdims=True)
        v = jnp.sum(jnp.where(slot == n + N_KV, x, 0.0), axis=0, keepdims=True)
        return k, v

    def finish(m):
        return jnp.where(m == -jnp.inf, 0.0, m)

    r_s = lax.broadcasted_iota(jnp.int32, (8, rb), 1)
    t_s = r_s >> 3
    mine_s = (r_s & 7) == n

    def slc_scores(s):
        blk = blk_refs[s][0, 0].reshape(rb, HEAD_DIM)
        sc = _dot_nt(qb, blk.astype(BF16))
        ok = mine_s & (js[s] * SLC_BLOCK + t_s <= q_pos) & (js[s] < past_blocks)
        return jnp.where(ok, sc, -jnp.inf)

    k_new, v_new = head_rows(nslc_ref[0])
    s_new = jnp.where(new_ok, jnp.sum(qf * k_new, axis=-1, keepdims=True), -jnp.inf)
    scores = [slc_scores(s) for s in range(n_sel)]
    m = s_new
    for sc in scores:
        m = jnp.maximum(m, jnp.max(sc, axis=-1, keepdims=True))
    m = finish(m)
    p_new = jnp.exp(s_new - m)
    l = p_new
    acc = p_new * v_new
    for s, sc in enumerate(scores):
        p = jnp.exp(sc - m)
        l = l + jnp.sum(p, axis=-1, keepdims=True)
        vals = pltpu.roll(blk_refs[s][0, 0].reshape(rb, HEAD_DIM), rb - N_KV, 0)
        acc = acc + _dot(p.astype(BF16), vals.astype(BF16))
    o_slc = acc / jnp.maximum(l, 1e-30)

    r_w = lax.broadcasted_iota(jnp.int32, (8, rw), 1)
    w_pos = past_len - w_buf + (r_w >> 3)
    w_ok = ((r_w & 7) == n) & (q_pos - w_pos >= 0) & (q_pos - w_pos <= WINDOW) & (w_pos >= 0)
    win = win_ref[0]
    s_w = jnp.where(w_ok, _dot_nt(qb, win.astype(BF16)), -jnp.inf)
    k_nw, v_nw = head_rows(nwin_ref[0])
    nw_ok = (q_pos - past_len >= 0) and (q_pos - past_len <= WINDOW)
    s_nw = jnp.sum(qf * k_nw, axis=-1, keepdims=True) if nw_ok else jnp.full((8, 1), -jnp.inf, F32)
    m = finish(jnp.maximum(s_nw, jnp.max(s_w, axis=-1, keepdims=True)))
    p_w = jnp.exp(s_w - m)
    p_nw = jnp.exp(s_nw - m)
    l = p_nw + jnp.sum(p_w, axis=-1, keepdims=True)
    acc = p_nw * v_nw + _dot(p_w.astype(BF16), pltpu.roll(win, rw - N_KV, 0).astype(BF16))
    o_win = acc / jnp.maximum(l, 1e-30)

    gates = jax.nn.sigmoid(gl_ref[0, 0])
    o_ref[0, 0] = gates[:, 0:1] * ocmp_ref[0, 0] + gates[:, 1:2] * o_slc + gates[:, 2:3] * o_win


def dec_attend(idx_flat, page_table, q8, gate8, ocmp8, new_slc, win2, new_win, cache5, *, q_pos, past_len, n_sel):
    B, n_pages = page_table.shape
    rw = win2.shape[1]
    past_blocks = past_len // SLC_BLOCK
    per_page = PAGE_SIZE // SLC_BLOCK

    def blk_spec(s):
        def imap(b, n, idx, pt):
            j = jnp.minimum(idx[(b * N_KV + n) * n_sel + s], past_blocks - 1)
            return (pt[b * n_pages + j // per_page], j % per_page, 0, 1, 0)
        return pl.BlockSpec((1, 1, SLC_BLOCK, 8, HEAD_DIM), imap)

    head_spec = pl.BlockSpec((1, 1, 8, HEAD_DIM), lambda b, n, idx, pt: (b, n, 0, 0))
    tok_spec = pl.BlockSpec((1, 8, HEAD_DIM), lambda b, n, idx, pt: (b, 0, 0))
    grid_spec = pltpu.PrefetchScalarGridSpec(
        num_scalar_prefetch=2,
        grid=(B, N_KV),
        in_specs=[head_spec, head_spec, head_spec, tok_spec,
                  pl.BlockSpec((1, rw, HEAD_DIM), lambda b, n, idx, pt: (b, 0, 0)), tok_spec]
        + [blk_spec(s) for s in range(n_sel)],
        out_specs=head_spec,
    )
    return pl.pallas_call(
        functools.partial(_dec_attend_kernel, q_pos=q_pos, past_len=past_len, n_sel=n_sel),
        grid_spec=grid_spec,
        out_shape=jax.ShapeDtypeStruct((B, N_KV, 8, HEAD_DIM), F32),
        compiler_params=_cp(("parallel", "arbitrary")),
        name="dec_attend",
    )(idx_flat, page_table.reshape(-1), q8, gate8, ocmp8, new_slc, win2, new_win, *([cache5] * n_sel))


def _pad_to(a, axis, size):
    pad = [(0, 0)] * a.ndim
    pad[axis] = (0, size - a.shape[axis])
    return jnp.pad(a, pad)


def _prep_weights(w_in_a, w_in_b, b_gate, w_kv, cmp_pe, cmp_w1, cmp_w2, w_mem_kv, w_out, w_up, conv_w, conv_b,
                  w_down, hgrn_lb):
    depth = w_out.shape[0]
    n_b = w_in_b.shape[0]
    n_gate = N_MIX_HEADS * N_BRANCH
    per_kv = GROUP * N_BRANCH
    half = CMP_STRIDE * HEAD_DIM
    w1 = cmp_w1.astype(BF16)
    lb = jnp.cumsum(jax.nn.softmax(hgrn_lb.astype(F32), axis=0), axis=0)
    lb = lb - lb[0]
    layers = lambda f, n=depth: [f(l) for l in range(n)]
    gate_w = lambda j: w_in_b[j][:, MIX_DIM:MIX_DIM + n_gate].reshape(D_MODEL, N_KV, per_kv)
    return dict(
        in_a=cast_bf16(w_in_a),
        in_b_qm=layers(lambda j: jnp.concatenate([w_in_b[j][:, :MIX_DIM], w_in_b[j][:, MIX_DIM + n_gate:]],
                                                 axis=-1).astype(BF16), n_b),
        in_b_gate=layers(lambda j: _pad_to(gate_w(j), 2, HEAD_DIM).reshape(D_MODEL, N_KV * HEAD_DIM).astype(BF16), n_b),
        b_gate=layers(lambda j: _pad_to(b_gate[j].reshape(N_KV, per_kv), 1, HEAD_DIM).reshape(N_KV * HEAD_DIM), n_b),
        kv=cast_bf16(w_kv[None]),
        mem_kv=cast_bf16(w_mem_kv),
        out=cast_bf16(w_out),
        up=cast_split_halves(w_up, D_FF_PAD),
        conv_w=layers(lambda l: _pad_to(_pad_to(conv_w[l], 1, D_FF_PAD), 0, 8)),
        conv_b=layers(lambda l: _pad_to(conv_b[l], 0, D_FF_PAD).reshape(1, D_FF_PAD)),
        down=cast_pad_rows(w_down, D_FF_PAD),
        cmp_w1=w1,
        cmp_w1ab=jnp.concatenate([w1[:, :half], w1[:, half:]], axis=2),
        cmp_w2=cmp_w2.astype(BF16),
        cmp_pe=cmp_pe.reshape(2, 1, 2 * half),
        log_lb=jnp.log(lb),
        log1m_lb=jnp.log1p(-lb),
    )


def kernel(x_prompt, x_sample, mem_prompt, cache_nsa_kv, page_table, cache_win_kv, state_hgrn, state_conv,
           cache_mem_kv, norm_mix, norm_ffn, norm_mem, norm_kv, norm_final, w_in_a, hgrn_lb, hgrn_onorm,
           w_in_b, b_gate, w_kv, cmp_pe, cmp_w1, cmp_w2, w_mem_kv, w_out, w_up, conv_w, conv_b, w_down):
    B, L, _ = x_prompt.shape
    Bs = x_sample.shape[0]
    depth = w_out.shape[0]
    n_a = w_in_a.shape[0]
    n_mem = mem_prompt.shape[1]
    n_pool, page, _, _, _ = cache_nsa_kv.shape
    n_pages = page_table.shape[1]
    past_len = n_pages * page
    w_buf = cache_win_kv.shape[1]
    assert x_sample.shape[1] == 1 and page == PAGE_SIZE and L % 128 == 0 and L // SLC_BLOCK <= HEAD_DIM
    W = _prep_weights(w_in_a, w_in_b, b_gate, w_kv, cmp_pe, cmp_w1, cmp_w2, w_mem_kv, w_out, w_up, conv_w,
                      conv_b, w_down, hgrn_lb)
    pe_h = pe_proj(W["cmp_pe"], W["cmp_w1"])
    onorm = hgrn_onorm.reshape(n_a, 1, MIX_DIM)
    log_lb = W["log_lb"].reshape(n_a, 1, MIX_DIM)
    log1m_lb = W["log1m_lb"].reshape(n_a, 1, MIX_DIM)

    def ffn(x, l, **kw):
        return conv_ffn(x, norm_ffn[l], W["up"], W["conv_w"][l], W["conv_b"][l], W["down"], l, **kw)

    M = B * L
    mem_flat = mem_prompt.reshape(B * n_mem, D_MODEL)
    mem_kv_p = [rms_matmul(mem_flat, norm_mem[l], W["mem_kv"], lead=(l,)).reshape(B, n_mem, 2 * MEM_DIM)
                for l in range(depth)]
    x = x_prompt.reshape(M, D_MODEL)
    hgrn_p, conv_p = [], []
    for l in range(depth):
        if l < n_a:
            z3 = rms_matmul(x, norm_mix[l], W["in_a"], lead=(l,)).reshape(B, L, -1)
            s0 = jnp.zeros((B, N_MIX_HEADS, HEAD_DIM, HEAD_DIM), F32)
            o_mix, s_new = hgrn_prompt(z3, log_lb[l], log1m_lb[l], onorm[l], s0)
            hgrn_p.append(s_new)
            o_mem = mem_attention(z3, 4 * MIX_DIM // MEM_DIM, mem_kv_p[l])
        else:
            j = l - n_a
            if j == 0:
                rows_p = rms_matmul(x, norm_kv, W["kv"], lead=(0,))
                rows3 = rows_p.reshape(B, L, -1)
                n_sub = L // CMP_STRIDE
                xsub = rows_p[:, :2 * N_KV * HEAD_DIM].reshape(B, n_sub, CMP_STRIDE, 2, N_KV, HEAD_DIM)
                xsub = xsub.transpose(3, 0, 4, 1, 2, 5).reshape(2, B * N_KV, n_sub, CMP_STRIDE * HEAD_DIM)
                cmp_p = cmp_mlp(xsub, W["cmp_w1"], W["cmp_w2"], pe_h)
            zq3 = rms_matmul(x, norm_mix[l], W["in_b_qm"][j]).reshape(B, L, -1)
            gt3 = rms_matmul(x, norm_mix[l], W["in_b_gate"][j], W["b_gate"][j]).reshape(B, L, -1)
            o_mix = nsa_prompt_t(zq3, gt3, cmp_p[0], cmp_p[1], rows3)
            o_mem = mem_attention(zq3, MIX_DIM // MEM_DIM, mem_kv_p[l])
        x = out_proj(o_mix.reshape(M, MIX_DIM), o_mem.reshape(M, MEM_DIM), W["out"], l, x)
        x, a_tail = ffn(x, l, seq_len=L)
        conv_p.append(a_tail.reshape(B, -1, 8, D_FF_PAD)[:, -1, 8 - (CONV_W - 1):, :D_FF])
    y_prompt = rmsnorm_rows(x, norm_final).reshape(B, L, D_MODEL)
    n_row4 = 4 * N_KV * HEAD_DIM
    w_keep = min(WINDOW, L)
    nsa_rows_prompt = rows3[:, :, :n_row4].reshape(B, L, 4, N_KV, HEAD_DIM)
    win_prompt = rows3[:, L - w_keep:, n_row4:].reshape(B, w_keep, 2, N_KV, HEAD_DIM)
    mem_kv_prompt = jnp.stack(mem_kv_p).reshape(depth, B, n_mem, 2, N_MEM_HEADS, HEAD_DIM)

    Ms = 16
    pad_rows = lambda a: _pad_to(a, 0, Ms)
    xs = pad_rows(x_sample.reshape(Bs, D_MODEL))
    hgrn_s, conv_s = [], []
    q_pos = past_len
    t_pad = -(-(past_len + 1) // SLC_BLOCK) * SLC_BLOCK
    n_blk = t_pad // SLC_BLOCK
    nc = t_pad // CMP_STRIDE - 1
    n_sel = min(N_SELECT, n_blk)
    for l in range(depth):
        if l < n_a:
            z = rms_matmul(xs, norm_mix[l], W["in_a"], lead=(l,))
            z3 = z[:Bs].reshape(Bs, 1, -1)
            o_mix, s_new = hgrn_step(z3, log_lb[l], log1m_lb[l], onorm[l], state_hgrn[l])
            hgrn_s.append(s_new)
            o_mix = o_mix.reshape(Bs, MIX_DIM)
            o_mem = mem_attention(z3, 4 * MIX_DIM // MEM_DIM, cache_mem_kv[l].reshape(Bs, n_mem, 2 * MEM_DIM))
        else:
            j = l - n_a
            if j == 0:
                rows_s = rms_matmul(xs, norm_kv, W["kv"], lead=(0,))[:Bs]
                new_cmp = rows_s[:, 0:1024].reshape(Bs, 8, HEAD_DIM)
                new_slc = rows_s[:, 1024:2048].reshape(Bs, 8, HEAD_DIM)
                new_win = rows_s[:, 2048:3072].reshape(Bs, 8, HEAD_DIM)
                cache4 = cache_nsa_kv.reshape(n_pool, PAGE_SIZE, 4 * N_KV, HEAD_DIM)
                cache5 = cache_nsa_kv.reshape(n_pool, PAGE_SIZE // SLC_BLOCK, SLC_BLOCK, 4 * N_KV, HEAD_DIM)
                win2 = cache_win_kv.reshape(Bs, w_buf * 2 * N_KV, HEAD_DIM)
                k_cmp_s, v_cmp_s = dec_cmp_tokens(cache4, page_table, new_cmp, W["cmp_w1ab"], W["cmp_w2"], pe_h)
                assert k_cmp_s.shape[2] > nc + 1
                ratio = SLC_BLOCK // CMP_STRIDE
                c_of_row = jnp.arange(k_cmp_s.shape[2])[:, None] - 1
                j_of_col = jnp.arange(-(-n_blk // HEAD_DIM) * HEAD_DIM)[None, :]
                member = ((c_of_row >= ratio * j_of_col - 1) & (c_of_row <= ratio * j_of_col + ratio - 1)).astype(BF16)
            zq = rms_matmul(xs, norm_mix[l], W["in_b_qm"][j])
            gt = rms_matmul(xs, norm_mix[l], W["in_b_gate"][j], W["b_gate"][j])[:Bs]
            z3 = zq[:Bs].reshape(Bs, 1, -1)
            per_head = lambda a: _pad_to(a.reshape(Bs, N_KV, GROUP, -1), 2, 8)
            q8 = per_head(zq[:Bs, :MIX_DIM])
            o_cmp, idx = dec_select(q8, k_cmp_s, v_cmp_s, member, q_pos=q_pos, nc=nc, n_blk=n_blk, n_sel=n_sel)
            idx_flat = idx[:, :N_KV, :n_sel].reshape(-1)
            gate9 = gt.reshape(Bs, N_KV, HEAD_DIM)[:, :, :GROUP * N_BRANCH]
            o_all = dec_attend(idx_flat, page_table, q8, _pad_to(per_head(gate9), 3, HEAD_DIM), o_cmp,
                               new_slc, win2, new_win, cache5, q_pos=q_pos, past_len=past_len, n_sel=n_sel)
            o_mix = o_all[:, :, :GROUP].reshape(Bs, MIX_DIM).astype(BF16)
            o_mem = mem_attention(z3, MIX_DIM // MEM_DIM, cache_mem_kv[l].reshape(Bs, n_mem, 2 * MEM_DIM))
        xs = out_proj(pad_rows(o_mix), pad_rows(o_mem.reshape(Bs, MEM_DIM)), W["out"], l, xs)
        prev = (pad_rows(_pad_to(state_conv[l][:, 1], 1, D_FF_PAD)), pad_rows(_pad_to(state_conv[l][:, 0], 1, D_FF_PAD)))
        xs, a_new = ffn(xs, l, prev=prev)
        conv_s.append(jnp.stack([state_conv[l][:, 1], a_new[0, :Bs, :D_FF]], axis=1))
    y_sample = rmsnorm_rows(xs, norm_final)[:Bs].reshape(Bs, 1, D_MODEL)
    nsa_rows_sample = rows_s[:, :n_row4].reshape(Bs, 1, 4, N_KV, HEAD_DIM)
    win_new = rows_s[:, n_row4:].reshape(Bs, 1, 2, N_KV, HEAD_DIM).astype(cache_win_kv.dtype)
    win_sample = jnp.concatenate([cache_win_kv, win_new], axis=1)[:, 1:]

    return (y_prompt, y_sample, nsa_rows_prompt, nsa_rows_sample, win_prompt, win_sample,
            jnp.stack(hgrn_p), jnp.stack(hgrn_s), jnp.stack(conv_p), jnp.stack(conv_s), mem_kv_prompt)
```

```python
import functools

import jax
import jax.numpy as jnp
from jax import lax
from jax.experimental import pallas as pl
from jax.experimental.pallas import tpu as pltpu

F32 = jnp.float32
BF16 = jnp.bfloat16

D_MODEL = 2048
HEAD_DIM = 128
MIX_DIM = 1536
N_MIX_HEADS = 12
MEM_DIM = 512
N_MEM_HEADS = 4
N_KV = 4
GROUP = 3
N_BRANCH = 3
CMP_STRIDE = 16
CMP_HID = 256
SLC_BLOCK = 64
N_SELECT = 16
WINDOW = 512
PAGE_SIZE = 128
D_FF = 5504
D_FF_PAD = 5632
CONV_W = 3
RMS_EPS = 1e-6
SEL_BONUS = 1e9
NEG_BIG = -1e30

HGRN_T = 64
HGRN_SUB = 8
HGRN_HEADS_PER_STEP = 4
FFN_TF = 512
VMEM_LIMIT = 56 * 1024 * 1024


def _cp(sem, vmem=VMEM_LIMIT):
    return pltpu.CompilerParams(dimension_semantics=sem, vmem_limit_bytes=vmem)


def _dot(a, b):
    return jnp.dot(a, b, preferred_element_type=F32)


def _dot_nt(a, b):
    return lax.dot_general(a, b, (((1,), (1,)), ((), ())), preferred_element_type=F32)


def _dot_tn(a, b):
    return lax.dot_general(a, b, (((0,), (0,)), ((), ())), preferred_element_type=F32)


def _pick(n, cands):
    for c in cands:
        if n % c == 0:
            return c
    return n


def _rms_matmul_kernel(x_ref, g_ref, w_ref, b_ref, o_ref, xn_ref):
    @pl.when(pl.program_id(1) == 0)
    def _():
        x = x_ref[...]
        y = x * lax.rsqrt(jnp.mean(x * x, axis=-1, keepdims=True) + RMS_EPS)
        xn_ref[...] = (y * g_ref[...]).astype(BF16)

    o_ref[...] = _dot(xn_ref[...], w_ref[...]) + b_ref[...]


def _layer_spec(lead, block, imap):
    return pl.BlockSpec((None,) * len(lead) + tuple(block), lambda *a: tuple(lead) + tuple(imap(*a)))


def rms_matmul(x, g, w, bias=None, lead=()):
    M, K = x.shape
    N = w.shape[-1]
    tm = _pick(M, (1024, 512, 256, 128))
    tn = _pick(N, (512, 256, 128))
    if bias is None:
        bias = jnp.zeros((N,), F32)
    return pl.pallas_call(
        _rms_matmul_kernel,
        grid=(M // tm, N // tn),
        in_specs=[
            pl.BlockSpec((tm, K), lambda i, j: (i, 0)),
            pl.BlockSpec((1, K), lambda i, j: (0, 0)),
            _layer_spec(lead, (K, tn), lambda i, j: (0, j)),
            pl.BlockSpec((1, tn), lambda i, j: (0, j)),
        ],
        out_specs=pl.BlockSpec((tm, tn), lambda i, j: (i, j)),
        out_shape=jax.ShapeDtypeStruct((M, N), F32),
        scratch_shapes=[pltpu.VMEM((tm, K), BF16)],
        compiler_params=_cp(("parallel", "arbitrary")),
        name="rms_matmul",
    )(x, g.reshape(1, K), w, bias.reshape(1, N))


def _rms_matmul_heads_kernel(x_ref, g_ref, w_ref, o_ref, oh_ref, xn_ref, *, n_head_steps):
    j = pl.program_id(1)

    @pl.when(j == 0)
    def _():
        x = x_ref[...]
        y = x * lax.rsqrt(jnp.mean(x * x, axis=-1, keepdims=True) + RMS_EPS)
        xn_ref[...] = (y * g_ref[...]).astype(BF16)

    y = _dot(xn_ref[...], w_ref[...])
    o_ref[...] = y

    @pl.when(j < n_head_steps)
    def _():
        oh_ref[...] = y.reshape(oh_ref.shape)


def rms_matmul_heads(x, g, w, lead, head_cols):
    M, K = x.shape
    N = w.shape[-1]
    tm = _pick(M, (512, 256, 128))
    tn = 8 * HEAD_DIM
    assert N % tn == 0 and head_cols % tn == 0
    n_head_steps = head_cols // tn
    return pl.pallas_call(
        functools.partial(_rms_matmul_heads_kernel, n_head_steps=n_head_steps),
        grid=(M // tm, N // tn),
        in_specs=[
            pl.BlockSpec((tm, K), lambda i, j: (i, 0)),
            pl.BlockSpec((1, K), lambda i, j: (0, 0)),
            _layer_spec(lead, (K, tn), lambda i, j: (0, j)),
        ],
        out_specs=[
            pl.BlockSpec((tm, tn), lambda i, j: (i, j)),
            pl.BlockSpec((tm, 8, HEAD_DIM), lambda i, j: (i, jnp.minimum(j, n_head_steps - 1), 0)),
        ],
        out_shape=[
            jax.ShapeDtypeStruct((M, N), F32),
            jax.ShapeDtypeStruct((M, head_cols // HEAD_DIM, HEAD_DIM), F32),
        ],
        scratch_shapes=[pltpu.VMEM((tm, K), BF16)],
        compiler_params=_cp(("parallel", "arbitrary")),
        name="rms_matmul_heads",
    )(x, g.reshape(1, K), w)


def _rmsnorm_kernel(x_ref, g_ref, o_ref):
    x = x_ref[...]
    y = x * lax.rsqrt(jnp.mean(x * x, axis=-1, keepdims=True) + RMS_EPS)
    o_ref[...] = y * g_ref[...]


def rmsnorm_rows(x, g):
    M, K = x.shape
    tm = _pick(M, (512, 256, 128))
    return pl.pallas_call(
        _rmsnorm_kernel,
        grid=(M // tm,),
        in_specs=[pl.BlockSpec((tm, K), lambda i: (i, 0)), pl.BlockSpec((1, K), lambda i: (0, 0))],
        out_specs=pl.BlockSpec((tm, K), lambda i: (i, 0)),
        out_shape=jax.ShapeDtypeStruct((M, K), F32),
        compiler_params=_cp(("parallel",)),
        name="final_rmsnorm",
    )(x, g.reshape(1, K))


def _cast_kernel(x_ref, o_ref):
    o_ref[...] = x_ref[...].astype(o_ref.dtype)


def cast_bf16(w):
    G, R, C = w.shape
    tr = _pick(R, (256, 128))
    return pl.pallas_call(
        _cast_kernel,
        grid=(G, R // tr),
        in_specs=[pl.BlockSpec((1, tr, C), lambda g, r: (g, r, 0))],
        out_specs=pl.BlockSpec((1, tr, C), lambda g, r: (g, r, 0)),
        out_shape=jax.ShapeDtypeStruct((G, R, C), BF16),
        compiler_params=_cp(("parallel", "parallel")),
        name="cast_bf16",
    )(w)


def _cast_halves_kernel(x_ref, o_ref):
    c = x_ref.shape[2]
    o_ref[0, 0, :, 0:c] = x_ref[0].astype(o_ref.dtype)
    o_ref[0, 0, :, c:] = jnp.zeros((o_ref.shape[2], o_ref.shape[3] - c), o_ref.dtype)


def cast_split_halves(w, cols_pad):
    G, R, C2 = w.shape
    C = C2 // 2
    tr = _pick(R, (256, 128))
    return pl.pallas_call(
        _cast_halves_kernel,
        grid=(G, 2, R // tr),
        in_specs=[pl.BlockSpec((1, tr, C), lambda g, h, r: (g, r, h))],
        out_specs=pl.BlockSpec((1, 1, tr, cols_pad), lambda g, h, r: (g, h, r, 0)),
        out_shape=jax.ShapeDtypeStruct((G, 2, R, cols_pad), BF16),
        compiler_params=_cp(("parallel", "parallel", "parallel")),
        name="cast_split_halves",
    )(w)


def _cast_pad_rows_kernel(x_ref, o_ref):
    r = x_ref.shape[1]
    o_ref[0, 0:r, :] = x_ref[0].astype(o_ref.dtype)
    o_ref[0, r:, :] = jnp.zeros((o_ref.shape[1] - r, o_ref.shape[2]), o_ref.dtype)


def cast_pad_rows(w, rows_pad):
    G, R, C = w.shape
    tc = _pick(C, (256, 128))
    assert R % 16 == 0 and rows_pad % 16 == 0
    return pl.pallas_call(
        _cast_pad_rows_kernel,
        grid=(G, C // tc),
        in_specs=[pl.BlockSpec((1, R, tc), lambda g, c: (g, 0, c))],
        out_specs=pl.BlockSpec((1, rows_pad, tc), lambda g, c: (g, 0, c)),
        out_shape=jax.ShapeDtypeStruct((G, rows_pad, C), BF16),
        compiler_params=_cp(("parallel", "parallel")),
        name="cast_pad_rows",
    )(w)


def _outproj_kernel(om_ref, oe_ref, w1_ref, w2_ref, x_ref, y_ref):
    y_ref[...] = x_ref[...] + _dot(om_ref[...], w1_ref[...]) + _dot(oe_ref[...], w2_ref[...])


def out_proj(o_mix, o_mem, w_out, layer, x):
    M = x.shape[0]
    tm = _pick(M, (1024, 512, 256, 128))
    tn = 512
    assert MIX_DIM % MEM_DIM == 0
    return pl.pallas_call(
        _outproj_kernel,
        grid=(M // tm, D_MODEL // tn),
        in_specs=[
            pl.BlockSpec((tm, MIX_DIM), lambda i, j: (i, 0)),
            pl.BlockSpec((tm, MEM_DIM), lambda i, j: (i, 0)),
            _layer_spec((layer,), (MIX_DIM, tn), lambda i, j: (0, j)),
            _layer_spec((layer,), (MEM_DIM, tn), lambda i, j: (MIX_DIM // MEM_DIM, j)),
            pl.BlockSpec((tm, tn), lambda i, j: (i, j)),
        ],
        out_specs=pl.BlockSpec((tm, tn), lambda i, j: (i, j)),
        out_shape=jax.ShapeDtypeStruct((M, D_MODEL), F32),
        compiler_params=_cp(("parallel", "parallel")),
        name="out_proj",
    )(o_mix, o_mem, w_out, w_out, x)


def _mem_attn_kernel(q_ref, kv_ref, o_ref):
    scale = HEAD_DIM ** -0.5
    for h in range(N_MEM_HEADS):
        q = q_ref[0, :, h * HEAD_DIM:(h + 1) * HEAD_DIM].astype(BF16)
        k = kv_ref[0, :, h * HEAD_DIM:(h + 1) * HEAD_DIM].astype(BF16)
        v = kv_ref[0, :, MEM_DIM + h * HEAD_DIM:MEM_DIM + (h + 1) * HEAD_DIM].astype(BF16)
        s = _dot_nt(q, k) * scale
        m = jnp.max(s, axis=-1, keepdims=True)
        p = jnp.exp(s - m)
        p = p / jnp.sum(p, axis=-1, keepdims=True)
        o_ref[0, :, h * HEAD_DIM:(h + 1) * HEAD_DIM] = _dot(p.astype(BF16), v).astype(o_ref.dtype)


def mem_attention(z3, col_block, kv):
    B, L, _ = z3.shape
    n_mem = kv.shape[1]
    tm = _pick(L, (1024, 512, 256, 128))
    return pl.pallas_call(
        _mem_attn_kernel,
        grid=(B, L // tm),
        in_specs=[
            pl.BlockSpec((1, tm, MEM_DIM), lambda b, i: (b, i, col_block)),
            pl.BlockSpec((1, n_mem, 2 * MEM_DIM), lambda b, i: (b, 0, 0)),
        ],
        out_specs=pl.BlockSpec((1, tm, MEM_DIM), lambda b, i: (b, i, 0)),
        out_shape=jax.ShapeDtypeStruct((B, L, MEM_DIM), BF16),
        compiler_params=_cp(("parallel", "parallel")),
        name="mem_attention",
    )(z3, kv)


def _ffn_kernel(*refs, blocks_per_seq, decode):
    if decode:
        (x_ref, g_ref, wa_ref, wu_ref, cw_ref, cb_ref, wd_ref, p1_ref, p2_ref,
         y_ref, at_ref, xn_ref, acc_ref) = refs
    else:
        (x_ref, g_ref, wa_ref, wu_ref, cw_ref, cb_ref, wd_ref,
         y_ref, at_ref, xn_ref, acc_ref, carry_ref) = refs
    i = pl.program_id(0)
    j = pl.program_id(1)

    @pl.when(j == 0)
    def _():
        x = x_ref[...]
        y = x * lax.rsqrt(jnp.mean(x * x, axis=-1, keepdims=True) + RMS_EPS)
        xn_ref[...] = (y * g_ref[...]).astype(BF16)
        acc_ref[...] = jnp.zeros_like(acc_ref)

    if not decode:
        @pl.when((i % blocks_per_seq) == 0)
        def _():
            carry_ref[j] = jnp.zeros(carry_ref.shape[1:], F32)

    xn = xn_ref[...]
    tm = xn.shape[0]
    n_tail = at_ref.shape[1]
    half = wa_ref.shape[1] // 2
    down = None
    for cs in (slice(0, half), slice(half, 2 * half)):
        a = _dot(xn, wa_ref[:, cs])
        u = _dot(xn, wu_ref[:, cs])
        if decode:
            a1 = p1_ref[:, cs]
            a2 = p2_ref[:, cs]
        else:
            prev = carry_ref[j, :, cs]
            row = lax.broadcasted_iota(jnp.int32, a.shape, 0)
            a1 = jnp.where(row == 0, prev[7:8], pltpu.roll(a, 1, 0))
            a2 = jnp.where(row == 0, prev[6:7], jnp.where(row == 1, prev[7:8], pltpu.roll(a, 2, 0)))
            carry_ref[j, :, cs] = a[tm - 8:tm]
        at_ref[0, :, cs] = a[tm - n_tail:tm]
        c = cb_ref[:, cs] + cw_ref[2:3, cs] * a
        c = c + cw_ref[0:1, cs] * a2
        c = c + cw_ref[1:2, cs] * a1
        h = (jax.nn.gelu(c) * u).astype(BF16)
        d = _dot(h, wd_ref[cs, :])
        down = d if down is None else down + d
    acc_ref[...] += down

    @pl.when(j == pl.num_programs(1) - 1)
    def _():
        y_ref[...] = x_ref[...] + acc_ref[...]


def conv_ffn(x, g, w_up, cw, cb, w_down, layer, *, seq_len=None, prev=None):
    M = x.shape[0]
    decode = prev is not None
    tm = M if decode else _pick(seq_len, (1024, 512, 256, 128))
    once = pl.Buffered(1)
    n_tail = tm if decode else 8
    nf = D_FF_PAD // FFN_TF
    in_specs = [
        pl.BlockSpec((tm, D_MODEL), lambda i, j: (i, 0), pipeline_mode=once),
        pl.BlockSpec((1, D_MODEL), lambda i, j: (0, 0)),
        _layer_spec((layer, 0), (D_MODEL, FFN_TF), lambda i, j: (0, j)),
        _layer_spec((layer, 1), (D_MODEL, FFN_TF), lambda i, j: (0, j)),
        pl.BlockSpec((8, FFN_TF), lambda i, j: (0, j)),
        pl.BlockSpec((1, FFN_TF), lambda i, j: (0, j)),
        _layer_spec((layer,), (FFN_TF, D_MODEL), lambda i, j: (j, 0)),
    ]
    args = [x, g.reshape(1, D_MODEL), w_up, w_up, cw, cb, w_down]
    scratch = [pltpu.VMEM((tm, D_MODEL), BF16), pltpu.VMEM((tm, D_MODEL), F32)]
    if decode:
        in_specs += [pl.BlockSpec((tm, FFN_TF), lambda i, j: (i, j))] * 2
        args += list(prev)
    else:
        scratch.append(pltpu.VMEM((nf, 8, FFN_TF), F32))
    return pl.pallas_call(
        functools.partial(_ffn_kernel, blocks_per_seq=(1 if decode else seq_len // tm), decode=decode),
        grid=(M // tm, nf),
        in_specs=in_specs,
        out_specs=[
            pl.BlockSpec((tm, D_MODEL), lambda i, j: (i, 0), pipeline_mode=once),
            pl.BlockSpec((1, n_tail, FFN_TF), lambda i, j: (i, 0, j)),
        ],
        out_shape=[
            jax.ShapeDtypeStruct((M, D_MODEL), F32),
            jax.ShapeDtypeStruct((M // tm, n_tail, D_FF_PAD), F32),
        ],
        scratch_shapes=scratch,
        compiler_params=_cp(("arbitrary", "arbitrary")),
        name="conv_ffn",
    )(*args)


def _hgrn_gates(q, f, log_lb, log1m_lb):
    qs = jax.nn.silu(q)
    log_f = jnp.logaddexp(log_lb, log1m_lb + jax.nn.log_sigmoid(f))
    k = 1.0 - jnp.exp(log_f)
    return qs, k, log_f


def _hgrn_out(o, g, onorm):
    y = o * lax.rsqrt(jnp.mean(o * o, axis=-1, keepdims=True) + RMS_EPS)
    return (y * onorm) * jax.nn.silu(g)


def _hgrn_kernel(q_ref, f_ref, i_ref, g_ref, llb_ref, l1m_ref, on_ref, s0_ref, o_ref, s_ref, st_ref,
                 *, n_chunks):
    c = pl.program_id(2)
    T = HGRN_T
    n_heads = st_ref.shape[0]

    @pl.when(c == 0)
    def _():
        for h in range(n_heads):
            st_ref[h] = s0_ref[0, h].T

    assert HGRN_SUB == 8
    rl1 = lax.broadcasted_iota(jnp.int32, (T, 1), 0) % HGRN_SUB
    t_i = lax.broadcasted_iota(jnp.int32, (T, T), 0)
    s_i = lax.broadcasted_iota(jnp.int32, (T, T), 1)
    tril = jnp.where(s_i <= t_i, 1.0, 0.0).astype(BF16)
    pair_masks = []
    size = 2 * HGRN_SUB
    while size <= T:
        same = (t_i & -size) == (s_i & -size)
        pair_masks.append((size, same & ((t_i & (size - 1)) >= size // 2) & ((s_i & (size - 1)) < size // 2)))
        size *= 2

    def shift_rows(x, d):
        return pltpu.roll(x.reshape(T // HGRN_SUB, HGRN_SUB, HEAD_DIM), d, 1).reshape(T, HEAD_DIM)

    def one_head(q, k, b, v, h, sl):
        hs = slice(h * HEAD_DIM, (h + 1) * HEAD_DIM)
        st = st_ref[h]
        o = _dot_nt((q * jnp.exp(b)).astype(BF16), st.astype(BF16))
        o = o + jnp.sum(q * k, axis=-1, keepdims=True) * v
        for d in range(1, HGRN_SUB):
            w = jnp.exp(b - shift_rows(b, d))
            a = jnp.sum(q * shift_rows(k, d) * w, axis=-1, keepdims=True)
            o = o + jnp.where(rl1 >= d, a, 0.0) * shift_rows(v, d)
        att = jnp.zeros((T, T), F32)
        for size, keep in pair_masks:
            half = size // 2
            refs = [jnp.broadcast_to(b[j * size + half - 1:j * size + half], (size, HEAD_DIM))
                    for j in range(T // size)]
            r = jnp.concatenate(refs, axis=0) if len(refs) > 1 else refs[0]
            qt = q * jnp.exp(jnp.minimum(b - r, 0.0))
            kt = k * jnp.exp(jnp.minimum(r - b, 0.0))
            att = att + jnp.where(keep, _dot_nt(qt.astype(BF16), kt.astype(BF16)), 0.0)
        o = o + _dot(att.astype(BF16), v.astype(BF16))
        bl = b[T - 1:T]
        kt = k * jnp.exp(bl - b)
        st_ref[h] = st * jnp.exp(bl) + _dot_tn(v.astype(BF16), kt.astype(BF16))
        o_ref[0, sl, hs] = _hgrn_out(o, g_ref[0, sl, hs], on_ref[:, hs]).astype(o_ref.dtype)

    def chunk(ci, carry):
        sl = pl.ds(pl.multiple_of(ci * T, T), T)
        q, k, lf = _hgrn_gates(q_ref[0, sl, :], f_ref[0, sl, :], llb_ref[...], l1m_ref[...])
        b = sum(_dot(tril, part) for part in _split3(lf))
        for h in range(n_heads):
            hs = slice(h * HEAD_DIM, (h + 1) * HEAD_DIM)
            one_head(q[:, hs], k[:, hs], b[:, hs], i_ref[0, sl, hs], h, sl)
        return carry

    lax.fori_loop(0, n_chunks, chunk, 0)

    @pl.when(c == pl.num_programs(2) - 1)
    def _():
        for h in range(n_heads):
            s_ref[0, h] = st_ref[h].T


def hgrn_prompt(z3, log_lb, log1m_lb, onorm, s0):
    B, L, _ = z3.shape
    H = N_MIX_HEADS
    hb = HGRN_HEADS_PER_STEP
    wb = hb * HEAD_DIM
    tc = _pick(L, (512, 256, 128, 64))
    zspec = lambda k: pl.BlockSpec((1, tc, wb), lambda b, h, c: (b, c, k * (H // hb) + h))
    vspec = pl.BlockSpec((1, wb), lambda b, h, c: (0, h))
    sspec = pl.BlockSpec((1, hb, HEAD_DIM, HEAD_DIM), lambda b, h, c: (b, h, 0, 0))
    return pl.pallas_call(
        functools.partial(_hgrn_kernel, n_chunks=tc // HGRN_T),
        grid=(B, H // hb, L // tc),
        in_specs=[zspec(0), zspec(1), zspec(2), zspec(3), vspec, vspec, vspec, sspec],
        out_specs=[pl.BlockSpec((1, tc, wb), lambda b, h, c: (b, c, h)), sspec],
        out_shape=[
            jax.ShapeDtypeStruct((B, L, MIX_DIM), BF16),
            jax.ShapeDtypeStruct((B, H, HEAD_DIM, HEAD_DIM), F32),
        ],
        scratch_shapes=[pltpu.VMEM((hb, HEAD_DIM, HEAD_DIM), F32)],
        compiler_params=_cp(("parallel", "parallel", "arbitrary")),
        name="hgrn_chunked",
    )(z3, z3, z3, z3, log_lb, log1m_lb, onorm, s0)


def _hgrn_step_kernel(z_ref, llb_ref, l1m_ref, on_ref, s0_ref, o_ref, s_ref):
    for h in range(N_MIX_HEADS):
        col = lambda k: slice((k * N_MIX_HEADS + h) * HEAD_DIM, (k * N_MIX_HEADS + h + 1) * HEAD_DIM)
        hs = slice(h * HEAD_DIM, (h + 1) * HEAD_DIM)
        q, k, lf = _hgrn_gates(z_ref[0, :, col(0)], z_ref[0, :, col(1)], llb_ref[:, hs], l1m_ref[:, hs])
        v = z_ref[0, :, col(2)]
        g = z_ref[0, :, col(3)]
        rows = jnp.concatenate([q, k, jnp.exp(lf), jnp.zeros((5, HEAD_DIM), F32)], axis=0)
        cols = rows.T
        s_new = cols[:, 2:3] * s0_ref[0, h] + cols[:, 1:2] * v
        s_ref[0, h] = s_new
        o = jnp.sum(cols[:, 0:1] * s_new, axis=0, keepdims=True)
        o_ref[0, :, hs] = _hgrn_out(o, g, on_ref[:, hs]).astype(o_ref.dtype)


def hgrn_step(z3, log_lb, log1m_lb, onorm, s0):
    B = z3.shape[0]
    W = z3.shape[2]
    H = N_MIX_HEADS
    vspec = pl.BlockSpec((1, MIX_DIM), lambda b: (0, 0))
    sspec = pl.BlockSpec((1, H, HEAD_DIM, HEAD_DIM), lambda b: (b, 0, 0, 0))
    return pl.pallas_call(
        _hgrn_step_kernel,
        grid=(B,),
        in_specs=[pl.BlockSpec((1, 1, W), lambda b: (b, 0, 0)), vspec, vspec, vspec, sspec],
        out_specs=[pl.BlockSpec((1, 1, MIX_DIM), lambda b: (b, 0, 0)), sspec],
        out_shape=[
            jax.ShapeDtypeStruct((B, 1, MIX_DIM), BF16),
            jax.ShapeDtypeStruct((B, H, HEAD_DIM, HEAD_DIM), F32),
        ],
        compiler_params=_cp(("parallel",)),
        name="hgrn_step",
    )(z3, log_lb, log1m_lb, onorm, s0)


def _pe_proj_kernel(pe_ref, w_ref, o_ref):
    pe = jnp.broadcast_to(pe_ref[0], (8, pe_ref.shape[2])).astype(BF16)
    o_ref[0] = _dot(pe, w_ref[0])


def pe_proj(pe, w1):
    K = pe.shape[2]
    return pl.pallas_call(
        _pe_proj_kernel,
        grid=(2,),
        in_specs=[pl.BlockSpec((1, 1, K), lambda i: (i, 0, 0)), pl.BlockSpec((1, K, CMP_HID), lambda i: (i, 0, 0))],
        out_specs=pl.BlockSpec((1, 8, CMP_HID), lambda i: (i, 0, 0)),
        out_shape=jax.ShapeDtypeStruct((2, 8, CMP_HID), F32),
        compiler_params=_cp(("parallel",)),
        name="cmp_pe_proj",
    )(pe, w1)


def _cmp_mlp_kernel(x_ref, w1_ref, w2_ref, pe_ref, o_ref):
    half = x_ref.shape[3]
    x = x_ref[0, 0].astype(BF16)
    a = _dot(x, w1_ref[0, 0:half])
    bm = _dot(x, w1_ref[0, half:2 * half])
    n = a.shape[0]
    h = a + pltpu.roll(bm, n - 1, 0) + pe_ref[0, 0:1]
    o_ref[0, 0] = _dot(jax.nn.gelu(h).astype(BF16), w2_ref[0])


def cmp_mlp(xsub, w1, w2, pe_h):
    _, G, n_sub, half = xsub.shape
    return pl.pallas_call(
        _cmp_mlp_kernel,
        grid=(2, G),
        in_specs=[
            pl.BlockSpec((1, 1, n_sub, half), lambda k, g: (k, g, 0, 0)),
            pl.BlockSpec((1, 2 * half, CMP_HID), lambda k, g: (k, 0, 0)),
            pl.BlockSpec((1, CMP_HID, HEAD_DIM), lambda k, g: (k, 0, 0)),
            pl.BlockSpec((1, 8, CMP_HID), lambda k, g: (k, 0, 0)),
        ],
        out_specs=pl.BlockSpec((1, 1, n_sub, HEAD_DIM), lambda k, g: (k, g, 0, 0)),
        out_shape=jax.ShapeDtypeStruct((2, G, n_sub, HEAD_DIM), F32),
        compiler_params=_cp(("parallel", "parallel")),
        name="cmp_mlp",
    )(xsub, w1, w2, pe_h)


LOG2E = 1.4426950408889634


def _masked_softmax_rows(s, mask, exp_fn=jnp.exp):
    s = jnp.where(mask, s, -jnp.inf)
    m = jnp.max(s, axis=-1, keepdims=True)
    m = jnp.where(m == -jnp.inf, 0.0, m)
    p = exp_fn(s - m)
    return p / jnp.maximum(jnp.sum(p, axis=-1, keepdims=True), 1e-30)


def _split3(p):
    hi = p.astype(BF16)
    r = p - hi.astype(F32)
    mid = r.astype(BF16)
    lo = (r - mid.astype(F32)).astype(BF16)
    return hi, mid, lo


def _masked_softmax_cols(s, mask, exp_fn):
    s = jnp.where(mask, s, -jnp.inf)
    m = jnp.max(s, axis=0, keepdims=True)
    m = jnp.where(m == -jnp.inf, 0.0, m)
    p = exp_fn(s - m)
    return p / jnp.maximum(jnp.sum(p, axis=0, keepdims=True), 1e-30)


def _nsa_prompt_t_kernel(q_ref, gt_ref, kc_ref, vc_ref, ks_ref, vs_ref, kw_ref, vw_ref, o_ref,
                         kaug_ref, vst_ref, kwb_ref, vwt_ref, vct_ref, score_ref, *, tq, tk, seq_len):
    qi = pl.program_id(2)
    n_blk = seq_len // SLC_BLOCK
    scale = HEAD_DIM ** -0.5 * LOG2E
    cols = GROUP * tq
    nc = kc_ref.shape[1]

    @pl.when(qi == 0)
    def _():
        r = lax.broadcasted_iota(jnp.int32, (seq_len, HEAD_DIM), 0) // SLC_BLOCK
        col = lax.broadcasted_iota(jnp.int32, (seq_len, HEAD_DIM), 1)
        kaug_ref[:, 0:HEAD_DIM] = ks_ref[0].astype(BF16)
        kaug_ref[:, HEAD_DIM:2 * HEAD_DIM] = jnp.where(r == col, 1.0, 0.0).astype(BF16)
        kwb_ref[...] = kw_ref[0].astype(BF16)
        for t in range(seq_len // tk):
            vst_ref[t] = vs_ref[0, t * tk:(t + 1) * tk, :].T.astype(BF16)
        for t in range(seq_len // tq):
            vwt_ref[t] = vw_ref[0, t * tq:(t + 1) * tq, :].T.astype(BF16)
        vct_ref[...] = vc_ref[0].T.astype(BF16)

    q0 = qi * tq
    qs = q_ref[0] * scale
    qb = jnp.concatenate([qs[:, g * HEAD_DIM:(g + 1) * HEAD_DIM] for g in range(GROUP)], axis=0).astype(BF16)

    def qpos(n_keys):
        return q0 + lax.broadcasted_iota(jnp.int32, (n_keys, cols), 1) % tq

    def kidx(n_keys):
        return lax.broadcasted_iota(jnp.int32, (n_keys, cols), 0)

    cend = CMP_STRIDE * kidx(nc) + (2 * CMP_STRIDE - 1)
    p_c = _masked_softmax_cols(_dot_nt(kc_ref[0].astype(BF16), qb), cend <= qpos(nc), jnp.exp2)
    o_cmp = _dot(vct_ref[...], p_c.astype(BF16))

    psum = p_c[:, 0:tq]
    for g in range(1, GROUP):
        psum = psum + p_c[:, g * tq:(g + 1) * tq]
    ji = lax.broadcasted_iota(jnp.int32, (HEAD_DIM, nc), 0)
    ci = lax.broadcasted_iota(jnp.int32, (HEAD_DIM, nc), 1)
    ratio = SLC_BLOCK // CMP_STRIDE
    member = ((ci >= ratio * ji - 1) & (ci <= ratio * ji + ratio - 1)).astype(BF16)
    imp = sum(_dot(member, part) for part in _split3(psum))
    blk = lax.broadcasted_iota(jnp.int32, (HEAD_DIM, tq), 0)
    cur = (q0 + lax.broadcasted_iota(jnp.int32, (HEAD_DIM, tq), 1)) // SLC_BLOCK
    valid = (blk <= cur) & (blk < n_blk)
    forced = valid & ((blk == 0) | (blk == cur) | (blk == cur - 1))
    score = jnp.where(forced, SEL_BONUS, jnp.where(valid, imp, -jnp.inf))
    score = score[0:n_blk]
    score_ref[...] = score
    blk = lax.broadcasted_iota(jnp.int32, (n_blk, tq), 0)
    per_tile = tq // SLC_BLOCK

    def rank_body(it, rank):
        for u in range(per_tile):
            i = it * per_tile + u
            si = score_ref[pl.ds(i, 1), :]
            later = jnp.where(blk > i, 1, 0)
            rank = rank + jnp.where(si > score, 1, jnp.where(si == score, later, 0))
        return rank

    rank = lax.fori_loop(0, qi + 1, rank_body, jnp.zeros((n_blk, tq), jnp.int32))
    bias_t = jnp.where(rank < N_SELECT, 0.0, NEG_BIG)
    if n_blk < HEAD_DIM:
        bias_t = jnp.concatenate([bias_t, jnp.zeros((HEAD_DIM - n_blk, tq), F32)], axis=0)
    bias = bias_t.T.astype(BF16)
    qaug = jnp.concatenate([qb, jnp.concatenate([bias] * GROUP, axis=0)], axis=1)

    qpos_k = qpos(tk)
    krow = kidx(tk)

    def slc_body(kt, carry, causal):
        m, l, acc = carry
        s = _dot_nt(kaug_ref[pl.ds(pl.multiple_of(kt * tk, tk), tk), :], qaug)
        if causal:
            s = jnp.where(kt * tk + krow <= qpos_k, s, -jnp.inf)
        m_new = jnp.maximum(m, jnp.max(s, axis=0, keepdims=True))
        alpha = jnp.exp2(m - m_new)
        p = jnp.exp2(s - m_new)
        l = alpha * l + jnp.sum(p, axis=0, keepdims=True)
        acc = alpha * acc + _dot(vst_ref[kt], p.astype(BF16))
        return m_new, l, acc

    init = (jnp.full((1, cols), -jnp.inf, F32), jnp.zeros((1, cols), F32), jnp.zeros((HEAD_DIM, cols), F32))
    n_full = q0 // tk

    def two_tiles(i, carry):
        return slc_body(2 * i + 1, slc_body(2 * i, carry, False), False)

    carry = lax.fori_loop(0, n_full // 2, two_tiles, init)
    carry = lax.fori_loop(n_full - n_full % 2, n_full, functools.partial(slc_body, causal=False), carry)
    _, l_s, acc_s = slc_body(n_full, carry, True)
    o_slc = acc_s / jnp.maximum(l_s, 1e-30)

    wk = min(WINDOW + tq, seq_len)
    k0 = jnp.minimum(jnp.maximum(q0 - WINDOW, 0), seq_len - wk)
    d = qpos(wk) - (k0 + kidx(wk))
    s_w = _dot_nt(kwb_ref[pl.ds(pl.multiple_of(k0, tq), wk), :], qb)
    s_w = jnp.where((d >= 0) & (d <= WINDOW), s_w, -jnp.inf)
    p_w = jnp.exp2(s_w - jnp.max(s_w, axis=0, keepdims=True))
    l_w = jnp.sum(p_w, axis=0, keepdims=True)
    p_wb = p_w.astype(BF16)
    t0 = k0 // tq
    o_win = sum(_dot(vwt_ref[t0 + i], p_wb[i * tq:(i + 1) * tq]) for i in range(wk // tq))
    o_win = o_win / jnp.maximum(l_w, 1e-30)

    gates = jax.nn.sigmoid(gt_ref[0]).T
    for g in range(GROUP):
        cs = slice(g * tq, (g + 1) * tq)
        c0 = g * N_BRANCH
        o = (gates[c0:c0 + 1] * o_cmp[:, cs] + gates[c0 + 1:c0 + 2] * o_slc[:, cs]
             + gates[c0 + 2:c0 + 3] * o_win[:, cs])
        o_ref[0, :, g * HEAD_DIM:(g + 1) * HEAD_DIM] = o.T.astype(o_ref.dtype)


def nsa_prompt_t(zq3, gates3, k_cmp, v_cmp, rows3):
    B, L, _ = zq3.shape
    tq = _pick(L, (512, 256, 128))
    tk = _pick(L, (512, 256, 128))
    assert WINDOW % tq == 0 and tk % tq == 0
    n_sub = k_cmp.shape[1]
    qw = GROUP * HEAD_DIM
    rspec = lambda kind: pl.BlockSpec((1, L, HEAD_DIM), lambda b, n, i: (b, 0, kind * N_KV + n))
    cspec = pl.BlockSpec((1, n_sub, HEAD_DIM), lambda b, n, i: (b * N_KV + n, 0, 0))
    return pl.pallas_call(
        functools.partial(_nsa_prompt_t_kernel, tq=tq, tk=tk, seq_len=L),
        grid=(B, N_KV, L // tq),
        in_specs=[
            pl.BlockSpec((1, tq, qw), lambda b, n, i: (b, i, n)),
            pl.BlockSpec((1, tq, HEAD_DIM), lambda b, n, i: (b, i, n)),
            cspec, cspec, rspec(2), rspec(3), rspec(4), rspec(5),
        ],
        out_specs=pl.BlockSpec((1, tq, qw), lambda b, n, i: (b, i, n)),
        out_shape=jax.ShapeDtypeStruct((B, L, MIX_DIM), BF16),
        scratch_shapes=[
            pltpu.VMEM((L, 2 * HEAD_DIM), BF16),
            pltpu.VMEM((L // tk, HEAD_DIM, tk), BF16),
            pltpu.VMEM((L, HEAD_DIM), BF16),
            pltpu.VMEM((L // tq, HEAD_DIM, tq), BF16),
            pltpu.VMEM((HEAD_DIM, n_sub), BF16),
            pltpu.VMEM((L // SLC_BLOCK, tq), F32),
        ],
        compiler_params=_cp(("parallel", "parallel", "arbitrary")),
        name="nsa_prompt",
    )(zq3, gates3, k_cmp, v_cmp, rows3, rows3, rows3, rows3)


def _dec_cmp_kernel(pt_ref, *refs, n_pg, n_groups):
    del pt_ref
    pg_refs = refs[:n_pg]
    new_ref, w1_ref, w2_ref, pe_ref, kc_ref, vc_ref, carry_ref = refs[n_pg:]
    g = pl.program_id(1)
    sub_pg = PAGE_SIZE // CMP_STRIDE
    n_sub = n_pg * sub_pg
    pos_per_chunk = 4
    kw = pos_per_chunk * HEAD_DIM

    @pl.when(g == 0)
    def _():
        carry_ref[...] = jnp.zeros_like(carry_ref)

    sub_i = lax.broadcasted_iota(jnp.int32, (8, n_sub, kw), 1)
    new_row = jnp.concatenate([new_ref[0], jnp.zeros((8, kw - HEAD_DIM), F32)], axis=1)[:, None, :]
    ys = [None, None]
    for pc in range(CMP_STRIDE // pos_per_chunk):
        x = jnp.concatenate(
            [jnp.concatenate([jnp.swapaxes(pg_refs[r][0, pl.ds(p, sub_pg, stride=CMP_STRIDE), :, :], 0, 1)
                              for r in range(n_pg)], axis=1)
             for p in range(pc * pos_per_chunk, (pc + 1) * pos_per_chunk)], axis=2)
        x = jnp.where(g < n_groups, x, 0.0)
        if pc == 0:
            x = jnp.where((g == n_groups) & (sub_i == 0), new_row, x)
        xb = x.astype(BF16)
        for kind in range(2):
            part = _dot(xb[kind * N_KV:(kind + 1) * N_KV].reshape(N_KV * n_sub, kw),
                        w1_ref[kind, pc * kw:(pc + 1) * kw, :])
            ys[kind] = part if ys[kind] is None else ys[kind] + part

    row = lax.broadcasted_iota(jnp.int32, (n_sub, CMP_HID), 0)
    for kind, out_ref in enumerate((kc_ref, vc_ref)):
        y = ys[kind]
        for n in range(N_KV):
            slot = kind * N_KV + n
            a = y[n * n_sub:(n + 1) * n_sub, 0:CMP_HID]
            bm = y[n * n_sub:(n + 1) * n_sub, CMP_HID:2 * CMP_HID]
            a_prev = jnp.where(row == 0, carry_ref[slot:slot + 1, :], pltpu.roll(a, 1, 0))
            carry_ref[slot:slot + 1, :] = a[n_sub - 1:n_sub]
            h = a_prev + bm + pe_ref[kind, 0:1]
            out_ref[0, n] = _dot(jax.nn.gelu(h).astype(BF16), w2_ref[kind])


def dec_cmp_tokens(cache4, page_table, new_cmp, w1ab, w2, pe_h):
    B, n_pages = page_table.shape
    n_pg = _pick(n_pages, (8, 4, 2))
    assert n_pages % n_pg == 0
    n_groups = n_pages // n_pg
    sub_pg = PAGE_SIZE // CMP_STRIDE
    n_sub = n_pg * sub_pg
    n_steps = n_groups + 1
    while (n_steps * n_sub) % HEAD_DIM:
        n_steps += 1
    half = CMP_STRIDE * HEAD_DIM

    def page_spec(r):
        def imap(b, g, pt):
            return (pt[b * n_pages + jnp.minimum(g * n_pg + r, n_pages - 1)], 0, 0, 0)
        return pl.BlockSpec((1, PAGE_SIZE, 8, HEAD_DIM), imap)

    out_spec = pl.BlockSpec((1, N_KV, n_sub, HEAD_DIM), lambda b, g, pt: (b, 0, g, 0))
    grid_spec = pltpu.PrefetchScalarGridSpec(
        num_scalar_prefetch=1,
        grid=(B, n_steps),
        in_specs=[page_spec(r) for r in range(n_pg)] + [
            pl.BlockSpec((1, 8, HEAD_DIM), lambda b, g, pt: (b, 0, 0)),
            pl.BlockSpec((2, half, 2 * CMP_HID), lambda b, g, pt: (0, 0, 0)),
            pl.BlockSpec((2, CMP_HID, HEAD_DIM), lambda b, g, pt: (0, 0, 0)),
            pl.BlockSpec((2, 8, CMP_HID), lambda b, g, pt: (0, 0, 0)),
        ],
        out_specs=[out_spec, out_spec],
        scratch_shapes=[pltpu.VMEM((8, CMP_HID), F32)],
    )
    tok = jax.ShapeDtypeStruct((B, N_KV, n_steps * n_sub, HEAD_DIM), F32)
    return pl.pallas_call(
        functools.partial(_dec_cmp_kernel, n_pg=n_pg, n_groups=n_groups),
        grid_spec=grid_spec,
        out_shape=[tok, tok],
        compiler_params=_cp(("parallel", "arbitrary")),
        name="dec_cmp_tokens",
    )(page_table.reshape(-1), *([cache4] * n_pg), new_cmp, w1ab, w2, pe_h)


def _dec_select_kernel(q_ref, kc_ref, vc_ref, mem_ref, ocmp_ref, idx_ref, *, q_pos, nc, n_blk, n_sel):
    C = kc_ref.shape[2]
    JB = mem_ref.shape[1]
    scale = HEAD_DIM ** -0.5
    c_i = lax.broadcasted_iota(jnp.int32, (8, C), 1) - 1
    g_i = lax.broadcasted_iota(jnp.int32, (8, C), 0)
    ok = (c_i >= 0) & (c_i < nc) & (CMP_STRIDE * c_i + (2 * CMP_STRIDE - 1) <= q_pos)
    psums = []
    for n in range(N_KV):
        qb = (q_ref[0, n] * scale).astype(BF16)
        p = _masked_softmax_rows(_dot_nt(qb, kc_ref[0, n].astype(BF16)), ok)
        ocmp_ref[0, n] = _dot(p.astype(BF16), vc_ref[0, n].astype(BF16))
        psums.append(jnp.sum(jnp.where(g_i < GROUP, p, 0.0), axis=0, keepdims=True))
    psum = jnp.concatenate(psums + [jnp.zeros((8 - N_KV, C), F32)], axis=0)
    imp = sum(_dot(part, mem_ref[...]) for part in _split3(psum))
    j_i = lax.broadcasted_iota(jnp.int32, (8, JB), 1)
    cur = q_pos // SLC_BLOCK
    valid = (j_i <= cur) & (j_i < n_blk)
    forced = valid & ((j_i == 0) | (j_i == cur) | (j_i == cur - 1))
    score = jnp.where(forced, SEL_BONUS, jnp.where(valid, imp, -jnp.inf))
    j_f = j_i.astype(F32)
    taken = j_i >= n_blk
    lane_o = lax.broadcasted_iota(jnp.int32, (8, HEAD_DIM), 1)
    out = jnp.zeros((8, HEAD_DIM), F32)
    for r in range(n_sel):
        m = jnp.max(jnp.where(taken, -jnp.inf, score), axis=-1, keepdims=True)
        cand = jnp.logical_not(taken) & (score == m)
        idx = jnp.min(jnp.where(cand, j_f, float(JB)), axis=-1, keepdims=True)
        out = jnp.where(lane_o == r, idx, out)
        taken = taken | (j_f == idx)
    idx_ref[0] = out.astype(jnp.int32)


def dec_select(q8, k_cmp, v_cmp, member, *, q_pos, nc, n_blk, n_sel):
    B, _, C, _ = k_cmp.shape
    JB = member.shape[1]
    head_spec = pl.BlockSpec((1, N_KV, 8, HEAD_DIM), lambda b: (b, 0, 0, 0))
    tok_spec = pl.BlockSpec((1, N_KV, C, HEAD_DIM), lambda b: (b, 0, 0, 0))
    return pl.pallas_call(
        functools.partial(_dec_select_kernel, q_pos=q_pos, nc=nc, n_blk=n_blk, n_sel=n_sel),
        grid=(B,),
        in_specs=[head_spec, tok_spec, tok_spec, pl.BlockSpec((C, JB), lambda b: (0, 0))],
        out_specs=[head_spec, pl.BlockSpec((1, 8, HEAD_DIM), lambda b: (b, 0, 0))],
        out_shape=[
            jax.ShapeDtypeStruct((B, N_KV, 8, HEAD_DIM), F32),
            jax.ShapeDtypeStruct((B, 8, HEAD_DIM), jnp.int32),
        ],
        compiler_params=_cp(("parallel",)),
        name="dec_select",
    )(q8, k_cmp, v_cmp, member)


def _dec_attend_kernel(idx_ref, pt_ref, q_ref, gl_ref, ocmp_ref, nslc_ref, win_ref, nwin_ref, *rest,
                       q_pos, past_len, n_sel):
    del pt_ref
    blk_refs = rest[:n_sel]
    o_ref = rest[n_sel]
    b = pl.program_id(0)
    n = pl.program_id(1)
    base = (b * N_KV + n) * n_sel
    scale = HEAD_DIM ** -0.5
    past_blocks = past_len // SLC_BLOCK
    rb = SLC_BLOCK * 8
    rw = win_ref.shape[1]
    w_buf = rw // 8
    qf = q_ref[0, 0] * scale
    qb = qf.astype(BF16)
    js = [idx_ref[base + s] for s in range(n_sel)]
    has_new = js[0] == past_blocks
    for s in range(1, n_sel):
        has_new = has_new | (js[s] == past_blocks)
    new_ok = has_new & (past_len <= q_pos)

    def head_rows(x):
        slot = lax.broadcasted_iota(jnp.int32, (8, HEAD_DIM), 0)
        k = jnp.sum(jnp.where(slot == n, x, 0.0), axis=0, keepdims=True)
        v = jnp.sum(jnp.where(slot == n + N_KV, x, 0.0), axis=0, keepdims=True)
        return k, v

    def finish(m):
        return jnp.where(m == -jnp.inf, 0.0, m)

    r_s = lax.broadcasted_iota(jnp.int32, (8, rb), 1)
    t_s = r_s >> 3
    mine_s = (r_s & 7) == n

    def slc_scores(s):
        blk = blk_refs[s][0, 0].reshape(rb, HEAD_DIM)
        sc = _dot_nt(qb, blk.astype(BF16))
        ok = mine_s & (js[s] * SLC_BLOCK + t_s <= q_pos) & (js[s] < past_blocks)
        return jnp.where(ok, sc, -jnp.inf)

    k_new, v_new = head_rows(nslc_ref[0])
    s_new = jnp.where(new_ok, jnp.sum(qf * k_new, axis=-1, keepdims=True), -jnp.inf)
    scores = [slc_scores(s) for s in range(n_sel)]
    m = s_new
    for sc in scores:
        m = jnp.maximum(m, jnp.max(sc, axis=-1, keepdims=True))
    m = finish(m)
    p_new = jnp.exp(s_new - m)
    l = p_new
    acc = p_new * v_new
    for s, sc in enumerate(scores):
        p = jnp.exp(sc - m)
        l = l + jnp.sum(p, axis=-1, keepdims=True)
        vals = pltpu.roll(blk_refs[s][0, 0].reshape(rb, HEAD_DIM), rb - N_KV, 0)
        acc = acc + _dot(p.astype(BF16), vals.astype(BF16))
    o_slc = acc / jnp.maximum(l, 1e-30)

    r_w = lax.broadcasted_iota(jnp.int32, (8, rw), 1)
    w_pos = past_len - w_buf + (r_w >> 3)
    w_ok = ((r_w & 7) == n) & (q_pos - w_pos >= 0) & (q_pos - w_pos <= WINDOW) & (w_pos >= 0)
    win = win_ref[0]
    s_w = jnp.where(w_ok, _dot_nt(qb, win.astype(BF16)), -jnp.inf)
    k_nw, v_nw = head_rows(nwin_ref[0])
    nw_ok = (q_pos - past_len >= 0) and (q_pos - past_len <= WINDOW)
    s_nw = jnp.sum(qf * k_nw, axis=-1, keepdims=True) if nw_ok else jnp.full((8, 1), -jnp.inf, F32)
    m = finish(jnp.maximum(s_nw, jnp.max(s_w, axis=-1, keepdims=True)))
    p_w = jnp.exp(s_w - m)
    p_nw = jnp.exp(s_nw - m)
    l = p_nw + jnp.sum(p_w, axis=-1, keepdims=True)
    acc = p_nw * v_nw + _dot(p_w.astype(BF16), pltpu.roll(win, rw - N_KV, 0).astype(BF16))
    o_win = acc / jnp.maximum(l, 1e-30)

    gates = jax.nn.sigmoid(gl_ref[0, 0])
    o_ref[0, 0] = gates[:, 0:1] * ocmp_ref[0, 0] + gates[:, 1:2] * o_slc + gates[:, 2:3] * o_win


def dec_attend(idx_flat, page_table, q8, gate8, ocmp8, new_slc, win2, new_win, cache5, *, q_pos, past_len, n_sel):
    B, n_pages = page_table.shape
    rw = win2.shape[1]
    past_blocks = past_len // SLC_BLOCK
    per_page = PAGE_SIZE // SLC_BLOCK

    def blk_spec(s):
        def imap(b, n, idx, pt):
            j = jnp.minimum(idx[(b * N_KV + n) * n_sel + s], past_blocks - 1)
            return (pt[b * n_pages + j // per_page], j % per_page, 0, 1, 0)
        return pl.BlockSpec((1, 1, SLC_BLOCK, 8, HEAD_DIM), imap)

    head_spec = pl.BlockSpec((1, 1, 8, HEAD_DIM), lambda b, n, idx, pt: (b, n, 0, 0))
    tok_spec = pl.BlockSpec((1, 8, HEAD_DIM), lambda b, n, idx, pt: (b, 0, 0))
    grid_spec = pltpu.PrefetchScalarGridSpec(
        num_scalar_prefetch=2,
        grid=(B, N_KV),
        in_specs=[head_spec, head_spec, head_spec, tok_spec,
                  pl.BlockSpec((1, rw, HEAD_DIM), lambda b, n, idx, pt: (b, 0, 0)), tok_spec]
        + [blk_spec(s) for s in range(n_sel)],
        out_specs=head_spec,
    )
    return pl.pallas_call(
        functools.partial(_dec_attend_kernel, q_pos=q_pos, past_len=past_len, n_sel=n_sel),
        grid_spec=grid_spec,
        out_shape=jax.ShapeDtypeStruct((B, N_KV, 8, HEAD_DIM), F32),
        compiler_params=_cp(("parallel", "arbitrary")),
        name="dec_attend",
    )(idx_flat, page_table.reshape(-1), q8, gate8, ocmp8, new_slc, win2, new_win, *([cache5] * n_sel))


def _pad_to(a, axis, size):
    pad = [(0, 0)] * a.ndim
    pad[axis] = (0, size - a.shape[axis])
    return jnp.pad(a, pad)


def _prep_weights(w_in_a, w_in_b, b_gate, w_kv, cmp_pe, cmp_w1, cmp_w2, w_mem_kv, w_out, w_up, conv_w, conv_b,
                  w_down, hgrn_lb):
    depth = w_out.shape[0]
    n_b = w_in_b.shape[0]
    n_gate = N_MIX_HEADS * N_BRANCH
    per_kv = GROUP * N_BRANCH
    half = CMP_STRIDE * HEAD_DIM
    w1 = cmp_w1.astype(BF16)
    lb = jnp.cumsum(jax.nn.softmax(hgrn_lb.astype(F32), axis=0), axis=0)
    lb = lb - lb[0]
    layers = lambda f, n=depth: [f(l) for l in range(n)]
    gate_w = lambda j: w_in_b[j][:, MIX_DIM:MIX_DIM + n_gate].reshape(D_MODEL, N_KV, per_kv)
    return dict(
        in_a=cast_bf16(w_in_a),
        in_b_qm=layers(lambda j: jnp.concatenate([w_in_b[j][:, :MIX_DIM], w_in_b[j][:, MIX_DIM + n_gate:]],
                                                 axis=-1).astype(BF16), n_b),
        in_b_gate=layers(lambda j: _pad_to(gate_w(j), 2, HEAD_DIM).reshape(D_MODEL, N_KV * HEAD_DIM).astype(BF16), n_b),
        b_gate=layers(lambda j: _pad_to(b_gate[j].reshape(N_KV, per_kv), 1, HEAD_DIM).reshape(N_KV * HEAD_DIM), n_b),
        kv=cast_bf16(w_kv[None]),
        mem_kv=cast_bf16(w_mem_kv),
        out=cast_bf16(w_out),
        up=cast_split_halves(w_up, D_FF_PAD),
        conv_w=layers(lambda l: _pad_to(_pad_to(conv_w[l], 1, D_FF_PAD), 0, 8)),
        conv_b=layers(lambda l: _pad_to(conv_b[l], 0, D_FF_PAD).reshape(1, D_FF_PAD)),
        down=cast_pad_rows(w_down, D_FF_PAD),
        cmp_w1=w1,
        cmp_w1ab=jnp.concatenate([w1[:, :half], w1[:, half:]], axis=2),
        cmp_w2=cmp_w2.astype(BF16),
        cmp_pe=cmp_pe.reshape(2, 1, 2 * half),
        log_lb=jnp.log(lb),
        log1m_lb=jnp.log1p(-lb),
    )


def kernel(x_prompt, x_sample, mem_prompt, cache_nsa_kv, page_table, cache_win_kv, state_hgrn, state_conv,
           cache_mem_kv, norm_mix, norm_ffn, norm_mem, norm_kv, norm_final, w_in_a, hgrn_lb, hgrn_onorm,
           w_in_b, b_gate, w_kv, cmp_pe, cmp_w1, cmp_w2, w_mem_kv, w_out, w_up, conv_w, conv_b, w_down):
    B, L, _ = x_prompt.shape
    Bs = x_sample.shape[0]
    depth = w_out.shape[0]
    n_a = w_in_a.shape[0]
    n_mem = mem_prompt.shape[1]
    n_pool, page, _, _, _ = cache_nsa_kv.shape
    n_pages = page_table.shape[1]
    past_len = n_pages * page
    w_buf = cache_win_kv.shape[1]
    assert x_sample.shape[1] == 1 and page == PAGE_SIZE and L % 128 == 0 and L // SLC_BLOCK <= HEAD_DIM
    W = _prep_weights(w_in_a, w_in_b, b_gate, w_kv, cmp_pe, cmp_w1, cmp_w2, w_mem_kv, w_out, w_up, conv_w,
                      conv_b, w_down, hgrn_lb)
    pe_h = pe_proj(W["cmp_pe"], W["cmp_w1"])
    onorm = hgrn_onorm.reshape(n_a, 1, MIX_DIM)
    log_lb = W["log_lb"].reshape(n_a, 1, MIX_DIM)
    log1m_lb = W["log1m_lb"].reshape(n_a, 1, MIX_DIM)

    def ffn(x, l, **kw):
        return conv_ffn(x, norm_ffn[l], W["up"], W["conv_w"][l], W["conv_b"][l], W["down"], l, **kw)

    M = B * L
    mem_flat = mem_prompt.reshape(B * n_mem, D_MODEL)
    mem_kv_both = [rms_matmul_heads(mem_flat, norm_mem[l], W["mem_kv"], (l,), 2 * MEM_DIM) for l in range(depth)]
    mem_kv_p = [flat.reshape(B, n_mem, 2 * MEM_DIM) for flat, _ in mem_kv_both]
    x = x_prompt.reshape(M, D_MODEL)
    hgrn_p, conv_p = [], []
    for l in range(depth):
        if l < n_a:
            z3 = rms_matmul(x, norm_mix[l], W["in_a"], lead=(l,)).reshape(B, L, -1)
            s0 = jnp.zeros((B, N_MIX_HEADS, HEAD_DIM, HEAD_DIM), F32)
            o_mix, s_new = hgrn_prompt(z3, log_lb[l], log1m_lb[l], onorm[l], s0)
            hgrn_p.append(s_new)
            o_mem = mem_attention(z3, 4 * MIX_DIM // MEM_DIM, mem_kv_p[l])
        else:
            j = l - n_a
            if j == 0:
                n_row4 = 4 * N_KV * HEAD_DIM
                rows_p, rows_by_head = rms_matmul_heads(x, norm_kv, W["kv"], (0,), n_row4)
                rows3 = rows_p.reshape(B, L, -1)
                n_sub = L // CMP_STRIDE
                xsub = rows_p[:, :2 * N_KV * HEAD_DIM].reshape(B, n_sub, CMP_STRIDE, 2, N_KV, HEAD_DIM)
                xsub = xsub.transpose(3, 0, 4, 1, 2, 5).reshape(2, B * N_KV, n_sub, CMP_STRIDE * HEAD_DIM)
                cmp_p = cmp_mlp(xsub, W["cmp_w1"], W["cmp_w2"], pe_h)
            zq3 = rms_matmul(x, norm_mix[l], W["in_b_qm"][j]).reshape(B, L, -1)
            gt3 = rms_matmul(x, norm_mix[l], W["in_b_gate"][j], W["b_gate"][j]).reshape(B, L, -1)
            o_mix = nsa_prompt_t(zq3, gt3, cmp_p[0], cmp_p[1], rows3)
            o_mem = mem_attention(zq3, MIX_DIM // MEM_DIM, mem_kv_p[l])
        x = out_proj(o_mix.reshape(M, MIX_DIM), o_mem.reshape(M, MEM_DIM), W["out"], l, x)
        x, a_tail = ffn(x, l, seq_len=L)
        conv_p.append(a_tail.reshape(B, -1, 8, D_FF_PAD)[:, -1, 8 - (CONV_W - 1):, :D_FF])
    y_prompt = rmsnorm_rows(x, norm_final).reshape(B, L, D_MODEL)
    w_keep = min(WINDOW, L)
    nsa_rows_prompt = rows_by_head.reshape(B, L, 4, N_KV, HEAD_DIM)
    win_prompt = rows3[:, L - w_keep:, n_row4:].reshape(B, w_keep, 2, N_KV, HEAD_DIM)
    mem_kv_prompt = jnp.stack([by_head for _, by_head in mem_kv_both]).reshape(
        depth, B, n_mem, 2, N_MEM_HEADS, HEAD_DIM)

    Ms = 16
    pad_rows = lambda a: _pad_to(a, 0, Ms)
    xs = pad_rows(x_sample.reshape(Bs, D_MODEL))
    hgrn_s, conv_s = [], []
    q_pos = past_len
    t_pad = -(-(past_len + 1) // SLC_BLOCK) * SLC_BLOCK
    n_blk = t_pad // SLC_BLOCK
    nc = t_pad // CMP_STRIDE - 1
    n_sel = min(N_SELECT, n_blk)
    for l in range(depth):
        if l < n_a:
            z = rms_matmul(xs, norm_mix[l], W["in_a"], lead=(l,))
            z3 = z[:Bs].reshape(Bs, 1, -1)
            o_mix, s_new = hgrn_step(z3, log_lb[l], log1m_lb[l], onorm[l], state_hgrn[l])
            hgrn_s.append(s_new)
            o_mix = o_mix.reshape(Bs, MIX_DIM)
            o_mem = mem_attention(z3, 4 * MIX_DIM // MEM_DIM, cache_mem_kv[l].reshape(Bs, n_mem, 2 * MEM_DIM))
        else:
            j = l - n_a
            if j == 0:
                rows_s = rms_matmul(xs, norm_kv, W["kv"], lead=(0,))[:Bs]
                new_cmp = rows_s[:, 0:1024].reshape(Bs, 8, HEAD_DIM)
                new_slc = rows_s[:, 1024:2048].reshape(Bs, 8, HEAD_DIM)
                new_win = rows_s[:, 2048:3072].reshape(Bs, 8, HEAD_DIM)
                cache4 = cache_nsa_kv.reshape(n_pool, PAGE_SIZE, 4 * N_KV, HEAD_DIM)
                cache5 = cache_nsa_kv.reshape(n_pool, PAGE_SIZE // SLC_BLOCK, SLC_BLOCK, 4 * N_KV, HEAD_DIM)
                win2 = cache_win_kv.reshape(Bs, w_buf * 2 * N_KV, HEAD_DIM)
                k_cmp_s, v_cmp_s = dec_cmp_tokens(cache4, page_table, new_cmp, W["cmp_w1ab"], W["cmp_w2"], pe_h)
                assert k_cmp_s.shape[2] > nc + 1
                ratio = SLC_BLOCK // CMP_STRIDE
                c_of_row = jnp.arange(k_cmp_s.shape[2])[:, None] - 1
                j_of_col = jnp.arange(-(-n_blk // HEAD_DIM) * HEAD_DIM)[None, :]
                member = ((c_of_row >= ratio * j_of_col - 1) & (c_of_row <= ratio * j_of_col + ratio - 1)).astype(BF16)
            zq = rms_matmul(xs, norm_mix[l], W["in_b_qm"][j])
            gt = rms_matmul(xs, norm_mix[l], W["in_b_gate"][j], W["b_gate"][j])[:Bs]
            z3 = zq[:Bs].reshape(Bs, 1, -1)
            per_head = lambda a: _pad_to(a.reshape(Bs, N_KV, GROUP, -1), 2, 8)
            q8 = per_head(zq[:Bs, :MIX_DIM])
            o_cmp, idx = dec_select(q8, k_cmp_s, v_cmp_s, member, q_pos=q_pos, nc=nc, n_blk=n_blk, n_sel=n_sel)
            idx_flat = idx[:, :N_KV, :n_sel].reshape(-1)
            gate9 = gt.reshape(Bs, N_KV, HEAD_DIM)[:, :, :GROUP * N_BRANCH]
            o_all = dec_attend(idx_flat, page_table, q8, _pad_to(per_head(gate9), 3, HEAD_DIM), o_cmp,
                               new_slc, win2, new_win, cache5, q_pos=q_pos, past_len=past_len, n_sel=n_sel)
            o_mix = o_all[:, :, :GROUP].reshape(Bs, MIX_DIM).astype(BF16)
            o_mem = mem_attention(z3, MIX_DIM // MEM_DIM, cache_mem_kv[l].reshape(Bs, n_mem, 2 * MEM_DIM))
        xs = out_proj(pad_rows(o_mix), pad_rows(o_mem.reshape(Bs, MEM_DIM)), W["out"], l, xs)
        prev = (pad_rows(_pad_to(state_conv[l][:, 1], 1, D_FF_PAD)), pad_rows(_pad_to(state_conv[l][:, 0], 1, D_FF_PAD)))
        xs, a_new = ffn(xs, l, prev=prev)
        conv_s.append(jnp.stack([state_conv[l][:, 1], a_new[0, :Bs, :D_FF]], axis=1))
    y_sample = rmsnorm_rows(xs, norm_final)[:Bs].reshape(Bs, 1, D_MODEL)
    nsa_rows_sample = rows_s[:, :n_row4].reshape(Bs, 1, 4, N_KV, HEAD_DIM)
    win_new = rows_s[:, n_row4:].reshape(Bs, 1, 2, N_KV, HEAD_DIM).astype(cache_win_kv.dtype)
    win_sample = jnp.concatenate([cache_win_kv, win_new], axis=1)[:, 1:]

    return (y_prompt, y_sample, nsa_rows_prompt, nsa_rows_sample, win_prompt, win_sample,
            jnp.stack(hgrn_p), jnp.stack(hgrn_s), jnp.stack(conv_p), jnp.stack(conv_s), mem_kv_prompt)
```

```python
import functools

import jax
import jax.numpy as jnp
from jax import lax
from jax.experimental import pallas as pl
from jax.experimental.pallas import tpu as pltpu

F32 = jnp.float32
BF16 = jnp.bfloat16

D_MODEL = 2048
HEAD_DIM = 128
MIX_DIM = 1536
N_MIX_HEADS = 12
MEM_DIM = 512
N_MEM_HEADS = 4
N_KV = 4
GROUP = 3
N_BRANCH = 3
CMP_STRIDE = 16
CMP_HID = 256
SLC_BLOCK = 64
N_SELECT = 16
WINDOW = 512
PAGE_SIZE = 128
D_FF = 5504
D_FF_PAD = 5632
CONV_W = 3
RMS_EPS = 1e-6
SEL_BONUS = 1e9
NEG_BIG = -1e30

HGRN_T = 64
HGRN_SUB = 8
HGRN_HEADS_PER_STEP = 4
FFN_TF = 512
VMEM_LIMIT = 56 * 1024 * 1024


def _cp(sem, vmem=VMEM_LIMIT):
    return pltpu.CompilerParams(dimension_semantics=sem, vmem_limit_bytes=vmem)


def _dot(a, b):
    return jnp.dot(a, b, preferred_element_type=F32)


def _dot_nt(a, b):
    return lax.dot_general(a, b, (((1,), (1,)), ((), ())), preferred_element_type=F32)


def _dot_tn(a, b):
    return lax.dot_general(a, b, (((0,), (0,)), ((), ())), preferred_element_type=F32)


def _pick(n, cands):
    for c in cands:
        if n % c == 0:
            return c
    return n


def _rms_matmul_kernel(x_ref, g_ref, w_ref, b_ref, o_ref, xn_ref):
    @pl.when(pl.program_id(1) == 0)
    def _():
        x = x_ref[...]
        y = x * lax.rsqrt(jnp.mean(x * x, axis=-1, keepdims=True) + RMS_EPS)
        xn_ref[...] = (y * g_ref[...]).astype(BF16)

    o_ref[...] = _dot(xn_ref[...], w_ref[...]) + b_ref[...]


def _layer_spec(lead, block, imap):
    return pl.BlockSpec((None,) * len(lead) + tuple(block), lambda *a: tuple(lead) + tuple(imap(*a)))


def rms_matmul(x, g, w, bias=None, lead=()):
    M, K = x.shape
    N = w.shape[-1]
    tm = _pick(M, (1024, 512, 256, 128))
    tn = _pick(N, (512, 256, 128))
    if bias is None:
        bias = jnp.zeros((N,), F32)
    return pl.pallas_call(
        _rms_matmul_kernel,
        grid=(M // tm, N // tn),
        in_specs=[
            pl.BlockSpec((tm, K), lambda i, j: (i, 0)),
            pl.BlockSpec((1, K), lambda i, j: (0, 0)),
            _layer_spec(lead, (K, tn), lambda i, j: (0, j)),
            pl.BlockSpec((1, tn), lambda i, j: (0, j)),
        ],
        out_specs=pl.BlockSpec((tm, tn), lambda i, j: (i, j)),
        out_shape=jax.ShapeDtypeStruct((M, N), F32),
        scratch_shapes=[pltpu.VMEM((tm, K), BF16)],
        compiler_params=_cp(("parallel", "arbitrary")),
        name="rms_matmul",
    )(x, g.reshape(1, K), w, bias.reshape(1, N))


def _rms_matmul_heads_kernel(x_ref, g_ref, w_ref, o_ref, oh_ref, xn_ref, *, n_head_steps):
    j = pl.program_id(1)

    @pl.when(j == 0)
    def _():
        x = x_ref[...]
        y = x * lax.rsqrt(jnp.mean(x * x, axis=-1, keepdims=True) + RMS_EPS)
        xn_ref[...] = (y * g_ref[...]).astype(BF16)

    y = _dot(xn_ref[...], w_ref[...])
    o_ref[...] = y

    @pl.when(j < n_head_steps)
    def _():
        oh_ref[...] = y.reshape(oh_ref.shape)


def rms_matmul_heads(x, g, w, lead, head_cols):
    M, K = x.shape
    N = w.shape[-1]
    tm = _pick(M, (512, 256, 128))
    tn = 8 * HEAD_DIM
    assert N % tn == 0 and head_cols % tn == 0
    n_head_steps = head_cols // tn
    return pl.pallas_call(
        functools.partial(_rms_matmul_heads_kernel, n_head_steps=n_head_steps),
        grid=(M // tm, N // tn),
        in_specs=[
            pl.BlockSpec((tm, K), lambda i, j: (i, 0)),
            pl.BlockSpec((1, K), lambda i, j: (0, 0)),
            _layer_spec(lead, (K, tn), lambda i, j: (0, j)),
        ],
        out_specs=[
            pl.BlockSpec((tm, tn), lambda i, j: (i, j)),
            pl.BlockSpec((tm, 8, HEAD_DIM), lambda i, j: (i, jnp.minimum(j, n_head_steps - 1), 0)),
        ],
        out_shape=[
            jax.ShapeDtypeStruct((M, N), F32),
            jax.ShapeDtypeStruct((M, head_cols // HEAD_DIM, HEAD_DIM), F32),
        ],
        scratch_shapes=[pltpu.VMEM((tm, K), BF16)],
        compiler_params=_cp(("parallel", "arbitrary")),
        name="rms_matmul_heads",
    )(x, g.reshape(1, K), w)


def _rmsnorm_kernel(x_ref, g_ref, o_ref):
    x = x_ref[...]
    y = x * lax.rsqrt(jnp.mean(x * x, axis=-1, keepdims=True) + RMS_EPS)
    o_ref[...] = y * g_ref[...]


def rmsnorm_rows(x, g):
    M, K = x.shape
    tm = _pick(M, (512, 256, 128))
    return pl.pallas_call(
        _rmsnorm_kernel,
        grid=(M // tm,),
        in_specs=[pl.BlockSpec((tm, K), lambda i: (i, 0)), pl.BlockSpec((1, K), lambda i: (0, 0))],
        out_specs=pl.BlockSpec((tm, K), lambda i: (i, 0)),
        out_shape=jax.ShapeDtypeStruct((M, K), F32),
        compiler_params=_cp(("parallel",)),
        name="final_rmsnorm",
    )(x, g.reshape(1, K))


def _cast_kernel(x_ref, o_ref):
    o_ref[...] = x_ref[...].astype(o_ref.dtype)


def cast_bf16(w):
    G, R, C = w.shape
    tr = _pick(R, (256, 128))
    return pl.pallas_call(
        _cast_kernel,
        grid=(G, R // tr),
        in_specs=[pl.BlockSpec((1, tr, C), lambda g, r: (g, r, 0))],
        out_specs=pl.BlockSpec((1, tr, C), lambda g, r: (g, r, 0)),
        out_shape=jax.ShapeDtypeStruct((G, R, C), BF16),
        compiler_params=_cp(("parallel", "parallel")),
        name="cast_bf16",
    )(w)


def _cast_halves_kernel(x_ref, o_ref):
    c = x_ref.shape[2]
    o_ref[0, 0, :, 0:c] = x_ref[0].astype(o_ref.dtype)
    o_ref[0, 0, :, c:] = jnp.zeros((o_ref.shape[2], o_ref.shape[3] - c), o_ref.dtype)


def cast_split_halves(w, cols_pad):
    G, R, C2 = w.shape
    C = C2 // 2
    tr = _pick(R, (256, 128))
    return pl.pallas_call(
        _cast_halves_kernel,
        grid=(G, 2, R // tr),
        in_specs=[pl.BlockSpec((1, tr, C), lambda g, h, r: (g, r, h))],
        out_specs=pl.BlockSpec((1, 1, tr, cols_pad), lambda g, h, r: (g, h, r, 0)),
        out_shape=jax.ShapeDtypeStruct((G, 2, R, cols_pad), BF16),
        compiler_params=_cp(("parallel", "parallel", "parallel")),
        name="cast_split_halves",
    )(w)


def _cast_pad_rows_kernel(x_ref, o_ref):
    r = x_ref.shape[1]
    o_ref[0, 0:r, :] = x_ref[0].astype(o_ref.dtype)
    o_ref[0, r:, :] = jnp.zeros((o_ref.shape[1] - r, o_ref.shape[2]), o_ref.dtype)


def cast_pad_rows(w, rows_pad):
    G, R, C = w.shape
    tc = _pick(C, (256, 128))
    assert R % 16 == 0 and rows_pad % 16 == 0
    return pl.pallas_call(
        _cast_pad_rows_kernel,
        grid=(G, C // tc),
        in_specs=[pl.BlockSpec((1, R, tc), lambda g, c: (g, 0, c))],
        out_specs=pl.BlockSpec((1, rows_pad, tc), lambda g, c: (g, 0, c)),
        out_shape=jax.ShapeDtypeStruct((G, rows_pad, C), BF16),
        compiler_params=_cp(("parallel", "parallel")),
        name="cast_pad_rows",
    )(w)


def _outproj_kernel(om_ref, oe_ref, w1_ref, w2_ref, x_ref, y_ref):
    y_ref[...] = x_ref[...] + _dot(om_ref[...], w1_ref[...]) + _dot(oe_ref[...], w2_ref[...])


def out_proj(o_mix, o_mem, w_out, layer, x):
    M = x.shape[0]
    tm = _pick(M, (1024, 512, 256, 128))
    tn = 512
    assert MIX_DIM % MEM_DIM == 0
    return pl.pallas_call(
        _outproj_kernel,
        grid=(M // tm, D_MODEL // tn),
        in_specs=[
            pl.BlockSpec((tm, MIX_DIM), lambda i, j: (i, 0)),
            pl.BlockSpec((tm, MEM_DIM), lambda i, j: (i, 0)),
            _layer_spec((layer,), (MIX_DIM, tn), lambda i, j: (0, j)),
            _layer_spec((layer,), (MEM_DIM, tn), lambda i, j: (MIX_DIM // MEM_DIM, j)),
            pl.BlockSpec((tm, tn), lambda i, j: (i, j)),
        ],
        out_specs=pl.BlockSpec((tm, tn), lambda i, j: (i, j)),
        out_shape=jax.ShapeDtypeStruct((M, D_MODEL), F32),
        compiler_params=_cp(("parallel", "parallel")),
        name="out_proj",
    )(o_mix, o_mem, w_out, w_out, x)


def _mem_attn_kernel(q_ref, kv_ref, o_ref):
    scale = HEAD_DIM ** -0.5
    slots = 2 * N_MEM_HEADS
    n_mem = kv_ref.shape[1] // slots
    for h in range(N_MEM_HEADS):
        q = q_ref[0, :, h * HEAD_DIM:(h + 1) * HEAD_DIM].astype(BF16)
        k = kv_ref[0, pl.ds(h, n_mem, stride=slots), :].astype(BF16)
        v = kv_ref[0, pl.ds(N_MEM_HEADS + h, n_mem, stride=slots), :].astype(BF16)
        s = _dot_nt(q, k) * scale
        m = jnp.max(s, axis=-1, keepdims=True)
        p = jnp.exp(s - m)
        p = p / jnp.sum(p, axis=-1, keepdims=True)
        o_ref[0, :, h * HEAD_DIM:(h + 1) * HEAD_DIM] = _dot(p.astype(BF16), v).astype(o_ref.dtype)


def mem_attention(z3, col_block, kv, lead=()):
    B, L, _ = z3.shape
    rows = kv.shape[-2]
    tm = _pick(L, (1024, 512, 256, 128))
    return pl.pallas_call(
        _mem_attn_kernel,
        grid=(B, L // tm),
        in_specs=[
            pl.BlockSpec((1, tm, MEM_DIM), lambda b, i: (b, i, col_block)),
            _layer_spec(lead, (1, rows, HEAD_DIM), lambda b, i: (b, 0, 0)),
        ],
        out_specs=pl.BlockSpec((1, tm, MEM_DIM), lambda b, i: (b, i, 0)),
        out_shape=jax.ShapeDtypeStruct((B, L, MEM_DIM), BF16),
        compiler_params=_cp(("parallel", "parallel")),
        name="mem_attention",
    )(z3, kv)


def _ffn_kernel(*refs, blocks_per_seq, decode):
    if decode:
        (x_ref, g_ref, wa_ref, wu_ref, cw_ref, cb_ref, wd_ref, p1_ref, p2_ref,
         y_ref, at_ref, xn_ref) = refs
    else:
        (x_ref, g_ref, wa_ref, wu_ref, cw_ref, cb_ref, wd_ref,
         y_ref, at_ref, xn_ref, carry_ref) = refs
    i = pl.program_id(0)
    j = pl.program_id(1)

    @pl.when(j == 0)
    def _():
        x = x_ref[...]
        y = x * lax.rsqrt(jnp.mean(x * x, axis=-1, keepdims=True) + RMS_EPS)
        xn_ref[...] = (y * g_ref[...]).astype(BF16)
        y_ref[...] = x

    if not decode:
        @pl.when((i % blocks_per_seq) == 0)
        def _():
            carry_ref[j] = jnp.zeros(carry_ref.shape[1:], F32)

    xn = xn_ref[...]
    tm = xn.shape[0]
    n_tail = at_ref.shape[1]
    half = wa_ref.shape[1] // 2
    down = None
    for cs in (slice(0, half), slice(half, 2 * half)):
        a = _dot(xn, wa_ref[:, cs])
        u = _dot(xn, wu_ref[:, cs])
        if decode:
            a1 = p1_ref[:, cs]
            a2 = p2_ref[:, cs]
        else:
            prev = carry_ref[j, :, cs]
            row = lax.broadcasted_iota(jnp.int32, a.shape, 0)
            a1 = jnp.where(row == 0, prev[7:8], pltpu.roll(a, 1, 0))
            a2 = jnp.where(row == 0, prev[6:7], jnp.where(row == 1, prev[7:8], pltpu.roll(a, 2, 0)))
            carry_ref[j, :, cs] = a[tm - 8:tm]
        at_ref[0, :, cs] = a[tm - n_tail:tm]
        c = cb_ref[:, cs] + cw_ref[2:3, cs] * a
        c = c + cw_ref[0:1, cs] * a2
        c = c + cw_ref[1:2, cs] * a1
        h = (jax.nn.gelu(c) * u).astype(BF16)
        d = _dot(h, wd_ref[cs, :])
        down = d if down is None else down + d
    y_ref[...] += down


def conv_ffn(x, g, w_up, cw, cb, w_down, layer, *, seq_len=None, prev=None):
    M = x.shape[0]
    decode = prev is not None
    tm = M if decode else _pick(seq_len, (1024, 512, 256, 128))
    n_tail = tm if decode else 8
    nf = D_FF_PAD // FFN_TF
    in_specs = [
        pl.BlockSpec((tm, D_MODEL), lambda i, j: (i, 0)),
        pl.BlockSpec((1, D_MODEL), lambda i, j: (0, 0)),
        _layer_spec((layer, 0), (D_MODEL, FFN_TF), lambda i, j: (0, j)),
        _layer_spec((layer, 1), (D_MODEL, FFN_TF), lambda i, j: (0, j)),
        pl.BlockSpec((8, FFN_TF), lambda i, j: (0, j)),
        pl.BlockSpec((1, FFN_TF), lambda i, j: (0, j)),
        _layer_spec((layer,), (FFN_TF, D_MODEL), lambda i, j: (j, 0)),
    ]
    args = [x, g.reshape(1, D_MODEL), w_up, w_up, cw, cb, w_down]
    scratch = [pltpu.VMEM((tm, D_MODEL), BF16)]
    if decode:
        in_specs += [pl.BlockSpec((tm, FFN_TF), lambda i, j: (i, j))] * 2
        args += list(prev)
    else:
        scratch.append(pltpu.VMEM((nf, 8, FFN_TF), F32))
    return pl.pallas_call(
        functools.partial(_ffn_kernel, blocks_per_seq=(1 if decode else seq_len // tm), decode=decode),
        grid=(M // tm, nf),
        in_specs=in_specs,
        out_specs=[
            pl.BlockSpec((tm, D_MODEL), lambda i, j: (i, 0)),
            pl.BlockSpec((1, n_tail, FFN_TF), lambda i, j: (i, 0, j)),
        ],
        out_shape=[
            jax.ShapeDtypeStruct((M, D_MODEL), F32),
            jax.ShapeDtypeStruct((M // tm, n_tail, D_FF_PAD), F32),
        ],
        scratch_shapes=scratch,
        compiler_params=_cp(("arbitrary", "arbitrary")),
        name="conv_ffn",
    )(*args)


def _hgrn_gates(q, f, log_lb, log1m_lb):
    qs = jax.nn.silu(q)
    log_f = jnp.logaddexp(log_lb, log1m_lb + jax.nn.log_sigmoid(f))
    k = 1.0 - jnp.exp(log_f)
    return qs, k, log_f


def _hgrn_out(o, g, onorm):
    y = o * lax.rsqrt(jnp.mean(o * o, axis=-1, keepdims=True) + RMS_EPS)
    return (y * onorm) * jax.nn.silu(g)


def _hgrn_kernel(q_ref, f_ref, i_ref, g_ref, llb_ref, l1m_ref, on_ref, s0_ref, o_ref, s_ref, st_ref,
                 *, n_chunks):
    c = pl.program_id(2)
    T = HGRN_T
    n_heads = st_ref.shape[0]

    @pl.when(c == 0)
    def _():
        for h in range(n_heads):
            st_ref[h] = s0_ref[0, h].T

    assert HGRN_SUB == 8
    rl1 = lax.broadcasted_iota(jnp.int32, (T, 1), 0) % HGRN_SUB
    t_i = lax.broadcasted_iota(jnp.int32, (T, T), 0)
    s_i = lax.broadcasted_iota(jnp.int32, (T, T), 1)
    tril = jnp.where(s_i <= t_i, 1.0, 0.0).astype(BF16)
    pair_masks = []
    size = 2 * HGRN_SUB
    while size <= T:
        same = (t_i & -size) == (s_i & -size)
        pair_masks.append((size, same & ((t_i & (size - 1)) >= size // 2) & ((s_i & (size - 1)) < size // 2)))
        size *= 2

    def shift_rows(x, d):
        return pltpu.roll(x.reshape(T // HGRN_SUB, HGRN_SUB, HEAD_DIM), d, 1).reshape(T, HEAD_DIM)

    def one_head(q, k, b, v, h, sl):
        hs = slice(h * HEAD_DIM, (h + 1) * HEAD_DIM)
        st = st_ref[h]
        o = _dot_nt((q * jnp.exp2(b)).astype(BF16), st.astype(BF16))
        o = o + jnp.sum(q * k, axis=-1, keepdims=True) * v
        for d in range(1, HGRN_SUB):
            w = jnp.exp2(b - shift_rows(b, d))
            a = jnp.sum(q * shift_rows(k, d) * w, axis=-1, keepdims=True)
            o = o + jnp.where(rl1 >= d, a, 0.0) * shift_rows(v, d)
        att = jnp.zeros((T, T), F32)
        for size, keep in pair_masks:
            half = size // 2
            refs = [jnp.broadcast_to(b[j * size + half - 1:j * size + half], (size, HEAD_DIM))
                    for j in range(T // size)]
            r = jnp.concatenate(refs, axis=0) if len(refs) > 1 else refs[0]
            qt = q * jnp.exp2(jnp.minimum(b - r, 0.0))
            kt = k * jnp.exp2(jnp.minimum(r - b, 0.0))
            att = att + jnp.where(keep, _dot_nt(qt.astype(BF16), kt.astype(BF16)), 0.0)
        o = o + _dot(att.astype(BF16), v.astype(BF16))
        bl = b[T - 1:T]
        kt = k * jnp.exp2(bl - b)
        st_ref[h] = st * jnp.exp2(bl) + _dot_tn(v.astype(BF16), kt.astype(BF16))
        o_ref[0, sl, hs] = _hgrn_out(o, g_ref[0, sl, hs], on_ref[:, hs]).astype(o_ref.dtype)

    def chunk(ci, carry):
        sl = pl.ds(pl.multiple_of(ci * T, T), T)
        q, k, lf = _hgrn_gates(q_ref[0, sl, :], f_ref[0, sl, :], llb_ref[...], l1m_ref[...])
        b = sum(_dot(tril, part) for part in _split3(lf)) * LOG2E
        for h in range(n_heads):
            hs = slice(h * HEAD_DIM, (h + 1) * HEAD_DIM)
            one_head(q[:, hs], k[:, hs], b[:, hs], i_ref[0, sl, hs], h, sl)
        return carry

    lax.fori_loop(0, n_chunks, chunk, 0)

    @pl.when(c == pl.num_programs(2) - 1)
    def _():
        for h in range(n_heads):
            s_ref[0, h] = st_ref[h].T


def hgrn_prompt(z3, log_lb, log1m_lb, onorm, s0):
    B, L, _ = z3.shape
    H = N_MIX_HEADS
    hb = HGRN_HEADS_PER_STEP
    wb = hb * HEAD_DIM
    tc = _pick(L, (512, 256, 128, 64))
    zspec = lambda k: pl.BlockSpec((1, tc, wb), lambda b, h, c: (b, c, k * (H // hb) + h))
    vspec = pl.BlockSpec((1, wb), lambda b, h, c: (0, h))
    sspec = pl.BlockSpec((1, hb, HEAD_DIM, HEAD_DIM), lambda b, h, c: (b, h, 0, 0))
    return pl.pallas_call(
        functools.partial(_hgrn_kernel, n_chunks=tc // HGRN_T),
        grid=(B, H // hb, L // tc),
        in_specs=[zspec(0), zspec(1), zspec(2), zspec(3), vspec, vspec, vspec, sspec],
        out_specs=[pl.BlockSpec((1, tc, wb), lambda b, h, c: (b, c, h)), sspec],
        out_shape=[
            jax.ShapeDtypeStruct((B, L, MIX_DIM), BF16),
            jax.ShapeDtypeStruct((B, H, HEAD_DIM, HEAD_DIM), F32),
        ],
        scratch_shapes=[pltpu.VMEM((hb, HEAD_DIM, HEAD_DIM), F32)],
        compiler_params=_cp(("parallel", "parallel", "arbitrary")),
        name="hgrn_chunked",
    )(z3, z3, z3, z3, log_lb, log1m_lb, onorm, s0)


def _hgrn_step_kernel(z_ref, llb_ref, l1m_ref, on_ref, s0_ref, o_ref, s_ref):
    for h in range(N_MIX_HEADS):
        col = lambda k: slice((k * N_MIX_HEADS + h) * HEAD_DIM, (k * N_MIX_HEADS + h + 1) * HEAD_DIM)
        hs = slice(h * HEAD_DIM, (h + 1) * HEAD_DIM)
        q, k, lf = _hgrn_gates(z_ref[0, :, col(0)], z_ref[0, :, col(1)], llb_ref[:, hs], l1m_ref[:, hs])
        v = z_ref[0, :, col(2)]
        g = z_ref[0, :, col(3)]
        rows = jnp.concatenate([q, k, jnp.exp(lf), jnp.zeros((5, HEAD_DIM), F32)], axis=0)
        cols = rows.T
        s_new = cols[:, 2:3] * s0_ref[0, h] + cols[:, 1:2] * v
        s_ref[0, h] = s_new
        o = jnp.sum(cols[:, 0:1] * s_new, axis=0, keepdims=True)
        o_ref[0, :, hs] = _hgrn_out(o, g, on_ref[:, hs]).astype(o_ref.dtype)


def hgrn_step(z3, log_lb, log1m_lb, onorm, s0):
    B = z3.shape[0]
    W = z3.shape[2]
    H = N_MIX_HEADS
    vspec = pl.BlockSpec((1, MIX_DIM), lambda b: (0, 0))
    sspec = pl.BlockSpec((1, H, HEAD_DIM, HEAD_DIM), lambda b: (b, 0, 0, 0))
    return pl.pallas_call(
        _hgrn_step_kernel,
        grid=(B,),
        in_specs=[pl.BlockSpec((1, 1, W), lambda b: (b, 0, 0)), vspec, vspec, vspec, sspec],
        out_specs=[pl.BlockSpec((1, 1, MIX_DIM), lambda b: (b, 0, 0)), sspec],
        out_shape=[
            jax.ShapeDtypeStruct((B, 1, MIX_DIM), BF16),
            jax.ShapeDtypeStruct((B, H, HEAD_DIM, HEAD_DIM), F32),
        ],
        compiler_params=_cp(("parallel",)),
        name="hgrn_step",
    )(z3, log_lb, log1m_lb, onorm, s0)


def _pe_proj_kernel(pe_ref, w_ref, o_ref):
    pe = jnp.broadcast_to(pe_ref[0], (8, pe_ref.shape[2])).astype(BF16)
    o_ref[0] = _dot(pe, w_ref[0])


def pe_proj(pe, w1):
    K = pe.shape[2]
    return pl.pallas_call(
        _pe_proj_kernel,
        grid=(2,),
        in_specs=[pl.BlockSpec((1, 1, K), lambda i: (i, 0, 0)), pl.BlockSpec((1, K, CMP_HID), lambda i: (i, 0, 0))],
        out_specs=pl.BlockSpec((1, 8, CMP_HID), lambda i: (i, 0, 0)),
        out_shape=jax.ShapeDtypeStruct((2, 8, CMP_HID), F32),
        compiler_params=_cp(("parallel",)),
        name="cmp_pe_proj",
    )(pe, w1)


def _cmp_mlp_kernel(x_ref, w1_ref, w2_ref, pe_ref, o_ref):
    n_sub = x_ref.shape[1] // CMP_STRIDE
    half = CMP_STRIDE * HEAD_DIM
    x = jnp.concatenate([x_ref[0, pl.ds(p, n_sub, stride=CMP_STRIDE), :] for p in range(CMP_STRIDE)],
                        axis=1).astype(BF16)
    a = _dot(x, w1_ref[0, 0:half])
    bm = _dot(x, w1_ref[0, half:2 * half])
    n = a.shape[0]
    h = a + pltpu.roll(bm, n - 1, 0) + pe_ref[0, 0:1]
    o_ref[0, 0] = _dot(jax.nn.gelu(h).astype(BF16), w2_ref[0])


def cmp_mlp(rows3, w1, w2, pe_h):
    B, L, _ = rows3.shape
    n_sub = L // CMP_STRIDE
    half = CMP_STRIDE * HEAD_DIM
    return pl.pallas_call(
        _cmp_mlp_kernel,
        grid=(2, B, N_KV),
        in_specs=[
            pl.BlockSpec((1, L, HEAD_DIM), lambda k, b, n: (b, 0, k * N_KV + n)),
            pl.BlockSpec((1, 2 * half, CMP_HID), lambda k, b, n: (k, 0, 0)),
            pl.BlockSpec((1, CMP_HID, HEAD_DIM), lambda k, b, n: (k, 0, 0)),
            pl.BlockSpec((1, 8, CMP_HID), lambda k, b, n: (k, 0, 0)),
        ],
        out_specs=pl.BlockSpec((1, 1, n_sub, HEAD_DIM), lambda k, b, n: (k, b * N_KV + n, 0, 0)),
        out_shape=jax.ShapeDtypeStruct((2, B * N_KV, n_sub, HEAD_DIM), F32),
        compiler_params=_cp(("parallel", "parallel", "parallel")),
        name="cmp_mlp",
    )(rows3, w1, w2, pe_h)


LOG2E = 1.4426950408889634


def _masked_softmax_rows(s, mask, exp_fn=jnp.exp):
    s = jnp.where(mask, s, -jnp.inf)
    m = jnp.max(s, axis=-1, keepdims=True)
    m = jnp.where(m == -jnp.inf, 0.0, m)
    p = exp_fn(s - m)
    return p / jnp.maximum(jnp.sum(p, axis=-1, keepdims=True), 1e-30)


def _split3(p):
    hi = p.astype(BF16)
    r = p - hi.astype(F32)
    mid = r.astype(BF16)
    lo = (r - mid.astype(F32)).astype(BF16)
    return hi, mid, lo


def _masked_softmax_cols(s, mask, exp_fn):
    s = jnp.where(mask, s, -jnp.inf)
    m = jnp.max(s, axis=0, keepdims=True)
    m = jnp.where(m == -jnp.inf, 0.0, m)
    p = exp_fn(s - m)
    return p / jnp.maximum(jnp.sum(p, axis=0, keepdims=True), 1e-30)


def _nsa_prompt_t_kernel(q_ref, gt_ref, kc_ref, vc_ref, ks_ref, vs_ref, kw_ref, vw_ref, o_ref,
                         kaug_ref, vst_ref, kwb_ref, vwt_ref, vct_ref, score_ref, *, tq, tk, seq_len):
    qi = pl.program_id(2)
    n_blk = seq_len // SLC_BLOCK
    scale = HEAD_DIM ** -0.5 * LOG2E
    cols = GROUP * tq
    nc = kc_ref.shape[1]

    @pl.when(qi == 0)
    def _():
        r = lax.broadcasted_iota(jnp.int32, (seq_len, HEAD_DIM), 0) // SLC_BLOCK
        col = lax.broadcasted_iota(jnp.int32, (seq_len, HEAD_DIM), 1)
        kaug_ref[:, 0:HEAD_DIM] = ks_ref[0].astype(BF16)
        kaug_ref[:, HEAD_DIM:2 * HEAD_DIM] = jnp.where(r == col, 1.0, 0.0).astype(BF16)
        kwb_ref[...] = kw_ref[0].astype(BF16)
        for t in range(seq_len // tk):
            vst_ref[t] = vs_ref[0, t * tk:(t + 1) * tk, :].T.astype(BF16)
        for t in range(seq_len // tq):
            vwt_ref[t] = vw_ref[0, t * tq:(t + 1) * tq, :].T.astype(BF16)
        vct_ref[...] = vc_ref[0].T.astype(BF16)

    q0 = qi * tq
    qs = q_ref[0] * scale
    qb = jnp.concatenate([qs[:, g * HEAD_DIM:(g + 1) * HEAD_DIM] for g in range(GROUP)], axis=0).astype(BF16)

    def qpos(n_keys):
        return q0 + lax.broadcasted_iota(jnp.int32, (n_keys, cols), 1) % tq

    def kidx(n_keys):
        return lax.broadcasted_iota(jnp.int32, (n_keys, cols), 0)

    cend = CMP_STRIDE * kidx(nc) + (2 * CMP_STRIDE - 1)
    p_c = _masked_softmax_cols(_dot_nt(kc_ref[0].astype(BF16), qb), cend <= qpos(nc), jnp.exp2)
    o_cmp = _dot(vct_ref[...], p_c.astype(BF16))

    psum = p_c[:, 0:tq]
    for g in range(1, GROUP):
        psum = psum + p_c[:, g * tq:(g + 1) * tq]
    ji = lax.broadcasted_iota(jnp.int32, (HEAD_DIM, nc), 0)
    ci = lax.broadcasted_iota(jnp.int32, (HEAD_DIM, nc), 1)
    ratio = SLC_BLOCK // CMP_STRIDE
    member = ((ci >= ratio * ji - 1) & (ci <= ratio * ji + ratio - 1)).astype(BF16)
    imp = sum(_dot(member, part) for part in _split3(psum))
    blk = lax.broadcasted_iota(jnp.int32, (HEAD_DIM, tq), 0)
    cur = (q0 + lax.broadcasted_iota(jnp.int32, (HEAD_DIM, tq), 1)) // SLC_BLOCK
    valid = (blk <= cur) & (blk < n_blk)
    forced = valid & ((blk == 0) | (blk == cur) | (blk == cur - 1))
    score = jnp.where(forced, SEL_BONUS, jnp.where(valid, imp, -jnp.inf))
    score = score[0:n_blk]
    score_ref[...] = score
    blk = lax.broadcasted_iota(jnp.int32, (n_blk, tq), 0)
    per_tile = tq // SLC_BLOCK

    def rank_body(it, rank):
        for u in range(per_tile):
            i = it * per_tile + u
            si = score_ref[pl.ds(i, 1), :]
            later = jnp.where(blk > i, 1, 0)
            rank = rank + jnp.where(si > score, 1, jnp.where(si == score, later, 0))
        return rank

    rank = lax.fori_loop(0, qi + 1, rank_body, jnp.zeros((n_blk, tq), jnp.int32))
    bias_t = jnp.where(rank < N_SELECT, 0.0, NEG_BIG)
    if n_blk < HEAD_DIM:
        bias_t = jnp.concatenate([bias_t, jnp.zeros((HEAD_DIM - n_blk, tq), F32)], axis=0)
    bias = bias_t.T.astype(BF16)
    qaug = jnp.concatenate([qb, jnp.concatenate([bias] * GROUP, axis=0)], axis=1)

    qpos_k = qpos(tk)
    krow = kidx(tk)

    def slc_body(kt, carry, causal):
        m, l, acc = carry
        s = _dot_nt(kaug_ref[pl.ds(pl.multiple_of(kt * tk, tk), tk), :], qaug)
        if causal:
            s = jnp.where(kt * tk + krow <= qpos_k, s, -jnp.inf)
        m_new = jnp.maximum(m, jnp.max(s, axis=0, keepdims=True))
        alpha = jnp.exp2(m - m_new)
        p = jnp.exp2(s - m_new)
        l = alpha * l + jnp.sum(p, axis=0, keepdims=True)
        acc = alpha * acc + _dot(vst_ref[kt], p.astype(BF16))
        return m_new, l, acc

    init = (jnp.full((1, cols), -jnp.inf, F32), jnp.zeros((1, cols), F32), jnp.zeros((HEAD_DIM, cols), F32))
    n_full = q0 // tk

    def two_tiles(i, carry):
        return slc_body(2 * i + 1, slc_body(2 * i, carry, False), False)

    carry = lax.fori_loop(0, n_full // 2, two_tiles, init)
    carry = lax.fori_loop(n_full - n_full % 2, n_full, functools.partial(slc_body, causal=False), carry)
    _, l_s, acc_s = slc_body(n_full, carry, True)
    o_slc = acc_s / jnp.maximum(l_s, 1e-30)

    wk = min(WINDOW + tq, seq_len)
    k0 = jnp.minimum(jnp.maximum(q0 - WINDOW, 0), seq_len - wk)
    d = qpos(wk) - (k0 + kidx(wk))
    s_w = _dot_nt(kwb_ref[pl.ds(pl.multiple_of(k0, tq), wk), :], qb)
    s_w = jnp.where((d >= 0) & (d <= WINDOW), s_w, -jnp.inf)
    p_w = jnp.exp2(s_w - jnp.max(s_w, axis=0, keepdims=True))
    l_w = jnp.sum(p_w, axis=0, keepdims=True)
    p_wb = p_w.astype(BF16)
    t0 = k0 // tq
    o_win = sum(_dot(vwt_ref[t0 + i], p_wb[i * tq:(i + 1) * tq]) for i in range(wk // tq))
    o_win = o_win / jnp.maximum(l_w, 1e-30)

    gates = jax.nn.sigmoid(gt_ref[0]).T
    for g in range(GROUP):
        cs = slice(g * tq, (g + 1) * tq)
        c0 = g * N_BRANCH
        o = (gates[c0:c0 + 1] * o_cmp[:, cs] + gates[c0 + 1:c0 + 2] * o_slc[:, cs]
             + gates[c0 + 2:c0 + 3] * o_win[:, cs])
        o_ref[0, :, g * HEAD_DIM:(g + 1) * HEAD_DIM] = o.T.astype(o_ref.dtype)


def nsa_prompt_t(zq3, gates3, k_cmp, v_cmp, rows3):
    B, L, _ = zq3.shape
    tq = _pick(L, (512, 256, 128))
    tk = _pick(L, (512, 256, 128))
    assert WINDOW % tq == 0 and tk % tq == 0
    n_sub = k_cmp.shape[1]
    qw = GROUP * HEAD_DIM
    rspec = lambda kind: pl.BlockSpec((1, L, HEAD_DIM), lambda b, n, i: (b, 0, kind * N_KV + n))
    cspec = pl.BlockSpec((1, n_sub, HEAD_DIM), lambda b, n, i: (b * N_KV + n, 0, 0))
    return pl.pallas_call(
        functools.partial(_nsa_prompt_t_kernel, tq=tq, tk=tk, seq_len=L),
        grid=(B, N_KV, L // tq),
        in_specs=[
            pl.BlockSpec((1, tq, qw), lambda b, n, i: (b, i, n)),
            pl.BlockSpec((1, tq, HEAD_DIM), lambda b, n, i: (b, i, n)),
            cspec, cspec, rspec(2), rspec(3), rspec(4), rspec(5),
        ],
        out_specs=pl.BlockSpec((1, tq, qw), lambda b, n, i: (b, i, n)),
        out_shape=jax.ShapeDtypeStruct((B, L, MIX_DIM), BF16),
        scratch_shapes=[
            pltpu.VMEM((L, 2 * HEAD_DIM), BF16),
            pltpu.VMEM((L // tk, HEAD_DIM, tk), BF16),
            pltpu.VMEM((L, HEAD_DIM), BF16),
            pltpu.VMEM((L // tq, HEAD_DIM, tq), BF16),
            pltpu.VMEM((HEAD_DIM, n_sub), BF16),
            pltpu.VMEM((L // SLC_BLOCK, tq), F32),
        ],
        compiler_params=_cp(("parallel", "parallel", "arbitrary")),
        name="nsa_prompt",
    )(zq3, gates3, k_cmp, v_cmp, rows3, rows3, rows3, rows3)


def _dec_cmp_kernel(pt_ref, *refs, n_pg, n_groups):
    del pt_ref
    pg_refs = refs[:n_pg]
    new_ref, w1_ref, w2_ref, pe_ref, kc_ref, vc_ref, carry_ref = refs[n_pg:]
    g = pl.program_id(1)
    sub_pg = PAGE_SIZE // CMP_STRIDE
    n_sub = n_pg * sub_pg
    pos_per_chunk = 4
    kw = pos_per_chunk * HEAD_DIM

    @pl.when(g == 0)
    def _():
        carry_ref[...] = jnp.zeros_like(carry_ref)

    sub_i = lax.broadcasted_iota(jnp.int32, (8, n_sub, kw), 1)
    new_row = jnp.concatenate([new_ref[0], jnp.zeros((8, kw - HEAD_DIM), F32)], axis=1)[:, None, :]
    ys = [None, None]
    for pc in range(CMP_STRIDE // pos_per_chunk):
        x = jnp.concatenate(
            [jnp.concatenate([jnp.swapaxes(pg_refs[r][0, pl.ds(p, sub_pg, stride=CMP_STRIDE), :, :], 0, 1)
                              for r in range(n_pg)], axis=1)
             for p in range(pc * pos_per_chunk, (pc + 1) * pos_per_chunk)], axis=2)
        x = jnp.where(g < n_groups, x, 0.0)
        if pc == 0:
            x = jnp.where((g == n_groups) & (sub_i == 0), new_row, x)
        xb = x.astype(BF16)
        for kind in range(2):
            part = _dot(xb[kind * N_KV:(kind + 1) * N_KV].reshape(N_KV * n_sub, kw),
                        w1_ref[kind, pc * kw:(pc + 1) * kw, :])
            ys[kind] = part if ys[kind] is None else ys[kind] + part

    row = lax.broadcasted_iota(jnp.int32, (n_sub, CMP_HID), 0)
    for kind, out_ref in enumerate((kc_ref, vc_ref)):
        y = ys[kind]
        for n in range(N_KV):
            slot = kind * N_KV + n
            a = y[n * n_sub:(n + 1) * n_sub, 0:CMP_HID]
            bm = y[n * n_sub:(n + 1) * n_sub, CMP_HID:2 * CMP_HID]
            a_prev = jnp.where(row == 0, carry_ref[slot:slot + 1, :], pltpu.roll(a, 1, 0))
            carry_ref[slot:slot + 1, :] = a[n_sub - 1:n_sub]
            h = a_prev + bm + pe_ref[kind, 0:1]
            out_ref[0, n] = _dot(jax.nn.gelu(h).astype(BF16), w2_ref[kind])


def dec_cmp_tokens(cache4, page_table, new_cmp, w1ab, w2, pe_h):
    B, n_pages = page_table.shape
    n_pg = _pick(n_pages, (8, 4, 2))
    assert n_pages % n_pg == 0
    n_groups = n_pages // n_pg
    sub_pg = PAGE_SIZE // CMP_STRIDE
    n_sub = n_pg * sub_pg
    n_steps = n_groups + 1
    while (n_steps * n_sub) % HEAD_DIM:
        n_steps += 1
    half = CMP_STRIDE * HEAD_DIM

    def page_spec(r):
        def imap(b, g, pt):
            return (pt[b * n_pages + jnp.minimum(g * n_pg + r, n_pages - 1)], 0, 0, 0)
        return pl.BlockSpec((1, PAGE_SIZE, 8, HEAD_DIM), imap)

    out_spec = pl.BlockSpec((1, N_KV, n_sub, HEAD_DIM), lambda b, g, pt: (b, 0, g, 0))
    grid_spec = pltpu.PrefetchScalarGridSpec(
        num_scalar_prefetch=1,
        grid=(B, n_steps),
        in_specs=[page_spec(r) for r in range(n_pg)] + [
            pl.BlockSpec((1, 8, HEAD_DIM), lambda b, g, pt: (b, 0, 0)),
            pl.BlockSpec((2, half, 2 * CMP_HID), lambda b, g, pt: (0, 0, 0)),
            pl.BlockSpec((2, CMP_HID, HEAD_DIM), lambda b, g, pt: (0, 0, 0)),
            pl.BlockSpec((2, 8, CMP_HID), lambda b, g, pt: (0, 0, 0)),
        ],
        out_specs=[out_spec, out_spec],
        scratch_shapes=[pltpu.VMEM((8, CMP_HID), F32)],
    )
    tok = jax.ShapeDtypeStruct((B, N_KV, n_steps * n_sub, HEAD_DIM), F32)
    return pl.pallas_call(
        functools.partial(_dec_cmp_kernel, n_pg=n_pg, n_groups=n_groups),
        grid_spec=grid_spec,
        out_shape=[tok, tok],
        compiler_params=_cp(("parallel", "arbitrary")),
        name="dec_cmp_tokens",
    )(page_table.reshape(-1), *([cache4] * n_pg), new_cmp, w1ab, w2, pe_h)


def _dec_select_kernel(q_ref, kc_ref, vc_ref, mem_ref, ocmp_ref, idx_ref, *, q_pos, nc, n_blk, n_sel):
    C = kc_ref.shape[2]
    JB = mem_ref.shape[1]
    scale = HEAD_DIM ** -0.5
    c_i = lax.broadcasted_iota(jnp.int32, (8, C), 1) - 1
    g_i = lax.broadcasted_iota(jnp.int32, (8, C), 0)
    ok = (c_i >= 0) & (c_i < nc) & (CMP_STRIDE * c_i + (2 * CMP_STRIDE - 1) <= q_pos)
    psums = []
    for n in range(N_KV):
        qb = (q_ref[0, n] * scale).astype(BF16)
        p = _masked_softmax_rows(_dot_nt(qb, kc_ref[0, n].astype(BF16)), ok)
        ocmp_ref[0, n] = _dot(p.astype(BF16), vc_ref[0, n].astype(BF16))
        psums.append(jnp.sum(jnp.where(g_i < GROUP, p, 0.0), axis=0, keepdims=True))
    psum = jnp.concatenate(psums + [jnp.zeros((8 - N_KV, C), F32)], axis=0)
    imp = sum(_dot(part, mem_ref[...]) for part in _split3(psum))
    j_i = lax.broadcasted_iota(jnp.int32, (8, JB), 1)
    cur = q_pos // SLC_BLOCK
    valid = (j_i <= cur) & (j_i < n_blk)
    forced = valid & ((j_i == 0) | (j_i == cur) | (j_i == cur - 1))
    score = jnp.where(forced, SEL_BONUS, jnp.where(valid, imp, -jnp.inf))
    j_f = j_i.astype(F32)
    taken = j_i >= n_blk
    lane_o = lax.broadcasted_iota(jnp.int32, (8, HEAD_DIM), 1)
    out = jnp.zeros((8, HEAD_DIM), F32)
    for r in range(n_sel):
        m = jnp.max(jnp.where(taken, -jnp.inf, score), axis=-1, keepdims=True)
        cand = jnp.logical_not(taken) & (score == m)
        idx = jnp.min(jnp.where(cand, j_f, float(JB)), axis=-1, keepdims=True)
        out = jnp.where(lane_o == r, idx, out)
        taken = taken | (j_f == idx)
    idx_ref[0] = out.astype(jnp.int32)


def dec_select(q8, k_cmp, v_cmp, member, *, q_pos, nc, n_blk, n_sel):
    B, _, C, _ = k_cmp.shape
    JB = member.shape[1]
    head_spec = pl.BlockSpec((1, N_KV, 8, HEAD_DIM), lambda b: (b, 0, 0, 0))
    tok_spec = pl.BlockSpec((1, N_KV, C, HEAD_DIM), lambda b: (b, 0, 0, 0))
    return pl.pallas_call(
        functools.partial(_dec_select_kernel, q_pos=q_pos, nc=nc, n_blk=n_blk, n_sel=n_sel),
        grid=(B,),
        in_specs=[head_spec, tok_spec, tok_spec, pl.BlockSpec((C, JB), lambda b: (0, 0))],
        out_specs=[head_spec, pl.BlockSpec((1, 8, HEAD_DIM), lambda b: (b, 0, 0))],
        out_shape=[
            jax.ShapeDtypeStruct((B, N_KV, 8, HEAD_DIM), F32),
            jax.ShapeDtypeStruct((B, 8, HEAD_DIM), jnp.int32),
        ],
        compiler_params=_cp(("parallel",)),
        name="dec_select",
    )(q8, k_cmp, v_cmp, member)


def _dec_attend_kernel(idx_ref, pt_ref, q_ref, gl_ref, ocmp_ref, nslc_ref, win_ref, nwin_ref, *rest,
                       q_pos, past_len, n_sel):
    del pt_ref
    blk_refs = rest[:n_sel]
    o_ref = rest[n_sel]
    b = pl.program_id(0)
    n = pl.program_id(1)
    base = (b * N_KV + n) * n_sel
    scale = HEAD_DIM ** -0.5
    past_blocks = past_len // SLC_BLOCK
    rb = SLC_BLOCK * 8
    rw = win_ref.shape[1]
    w_buf = rw // 8
    qf = q_ref[0, 0] * scale
    qb = qf.astype(BF16)
    js = [idx_ref[base + s] for s in range(n_sel)]
    has_new = js[0] == past_blocks
    for s in range(1, n_sel):
        has_new = has_new | (js[s] == past_blocks)
    new_ok = has_new & (past_len <= q_pos)

    def head_rows(x):
        slot = lax.broadcasted_iota(jnp.int32, (8, HEAD_DIM), 0)
        k = jnp.sum(jnp.where(slot == n, x, 0.0), axis=0, keepdims=True)
        v = jnp.sum(jnp.where(slot == n + N_KV, x, 0.0), axis=0, keepdims=True)
        return k, v

    def finish(m):
        return jnp.where(m == -jnp.inf, 0.0, m)

    r_s = lax.broadcasted_iota(jnp.int32, (8, rb), 1)
    t_s = r_s >> 3
    mine_s = (r_s & 7) == n

    def slc_scores(s):
        blk = blk_refs[s][0, 0].reshape(rb, HEAD_DIM)
        sc = _dot_nt(qb, blk.astype(BF16))
        ok = mine_s & (js[s] * SLC_BLOCK + t_s <= q_pos) & (js[s] < past_blocks)
        return jnp.where(ok, sc, -jnp.inf)

    k_new, v_new = head_rows(nslc_ref[0])
    s_new = jnp.where(new_ok, jnp.sum(qf * k_new, axis=-1, keepdims=True), -jnp.inf)
    scores = [slc_scores(s) for s in range(n_sel)]
    m = s_new
    for sc in scores:
        m = jnp.maximum(m, jnp.max(sc, axis=-1, keepdims=True))
    m = finish(m)
    p_new = jnp.exp(s_new - m)
    l = p_new
    acc = p_new * v_new
    for s, sc in enumerate(scores):
        p = jnp.exp(sc - m)
        l = l + jnp.sum(p, axis=-1, keepdims=True)
        vals = pltpu.roll(blk_refs[s][0, 0].reshape(rb, HEAD_DIM), rb - N_KV, 0)
        acc = acc + _dot(p.astype(BF16), vals.astype(BF16))
    o_slc = acc / jnp.maximum(l, 1e-30)

    r_w = lax.broadcasted_iota(jnp.int32, (8, rw), 1)
    w_pos = past_len - w_buf + (r_w >> 3)
    w_ok = ((r_w & 7) == n) & (q_pos - w_pos >= 0) & (q_pos - w_pos <= WINDOW) & (w_pos >= 0)
    win = win_ref[0]
    s_w = jnp.where(w_ok, _dot_nt(qb, win.astype(BF16)), -jnp.inf)
    k_nw, v_nw = head_rows(nwin_ref[0])
    nw_ok = (q_pos - past_len >= 0) and (q_pos - past_len <= WINDOW)
    s_nw = jnp.sum(qf * k_nw, axis=-1, keepdims=True) if nw_ok else jnp.full((8, 1), -jnp.inf, F32)
    m = finish(jnp.maximum(s_nw, jnp.max(s_w, axis=-1, keepdims=True)))
    p_w = jnp.exp(s_w - m)
    p_nw = jnp.exp(s_nw - m)
    l = p_nw + jnp.sum(p_w, axis=-1, keepdims=True)
    acc = p_nw * v_nw + _dot(p_w.astype(BF16), pltpu.roll(win, rw - N_KV, 0).astype(BF16))
    o_win = acc / jnp.maximum(l, 1e-30)

    gates = jax.nn.sigmoid(gl_ref[0, 0])
    o_ref[0, 0] = gates[:, 0:1] * ocmp_ref[0, 0] + gates[:, 1:2] * o_slc + gates[:, 2:3] * o_win


def dec_attend(idx_flat, page_table, q8, gate8, ocmp8, new_slc, win2, new_win, cache5, *, q_pos, past_len, n_sel):
    B, n_pages = page_table.shape
    rw = win2.shape[1]
    past_blocks = past_len // SLC_BLOCK
    per_page = PAGE_SIZE // SLC_BLOCK

    def blk_spec(s):
        def imap(b, n, idx, pt):
            j = jnp.minimum(idx[(b * N_KV + n) * n_sel + s], past_blocks - 1)
            return (pt[b * n_pages + j // per_page], j % per_page, 0, 1, 0)
        return pl.BlockSpec((1, 1, SLC_BLOCK, 8, HEAD_DIM), imap)

    head_spec = pl.BlockSpec((1, 1, 8, HEAD_DIM), lambda b, n, idx, pt: (b, n, 0, 0))
    tok_spec = pl.BlockSpec((1, 8, HEAD_DIM), lambda b, n, idx, pt: (b, 0, 0))
    grid_spec = pltpu.PrefetchScalarGridSpec(
        num_scalar_prefetch=2,
        grid=(B, N_KV),
        in_specs=[head_spec, head_spec, head_spec, tok_spec,
                  pl.BlockSpec((1, rw, HEAD_DIM), lambda b, n, idx, pt: (b, 0, 0)), tok_spec]
        + [blk_spec(s) for s in range(n_sel)],
        out_specs=head_spec,
    )
    return pl.pallas_call(
        functools.partial(_dec_attend_kernel, q_pos=q_pos, past_len=past_len, n_sel=n_sel),
        grid_spec=grid_spec,
        out_shape=jax.ShapeDtypeStruct((B, N_KV, 8, HEAD_DIM), F32),
        compiler_params=_cp(("parallel", "arbitrary")),
        name="dec_attend",
    )(idx_flat, page_table.reshape(-1), q8, gate8, ocmp8, new_slc, win2, new_win, *([cache5] * n_sel))


def _pad_to(a, axis, size):
    pad = [(0, 0)] * a.ndim
    pad[axis] = (0, size - a.shape[axis])
    return jnp.pad(a, pad)


def _prep_weights(w_in_a, w_in_b, b_gate, w_kv, cmp_pe, cmp_w1, cmp_w2, w_mem_kv, w_out, w_up, conv_w, conv_b,
                  w_down, hgrn_lb):
    depth = w_out.shape[0]
    n_b = w_in_b.shape[0]
    n_gate = N_MIX_HEADS * N_BRANCH
    per_kv = GROUP * N_BRANCH
    half = CMP_STRIDE * HEAD_DIM
    w1 = cmp_w1.astype(BF16)
    lb = jnp.cumsum(jax.nn.softmax(hgrn_lb.astype(F32), axis=0), axis=0)
    lb = lb - lb[0]
    layers = lambda f, n=depth: [f(l) for l in range(n)]
    gate_w = lambda j: w_in_b[j][:, MIX_DIM:MIX_DIM + n_gate].reshape(D_MODEL, N_KV, per_kv)
    return dict(
        in_a=cast_bf16(w_in_a),
        in_b_qm=layers(lambda j: jnp.concatenate([w_in_b[j][:, :MIX_DIM], w_in_b[j][:, MIX_DIM + n_gate:]],
                                                 axis=-1).astype(BF16), n_b),
        in_b_gate=layers(lambda j: _pad_to(gate_w(j), 2, HEAD_DIM).reshape(D_MODEL, N_KV * HEAD_DIM).astype(BF16), n_b),
        b_gate=layers(lambda j: _pad_to(b_gate[j].reshape(N_KV, per_kv), 1, HEAD_DIM).reshape(N_KV * HEAD_DIM), n_b),
        kv=cast_bf16(w_kv[None]),
        mem_kv=cast_bf16(w_mem_kv),
        out=cast_bf16(w_out),
        up=cast_split_halves(w_up, D_FF_PAD),
        conv_w=layers(lambda l: _pad_to(_pad_to(conv_w[l], 1, D_FF_PAD), 0, 8)),
        conv_b=layers(lambda l: _pad_to(conv_b[l], 0, D_FF_PAD).reshape(1, D_FF_PAD)),
        down=cast_pad_rows(w_down, D_FF_PAD),
        cmp_w1=w1,
        cmp_w1ab=jnp.concatenate([w1[:, :half], w1[:, half:]], axis=2),
        cmp_w2=cmp_w2.astype(BF16),
        cmp_pe=cmp_pe.reshape(2, 1, 2 * half),
        log_lb=jnp.log(lb),
        log1m_lb=jnp.log1p(-lb),
    )


def kernel(x_prompt, x_sample, mem_prompt, cache_nsa_kv, page_table, cache_win_kv, state_hgrn, state_conv,
           cache_mem_kv, norm_mix, norm_ffn, norm_mem, norm_kv, norm_final, w_in_a, hgrn_lb, hgrn_onorm,
           w_in_b, b_gate, w_kv, cmp_pe, cmp_w1, cmp_w2, w_mem_kv, w_out, w_up, conv_w, conv_b, w_down):
    B, L, _ = x_prompt.shape
    Bs = x_sample.shape[0]
    depth = w_out.shape[0]
    n_a = w_in_a.shape[0]
    n_mem = mem_prompt.shape[1]
    n_pool, page, _, _, _ = cache_nsa_kv.shape
    n_pages = page_table.shape[1]
    past_len = n_pages * page
    w_buf = cache_win_kv.shape[1]
    assert x_sample.shape[1] == 1 and page == PAGE_SIZE and L % 128 == 0 and L // SLC_BLOCK <= HEAD_DIM
    W = _prep_weights(w_in_a, w_in_b, b_gate, w_kv, cmp_pe, cmp_w1, cmp_w2, w_mem_kv, w_out, w_up, conv_w,
                      conv_b, w_down, hgrn_lb)
    pe_h = pe_proj(W["cmp_pe"], W["cmp_w1"])
    onorm = hgrn_onorm.reshape(n_a, 1, MIX_DIM)
    log_lb = W["log_lb"].reshape(n_a, 1, MIX_DIM)
    log1m_lb = W["log1m_lb"].reshape(n_a, 1, MIX_DIM)

    def ffn(x, l, **kw):
        return conv_ffn(x, norm_ffn[l], W["up"], W["conv_w"][l], W["conv_b"][l], W["down"], l, **kw)

    M = B * L
    mem_flat = mem_prompt.reshape(B * n_mem, D_MODEL)
    mem_kv_both = [rms_matmul_heads(mem_flat, norm_mem[l], W["mem_kv"], (l,), 2 * MEM_DIM) for l in range(depth)]
    mem_kv_p = [by_head.reshape(B, n_mem * 2 * N_MEM_HEADS, HEAD_DIM) for _, by_head in mem_kv_both]
    mem_cache = cache_mem_kv.reshape(depth, Bs, n_mem * 2 * N_MEM_HEADS, HEAD_DIM)
    x = x_prompt.reshape(M, D_MODEL)
    hgrn_p, conv_p = [], []
    for l in range(depth):
        if l < n_a:
            z3 = rms_matmul(x, norm_mix[l], W["in_a"], lead=(l,)).reshape(B, L, -1)
            s0 = jnp.zeros((B, N_MIX_HEADS, HEAD_DIM, HEAD_DIM), F32)
            o_mix, s_new = hgrn_prompt(z3, log_lb[l], log1m_lb[l], onorm[l], s0)
            hgrn_p.append(s_new)
            o_mem = mem_attention(z3, 4 * MIX_DIM // MEM_DIM, mem_kv_p[l])
        else:
            j = l - n_a
            if j == 0:
                n_row4 = 4 * N_KV * HEAD_DIM
                rows_p, rows_by_head = rms_matmul_heads(x, norm_kv, W["kv"], (0,), n_row4)
                rows3 = rows_p.reshape(B, L, -1)
                cmp_p = cmp_mlp(rows3, W["cmp_w1"], W["cmp_w2"], pe_h)
            zq3 = rms_matmul(x, norm_mix[l], W["in_b_qm"][j]).reshape(B, L, -1)
            gt3 = rms_matmul(x, norm_mix[l], W["in_b_gate"][j], W["b_gate"][j]).reshape(B, L, -1)
            o_mix = nsa_prompt_t(zq3, gt3, cmp_p[0], cmp_p[1], rows3)
            o_mem = mem_attention(zq3, MIX_DIM // MEM_DIM, mem_kv_p[l])
        x = out_proj(o_mix.reshape(M, MIX_DIM), o_mem.reshape(M, MEM_DIM), W["out"], l, x)
        x, a_tail = ffn(x, l, seq_len=L)
        conv_p.append(a_tail.reshape(B, -1, 8, D_FF_PAD)[:, -1, 8 - (CONV_W - 1):, :D_FF])
    y_prompt = rmsnorm_rows(x, norm_final).reshape(B, L, D_MODEL)
    w_keep = min(WINDOW, L)
    nsa_rows_prompt = rows_by_head.reshape(B, L, 4, N_KV, HEAD_DIM)
    win_prompt = rows3[:, L - w_keep:, n_row4:].reshape(B, w_keep, 2, N_KV, HEAD_DIM)
    mem_kv_prompt = jnp.stack([by_head for _, by_head in mem_kv_both]).reshape(
        depth, B, n_mem, 2, N_MEM_HEADS, HEAD_DIM)

    Ms = 16
    pad_rows = lambda a: _pad_to(a, 0, Ms)
    xs = pad_rows(x_sample.reshape(Bs, D_MODEL))
    hgrn_s, conv_s = [], []
    q_pos = past_len
    t_pad = -(-(past_len + 1) // SLC_BLOCK) * SLC_BLOCK
    n_blk = t_pad // SLC_BLOCK
    nc = t_pad // CMP_STRIDE - 1
    n_sel = min(N_SELECT, n_blk)
    for l in range(depth):
        if l < n_a:
            z = rms_matmul(xs, norm_mix[l], W["in_a"], lead=(l,))
            z3 = z[:Bs].reshape(Bs, 1, -1)
            o_mix, s_new = hgrn_step(z3, log_lb[l], log1m_lb[l], onorm[l], state_hgrn[l])
            hgrn_s.append(s_new)
            o_mix = o_mix.reshape(Bs, MIX_DIM)
            o_mem = mem_attention(z3, 4 * MIX_DIM // MEM_DIM, mem_cache, lead=(l,))
        else:
            j = l - n_a
            if j == 0:
                rows_s = rms_matmul(xs, norm_kv, W["kv"], lead=(0,))[:Bs]
                new_cmp = rows_s[:, 0:1024].reshape(Bs, 8, HEAD_DIM)
                new_slc = rows_s[:, 1024:2048].reshape(Bs, 8, HEAD_DIM)
                new_win = rows_s[:, 2048:3072].reshape(Bs, 8, HEAD_DIM)
                cache4 = cache_nsa_kv.reshape(n_pool, PAGE_SIZE, 4 * N_KV, HEAD_DIM)
                cache5 = cache_nsa_kv.reshape(n_pool, PAGE_SIZE // SLC_BLOCK, SLC_BLOCK, 4 * N_KV, HEAD_DIM)
                win2 = cache_win_kv.reshape(Bs, w_buf * 2 * N_KV, HEAD_DIM)
                k_cmp_s, v_cmp_s = dec_cmp_tokens(cache4, page_table, new_cmp, W["cmp_w1ab"], W["cmp_w2"], pe_h)
                assert k_cmp_s.shape[2] > nc + 1
                ratio = SLC_BLOCK // CMP_STRIDE
                c_of_row = jnp.arange(k_cmp_s.shape[2])[:, None] - 1
                j_of_col = jnp.arange(-(-n_blk // HEAD_DIM) * HEAD_DIM)[None, :]
                member = ((c_of_row >= ratio * j_of_col - 1) & (c_of_row <= ratio * j_of_col + ratio - 1)).astype(BF16)
            zq = rms_matmul(xs, norm_mix[l], W["in_b_qm"][j])
            gt = rms_matmul(xs, norm_mix[l], W["in_b_gate"][j], W["b_gate"][j])[:Bs]
            z3 = zq[:Bs].reshape(Bs, 1, -1)
            per_head = lambda a: _pad_to(a.reshape(Bs, N_KV, GROUP, -1), 2, 8)
            q8 = per_head(zq[:Bs, :MIX_DIM])
            o_cmp, idx = dec_select(q8, k_cmp_s, v_cmp_s, member, q_pos=q_pos, nc=nc, n_blk=n_blk, n_sel=n_sel)
            idx_flat = idx[:, :N_KV, :n_sel].reshape(-1)
            gate9 = gt.reshape(Bs, N_KV, HEAD_DIM)[:, :, :GROUP * N_BRANCH]
            o_all = dec_attend(idx_flat, page_table, q8, _pad_to(per_head(gate9), 3, HEAD_DIM), o_cmp,
                               new_slc, win2, new_win, cache5, q_pos=q_pos, past_len=past_len, n_sel=n_sel)
            o_mix = o_all[:, :, :GROUP].reshape(Bs, MIX_DIM).astype(BF16)
            o_mem = mem_attention(z3, MIX_DIM // MEM_DIM, mem_cache, lead=(l,))
        xs = out_proj(pad_rows(o_mix), pad_rows(o_mem.reshape(Bs, MEM_DIM)), W["out"], l, xs)
        prev = (pad_rows(_pad_to(state_conv[l][:, 1], 1, D_FF_PAD)), pad_rows(_pad_to(state_conv[l][:, 0], 1, D_FF_PAD)))
        xs, a_new = ffn(xs, l, prev=prev)
        conv_s.append(jnp.stack([state_conv[l][:, 1], a_new[0, :Bs, :D_FF]], axis=1))
    y_sample = rmsnorm_rows(xs, norm_final)[:Bs].reshape(Bs, 1, D_MODEL)
    nsa_rows_sample = rows_s[:, :n_row4].reshape(Bs, 1, 4, N_KV, HEAD_DIM)
    win_new = rows_s[:, n_row4:].reshape(Bs, 1, 2, N_KV, HEAD_DIM).astype(cache_win_kv.dtype)
    win_sample = jnp.concatenate([cache_win_kv, win_new], axis=1)[:, 1:]

    return (y_prompt, y_sample, nsa_rows_prompt, nsa_rows_sample, win_prompt, win_sample,
            jnp.stack(hgrn_p), jnp.stack(hgrn_s), jnp.stack(conv_p), jnp.stack(conv_s), mem_kv_prompt)
```

```python
import functools

import jax
import jax.numpy as jnp
from jax import lax
from jax.experimental import pallas as pl
from jax.experimental.pallas import tpu as pltpu

F32 = jnp.float32
BF16 = jnp.bfloat16

D_MODEL = 2048
HEAD_DIM = 128
MIX_DIM = 1536
N_MIX_HEADS = 12
MEM_DIM = 512
N_MEM_HEADS = 4
N_KV = 4
GROUP = 3
N_BRANCH = 3
CMP_STRIDE = 16
CMP_HID = 256
SLC_BLOCK = 64
N_SELECT = 16
WINDOW = 512
PAGE_SIZE = 128
D_FF = 5504
D_FF_PAD = 5632
CONV_W = 3
RMS_EPS = 1e-6
SEL_BONUS = 1e9
NEG_BIG = -1e30

HGRN_T = 64
HGRN_SUB = 8
HGRN_HEADS_PER_STEP = 4
FFN_TF = 512
VMEM_LIMIT = 56 * 1024 * 1024


def _cp(sem, vmem=VMEM_LIMIT):
    return pltpu.CompilerParams(dimension_semantics=sem, vmem_limit_bytes=vmem)


def _dot(a, b):
    return jnp.dot(a, b, preferred_element_type=F32)


def _dot_nt(a, b):
    return lax.dot_general(a, b, (((1,), (1,)), ((), ())), preferred_element_type=F32)


def _dot_tn(a, b):
    return lax.dot_general(a, b, (((0,), (0,)), ((), ())), preferred_element_type=F32)


def _pick(n, cands):
    for c in cands:
        if n % c == 0:
            return c
    return n


def _rms_matmul_kernel(x_ref, g_ref, w_ref, b_ref, o_ref, xn_ref):
    @pl.when(pl.program_id(1) == 0)
    def _():
        x = x_ref[...]
        y = x * lax.rsqrt(jnp.mean(x * x, axis=-1, keepdims=True) + RMS_EPS)
        xn_ref[...] = (y * g_ref[...]).astype(BF16)

    o_ref[...] = _dot(xn_ref[...], w_ref[...]) + b_ref[...]


def _layer_spec(lead, block, imap):
    return pl.BlockSpec((None,) * len(lead) + tuple(block), lambda *a: tuple(lead) + tuple(imap(*a)))


def rms_matmul(x, g, w, bias=None, lead=()):
    M, K = x.shape
    N = w.shape[-1]
    tm = _pick(M, (1024, 512, 256, 128))
    tn = _pick(N, (512, 256, 128))
    if bias is None:
        bias = jnp.zeros((N,), F32)
    return pl.pallas_call(
        _rms_matmul_kernel,
        grid=(M // tm, N // tn),
        in_specs=[
            pl.BlockSpec((tm, K), lambda i, j: (i, 0)),
            pl.BlockSpec((1, K), lambda i, j: (0, 0)),
            _layer_spec(lead, (K, tn), lambda i, j: (0, j)),
            pl.BlockSpec((1, tn), lambda i, j: (0, j)),
        ],
        out_specs=pl.BlockSpec((tm, tn), lambda i, j: (i, j)),
        out_shape=jax.ShapeDtypeStruct((M, N), F32),
        scratch_shapes=[pltpu.VMEM((tm, K), BF16)],
        compiler_params=_cp(("parallel", "arbitrary")),
        name="rms_matmul",
    )(x, g.reshape(1, K), w, bias.reshape(1, N))


def _rms_matmul_heads_kernel(x_ref, g_ref, w_ref, o_ref, oh_ref, xn_ref, *, n_head_steps):
    j = pl.program_id(1)

    @pl.when(j == 0)
    def _():
        x = x_ref[...]
        y = x * lax.rsqrt(jnp.mean(x * x, axis=-1, keepdims=True) + RMS_EPS)
        xn_ref[...] = (y * g_ref[...]).astype(BF16)

    y = _dot(xn_ref[...], w_ref[...])
    o_ref[...] = y

    @pl.when(j < n_head_steps)
    def _():
        oh_ref[...] = y.reshape(oh_ref.shape)


def rms_matmul_heads(x, g, w, lead, head_cols):
    M, K = x.shape
    N = w.shape[-1]
    tm = _pick(M, (512, 256, 128))
    tn = 8 * HEAD_DIM
    assert N % tn == 0 and head_cols % tn == 0
    n_head_steps = head_cols // tn
    return pl.pallas_call(
        functools.partial(_rms_matmul_heads_kernel, n_head_steps=n_head_steps),
        grid=(M // tm, N // tn),
        in_specs=[
            pl.BlockSpec((tm, K), lambda i, j: (i, 0)),
            pl.BlockSpec((1, K), lambda i, j: (0, 0)),
            _layer_spec(lead, (K, tn), lambda i, j: (0, j)),
        ],
        out_specs=[
            pl.BlockSpec((tm, tn), lambda i, j: (i, j)),
            pl.BlockSpec((tm, 8, HEAD_DIM), lambda i, j: (i, jnp.minimum(j, n_head_steps - 1), 0)),
        ],
        out_shape=[
            jax.ShapeDtypeStruct((M, N), F32),
            jax.ShapeDtypeStruct((M, head_cols // HEAD_DIM, HEAD_DIM), F32),
        ],
        scratch_shapes=[pltpu.VMEM((tm, K), BF16)],
        compiler_params=_cp(("parallel", "arbitrary")),
        name="rms_matmul_heads",
    )(x, g.reshape(1, K), w)


def _rmsnorm_kernel(x_ref, g_ref, o_ref):
    x = x_ref[...]
    y = x * lax.rsqrt(jnp.mean(x * x, axis=-1, keepdims=True) + RMS_EPS)
    o_ref[...] = y * g_ref[...]


def rmsnorm_rows(x, g):
    M, K = x.shape
    tm = _pick(M, (512, 256, 128))
    return pl.pallas_call(
        _rmsnorm_kernel,
        grid=(M // tm,),
        in_specs=[pl.BlockSpec((tm, K), lambda i: (i, 0)), pl.BlockSpec((1, K), lambda i: (0, 0))],
        out_specs=pl.BlockSpec((tm, K), lambda i: (i, 0)),
        out_shape=jax.ShapeDtypeStruct((M, K), F32),
        compiler_params=_cp(("parallel",)),
        name="final_rmsnorm",
    )(x, g.reshape(1, K))


def _cast_kernel(x_ref, o_ref):
    o_ref[...] = x_ref[...].astype(o_ref.dtype)


def cast_bf16(w):
    G, R, C = w.shape
    tr = _pick(R, (256, 128))
    return pl.pallas_call(
        _cast_kernel,
        grid=(G, R // tr),
        in_specs=[pl.BlockSpec((1, tr, C), lambda g, r: (g, r, 0))],
        out_specs=pl.BlockSpec((1, tr, C), lambda g, r: (g, r, 0)),
        out_shape=jax.ShapeDtypeStruct((G, R, C), BF16),
        compiler_params=_cp(("parallel", "parallel")),
        name="cast_bf16",
    )(w)


def _cast_halves_kernel(x_ref, o_ref):
    c = x_ref.shape[2]
    o_ref[0, 0, :, 0:c] = x_ref[0].astype(o_ref.dtype)
    o_ref[0, 0, :, c:] = jnp.zeros((o_ref.shape[2], o_ref.shape[3] - c), o_ref.dtype)


def cast_split_halves(w, cols_pad):
    G, R, C2 = w.shape
    C = C2 // 2
    tr = _pick(R, (256, 128))
    return pl.pallas_call(
        _cast_halves_kernel,
        grid=(G, 2, R // tr),
        in_specs=[pl.BlockSpec((1, tr, C), lambda g, h, r: (g, r, h))],
        out_specs=pl.BlockSpec((1, 1, tr, cols_pad), lambda g, h, r: (g, h, r, 0)),
        out_shape=jax.ShapeDtypeStruct((G, 2, R, cols_pad), BF16),
        compiler_params=_cp(("parallel", "parallel", "parallel")),
        name="cast_split_halves",
    )(w)


def _cast_pad_rows_kernel(x_ref, o_ref):
    r = x_ref.shape[1]
    o_ref[0, 0:r, :] = x_ref[0].astype(o_ref.dtype)
    o_ref[0, r:, :] = jnp.zeros((o_ref.shape[1] - r, o_ref.shape[2]), o_ref.dtype)


def cast_pad_rows(w, rows_pad):
    G, R, C = w.shape
    tc = _pick(C, (256, 128))
    assert R % 16 == 0 and rows_pad % 16 == 0
    return pl.pallas_call(
        _cast_pad_rows_kernel,
        grid=(G, C // tc),
        in_specs=[pl.BlockSpec((1, R, tc), lambda g, c: (g, 0, c))],
        out_specs=pl.BlockSpec((1, rows_pad, tc), lambda g, c: (g, 0, c)),
        out_shape=jax.ShapeDtypeStruct((G, rows_pad, C), BF16),
        compiler_params=_cp(("parallel", "parallel")),
        name="cast_pad_rows",
    )(w)


def _outproj_kernel(om_ref, oe_ref, w1_ref, w2_ref, x_ref, y_ref):
    y_ref[...] = x_ref[...] + _dot(om_ref[...], w1_ref[...]) + _dot(oe_ref[...], w2_ref[...])


def out_proj(o_mix, o_mem, w_out, layer, x):
    M = x.shape[0]
    tm = _pick(M, (1024, 512, 256, 128))
    tn = 512
    assert MIX_DIM % MEM_DIM == 0
    return pl.pallas_call(
        _outproj_kernel,
        grid=(M // tm, D_MODEL // tn),
        in_specs=[
            pl.BlockSpec((tm, MIX_DIM), lambda i, j: (i, 0)),
            pl.BlockSpec((tm, MEM_DIM), lambda i, j: (i, 0)),
            _layer_spec((layer,), (MIX_DIM, tn), lambda i, j: (0, j)),
            _layer_spec((layer,), (MEM_DIM, tn), lambda i, j: (MIX_DIM // MEM_DIM, j)),
            pl.BlockSpec((tm, tn), lambda i, j: (i, j)),
        ],
        out_specs=pl.BlockSpec((tm, tn), lambda i, j: (i, j)),
        out_shape=jax.ShapeDtypeStruct((M, D_MODEL), F32),
        compiler_params=_cp(("parallel", "parallel")),
        name="out_proj",
    )(o_mix, o_mem, w_out, w_out, x)


def _mem_attn_kernel(q_ref, kv_ref, o_ref):
    scale = HEAD_DIM ** -0.5
    slots = 2 * N_MEM_HEADS
    n_mem = kv_ref.shape[1] // slots
    for h in range(N_MEM_HEADS):
        q = q_ref[0, :, h * HEAD_DIM:(h + 1) * HEAD_DIM].astype(BF16)
        k = kv_ref[0, pl.ds(h, n_mem, stride=slots), :].astype(BF16)
        v = kv_ref[0, pl.ds(N_MEM_HEADS + h, n_mem, stride=slots), :].astype(BF16)
        s = _dot_nt(q, k) * scale
        m = jnp.max(s, axis=-1, keepdims=True)
        p = jnp.exp(s - m)
        p = p / jnp.sum(p, axis=-1, keepdims=True)
        o_ref[0, :, h * HEAD_DIM:(h + 1) * HEAD_DIM] = _dot(p.astype(BF16), v).astype(o_ref.dtype)


def mem_attention(z3, col_block, kv, lead=()):
    B, L, _ = z3.shape
    rows = kv.shape[-2]
    tm = _pick(L, (1024, 512, 256, 128))
    return pl.pallas_call(
        _mem_attn_kernel,
        grid=(B, L // tm),
        in_specs=[
            pl.BlockSpec((1, tm, MEM_DIM), lambda b, i: (b, i, col_block)),
            _layer_spec(lead, (1, rows, HEAD_DIM), lambda b, i: (b, 0, 0)),
        ],
        out_specs=pl.BlockSpec((1, tm, MEM_DIM), lambda b, i: (b, i, 0)),
        out_shape=jax.ShapeDtypeStruct((B, L, MEM_DIM), BF16),
        compiler_params=_cp(("parallel", "parallel")),
        name="mem_attention",
    )(z3, kv)


def _ffn_kernel(*refs, blocks_per_seq, decode):
    if decode:
        (x_ref, g_ref, wa_ref, wu_ref, cw_ref, cb_ref, wd_ref, p1_ref, p2_ref,
         y_ref, at_ref, xn_ref) = refs
    else:
        (x_ref, g_ref, wa_ref, wu_ref, cw_ref, cb_ref, wd_ref,
         y_ref, at_ref, xn_ref, carry_ref) = refs
    i = pl.program_id(0)
    j = pl.program_id(1)

    @pl.when(j == 0)
    def _():
        x = x_ref[...]
        y = x * lax.rsqrt(jnp.mean(x * x, axis=-1, keepdims=True) + RMS_EPS)
        xn_ref[...] = (y * g_ref[...]).astype(BF16)
        y_ref[...] = x

    if not decode:
        @pl.when((i % blocks_per_seq) == 0)
        def _():
            carry_ref[j] = jnp.zeros(carry_ref.shape[1:], F32)

    xn = xn_ref[...]
    tm = xn.shape[0]
    n_tail = at_ref.shape[1]
    half = wa_ref.shape[1] // 2
    down = None
    for cs in (slice(0, half), slice(half, 2 * half)):
        a = _dot(xn, wa_ref[:, cs])
        u = _dot(xn, wu_ref[:, cs])
        if decode:
            a1 = p1_ref[:, cs]
            a2 = p2_ref[:, cs]
        else:
            prev = carry_ref[j, :, cs]
            row = lax.broadcasted_iota(jnp.int32, a.shape, 0)
            a1 = jnp.where(row == 0, prev[7:8], pltpu.roll(a, 1, 0))
            a2 = jnp.where(row == 0, prev[6:7], jnp.where(row == 1, prev[7:8], pltpu.roll(a, 2, 0)))
            carry_ref[j, :, cs] = a[tm - 8:tm]
        at_ref[0, :, cs] = a[tm - n_tail:tm]
        c = cb_ref[:, cs] + cw_ref[2:3, cs] * a
        c = c + cw_ref[0:1, cs] * a2
        c = c + cw_ref[1:2, cs] * a1
        h = (jax.nn.gelu(c) * u).astype(BF16)
        d = _dot(h, wd_ref[cs, :])
        down = d if down is None else down + d
    y_ref[...] += down


def conv_ffn(x, g, w_up, cw, cb, w_down, layer, *, seq_len=None, prev=None):
    M = x.shape[0]
    decode = prev is not None
    tm = M if decode else _pick(seq_len, (1024, 512, 256, 128))
    n_tail = tm if decode else 8
    nf = D_FF_PAD // FFN_TF
    in_specs = [
        pl.BlockSpec((tm, D_MODEL), lambda i, j: (i, 0)),
        pl.BlockSpec((1, D_MODEL), lambda i, j: (0, 0)),
        _layer_spec((layer, 0), (D_MODEL, FFN_TF), lambda i, j: (0, j)),
        _layer_spec((layer, 1), (D_MODEL, FFN_TF), lambda i, j: (0, j)),
        pl.BlockSpec((8, FFN_TF), lambda i, j: (0, j)),
        pl.BlockSpec((1, FFN_TF), lambda i, j: (0, j)),
        _layer_spec((layer,), (FFN_TF, D_MODEL), lambda i, j: (j, 0)),
    ]
    args = [x, g.reshape(1, D_MODEL), w_up, w_up, cw, cb, w_down]
    scratch = [pltpu.VMEM((tm, D_MODEL), BF16)]
    if decode:
        in_specs += [pl.BlockSpec((tm, FFN_TF), lambda i, j: (i, j))] * 2
        args += list(prev)
    else:
        scratch.append(pltpu.VMEM((nf, 8, FFN_TF), F32))
    return pl.pallas_call(
        functools.partial(_ffn_kernel, blocks_per_seq=(1 if decode else seq_len // tm), decode=decode),
        grid=(M // tm, nf),
        in_specs=in_specs,
        out_specs=[
            pl.BlockSpec((tm, D_MODEL), lambda i, j: (i, 0)),
            pl.BlockSpec((1, n_tail, FFN_TF), lambda i, j: (i, 0, j)),
        ],
        out_shape=[
            jax.ShapeDtypeStruct((M, D_MODEL), F32),
            jax.ShapeDtypeStruct((M // tm, n_tail, D_FF_PAD), F32),
        ],
        scratch_shapes=scratch,
        compiler_params=_cp(("arbitrary", "arbitrary")),
        name="conv_ffn",
    )(*args)


def _hgrn_gates(q, f, log_lb, log1m_lb):
    qs = jax.nn.silu(q)
    log_f = jnp.logaddexp(log_lb, log1m_lb + jax.nn.log_sigmoid(f))
    k = 1.0 - jnp.exp(log_f)
    return qs, k, log_f


def _hgrn_out(o, g, onorm):
    y = o * lax.rsqrt(jnp.mean(o * o, axis=-1, keepdims=True) + RMS_EPS)
    return (y * onorm) * jax.nn.silu(g)


def _hgrn_kernel(q_ref, f_ref, i_ref, g_ref, llb_ref, l1m_ref, on_ref, s0_ref, o_ref, s_ref, st_ref,
                 *, n_chunks):
    c = pl.program_id(2)
    T = HGRN_T
    n_heads = st_ref.shape[0]

    @pl.when(c == 0)
    def _():
        for h in range(n_heads):
            st_ref[h] = s0_ref[0, h].T

    assert HGRN_SUB == 8
    rl1 = lax.broadcasted_iota(jnp.int32, (T, 1), 0) % HGRN_SUB
    t_i = lax.broadcasted_iota(jnp.int32, (T, T), 0)
    s_i = lax.broadcasted_iota(jnp.int32, (T, T), 1)
    tril = jnp.where(s_i <= t_i, 1.0, 0.0).astype(BF16)
    pair_masks = []
    size = 2 * HGRN_SUB
    while size <= T:
        same = (t_i & -size) == (s_i & -size)
        pair_masks.append((size, same & ((t_i & (size - 1)) >= size // 2) & ((s_i & (size - 1)) < size // 2)))
        size *= 2

    def shift_rows(x, d):
        return pltpu.roll(x.reshape(T // HGRN_SUB, HGRN_SUB, HEAD_DIM), d, 1).reshape(T, HEAD_DIM)

    def one_head(q, k, b, v, h, sl):
        hs = slice(h * HEAD_DIM, (h + 1) * HEAD_DIM)
        st = st_ref[h]
        o = _dot_nt((q * jnp.exp2(b)).astype(BF16), st.astype(BF16))
        o = o + jnp.sum(q * k, axis=-1, keepdims=True) * v
        for d in range(1, HGRN_SUB):
            w = jnp.exp2(b - shift_rows(b, d))
            a = jnp.sum(q * shift_rows(k, d) * w, axis=-1, keepdims=True)
            o = o + jnp.where(rl1 >= d, a, 0.0) * shift_rows(v, d)
        att = jnp.zeros((T, T), F32)
        for size, keep in pair_masks:
            half = size // 2
            refs = [jnp.broadcast_to(b[j * size + half - 1:j * size + half], (size, HEAD_DIM))
                    for j in range(T // size)]
            r = jnp.concatenate(refs, axis=0) if len(refs) > 1 else refs[0]
            qt = q * jnp.exp2(jnp.minimum(b - r, 0.0))
            kt = k * jnp.exp2(jnp.minimum(r - b, 0.0))
            att = att + jnp.where(keep, _dot_nt(qt.astype(BF16), kt.astype(BF16)), 0.0)
        o = o + _dot(att.astype(BF16), v.astype(BF16))
        bl = b[T - 1:T]
        kt = k * jnp.exp2(bl - b)
        st_ref[h] = st * jnp.exp2(bl) + _dot_tn(v.astype(BF16), kt.astype(BF16))
        o_ref[0, sl, hs] = _hgrn_out(o, g_ref[0, sl, hs], on_ref[:, hs]).astype(o_ref.dtype)

    def chunk(ci, carry):
        sl = pl.ds(pl.multiple_of(ci * T, T), T)
        q, k, lf = _hgrn_gates(q_ref[0, sl, :], f_ref[0, sl, :], llb_ref[...], l1m_ref[...])
        b = sum(_dot(tril, part) for part in _split3(lf)) * LOG2E
        for h in range(n_heads):
            hs = slice(h * HEAD_DIM, (h + 1) * HEAD_DIM)
            one_head(q[:, hs], k[:, hs], b[:, hs], i_ref[0, sl, hs], h, sl)
        return carry

    lax.fori_loop(0, n_chunks, chunk, 0)

    @pl.when(c == pl.num_programs(2) - 1)
    def _():
        for h in range(n_heads):
            s_ref[0, h] = st_ref[h].T


def hgrn_prompt(z3, log_lb, log1m_lb, onorm, s0):
    B, L, _ = z3.shape
    H = N_MIX_HEADS
    hb = HGRN_HEADS_PER_STEP
    wb = hb * HEAD_DIM
    tc = _pick(L, (512, 256, 128, 64))
    zspec = lambda k: pl.BlockSpec((1, tc, wb), lambda b, h, c: (b, c, k * (H // hb) + h))
    vspec = pl.BlockSpec((1, wb), lambda b, h, c: (0, h))
    sspec = pl.BlockSpec((1, hb, HEAD_DIM, HEAD_DIM), lambda b, h, c: (b, h, 0, 0))
    return pl.pallas_call(
        functools.partial(_hgrn_kernel, n_chunks=tc // HGRN_T),
        grid=(B, H // hb, L // tc),
        in_specs=[zspec(0), zspec(1), zspec(2), zspec(3), vspec, vspec, vspec, sspec],
        out_specs=[pl.BlockSpec((1, tc, wb), lambda b, h, c: (b, c, h)), sspec],
        out_shape=[
            jax.ShapeDtypeStruct((B, L, MIX_DIM), BF16),
            jax.ShapeDtypeStruct((B, H, HEAD_DIM, HEAD_DIM), F32),
        ],
        scratch_shapes=[pltpu.VMEM((hb, HEAD_DIM, HEAD_DIM), F32)],
        compiler_params=_cp(("parallel", "parallel", "arbitrary")),
        name="hgrn_chunked",
    )(z3, z3, z3, z3, log_lb, log1m_lb, onorm, s0)


def _hgrn_step_kernel(z_ref, llb_ref, l1m_ref, on_ref, s0_ref, o_ref, s_ref):
    for h in range(N_MIX_HEADS):
        col = lambda k: slice((k * N_MIX_HEADS + h) * HEAD_DIM, (k * N_MIX_HEADS + h + 1) * HEAD_DIM)
        hs = slice(h * HEAD_DIM, (h + 1) * HEAD_DIM)
        q, k, lf = _hgrn_gates(z_ref[0, :, col(0)], z_ref[0, :, col(1)], llb_ref[:, hs], l1m_ref[:, hs])
        v = z_ref[0, :, col(2)]
        g = z_ref[0, :, col(3)]
        rows = jnp.concatenate([q, k, jnp.exp(lf), jnp.zeros((5, HEAD_DIM), F32)], axis=0)
        cols = rows.T
        s_new = cols[:, 2:3] * s0_ref[0, h] + cols[:, 1:2] * v
        s_ref[0, h] = s_new
        o = jnp.sum(cols[:, 0:1] * s_new, axis=0, keepdims=True)
        o_ref[0, :, hs] = _hgrn_out(o, g, on_ref[:, hs]).astype(o_ref.dtype)


def hgrn_step(z3, log_lb, log1m_lb, onorm, s0):
    B = z3.shape[0]
    W = z3.shape[2]
    H = N_MIX_HEADS
    vspec = pl.BlockSpec((1, MIX_DIM), lambda b: (0, 0))
    sspec = pl.BlockSpec((1, H, HEAD_DIM, HEAD_DIM), lambda b: (b, 0, 0, 0))
    return pl.pallas_call(
        _hgrn_step_kernel,
        grid=(B,),
        in_specs=[pl.BlockSpec((1, 1, W), lambda b: (b, 0, 0)), vspec, vspec, vspec, sspec],
        out_specs=[pl.BlockSpec((1, 1, MIX_DIM), lambda b: (b, 0, 0)), sspec],
        out_shape=[
            jax.ShapeDtypeStruct((B, 1, MIX_DIM), BF16),
            jax.ShapeDtypeStruct((B, H, HEAD_DIM, HEAD_DIM), F32),
        ],
        compiler_params=_cp(("parallel",)),
        name="hgrn_step",
    )(z3, log_lb, log1m_lb, onorm, s0)


def _pe_proj_kernel(pe_ref, w_ref, o_ref):
    pe = jnp.broadcast_to(pe_ref[0], (8, pe_ref.shape[2])).astype(BF16)
    o_ref[0] = _dot(pe, w_ref[0])


def pe_proj(pe, w1):
    K = pe.shape[2]
    return pl.pallas_call(
        _pe_proj_kernel,
        grid=(2,),
        in_specs=[pl.BlockSpec((1, 1, K), lambda i: (i, 0, 0)), pl.BlockSpec((1, K, CMP_HID), lambda i: (i, 0, 0))],
        out_specs=pl.BlockSpec((1, 8, CMP_HID), lambda i: (i, 0, 0)),
        out_shape=jax.ShapeDtypeStruct((2, 8, CMP_HID), F32),
        compiler_params=_cp(("parallel",)),
        name="cmp_pe_proj",
    )(pe, w1)


def _cmp_mlp_kernel(x_ref, w1_ref, w2_ref, pe_ref, o_ref):
    n_sub = x_ref.shape[1] // CMP_STRIDE
    half = CMP_STRIDE * HEAD_DIM
    x = jnp.concatenate([x_ref[0, pl.ds(p, n_sub, stride=CMP_STRIDE), :] for p in range(CMP_STRIDE)],
                        axis=1).astype(BF16)
    a = _dot(x, w1_ref[0, 0:half])
    bm = _dot(x, w1_ref[0, half:2 * half])
    n = a.shape[0]
    h = a + pltpu.roll(bm, n - 1, 0) + pe_ref[0, 0:1]
    o_ref[0, 0] = _dot(jax.nn.gelu(h).astype(BF16), w2_ref[0])


def cmp_mlp(rows3, w1, w2, pe_h):
    B, L, _ = rows3.shape
    n_sub = L // CMP_STRIDE
    half = CMP_STRIDE * HEAD_DIM
    return pl.pallas_call(
        _cmp_mlp_kernel,
        grid=(2, B, N_KV),
        in_specs=[
            pl.BlockSpec((1, L, HEAD_DIM), lambda k, b, n: (b, 0, k * N_KV + n)),
            pl.BlockSpec((1, 2 * half, CMP_HID), lambda k, b, n: (k, 0, 0)),
            pl.BlockSpec((1, CMP_HID, HEAD_DIM), lambda k, b, n: (k, 0, 0)),
            pl.BlockSpec((1, 8, CMP_HID), lambda k, b, n: (k, 0, 0)),
        ],
        out_specs=pl.BlockSpec((1, 1, n_sub, HEAD_DIM), lambda k, b, n: (k, b * N_KV + n, 0, 0)),
        out_shape=jax.ShapeDtypeStruct((2, B * N_KV, n_sub, HEAD_DIM), F32),
        compiler_params=_cp(("parallel", "parallel", "parallel")),
        name="cmp_mlp",
    )(rows3, w1, w2, pe_h)


LOG2E = 1.4426950408889634


def _masked_softmax_rows(s, mask, exp_fn=jnp.exp):
    s = jnp.where(mask, s, -jnp.inf)
    m = jnp.max(s, axis=-1, keepdims=True)
    m = jnp.where(m == -jnp.inf, 0.0, m)
    p = exp_fn(s - m)
    return p / jnp.maximum(jnp.sum(p, axis=-1, keepdims=True), 1e-30)


def _split3(p):
    hi = p.astype(BF16)
    r = p - hi.astype(F32)
    mid = r.astype(BF16)
    lo = (r - mid.astype(F32)).astype(BF16)
    return hi, mid, lo


def _masked_softmax_cols(s, mask, exp_fn):
    s = jnp.where(mask, s, -jnp.inf)
    m = jnp.max(s, axis=0, keepdims=True)
    m = jnp.where(m == -jnp.inf, 0.0, m)
    p = exp_fn(s - m)
    return p / jnp.maximum(jnp.sum(p, axis=0, keepdims=True), 1e-30)


def _nsa_prompt_t_kernel(q_ref, gt_ref, kc_ref, vc_ref, ks_ref, vs_ref, kw_ref, vw_ref, o_ref,
                         kaug_ref, vst_ref, kwb_ref, vwt_ref, vct_ref, score_ref, *, tq, tk, seq_len):
    qi = pl.program_id(2)
    n_blk = seq_len // SLC_BLOCK
    scale = HEAD_DIM ** -0.5 * LOG2E
    cols = GROUP * tq
    nc = kc_ref.shape[1]

    @pl.when(qi == 0)
    def _():
        r = lax.broadcasted_iota(jnp.int32, (seq_len, HEAD_DIM), 0) // SLC_BLOCK
        col = lax.broadcasted_iota(jnp.int32, (seq_len, HEAD_DIM), 1)
        kaug_ref[:, 0:HEAD_DIM] = ks_ref[0].astype(BF16)
        kaug_ref[:, HEAD_DIM:2 * HEAD_DIM] = jnp.where(r == col, 1.0, 0.0).astype(BF16)
        kwb_ref[...] = kw_ref[0].astype(BF16)
        for t in range(seq_len // tk):
            vst_ref[t] = vs_ref[0, t * tk:(t + 1) * tk, :].T.astype(BF16)
        for t in range(seq_len // tq):
            vwt_ref[t] = vw_ref[0, t * tq:(t + 1) * tq, :].T.astype(BF16)
        vct_ref[...] = vc_ref[0].T.astype(BF16)

    q0 = qi * tq
    qs = q_ref[0] * scale
    qb = jnp.concatenate([qs[:, g * HEAD_DIM:(g + 1) * HEAD_DIM] for g in range(GROUP)], axis=0).astype(BF16)

    def qpos(n_keys):
        return q0 + lax.broadcasted_iota(jnp.int32, (n_keys, cols), 1) % tq

    def kidx(n_keys):
        return lax.broadcasted_iota(jnp.int32, (n_keys, cols), 0)

    cend = CMP_STRIDE * kidx(nc) + (2 * CMP_STRIDE - 1)
    p_c = _masked_softmax_cols(_dot_nt(kc_ref[0].astype(BF16), qb), cend <= qpos(nc), jnp.exp2)
    o_cmp = _dot(vct_ref[...], p_c.astype(BF16))

    psum = p_c[:, 0:tq]
    for g in range(1, GROUP):
        psum = psum + p_c[:, g * tq:(g + 1) * tq]
    ji = lax.broadcasted_iota(jnp.int32, (HEAD_DIM, nc), 0)
    ci = lax.broadcasted_iota(jnp.int32, (HEAD_DIM, nc), 1)
    ratio = SLC_BLOCK // CMP_STRIDE
    member = ((ci >= ratio * ji - 1) & (ci <= ratio * ji + ratio - 1)).astype(BF16)
    imp = sum(_dot(member, part) for part in _split3(psum))
    blk = lax.broadcasted_iota(jnp.int32, (HEAD_DIM, tq), 0)
    cur = (q0 + lax.broadcasted_iota(jnp.int32, (HEAD_DIM, tq), 1)) // SLC_BLOCK
    valid = (blk <= cur) & (blk < n_blk)
    forced = valid & ((blk == 0) | (blk == cur) | (blk == cur - 1))
    score = jnp.where(forced, SEL_BONUS, jnp.where(valid, imp, -jnp.inf))
    score = score[0:n_blk]
    score_ref[...] = score
    blk = lax.broadcasted_iota(jnp.int32, (n_blk, tq), 0)
    per_tile = tq // SLC_BLOCK

    def rank_body(it, rank):
        for u in range(per_tile):
            i = it * per_tile + u
            si = score_ref[pl.ds(i, 1), :]
            later = jnp.where(blk > i, 1, 0)
            rank = rank + jnp.where(si > score, 1, jnp.where(si == score, later, 0))
        return rank

    rank = lax.fori_loop(0, qi + 1, rank_body, jnp.zeros((n_blk, tq), jnp.int32))
    bias_t = jnp.where(rank < N_SELECT, 0.0, NEG_BIG)
    if n_blk < HEAD_DIM:
        bias_t = jnp.concatenate([bias_t, jnp.zeros((HEAD_DIM - n_blk, tq), F32)], axis=0)
    bias = bias_t.T.astype(BF16)
    qaug = jnp.concatenate([qb, jnp.concatenate([bias] * GROUP, axis=0)], axis=1)

    qpos_k = qpos(tk)
    krow = kidx(tk)

    def slc_body(kt, carry, causal):
        m, l, acc = carry
        s = _dot_nt(kaug_ref[pl.ds(pl.multiple_of(kt * tk, tk), tk), :], qaug)
        if causal:
            s = jnp.where(kt * tk + krow <= qpos_k, s, -jnp.inf)
        m_new = jnp.maximum(m, jnp.max(s, axis=0, keepdims=True))
        alpha = jnp.exp2(m - m_new)
        p = jnp.exp2(s - m_new)
        l = alpha * l + jnp.sum(p, axis=0, keepdims=True)
        acc = alpha * acc + _dot(vst_ref[kt], p.astype(BF16))
        return m_new, l, acc

    init = (jnp.full((1, cols), -jnp.inf, F32), jnp.zeros((1, cols), F32), jnp.zeros((HEAD_DIM, cols), F32))
    n_full = q0 // tk

    def two_tiles(i, carry):
        return slc_body(2 * i + 1, slc_body(2 * i, carry, False), False)

    carry = lax.fori_loop(0, n_full // 2, two_tiles, init)
    carry = lax.fori_loop(n_full - n_full % 2, n_full, functools.partial(slc_body, causal=False), carry)
    _, l_s, acc_s = slc_body(n_full, carry, True)
    o_slc = acc_s / jnp.maximum(l_s, 1e-30)

    wk = min(WINDOW + tq, seq_len)
    k0 = jnp.minimum(jnp.maximum(q0 - WINDOW, 0), seq_len - wk)
    d = qpos(wk) - (k0 + kidx(wk))
    s_w = _dot_nt(kwb_ref[pl.ds(pl.multiple_of(k0, tq), wk), :], qb)
    s_w = jnp.where((d >= 0) & (d <= WINDOW), s_w, -jnp.inf)
    p_w = jnp.exp2(s_w - jnp.max(s_w, axis=0, keepdims=True))
    l_w = jnp.sum(p_w, axis=0, keepdims=True)
    p_wb = p_w.astype(BF16)
    t0 = k0 // tq
    o_win = sum(_dot(vwt_ref[t0 + i], p_wb[i * tq:(i + 1) * tq]) for i in range(wk // tq))
    o_win = o_win / jnp.maximum(l_w, 1e-30)

    gates = jax.nn.sigmoid(gt_ref[0]).T
    for g in range(GROUP):
        cs = slice(g * tq, (g + 1) * tq)
        c0 = g * N_BRANCH
        o = (gates[c0:c0 + 1] * o_cmp[:, cs] + gates[c0 + 1:c0 + 2] * o_slc[:, cs]
             + gates[c0 + 2:c0 + 3] * o_win[:, cs])
        o_ref[0, :, g * HEAD_DIM:(g + 1) * HEAD_DIM] = o.T.astype(o_ref.dtype)


def nsa_prompt_t(zq3, k_cmp, v_cmp, rows3):
    B, L, _ = zq3.shape
    gate_block0 = (MIX_DIM + MEM_DIM) // HEAD_DIM
    tq = _pick(L, (512, 256, 128))
    tk = _pick(L, (512, 256, 128))
    assert WINDOW % tq == 0 and tk % tq == 0
    n_sub = k_cmp.shape[1]
    qw = GROUP * HEAD_DIM
    rspec = lambda kind: pl.BlockSpec((1, L, HEAD_DIM), lambda b, n, i: (b, 0, kind * N_KV + n))
    cspec = pl.BlockSpec((1, n_sub, HEAD_DIM), lambda b, n, i: (b * N_KV + n, 0, 0))
    return pl.pallas_call(
        functools.partial(_nsa_prompt_t_kernel, tq=tq, tk=tk, seq_len=L),
        grid=(B, N_KV, L // tq),
        in_specs=[
            pl.BlockSpec((1, tq, qw), lambda b, n, i: (b, i, n)),
            pl.BlockSpec((1, tq, HEAD_DIM), lambda b, n, i: (b, i, gate_block0 + n)),
            cspec, cspec, rspec(2), rspec(3), rspec(4), rspec(5),
        ],
        out_specs=pl.BlockSpec((1, tq, qw), lambda b, n, i: (b, i, n)),
        out_shape=jax.ShapeDtypeStruct((B, L, MIX_DIM), BF16),
        scratch_shapes=[
            pltpu.VMEM((L, 2 * HEAD_DIM), BF16),
            pltpu.VMEM((L // tk, HEAD_DIM, tk), BF16),
            pltpu.VMEM((L, HEAD_DIM), BF16),
            pltpu.VMEM((L // tq, HEAD_DIM, tq), BF16),
            pltpu.VMEM((HEAD_DIM, n_sub), BF16),
            pltpu.VMEM((L // SLC_BLOCK, tq), F32),
        ],
        compiler_params=_cp(("parallel", "parallel", "arbitrary")),
        name="nsa_prompt",
    )(zq3, zq3, k_cmp, v_cmp, rows3, rows3, rows3, rows3)


def _dec_cmp_kernel(pt_ref, *refs, n_pg, n_groups):
    del pt_ref
    pg_refs = refs[:n_pg]
    new_ref, w1_ref, w2_ref, pe_ref, kc_ref, vc_ref, carry_ref = refs[n_pg:]
    g = pl.program_id(1)
    sub_pg = PAGE_SIZE // CMP_STRIDE
    n_sub = n_pg * sub_pg
    pos_per_chunk = 4
    kw = pos_per_chunk * HEAD_DIM

    @pl.when(g == 0)
    def _():
        carry_ref[...] = jnp.zeros_like(carry_ref)

    sub_i = lax.broadcasted_iota(jnp.int32, (8, n_sub, kw), 1)
    new_row = jnp.concatenate([new_ref[0], jnp.zeros((8, kw - HEAD_DIM), F32)], axis=1)[:, None, :]
    ys = [None, None]
    for pc in range(CMP_STRIDE // pos_per_chunk):
        x = jnp.concatenate(
            [jnp.concatenate([jnp.swapaxes(pg_refs[r][0, pl.ds(p, sub_pg, stride=CMP_STRIDE), :, :], 0, 1)
                              for r in range(n_pg)], axis=1)
             for p in range(pc * pos_per_chunk, (pc + 1) * pos_per_chunk)], axis=2)
        x = jnp.where(g < n_groups, x, 0.0)
        if pc == 0:
            x = jnp.where((g == n_groups) & (sub_i == 0), new_row, x)
        xb = x.astype(BF16)
        for kind in range(2):
            part = _dot(xb[kind * N_KV:(kind + 1) * N_KV].reshape(N_KV * n_sub, kw),
                        w1_ref[kind, pc * kw:(pc + 1) * kw, :])
            ys[kind] = part if ys[kind] is None else ys[kind] + part

    row = lax.broadcasted_iota(jnp.int32, (n_sub, CMP_HID), 0)
    for kind, out_ref in enumerate((kc_ref, vc_ref)):
        y = ys[kind]
        for n in range(N_KV):
            slot = kind * N_KV + n
            a = y[n * n_sub:(n + 1) * n_sub, 0:CMP_HID]
            bm = y[n * n_sub:(n + 1) * n_sub, CMP_HID:2 * CMP_HID]
            a_prev = jnp.where(row == 0, carry_ref[slot:slot + 1, :], pltpu.roll(a, 1, 0))
            carry_ref[slot:slot + 1, :] = a[n_sub - 1:n_sub]
            h = a_prev + bm + pe_ref[kind, 0:1]
            out_ref[0, n] = _dot(jax.nn.gelu(h).astype(BF16), w2_ref[kind])


def dec_cmp_tokens(cache4, page_table, new_cmp, w1ab, w2, pe_h):
    B, n_pages = page_table.shape
    n_pg = _pick(n_pages, (8, 4, 2))
    assert n_pages % n_pg == 0
    n_groups = n_pages // n_pg
    sub_pg = PAGE_SIZE // CMP_STRIDE
    n_sub = n_pg * sub_pg
    n_steps = n_groups + 1
    while (n_steps * n_sub) % HEAD_DIM:
        n_steps += 1
    half = CMP_STRIDE * HEAD_DIM

    def page_spec(r):
        def imap(b, g, pt):
            return (pt[b * n_pages + jnp.minimum(g * n_pg + r, n_pages - 1)], 0, 0, 0)
        return pl.BlockSpec((1, PAGE_SIZE, 8, HEAD_DIM), imap)

    out_spec = pl.BlockSpec((1, N_KV, n_sub, HEAD_DIM), lambda b, g, pt: (b, 0, g, 0))
    grid_spec = pltpu.PrefetchScalarGridSpec(
        num_scalar_prefetch=1,
        grid=(B, n_steps),
        in_specs=[page_spec(r) for r in range(n_pg)] + [
            pl.BlockSpec((1, 8, HEAD_DIM), lambda b, g, pt: (b, 0, 0)),
            pl.BlockSpec((2, half, 2 * CMP_HID), lambda b, g, pt: (0, 0, 0)),
            pl.BlockSpec((2, CMP_HID, HEAD_DIM), lambda b, g, pt: (0, 0, 0)),
            pl.BlockSpec((2, 8, CMP_HID), lambda b, g, pt: (0, 0, 0)),
        ],
        out_specs=[out_spec, out_spec],
        scratch_shapes=[pltpu.VMEM((8, CMP_HID), F32)],
    )
    tok = jax.ShapeDtypeStruct((B, N_KV, n_steps * n_sub, HEAD_DIM), F32)
    return pl.pallas_call(
        functools.partial(_dec_cmp_kernel, n_pg=n_pg, n_groups=n_groups),
        grid_spec=grid_spec,
        out_shape=[tok, tok],
        compiler_params=_cp(("parallel", "arbitrary")),
        name="dec_cmp_tokens",
    )(page_table.reshape(-1), *([cache4] * n_pg), new_cmp, w1ab, w2, pe_h)


def _dec_select_kernel(q_ref, kc_ref, vc_ref, mem_ref, ocmp_ref, idx_ref, *, q_pos, nc, n_blk, n_sel):
    C = kc_ref.shape[2]
    JB = mem_ref.shape[1]
    scale = HEAD_DIM ** -0.5
    c_i = lax.broadcasted_iota(jnp.int32, (8, C), 1) - 1
    g_i = lax.broadcasted_iota(jnp.int32, (8, C), 0)
    ok = (c_i >= 0) & (c_i < nc) & (CMP_STRIDE * c_i + (2 * CMP_STRIDE - 1) <= q_pos)
    psums = []
    for n in range(N_KV):
        qb = (q_ref[0, n] * scale).astype(BF16)
        p = _masked_softmax_rows(_dot_nt(qb, kc_ref[0, n].astype(BF16)), ok)
        ocmp_ref[0, n] = _dot(p.astype(BF16), vc_ref[0, n].astype(BF16))
        psums.append(jnp.sum(jnp.where(g_i < GROUP, p, 0.0), axis=0, keepdims=True))
    psum = jnp.concatenate(psums + [jnp.zeros((8 - N_KV, C), F32)], axis=0)
    imp = sum(_dot(part, mem_ref[...]) for part in _split3(psum))
    j_i = lax.broadcasted_iota(jnp.int32, (8, JB), 1)
    cur = q_pos // SLC_BLOCK
    valid = (j_i <= cur) & (j_i < n_blk)
    forced = valid & ((j_i == 0) | (j_i == cur) | (j_i == cur - 1))
    score = jnp.where(forced, SEL_BONUS, jnp.where(valid, imp, -jnp.inf))
    j_f = j_i.astype(F32)
    taken = j_i >= n_blk
    lane_o = lax.broadcasted_iota(jnp.int32, (8, HEAD_DIM), 1)
    out = jnp.zeros((8, HEAD_DIM), F32)
    for r in range(n_sel):
        m = jnp.max(jnp.where(taken, -jnp.inf, score), axis=-1, keepdims=True)
        cand = jnp.logical_not(taken) & (score == m)
        idx = jnp.min(jnp.where(cand, j_f, float(JB)), axis=-1, keepdims=True)
        out = jnp.where(lane_o == r, idx, out)
        taken = taken | (j_f == idx)
    idx_ref[0] = out.astype(jnp.int32)


def dec_select(q8, k_cmp, v_cmp, member, *, q_pos, nc, n_blk, n_sel):
    B, _, C, _ = k_cmp.shape
    JB = member.shape[1]
    head_spec = pl.BlockSpec((1, N_KV, 8, HEAD_DIM), lambda b: (b, 0, 0, 0))
    tok_spec = pl.BlockSpec((1, N_KV, C, HEAD_DIM), lambda b: (b, 0, 0, 0))
    return pl.pallas_call(
        functools.partial(_dec_select_kernel, q_pos=q_pos, nc=nc, n_blk=n_blk, n_sel=n_sel),
        grid=(B,),
        in_specs=[head_spec, tok_spec, tok_spec, pl.BlockSpec((C, JB), lambda b: (0, 0))],
        out_specs=[head_spec, pl.BlockSpec((1, 8, HEAD_DIM), lambda b: (b, 0, 0))],
        out_shape=[
            jax.ShapeDtypeStruct((B, N_KV, 8, HEAD_DIM), F32),
            jax.ShapeDtypeStruct((B, 8, HEAD_DIM), jnp.int32),
        ],
        compiler_params=_cp(("parallel",)),
        name="dec_select",
    )(q8, k_cmp, v_cmp, member)


def _dec_attend_kernel(idx_ref, pt_ref, q_ref, gl_ref, ocmp_ref, nslc_ref, win_ref, nwin_ref, *rest,
                       q_pos, past_len, n_sel):
    del pt_ref
    blk_refs = rest[:n_sel]
    o_ref = rest[n_sel]
    b = pl.program_id(0)
    n = pl.program_id(1)
    base = (b * N_KV + n) * n_sel
    scale = HEAD_DIM ** -0.5
    past_blocks = past_len // SLC_BLOCK
    rb = SLC_BLOCK * 8
    rw = win_ref.shape[1]
    w_buf = rw // 8
    qf = q_ref[0, 0] * scale
    qb = qf.astype(BF16)
    js = [idx_ref[base + s] for s in range(n_sel)]
    has_new = js[0] == past_blocks
    for s in range(1, n_sel):
        has_new = has_new | (js[s] == past_blocks)
    new_ok = has_new & (past_len <= q_pos)

    def head_rows(x):
        slot = lax.broadcasted_iota(jnp.int32, (8, HEAD_DIM), 0)
        k = jnp.sum(jnp.where(slot == n, x, 0.0), axis=0, keepdims=True)
        v = jnp.sum(jnp.where(slot == n + N_KV, x, 0.0), axis=0, keepdims=True)
        return k, v

    def finish(m):
        return jnp.where(m == -jnp.inf, 0.0, m)

    r_s = lax.broadcasted_iota(jnp.int32, (8, rb), 1)
    t_s = r_s >> 3
    mine_s = (r_s & 7) == n

    def slc_scores(s):
        blk = blk_refs[s][0, 0].reshape(rb, HEAD_DIM)
        sc = _dot_nt(qb, blk.astype(BF16))
        ok = mine_s & (js[s] * SLC_BLOCK + t_s <= q_pos) & (js[s] < past_blocks)
        return jnp.where(ok, sc, -jnp.inf)

    k_new, v_new = head_rows(nslc_ref[0])
    s_new = jnp.where(new_ok, jnp.sum(qf * k_new, axis=-1, keepdims=True), -jnp.inf)
    scores = [slc_scores(s) for s in range(n_sel)]
    m = s_new
    for sc in scores:
        m = jnp.maximum(m, jnp.max(sc, axis=-1, keepdims=True))
    m = finish(m)
    p_new = jnp.exp(s_new - m)
    l = p_new
    acc = p_new * v_new
    for s, sc in enumerate(scores):
        p = jnp.exp(sc - m)
        l = l + jnp.sum(p, axis=-1, keepdims=True)
        acc = acc + _dot(pltpu.roll(p, N_KV, 1).astype(BF16), blk_refs[s][0, 0].reshape(rb, HEAD_DIM).astype(BF16))
    o_slc = acc / jnp.maximum(l, 1e-30)

    r_w = lax.broadcasted_iota(jnp.int32, (8, rw), 1)
    w_pos = past_len - w_buf + (r_w >> 3)
    w_ok = ((r_w & 7) == n) & (q_pos - w_pos >= 0) & (q_pos - w_pos <= WINDOW) & (w_pos >= 0)
    win = win_ref[0].astype(BF16)
    s_w = jnp.where(w_ok, _dot_nt(qb, win), -jnp.inf)
    k_nw, v_nw = head_rows(nwin_ref[0])
    nw_ok = (q_pos - past_len >= 0) and (q_pos - past_len <= WINDOW)
    s_nw = jnp.sum(qf * k_nw, axis=-1, keepdims=True) if nw_ok else jnp.full((8, 1), -jnp.inf, F32)
    m = finish(jnp.maximum(s_nw, jnp.max(s_w, axis=-1, keepdims=True)))
    p_w = jnp.exp(s_w - m)
    p_nw = jnp.exp(s_nw - m)
    l = p_nw + jnp.sum(p_w, axis=-1, keepdims=True)
    acc = p_nw * v_nw + _dot(pltpu.roll(p_w, N_KV, 1).astype(BF16), win)
    o_win = acc / jnp.maximum(l, 1e-30)

    gates = jax.nn.sigmoid(gl_ref[0, 0])
    o_ref[0, 0] = gates[:, 0:1] * ocmp_ref[0, 0] + gates[:, 1:2] * o_slc + gates[:, 2:3] * o_win


def dec_attend(idx_flat, page_table, q8, gate8, ocmp8, new_slc, win2, new_win, cache5, *, q_pos, past_len, n_sel):
    B, n_pages = page_table.shape
    rw = win2.shape[1]
    past_blocks = past_len // SLC_BLOCK
    per_page = PAGE_SIZE // SLC_BLOCK

    def blk_spec(s):
        def imap(b, n, idx, pt):
            j = jnp.minimum(idx[(b * N_KV + n) * n_sel + s], past_blocks - 1)
            return (pt[b * n_pages + j // per_page], j % per_page, 0, 1, 0)
        return pl.BlockSpec((1, 1, SLC_BLOCK, 8, HEAD_DIM), imap)

    head_spec = pl.BlockSpec((1, 1, 8, HEAD_DIM), lambda b, n, idx, pt: (b, n, 0, 0))
    tok_spec = pl.BlockSpec((1, 8, HEAD_DIM), lambda b, n, idx, pt: (b, 0, 0))
    grid_spec = pltpu.PrefetchScalarGridSpec(
        num_scalar_prefetch=2,
        grid=(B, N_KV),
        in_specs=[head_spec, head_spec, head_spec, tok_spec,
                  pl.BlockSpec((1, rw, HEAD_DIM), lambda b, n, idx, pt: (b, 0, 0)), tok_spec]
        + [blk_spec(s) for s in range(n_sel)],
        out_specs=head_spec,
    )
    return pl.pallas_call(
        functools.partial(_dec_attend_kernel, q_pos=q_pos, past_len=past_len, n_sel=n_sel),
        grid_spec=grid_spec,
        out_shape=jax.ShapeDtypeStruct((B, N_KV, 8, HEAD_DIM), F32),
        compiler_params=_cp(("parallel", "arbitrary")),
        name="dec_attend",
    )(idx_flat, page_table.reshape(-1), q8, gate8, ocmp8, new_slc, win2, new_win, *([cache5] * n_sel))


def _pad_to(a, axis, size):
    pad = [(0, 0)] * a.ndim
    pad[axis] = (0, size - a.shape[axis])
    return jnp.pad(a, pad)


def _prep_weights(w_in_a, w_in_b, b_gate, w_kv, cmp_pe, cmp_w1, cmp_w2, w_mem_kv, w_out, w_up, conv_w, conv_b,
                  w_down, hgrn_lb):
    depth = w_out.shape[0]
    n_b = w_in_b.shape[0]
    n_gate = N_MIX_HEADS * N_BRANCH
    per_kv = GROUP * N_BRANCH
    half = CMP_STRIDE * HEAD_DIM
    w1 = cmp_w1.astype(BF16)
    lb = jnp.cumsum(jax.nn.softmax(hgrn_lb.astype(F32), axis=0), axis=0)
    lb = lb - lb[0]
    layers = lambda f, n=depth: [f(l) for l in range(n)]
    gate_w = lambda j: w_in_b[j][:, MIX_DIM:MIX_DIM + n_gate].reshape(D_MODEL, N_KV, per_kv)
    return dict(
        in_a=cast_bf16(w_in_a),
        in_b=layers(lambda j: jnp.concatenate(
            [w_in_b[j][:, :MIX_DIM], w_in_b[j][:, MIX_DIM + n_gate:],
             _pad_to(gate_w(j), 2, HEAD_DIM).reshape(D_MODEL, N_KV * HEAD_DIM)], axis=-1).astype(BF16), n_b),
        b_in_b=layers(lambda j: jnp.concatenate(
            [jnp.zeros((MIX_DIM + MEM_DIM,), F32),
             _pad_to(b_gate[j].reshape(N_KV, per_kv), 1, HEAD_DIM).reshape(N_KV * HEAD_DIM)]), n_b),
        kv=cast_bf16(w_kv[None]),
        mem_kv=cast_bf16(w_mem_kv),
        out=cast_bf16(w_out),
        up=cast_split_halves(w_up, D_FF_PAD),
        conv_w=layers(lambda l: _pad_to(_pad_to(conv_w[l], 1, D_FF_PAD), 0, 8)),
        conv_b=layers(lambda l: _pad_to(conv_b[l], 0, D_FF_PAD).reshape(1, D_FF_PAD)),
        down=cast_pad_rows(w_down, D_FF_PAD),
        cmp_w1=w1,
        cmp_w1ab=jnp.concatenate([w1[:, :half], w1[:, half:]], axis=2),
        cmp_w2=cmp_w2.astype(BF16),
        cmp_pe=cmp_pe.reshape(2, 1, 2 * half),
        log_lb=jnp.log(lb),
        log1m_lb=jnp.log1p(-lb),
    )


def kernel(x_prompt, x_sample, mem_prompt, cache_nsa_kv, page_table, cache_win_kv, state_hgrn, state_conv,
           cache_mem_kv, norm_mix, norm_ffn, norm_mem, norm_kv, norm_final, w_in_a, hgrn_lb, hgrn_onorm,
           w_in_b, b_gate, w_kv, cmp_pe, cmp_w1, cmp_w2, w_mem_kv, w_out, w_up, conv_w, conv_b, w_down):
    B, L, _ = x_prompt.shape
    Bs = x_sample.shape[0]
    depth = w_out.shape[0]
    n_a = w_in_a.shape[0]
    n_mem = mem_prompt.shape[1]
    n_pool, page, _, _, _ = cache_nsa_kv.shape
    n_pages = page_table.shape[1]
    past_len = n_pages * page
    w_buf = cache_win_kv.shape[1]
    assert x_sample.shape[1] == 1 and page == PAGE_SIZE and L % 128 == 0 and L // SLC_BLOCK <= HEAD_DIM
    W = _prep_weights(w_in_a, w_in_b, b_gate, w_kv, cmp_pe, cmp_w1, cmp_w2, w_mem_kv, w_out, w_up, conv_w,
                      conv_b, w_down, hgrn_lb)
    pe_h = pe_proj(W["cmp_pe"], W["cmp_w1"])
    onorm = hgrn_onorm.reshape(n_a, 1, MIX_DIM)
    log_lb = W["log_lb"].reshape(n_a, 1, MIX_DIM)
    log1m_lb = W["log1m_lb"].reshape(n_a, 1, MIX_DIM)

    def ffn(x, l, **kw):
        return conv_ffn(x, norm_ffn[l], W["up"], W["conv_w"][l], W["conv_b"][l], W["down"], l, **kw)

    M = B * L
    mem_flat = mem_prompt.reshape(B * n_mem, D_MODEL)
    mem_kv_both = [rms_matmul_heads(mem_flat, norm_mem[l], W["mem_kv"], (l,), 2 * MEM_DIM) for l in range(depth)]
    mem_kv_p = [by_head.reshape(B, n_mem * 2 * N_MEM_HEADS, HEAD_DIM) for _, by_head in mem_kv_both]
    mem_cache = cache_mem_kv.reshape(depth, Bs, n_mem * 2 * N_MEM_HEADS, HEAD_DIM)
    x = x_prompt.reshape(M, D_MODEL)
    hgrn_p, conv_p = [], []
    for l in range(depth):
        if l < n_a:
            z3 = rms_matmul(x, norm_mix[l], W["in_a"], lead=(l,)).reshape(B, L, -1)
            s0 = jnp.zeros((B, N_MIX_HEADS, HEAD_DIM, HEAD_DIM), F32)
            o_mix, s_new = hgrn_prompt(z3, log_lb[l], log1m_lb[l], onorm[l], s0)
            hgrn_p.append(s_new)
            o_mem = mem_attention(z3, 4 * MIX_DIM // MEM_DIM, mem_kv_p[l])
        else:
            j = l - n_a
            if j == 0:
                n_row4 = 4 * N_KV * HEAD_DIM
                rows_p, rows_by_head = rms_matmul_heads(x, norm_kv, W["kv"], (0,), n_row4)
                rows3 = rows_p.reshape(B, L, -1)
                cmp_p = cmp_mlp(rows3, W["cmp_w1"], W["cmp_w2"], pe_h)
            zq3 = rms_matmul(x, norm_mix[l], W["in_b"][j], W["b_in_b"][j]).reshape(B, L, -1)
            o_mix = nsa_prompt_t(zq3, cmp_p[0], cmp_p[1], rows3)
            o_mem = mem_attention(zq3, MIX_DIM // MEM_DIM, mem_kv_p[l])
        x = out_proj(o_mix.reshape(M, MIX_DIM), o_mem.reshape(M, MEM_DIM), W["out"], l, x)
        x, a_tail = ffn(x, l, seq_len=L)
        conv_p.append(a_tail.reshape(B, -1, 8, D_FF_PAD)[:, -1, 8 - (CONV_W - 1):, :D_FF])
    y_prompt = rmsnorm_rows(x, norm_final).reshape(B, L, D_MODEL)
    w_keep = min(WINDOW, L)
    nsa_rows_prompt = rows_by_head.reshape(B, L, 4, N_KV, HEAD_DIM)
    win_prompt = rows3[:, L - w_keep:, n_row4:].reshape(B, w_keep, 2, N_KV, HEAD_DIM)
    mem_kv_prompt = jnp.stack([by_head for _, by_head in mem_kv_both]).reshape(
        depth, B, n_mem, 2, N_MEM_HEADS, HEAD_DIM)

    Ms = 16
    pad_rows = lambda a: _pad_to(a, 0, Ms)
    xs = pad_rows(x_sample.reshape(Bs, D_MODEL))
    hgrn_s, conv_s = [], []
    q_pos = past_len
    t_pad = -(-(past_len + 1) // SLC_BLOCK) * SLC_BLOCK
    n_blk = t_pad // SLC_BLOCK
    nc = t_pad // CMP_STRIDE - 1
    n_sel = min(N_SELECT, n_blk)
    for l in range(depth):
        if l < n_a:
            z = rms_matmul(xs, norm_mix[l], W["in_a"], lead=(l,))
            z3 = z[:Bs].reshape(Bs, 1, -1)
            o_mix, s_new = hgrn_step(z3, log_lb[l], log1m_lb[l], onorm[l], state_hgrn[l])
            hgrn_s.append(s_new)
            o_mix = o_mix.reshape(Bs, MIX_DIM)
            o_mem = mem_attention(z3, 4 * MIX_DIM // MEM_DIM, mem_cache, lead=(l,))
        else:
            j = l - n_a
            if j == 0:
                rows_s = rms_matmul(xs, norm_kv, W["kv"], lead=(0,))[:Bs]
                new_cmp = rows_s[:, 0:1024].reshape(Bs, 8, HEAD_DIM)
                new_slc = rows_s[:, 1024:2048].reshape(Bs, 8, HEAD_DIM)
                new_win = rows_s[:, 2048:3072].reshape(Bs, 8, HEAD_DIM)
                cache4 = cache_nsa_kv.reshape(n_pool, PAGE_SIZE, 4 * N_KV, HEAD_DIM)
                cache5 = cache_nsa_kv.reshape(n_pool, PAGE_SIZE // SLC_BLOCK, SLC_BLOCK, 4 * N_KV, HEAD_DIM)
                win2 = cache_win_kv.reshape(Bs, w_buf * 2 * N_KV, HEAD_DIM)
                k_cmp_s, v_cmp_s = dec_cmp_tokens(cache4, page_table, new_cmp, W["cmp_w1ab"], W["cmp_w2"], pe_h)
                assert k_cmp_s.shape[2] > nc + 1
                ratio = SLC_BLOCK // CMP_STRIDE
                c_of_row = jnp.arange(k_cmp_s.shape[2])[:, None] - 1
                j_of_col = jnp.arange(-(-n_blk // HEAD_DIM) * HEAD_DIM)[None, :]
                member = ((c_of_row >= ratio * j_of_col - 1) & (c_of_row <= ratio * j_of_col + ratio - 1)).astype(BF16)
            zq = rms_matmul(xs, norm_mix[l], W["in_b"][j], W["b_in_b"][j])
            gt = zq[:Bs, MIX_DIM + MEM_DIM:]
            z3 = zq[:Bs].reshape(Bs, 1, -1)
            per_head = lambda a: _pad_to(a.reshape(Bs, N_KV, GROUP, -1), 2, 8)
            q8 = per_head(zq[:Bs, :MIX_DIM])
            o_cmp, idx = dec_select(q8, k_cmp_s, v_cmp_s, member, q_pos=q_pos, nc=nc, n_blk=n_blk, n_sel=n_sel)
            idx_flat = idx[:, :N_KV, :n_sel].reshape(-1)
            gate9 = gt.reshape(Bs, N_KV, HEAD_DIM)[:, :, :GROUP * N_BRANCH]
            o_all = dec_attend(idx_flat, page_table, q8, _pad_to(per_head(gate9), 3, HEAD_DIM), o_cmp,
                               new_slc, win2, new_win, cache5, q_pos=q_pos, past_len=past_len, n_sel=n_sel)
            o_mix = o_all[:, :, :GROUP].reshape(Bs, MIX_DIM).astype(BF16)
            o_mem = mem_attention(z3, MIX_DIM // MEM_DIM, mem_cache, lead=(l,))
        xs = out_proj(pad_rows(o_mix), pad_rows(o_mem.reshape(Bs, MEM_DIM)), W["out"], l, xs)
        prev = (pad_rows(_pad_to(state_conv[l][:, 1], 1, D_FF_PAD)), pad_rows(_pad_to(state_conv[l][:, 0], 1, D_FF_PAD)))
        xs, a_new = ffn(xs, l, prev=prev)
        conv_s.append(jnp.stack([state_conv[l][:, 1], a_new[0, :Bs, :D_FF]], axis=1))
    y_sample = rmsnorm_rows(xs, norm_final)[:Bs].reshape(Bs, 1, D_MODEL)
    nsa_rows_sample = rows_s[:, :n_row4].reshape(Bs, 1, 4, N_KV, HEAD_DIM)
    win_new = rows_s[:, n_row4:].reshape(Bs, 1, 2, N_KV, HEAD_DIM).astype(cache_win_kv.dtype)
    win_sample = jnp.concatenate([cache_win_kv, win_new], axis=1)[:, 1:]

    return (y_prompt, y_sample, nsa_rows_prompt, nsa_rows_sample, win_prompt, win_sample,
            jnp.stack(hgrn_p), jnp.stack(hgrn_s), jnp.stack(conv_p), jnp.stack(conv_s), mem_kv_prompt)
```

```python
import functools

import jax
import jax.numpy as jnp
from jax import lax
from jax.experimental import pallas as pl
from jax.experimental.pallas import tpu as pltpu

F32 = jnp.float32
BF16 = jnp.bfloat16

D_MODEL = 2048
HEAD_DIM = 128
MIX_DIM = 1536
N_MIX_HEADS = 12
MEM_DIM = 512
N_MEM_HEADS = 4
N_KV = 4
GROUP = 3
N_BRANCH = 3
CMP_STRIDE = 16
CMP_HID = 256
SLC_BLOCK = 64
N_SELECT = 16
WINDOW = 512
PAGE_SIZE = 128
D_FF = 5504
D_FF_PAD = 5632
CONV_W = 3
RMS_EPS = 1e-6
SEL_BONUS = 1e9
NEG_BIG = -1e30

HGRN_T = 64
HGRN_SUB = 8
HGRN_HEADS_PER_STEP = 6
FFN_TF = 512
VMEM_LIMIT = 56 * 1024 * 1024


def _cp(sem, vmem=VMEM_LIMIT):
    return pltpu.CompilerParams(dimension_semantics=sem, vmem_limit_bytes=vmem)


def _dot(a, b):
    return jnp.dot(a, b, preferred_element_type=F32)


def _dot_nt(a, b):
    return lax.dot_general(a, b, (((1,), (1,)), ((), ())), preferred_element_type=F32)


def _dot_tn(a, b):
    return lax.dot_general(a, b, (((0,), (0,)), ((), ())), preferred_element_type=F32)


def _pick(n, cands):
    for c in cands:
        if n % c == 0:
            return c
    return n


def _rms_matmul_kernel(x_ref, g_ref, w_ref, b_ref, o_ref, xn_ref):
    @pl.when(pl.program_id(1) == 0)
    def _():
        x = x_ref[...]
        y = x * lax.rsqrt(jnp.mean(x * x, axis=-1, keepdims=True) + RMS_EPS)
        xn_ref[...] = (y * g_ref[...]).astype(BF16)

    o_ref[...] = _dot(xn_ref[...], w_ref[...]) + b_ref[...]


def _layer_spec(lead, block, imap):
    return pl.BlockSpec((None,) * len(lead) + tuple(block), lambda *a: tuple(lead) + tuple(imap(*a)))


def rms_matmul(x, g, w, bias=None, lead=()):
    M, K = x.shape
    N = w.shape[-1]
    tm = _pick(M, (1024, 512, 256, 128))
    tn = _pick(N, (512, 256, 128))
    if bias is None:
        bias = jnp.zeros((N,), F32)
    return pl.pallas_call(
        _rms_matmul_kernel,
        grid=(M // tm, N // tn),
        in_specs=[
            pl.BlockSpec((tm, K), lambda i, j: (i, 0)),
            pl.BlockSpec((1, K), lambda i, j: (0, 0)),
            _layer_spec(lead, (K, tn), lambda i, j: (0, j)),
            pl.BlockSpec((1, tn), lambda i, j: (0, j)),
        ],
        out_specs=pl.BlockSpec((tm, tn), lambda i, j: (i, j)),
        out_shape=jax.ShapeDtypeStruct((M, N), F32),
        scratch_shapes=[pltpu.VMEM((tm, K), BF16)],
        compiler_params=_cp(("parallel", "arbitrary")),
        name="rms_matmul",
    )(x, g.reshape(1, K), w, bias.reshape(1, N))


def _rms_matmul_heads_kernel(x_ref, g_ref, w_ref, o_ref, oh_ref, xn_ref, *, n_head_steps):
    j = pl.program_id(1)

    @pl.when(j == 0)
    def _():
        x = x_ref[...]
        y = x * lax.rsqrt(jnp.mean(x * x, axis=-1, keepdims=True) + RMS_EPS)
        xn_ref[...] = (y * g_ref[...]).astype(BF16)

    y = _dot(xn_ref[...], w_ref[...])
    o_ref[...] = y

    @pl.when(j < n_head_steps)
    def _():
        oh_ref[...] = y.reshape(oh_ref.shape)


def rms_matmul_heads(x, g, w, lead, head_cols):
    M, K = x.shape
    N = w.shape[-1]
    tm = _pick(M, (512, 256, 128))
    tn = 8 * HEAD_DIM
    assert N % tn == 0 and head_cols % tn == 0
    n_head_steps = head_cols // tn
    return pl.pallas_call(
        functools.partial(_rms_matmul_heads_kernel, n_head_steps=n_head_steps),
        grid=(M // tm, N // tn),
        in_specs=[
            pl.BlockSpec((tm, K), lambda i, j: (i, 0)),
            pl.BlockSpec((1, K), lambda i, j: (0, 0)),
            _layer_spec(lead, (K, tn), lambda i, j: (0, j)),
        ],
        out_specs=[
            pl.BlockSpec((tm, tn), lambda i, j: (i, j)),
            pl.BlockSpec((tm, 8, HEAD_DIM), lambda i, j: (i, jnp.minimum(j, n_head_steps - 1), 0)),
        ],
        out_shape=[
            jax.ShapeDtypeStruct((M, N), F32),
            jax.ShapeDtypeStruct((M, head_cols // HEAD_DIM, HEAD_DIM), F32),
        ],
        scratch_shapes=[pltpu.VMEM((tm, K), BF16)],
        compiler_params=_cp(("parallel", "arbitrary")),
        name="rms_matmul_heads",
    )(x, g.reshape(1, K), w)


def _rmsnorm_kernel(x_ref, g_ref, o_ref):
    x = x_ref[...]
    y = x * lax.rsqrt(jnp.mean(x * x, axis=-1, keepdims=True) + RMS_EPS)
    o_ref[...] = y * g_ref[...]


def rmsnorm_rows(x, g):
    M, K = x.shape
    tm = _pick(M, (512, 256, 128))
    return pl.pallas_call(
        _rmsnorm_kernel,
        grid=(M // tm,),
        in_specs=[pl.BlockSpec((tm, K), lambda i: (i, 0)), pl.BlockSpec((1, K), lambda i: (0, 0))],
        out_specs=pl.BlockSpec((tm, K), lambda i: (i, 0)),
        out_shape=jax.ShapeDtypeStruct((M, K), F32),
        compiler_params=_cp(("parallel",)),
        name="final_rmsnorm",
    )(x, g.reshape(1, K))


def _cast_kernel(x_ref, o_ref):
    o_ref[...] = x_ref[...].astype(o_ref.dtype)


def cast_bf16(w):
    G, R, C = w.shape
    tr = _pick(R, (256, 128))
    return pl.pallas_call(
        _cast_kernel,
        grid=(G, R // tr),
        in_specs=[pl.BlockSpec((1, tr, C), lambda g, r: (g, r, 0))],
        out_specs=pl.BlockSpec((1, tr, C), lambda g, r: (g, r, 0)),
        out_shape=jax.ShapeDtypeStruct((G, R, C), BF16),
        compiler_params=_cp(("parallel", "parallel")),
        name="cast_bf16",
    )(w)


def _cast_halves_kernel(x_ref, o_ref):
    c = x_ref.shape[2]
    o_ref[0, 0, :, 0:c] = x_ref[0].astype(o_ref.dtype)
    o_ref[0, 0, :, c:] = jnp.zeros((o_ref.shape[2], o_ref.shape[3] - c), o_ref.dtype)


def cast_split_halves(w, cols_pad):
    G, R, C2 = w.shape
    C = C2 // 2
    tr = _pick(R, (256, 128))
    return pl.pallas_call(
        _cast_halves_kernel,
        grid=(G, 2, R // tr),
        in_specs=[pl.BlockSpec((1, tr, C), lambda g, h, r: (g, r, h))],
        out_specs=pl.BlockSpec((1, 1, tr, cols_pad), lambda g, h, r: (g, h, r, 0)),
        out_shape=jax.ShapeDtypeStruct((G, 2, R, cols_pad), BF16),
        compiler_params=_cp(("parallel", "parallel", "parallel")),
        name="cast_split_halves",
    )(w)


def _cast_pad_rows_kernel(x_ref, o_ref):
    r = x_ref.shape[1]
    o_ref[0, 0:r, :] = x_ref[0].astype(o_ref.dtype)
    o_ref[0, r:, :] = jnp.zeros((o_ref.shape[1] - r, o_ref.shape[2]), o_ref.dtype)


def cast_pad_rows(w, rows_pad):
    G, R, C = w.shape
    tc = _pick(C, (256, 128))
    assert R % 16 == 0 and rows_pad % 16 == 0
    return pl.pallas_call(
        _cast_pad_rows_kernel,
        grid=(G, C // tc),
        in_specs=[pl.BlockSpec((1, R, tc), lambda g, c: (g, 0, c))],
        out_specs=pl.BlockSpec((1, rows_pad, tc), lambda g, c: (g, 0, c)),
        out_shape=jax.ShapeDtypeStruct((G, rows_pad, C), BF16),
        compiler_params=_cp(("parallel", "parallel")),
        name="cast_pad_rows",
    )(w)


def _outproj_kernel(om_ref, oe_ref, w1_ref, w2_ref, x_ref, y_ref):
    y_ref[...] = x_ref[...] + _dot(om_ref[...], w1_ref[...]) + _dot(oe_ref[...], w2_ref[...])


def out_proj(o_mix, o_mem, w_out, layer, x):
    M = x.shape[0]
    tm = _pick(M, (1024, 512, 256, 128))
    tn = 1024
    assert MIX_DIM % MEM_DIM == 0
    return pl.pallas_call(
        _outproj_kernel,
        grid=(M // tm, D_MODEL // tn),
        in_specs=[
            pl.BlockSpec((tm, MIX_DIM), lambda i, j: (i, 0)),
            pl.BlockSpec((tm, MEM_DIM), lambda i, j: (i, 0)),
            _layer_spec((layer,), (MIX_DIM, tn), lambda i, j: (0, j)),
            _layer_spec((layer,), (MEM_DIM, tn), lambda i, j: (MIX_DIM // MEM_DIM, j)),
            pl.BlockSpec((tm, tn), lambda i, j: (i, j)),
        ],
        out_specs=pl.BlockSpec((tm, tn), lambda i, j: (i, j)),
        out_shape=jax.ShapeDtypeStruct((M, D_MODEL), F32),
        compiler_params=_cp(("parallel", "parallel")),
        name="out_proj",
    )(o_mix, o_mem, w_out, w_out, x)


def _mem_attn_kernel(q_ref, kv_ref, o_ref):
    scale = HEAD_DIM ** -0.5
    slots = 2 * N_MEM_HEADS
    n_mem = kv_ref.shape[1] // slots
    for h in range(N_MEM_HEADS):
        q = q_ref[0, :, h * HEAD_DIM:(h + 1) * HEAD_DIM].astype(BF16)
        k = kv_ref[0, pl.ds(h, n_mem, stride=slots), :].astype(BF16)
        v = kv_ref[0, pl.ds(N_MEM_HEADS + h, n_mem, stride=slots), :].astype(BF16)
        s = _dot_nt(q, k) * scale
        m = jnp.max(s, axis=-1, keepdims=True)
        p = jnp.exp(s - m)
        p = p / jnp.sum(p, axis=-1, keepdims=True)
        o_ref[0, :, h * HEAD_DIM:(h + 1) * HEAD_DIM] = _dot(p.astype(BF16), v).astype(o_ref.dtype)


def mem_attention(z3, col_block, kv, lead=()):
    B, L, _ = z3.shape
    rows = kv.shape[-2]
    tm = _pick(L, (1024, 512, 256, 128))
    return pl.pallas_call(
        _mem_attn_kernel,
        grid=(B, L // tm),
        in_specs=[
            pl.BlockSpec((1, tm, MEM_DIM), lambda b, i: (b, i, col_block)),
            _layer_spec(lead, (1, rows, HEAD_DIM), lambda b, i: (b, 0, 0)),
        ],
        out_specs=pl.BlockSpec((1, tm, MEM_DIM), lambda b, i: (b, i, 0)),
        out_shape=jax.ShapeDtypeStruct((B, L, MEM_DIM), BF16),
        compiler_params=_cp(("parallel", "parallel")),
        name="mem_attention",
    )(z3, kv)


def _ffn_kernel(*refs, blocks_per_seq, decode):
    if decode:
        (x_ref, g_ref, wa_ref, wu_ref, cw_ref, cb_ref, wd_ref, p1_ref, p2_ref,
         y_ref, at_ref, xn_ref) = refs
    else:
        (x_ref, g_ref, wa_ref, wu_ref, cw_ref, cb_ref, wd_ref,
         y_ref, at_ref, xn_ref, carry_ref) = refs
    i = pl.program_id(0)
    j = pl.program_id(1)

    @pl.when(j == 0)
    def _():
        x = x_ref[...]
        y = x * lax.rsqrt(jnp.mean(x * x, axis=-1, keepdims=True) + RMS_EPS)
        xn_ref[...] = (y * g_ref[...]).astype(BF16)
        y_ref[...] = x

    if not decode:
        @pl.when((i % blocks_per_seq) == 0)
        def _():
            carry_ref[j] = jnp.zeros(carry_ref.shape[1:], F32)

    xn = xn_ref[...]
    tm = xn.shape[0]
    n_tail = at_ref.shape[1]
    half = wa_ref.shape[1] // 2
    down = None
    for cs in (slice(0, half), slice(half, 2 * half)):
        a = _dot(xn, wa_ref[:, cs])
        u = _dot(xn, wu_ref[:, cs])
        if decode:
            a1 = p1_ref[:, cs]
            a2 = p2_ref[:, cs]
        else:
            prev = carry_ref[j, :, cs]
            row = lax.broadcasted_iota(jnp.int32, a.shape, 0)
            a1 = jnp.where(row == 0, prev[7:8], pltpu.roll(a, 1, 0))
            a2 = jnp.where(row == 0, prev[6:7], jnp.where(row == 1, prev[7:8], pltpu.roll(a, 2, 0)))
            carry_ref[j, :, cs] = a[tm - 8:tm]
        at_ref[0, :, cs] = a[tm - n_tail:tm]
        c = cb_ref[:, cs] + cw_ref[2:3, cs] * a
        c = c + cw_ref[0:1, cs] * a2
        c = c + cw_ref[1:2, cs] * a1
        h = (jax.nn.gelu(c) * u).astype(BF16)
        d = _dot(h, wd_ref[cs, :])
        down = d if down is None else down + d
    y_ref[...] += down


def conv_ffn(x, g, w_up, cw, cb, w_down, layer, *, seq_len=None, prev=None):
    M = x.shape[0]
    decode = prev is not None
    tm = M if decode else _pick(seq_len, (1024, 512, 256, 128))
    n_tail = tm if decode else 8
    nf = D_FF_PAD // FFN_TF
    in_specs = [
        pl.BlockSpec((tm, D_MODEL), lambda i, j: (i, 0)),
        pl.BlockSpec((1, D_MODEL), lambda i, j: (0, 0)),
        _layer_spec((layer, 0), (D_MODEL, FFN_TF), lambda i, j: (0, j)),
        _layer_spec((layer, 1), (D_MODEL, FFN_TF), lambda i, j: (0, j)),
        pl.BlockSpec((8, FFN_TF), lambda i, j: (0, j)),
        pl.BlockSpec((1, FFN_TF), lambda i, j: (0, j)),
        _layer_spec((layer,), (FFN_TF, D_MODEL), lambda i, j: (j, 0)),
    ]
    args = [x, g.reshape(1, D_MODEL), w_up, w_up, cw, cb, w_down]
    scratch = [pltpu.VMEM((tm, D_MODEL), BF16)]
    if decode:
        in_specs += [pl.BlockSpec((tm, FFN_TF), lambda i, j: (i, j))] * 2
        args += list(prev)
    else:
        scratch.append(pltpu.VMEM((nf, 8, FFN_TF), F32))
    return pl.pallas_call(
        functools.partial(_ffn_kernel, blocks_per_seq=(1 if decode else seq_len // tm), decode=decode),
        grid=(M // tm, nf),
        in_specs=in_specs,
        out_specs=[
            pl.BlockSpec((tm, D_MODEL), lambda i, j: (i, 0)),
            pl.BlockSpec((1, n_tail, FFN_TF), lambda i, j: (i, 0, j)),
        ],
        out_shape=[
            jax.ShapeDtypeStruct((M, D_MODEL), F32),
            jax.ShapeDtypeStruct((M // tm, n_tail, D_FF_PAD), F32),
        ],
        scratch_shapes=scratch,
        compiler_params=_cp(("arbitrary", "arbitrary")),
        name="conv_ffn",
    )(*args)


def _hgrn_gates(q, f, log_lb, log1m_lb):
    qs = jax.nn.silu(q)
    log_f = jnp.logaddexp(log_lb, log1m_lb + jax.nn.log_sigmoid(f))
    k = 1.0 - jnp.exp(log_f)
    return qs, k, log_f


def _hgrn_out(o, g, onorm):
    y = o * lax.rsqrt(jnp.mean(o * o, axis=-1, keepdims=True) + RMS_EPS)
    return (y * onorm) * jax.nn.silu(g)


def _hgrn_kernel(q_ref, f_ref, i_ref, g_ref, llb_ref, l1m_ref, on_ref, s0_ref, o_ref, s_ref, st_ref,
                 *, n_chunks):
    c = pl.program_id(2)
    T = HGRN_T
    n_heads = st_ref.shape[0]

    @pl.when(c == 0)
    def _():
        for h in range(n_heads):
            st_ref[h] = s0_ref[0, h].T

    assert HGRN_SUB == 8
    rl1 = lax.broadcasted_iota(jnp.int32, (T, 1), 0) % HGRN_SUB
    t_i = lax.broadcasted_iota(jnp.int32, (T, T), 0)
    s_i = lax.broadcasted_iota(jnp.int32, (T, T), 1)
    tril = jnp.where(s_i <= t_i, 1.0, 0.0).astype(BF16)
    pair_masks = []
    size = 2 * HGRN_SUB
    while size <= T:
        same = (t_i & -size) == (s_i & -size)
        pair_masks.append((size, same & ((t_i & (size - 1)) >= size // 2) & ((s_i & (size - 1)) < size // 2)))
        size *= 2

    def shift_rows(x, d):
        return pltpu.roll(x.reshape(T // HGRN_SUB, HGRN_SUB, HEAD_DIM), d, 1).reshape(T, HEAD_DIM)

    def one_head(q, k, b, v, h, sl):
        hs = slice(h * HEAD_DIM, (h + 1) * HEAD_DIM)
        st = st_ref[h]
        o = _dot_nt((q * jnp.exp2(b)).astype(BF16), st.astype(BF16))
        o = o + jnp.sum(q * k, axis=-1, keepdims=True) * v
        for d in range(1, HGRN_SUB):
            w = jnp.exp2(b - shift_rows(b, d))
            a = jnp.sum(q * shift_rows(k, d) * w, axis=-1, keepdims=True)
            o = o + jnp.where(rl1 >= d, a, 0.0) * shift_rows(v, d)
        att = jnp.zeros((T, T), F32)
        for size, keep in pair_masks:
            half = size // 2
            refs = [jnp.broadcast_to(b[j * size + half - 1:j * size + half], (size, HEAD_DIM))
                    for j in range(T // size)]
            r = jnp.concatenate(refs, axis=0) if len(refs) > 1 else refs[0]
            qt = q * jnp.exp2(jnp.minimum(b - r, 0.0))
            kt = k * jnp.exp2(jnp.minimum(r - b, 0.0))
            att = att + jnp.where(keep, _dot_nt(qt.astype(BF16), kt.astype(BF16)), 0.0)
        o = o + _dot(att.astype(BF16), v.astype(BF16))
        bl = b[T - 1:T]
        kt = k * jnp.exp2(bl - b)
        st_ref[h] = st * jnp.exp2(bl) + _dot_tn(v.astype(BF16), kt.astype(BF16))
        o_ref[0, sl, hs] = _hgrn_out(o, g_ref[0, sl, hs], on_ref[:, hs]).astype(o_ref.dtype)

    def chunk(ci, carry):
        sl = pl.ds(pl.multiple_of(ci * T, T), T)
        q, k, lf = _hgrn_gates(q_ref[0, sl, :], f_ref[0, sl, :], llb_ref[...], l1m_ref[...])
        b = sum(_dot(tril, part) for part in _split3(lf)) * LOG2E
        for h in range(n_heads):
            hs = slice(h * HEAD_DIM, (h + 1) * HEAD_DIM)
            one_head(q[:, hs], k[:, hs], b[:, hs], i_ref[0, sl, hs], h, sl)
        return carry

    lax.fori_loop(0, n_chunks, chunk, 0)

    @pl.when(c == pl.num_programs(2) - 1)
    def _():
        for h in range(n_heads):
            s_ref[0, h] = st_ref[h].T


def hgrn_prompt(z3, log_lb, log1m_lb, onorm, s0):
    B, L, _ = z3.shape
    H = N_MIX_HEADS
    hb = HGRN_HEADS_PER_STEP
    wb = hb * HEAD_DIM
    tc = _pick(L, (512, 256, 128, 64))
    zspec = lambda k: pl.BlockSpec((1, tc, wb), lambda b, h, c: (b, c, k * (H // hb) + h))
    vspec = pl.BlockSpec((1, wb), lambda b, h, c: (0, h))
    sspec = pl.BlockSpec((1, hb, HEAD_DIM, HEAD_DIM), lambda b, h, c: (b, h, 0, 0))
    return pl.pallas_call(
        functools.partial(_hgrn_kernel, n_chunks=tc // HGRN_T),
        grid=(B, H // hb, L // tc),
        in_specs=[zspec(0), zspec(1), zspec(2), zspec(3), vspec, vspec, vspec, sspec],
        out_specs=[pl.BlockSpec((1, tc, wb), lambda b, h, c: (b, c, h)), sspec],
        out_shape=[
            jax.ShapeDtypeStruct((B, L, MIX_DIM), BF16),
            jax.ShapeDtypeStruct((B, H, HEAD_DIM, HEAD_DIM), F32),
        ],
        scratch_shapes=[pltpu.VMEM((hb, HEAD_DIM, HEAD_DIM), F32)],
        compiler_params=_cp(("parallel", "parallel", "arbitrary")),
        name="hgrn_chunked",
    )(z3, z3, z3, z3, log_lb, log1m_lb, onorm, s0)


def _hgrn_step_kernel(z_ref, llb_ref, l1m_ref, on_ref, s0_ref, o_ref, s_ref):
    for h in range(N_MIX_HEADS):
        col = lambda k: slice((k * N_MIX_HEADS + h) * HEAD_DIM, (k * N_MIX_HEADS + h + 1) * HEAD_DIM)
        hs = slice(h * HEAD_DIM, (h + 1) * HEAD_DIM)
        q, k, lf = _hgrn_gates(z_ref[0, :, col(0)], z_ref[0, :, col(1)], llb_ref[:, hs], l1m_ref[:, hs])
        v = z_ref[0, :, col(2)]
        g = z_ref[0, :, col(3)]
        rows = jnp.concatenate([q, k, jnp.exp(lf), jnp.zeros((5, HEAD_DIM), F32)], axis=0)
        cols = rows.T
        s_new = cols[:, 2:3] * s0_ref[0, h] + cols[:, 1:2] * v
        s_ref[0, h] = s_new
        o = jnp.sum(cols[:, 0:1] * s_new, axis=0, keepdims=True)
        o_ref[0, :, hs] = _hgrn_out(o, g, on_ref[:, hs]).astype(o_ref.dtype)


def hgrn_step(z3, log_lb, log1m_lb, onorm, s0):
    B = z3.shape[0]
    W = z3.shape[2]
    H = N_MIX_HEADS
    vspec = pl.BlockSpec((1, MIX_DIM), lambda b: (0, 0))
    sspec = pl.BlockSpec((1, H, HEAD_DIM, HEAD_DIM), lambda b: (b, 0, 0, 0))
    return pl.pallas_call(
        _hgrn_step_kernel,
        grid=(B,),
        in_specs=[pl.BlockSpec((1, 1, W), lambda b: (b, 0, 0)), vspec, vspec, vspec, sspec],
        out_specs=[pl.BlockSpec((1, 1, MIX_DIM), lambda b: (b, 0, 0)), sspec],
        out_shape=[
            jax.ShapeDtypeStruct((B, 1, MIX_DIM), BF16),
            jax.ShapeDtypeStruct((B, H, HEAD_DIM, HEAD_DIM), F32),
        ],
        compiler_params=_cp(("parallel",)),
        name="hgrn_step",
    )(z3, log_lb, log1m_lb, onorm, s0)


def _pe_proj_kernel(pe_ref, w_ref, o_ref):
    pe = jnp.broadcast_to(pe_ref[0], (8, pe_ref.shape[2])).astype(BF16)
    o_ref[0] = _dot(pe, w_ref[0])


def pe_proj(pe, w1):
    K = pe.shape[2]
    return pl.pallas_call(
        _pe_proj_kernel,
        grid=(2,),
        in_specs=[pl.BlockSpec((1, 1, K), lambda i: (i, 0, 0)), pl.BlockSpec((1, K, CMP_HID), lambda i: (i, 0, 0))],
        out_specs=pl.BlockSpec((1, 8, CMP_HID), lambda i: (i, 0, 0)),
        out_shape=jax.ShapeDtypeStruct((2, 8, CMP_HID), F32),
        compiler_params=_cp(("parallel",)),
        name="cmp_pe_proj",
    )(pe, w1)


def _cmp_mlp_kernel(x_ref, w1_ref, w2_ref, pe_ref, o_ref):
    n_sub = x_ref.shape[1] // CMP_STRIDE
    half = CMP_STRIDE * HEAD_DIM
    x = jnp.concatenate([x_ref[0, pl.ds(p, n_sub, stride=CMP_STRIDE), :] for p in range(CMP_STRIDE)],
                        axis=1).astype(BF16)
    a = _dot(x, w1_ref[0, 0:half])
    bm = _dot(x, w1_ref[0, half:2 * half])
    n = a.shape[0]
    h = a + pltpu.roll(bm, n - 1, 0) + pe_ref[0, 0:1]
    o_ref[0, 0] = _dot(jax.nn.gelu(h).astype(BF16), w2_ref[0])


def cmp_mlp(rows3, w1, w2, pe_h):
    B, L, _ = rows3.shape
    n_sub = L // CMP_STRIDE
    half = CMP_STRIDE * HEAD_DIM
    return pl.pallas_call(
        _cmp_mlp_kernel,
        grid=(2, B, N_KV),
        in_specs=[
            pl.BlockSpec((1, L, HEAD_DIM), lambda k, b, n: (b, 0, k * N_KV + n)),
            pl.BlockSpec((1, 2 * half, CMP_HID), lambda k, b, n: (k, 0, 0)),
            pl.BlockSpec((1, CMP_HID, HEAD_DIM), lambda k, b, n: (k, 0, 0)),
            pl.BlockSpec((1, 8, CMP_HID), lambda k, b, n: (k, 0, 0)),
        ],
        out_specs=pl.BlockSpec((1, 1, n_sub, HEAD_DIM), lambda k, b, n: (k, b * N_KV + n, 0, 0)),
        out_shape=jax.ShapeDtypeStruct((2, B * N_KV, n_sub, HEAD_DIM), F32),
        compiler_params=_cp(("parallel", "parallel", "parallel")),
        name="cmp_mlp",
    )(rows3, w1, w2, pe_h)


LOG2E = 1.4426950408889634


def _masked_softmax_rows(s, mask, exp_fn=jnp.exp):
    s = jnp.where(mask, s, -jnp.inf)
    m = jnp.max(s, axis=-1, keepdims=True)
    m = jnp.where(m == -jnp.inf, 0.0, m)
    p = exp_fn(s - m)
    return p / jnp.maximum(jnp.sum(p, axis=-1, keepdims=True), 1e-30)


def _split3(p):
    hi = p.astype(BF16)
    r = p - hi.astype(F32)
    mid = r.astype(BF16)
    lo = (r - mid.astype(F32)).astype(BF16)
    return hi, mid, lo


def _masked_softmax_cols(s, mask, exp_fn):
    s = jnp.where(mask, s, -jnp.inf)
    m = jnp.max(s, axis=0, keepdims=True)
    m = jnp.where(m == -jnp.inf, 0.0, m)
    p = exp_fn(s - m)
    return p / jnp.maximum(jnp.sum(p, axis=0, keepdims=True), 1e-30)


def _nsa_prompt_t_kernel(q_ref, gt_ref, kc_ref, vc_ref, ks_ref, vs_ref, kw_ref, vw_ref, o_ref,
                         kaug_ref, vst_ref, kwb_ref, vwt_ref, vct_ref, score_ref, *, tq, tk, tw, seq_len):
    qi = pl.program_id(2)
    n_blk = seq_len // SLC_BLOCK
    scale = HEAD_DIM ** -0.5 * LOG2E
    cols = GROUP * tq
    nc = kc_ref.shape[1]

    @pl.when(qi == 0)
    def _():
        r = lax.broadcasted_iota(jnp.int32, (seq_len, HEAD_DIM), 0) // SLC_BLOCK
        col = lax.broadcasted_iota(jnp.int32, (seq_len, HEAD_DIM), 1)
        kaug_ref[:, 0:HEAD_DIM] = ks_ref[0].astype(BF16)
        kaug_ref[:, HEAD_DIM:2 * HEAD_DIM] = jnp.where(r == col, 1.0, 0.0).astype(BF16)
        kwb_ref[...] = kw_ref[0].astype(BF16)
        for t in range(seq_len // tk):
            vst_ref[t] = vs_ref[0, t * tk:(t + 1) * tk, :].T.astype(BF16)
        for t in range(seq_len // tw):
            vwt_ref[t] = vw_ref[0, t * tw:(t + 1) * tw, :].T.astype(BF16)
        vct_ref[...] = vc_ref[0].T.astype(BF16)

    q0 = qi * tq
    qs = q_ref[0] * scale
    qb = jnp.concatenate([qs[:, g * HEAD_DIM:(g + 1) * HEAD_DIM] for g in range(GROUP)], axis=0).astype(BF16)

    def qpos(n_keys):
        return q0 + lax.broadcasted_iota(jnp.int32, (n_keys, cols), 1) % tq

    def kidx(n_keys):
        return lax.broadcasted_iota(jnp.int32, (n_keys, cols), 0)

    cend = CMP_STRIDE * kidx(nc) + (2 * CMP_STRIDE - 1)
    p_c = _masked_softmax_cols(_dot_nt(kc_ref[0].astype(BF16), qb), cend <= qpos(nc), jnp.exp2)
    o_cmp = _dot(vct_ref[...], p_c.astype(BF16))

    psum = p_c[:, 0:tq]
    for g in range(1, GROUP):
        psum = psum + p_c[:, g * tq:(g + 1) * tq]
    ji = lax.broadcasted_iota(jnp.int32, (HEAD_DIM, nc), 0)
    ci = lax.broadcasted_iota(jnp.int32, (HEAD_DIM, nc), 1)
    ratio = SLC_BLOCK // CMP_STRIDE
    member = ((ci >= ratio * ji - 1) & (ci <= ratio * ji + ratio - 1)).astype(BF16)
    imp = sum(_dot(member, part) for part in _split3(psum))
    blk = lax.broadcasted_iota(jnp.int32, (HEAD_DIM, tq), 0)
    cur = (q0 + lax.broadcasted_iota(jnp.int32, (HEAD_DIM, tq), 1)) // SLC_BLOCK
    valid = (blk <= cur) & (blk < n_blk)
    forced = valid & ((blk == 0) | (blk == cur) | (blk == cur - 1))
    score = jnp.where(forced, SEL_BONUS, jnp.where(valid, imp, -jnp.inf))
    score = score[0:n_blk]
    score_ref[...] = score
    blk = lax.broadcasted_iota(jnp.int32, (n_blk, tq), 0)
    per_tile = tq // SLC_BLOCK

    def rank_body(it, rank):
        for u in range(per_tile):
            i = it * per_tile + u
            si = score_ref[pl.ds(i, 1), :]
            later = jnp.where(blk > i, 1, 0)
            rank = rank + jnp.where(si > score, 1, jnp.where(si == score, later, 0))
        return rank

    rank = lax.fori_loop(0, qi + 1, rank_body, jnp.zeros((n_blk, tq), jnp.int32))
    bias_t = jnp.where(rank < N_SELECT, 0.0, NEG_BIG)
    if n_blk < HEAD_DIM:
        bias_t = jnp.concatenate([bias_t, jnp.zeros((HEAD_DIM - n_blk, tq), F32)], axis=0)
    bias = bias_t.T.astype(BF16)
    qaug = jnp.concatenate([qb, jnp.concatenate([bias] * GROUP, axis=0)], axis=1)

    qpos_k = qpos(tk)
    krow = kidx(tk)

    def slc_body(kt, carry, causal):
        m, l, acc = carry
        s = _dot_nt(kaug_ref[pl.ds(pl.multiple_of(kt * tk, tk), tk), :], qaug)
        if causal:
            s = jnp.where(kt * tk + krow <= qpos_k, s, -jnp.inf)
        m_new = jnp.maximum(m, jnp.max(s, axis=0, keepdims=True))
        alpha = jnp.exp2(m - m_new)
        p = jnp.exp2(s - m_new)
        l = alpha * l + jnp.sum(p, axis=0, keepdims=True)
        acc = alpha * acc + _dot(vst_ref[kt], p.astype(BF16))
        return m_new, l, acc

    init = (jnp.full((1, cols), -jnp.inf, F32), jnp.zeros((1, cols), F32), jnp.zeros((HEAD_DIM, cols), F32))
    n_full = q0 // tk

    def two_tiles(i, carry):
        return slc_body(2 * i + 1, slc_body(2 * i, carry, False), False)

    carry = lax.fori_loop(0, n_full // 2, two_tiles, init)
    carry = lax.fori_loop(n_full - n_full % 2, n_full, functools.partial(slc_body, causal=False), carry)
    _, l_s, acc_s = slc_body(n_full, carry, True)
    o_slc = acc_s / jnp.maximum(l_s, 1e-30)

    wk = min(WINDOW + tw, seq_len)
    d_rel = (lax.broadcasted_iota(jnp.int32, (wk, GROUP * tw), 1) % tw
             - lax.broadcasted_iota(jnp.int32, (wk, GROUP * tw), 0))
    win_parts = []
    for h in range(tq // tw):
        qh = q0 + h * tw
        k0 = jnp.minimum(jnp.maximum(qh - WINDOW, 0), seq_len - wk)
        qb_h = jnp.concatenate([qb[g * tq + h * tw:g * tq + (h + 1) * tw] for g in range(GROUP)], axis=0)
        d = d_rel + (qh - k0)
        s_w = _dot_nt(kwb_ref[pl.ds(pl.multiple_of(k0, tw), wk), :], qb_h)
        s_w = jnp.where((d >= 0) & (d <= WINDOW), s_w, -jnp.inf)
        p_w = jnp.exp2(s_w - jnp.max(s_w, axis=0, keepdims=True))
        l_w = jnp.sum(p_w, axis=0, keepdims=True)
        p_wb = p_w.astype(BF16)
        t0 = k0 // tw
        o_h = sum(_dot(vwt_ref[t0 + i], p_wb[i * tw:(i + 1) * tw]) for i in range(wk // tw))
        win_parts.append(o_h / jnp.maximum(l_w, 1e-30))
    o_win = jnp.concatenate([win_parts[h][:, g * tw:(g + 1) * tw]
                             for g in range(GROUP) for h in range(tq // tw)], axis=1)

    gates = jax.nn.sigmoid(gt_ref[0]).T
    for g in range(GROUP):
        cs = slice(g * tq, (g + 1) * tq)
        c0 = g * N_BRANCH
        o = (gates[c0:c0 + 1] * o_cmp[:, cs] + gates[c0 + 1:c0 + 2] * o_slc[:, cs]
             + gates[c0 + 2:c0 + 3] * o_win[:, cs])
        o_ref[0, :, g * HEAD_DIM:(g + 1) * HEAD_DIM] = o.T.astype(o_ref.dtype)


def nsa_prompt_t(zq3, k_cmp, v_cmp, rows3):
    B, L, _ = zq3.shape
    gate_block0 = (MIX_DIM + MEM_DIM) // HEAD_DIM
    tq = _pick(L, (512, 256, 128))
    tk = _pick(L, (512, 256, 128))
    tw = min(tq, 256)
    assert WINDOW % tw == 0 and tk % tq == 0 and tq % tw == 0
    n_sub = k_cmp.shape[1]
    qw = GROUP * HEAD_DIM
    rspec = lambda kind: pl.BlockSpec((1, L, HEAD_DIM), lambda b, n, i: (b, 0, kind * N_KV + n))
    cspec = pl.BlockSpec((1, n_sub, HEAD_DIM), lambda b, n, i: (b * N_KV + n, 0, 0))
    return pl.pallas_call(
        functools.partial(_nsa_prompt_t_kernel, tq=tq, tk=tk, tw=tw, seq_len=L),
        grid=(B, N_KV, L // tq),
        in_specs=[
            pl.BlockSpec((1, tq, qw), lambda b, n, i: (b, i, n)),
            pl.BlockSpec((1, tq, HEAD_DIM), lambda b, n, i: (b, i, gate_block0 + n)),
            cspec, cspec, rspec(2), rspec(3), rspec(4), rspec(5),
        ],
        out_specs=pl.BlockSpec((1, tq, qw), lambda b, n, i: (b, i, n)),
        out_shape=jax.ShapeDtypeStruct((B, L, MIX_DIM), BF16),
        scratch_shapes=[
            pltpu.VMEM((L, 2 * HEAD_DIM), BF16),
            pltpu.VMEM((L // tk, HEAD_DIM, tk), BF16),
            pltpu.VMEM((L, HEAD_DIM), BF16),
            pltpu.VMEM((L // tw, HEAD_DIM, tw), BF16),
            pltpu.VMEM((HEAD_DIM, n_sub), BF16),
            pltpu.VMEM((L // SLC_BLOCK, tq), F32),
        ],
        compiler_params=_cp(("parallel", "parallel", "arbitrary")),
        name="nsa_prompt",
    )(zq3, zq3, k_cmp, v_cmp, rows3, rows3, rows3, rows3)


def _dec_cmp_kernel(pt_ref, *refs, n_pg, n_groups):
    del pt_ref
    pg_refs = refs[:n_pg]
    new_ref, w1_ref, w2_ref, pe_ref, kc_ref, vc_ref, carry_ref = refs[n_pg:]
    g = pl.program_id(1)
    sub_pg = PAGE_SIZE // CMP_STRIDE
    n_sub = n_pg * sub_pg
    pos_per_chunk = 4
    kw = pos_per_chunk * HEAD_DIM

    @pl.when(g == 0)
    def _():
        carry_ref[...] = jnp.zeros_like(carry_ref)

    sub_i = lax.broadcasted_iota(jnp.int32, (8, n_sub, kw), 1)
    new_row = jnp.concatenate([new_ref[0], jnp.zeros((8, kw - HEAD_DIM), F32)], axis=1)[:, None, :]
    ys = [None, None]
    for pc in range(CMP_STRIDE // pos_per_chunk):
        x = jnp.concatenate(
            [jnp.concatenate([jnp.swapaxes(pg_refs[r][0, pl.ds(p, sub_pg, stride=CMP_STRIDE), :, :], 0, 1)
                              for r in range(n_pg)], axis=1)
             for p in range(pc * pos_per_chunk, (pc + 1) * pos_per_chunk)], axis=2)
        x = jnp.where(g < n_groups, x, 0.0)
        if pc == 0:
            x = jnp.where((g == n_groups) & (sub_i == 0), new_row, x)
        xb = x.astype(BF16)
        for kind in range(2):
            part = _dot(xb[kind * N_KV:(kind + 1) * N_KV].reshape(N_KV * n_sub, kw),
                        w1_ref[kind, pc * kw:(pc + 1) * kw, :])
            ys[kind] = part if ys[kind] is None else ys[kind] + part

    row = lax.broadcasted_iota(jnp.int32, (n_sub, CMP_HID), 0)
    for kind, out_ref in enumerate((kc_ref, vc_ref)):
        y = ys[kind]
        for n in range(N_KV):
            slot = kind * N_KV + n
            a = y[n * n_sub:(n + 1) * n_sub, 0:CMP_HID]
            bm = y[n * n_sub:(n + 1) * n_sub, CMP_HID:2 * CMP_HID]
            a_prev = jnp.where(row == 0, carry_ref[slot:slot + 1, :], pltpu.roll(a, 1, 0))
            carry_ref[slot:slot + 1, :] = a[n_sub - 1:n_sub]
            h = a_prev + bm + pe_ref[kind, 0:1]
            out_ref[0, n] = _dot(jax.nn.gelu(h).astype(BF16), w2_ref[kind])


def dec_cmp_tokens(cache4, page_table, new_cmp, w1ab, w2, pe_h):
    B, n_pages = page_table.shape
    n_pg = _pick(n_pages, (8, 4, 2))
    assert n_pages % n_pg == 0
    n_groups = n_pages // n_pg
    sub_pg = PAGE_SIZE // CMP_STRIDE
    n_sub = n_pg * sub_pg
    n_steps = n_groups + 1
    while (n_steps * n_sub) % HEAD_DIM:
        n_steps += 1
    half = CMP_STRIDE * HEAD_DIM

    def page_spec(r):
        def imap(b, g, pt):
            return (pt[b * n_pages + jnp.minimum(g * n_pg + r, n_pages - 1)], 0, 0, 0)
        return pl.BlockSpec((1, PAGE_SIZE, 8, HEAD_DIM), imap)

    out_spec = pl.BlockSpec((1, N_KV, n_sub, HEAD_DIM), lambda b, g, pt: (b, 0, g, 0))
    grid_spec = pltpu.PrefetchScalarGridSpec(
        num_scalar_prefetch=1,
        grid=(B, n_steps),
        in_specs=[page_spec(r) for r in range(n_pg)] + [
            pl.BlockSpec((1, 8, HEAD_DIM), lambda b, g, pt: (b, 0, 0)),
            pl.BlockSpec((2, half, 2 * CMP_HID), lambda b, g, pt: (0, 0, 0)),
            pl.BlockSpec((2, CMP_HID, HEAD_DIM), lambda b, g, pt: (0, 0, 0)),
            pl.BlockSpec((2, 8, CMP_HID), lambda b, g, pt: (0, 0, 0)),
        ],
        out_specs=[out_spec, out_spec],
        scratch_shapes=[pltpu.VMEM((8, CMP_HID), F32)],
    )
    tok = jax.ShapeDtypeStruct((B, N_KV, n_steps * n_sub, HEAD_DIM), F32)
    return pl.pallas_call(
        functools.partial(_dec_cmp_kernel, n_pg=n_pg, n_groups=n_groups),
        grid_spec=grid_spec,
        out_shape=[tok, tok],
        compiler_params=_cp(("parallel", "arbitrary")),
        name="dec_cmp_tokens",
    )(page_table.reshape(-1), *([cache4] * n_pg), new_cmp, w1ab, w2, pe_h)


def _dec_select_kernel(q_ref, kc_ref, vc_ref, mem_ref, ocmp_ref, idx_ref, *, q_pos, nc, n_blk, n_sel):
    C = kc_ref.shape[2]
    JB = mem_ref.shape[1]
    scale = HEAD_DIM ** -0.5
    c_i = lax.broadcasted_iota(jnp.int32, (8, C), 1) - 1
    g_i = lax.broadcasted_iota(jnp.int32, (8, C), 0)
    ok = (c_i >= 0) & (c_i < nc) & (CMP_STRIDE * c_i + (2 * CMP_STRIDE - 1) <= q_pos)
    psums = []
    for n in range(N_KV):
        qb = (q_ref[0, n] * scale).astype(BF16)
        p = _masked_softmax_rows(_dot_nt(qb, kc_ref[0, n].astype(BF16)), ok)
        ocmp_ref[0, n] = _dot(p.astype(BF16), vc_ref[0, n].astype(BF16))
        psums.append(jnp.sum(jnp.where(g_i < GROUP, p, 0.0), axis=0, keepdims=True))
    psum = jnp.concatenate(psums + [jnp.zeros((8 - N_KV, C), F32)], axis=0)
    imp = sum(_dot(part, mem_ref[...]) for part in _split3(psum))
    j_i = lax.broadcasted_iota(jnp.int32, (8, JB), 1)
    cur = q_pos // SLC_BLOCK
    valid = (j_i <= cur) & (j_i < n_blk)
    forced = valid & ((j_i == 0) | (j_i == cur) | (j_i == cur - 1))
    score = jnp.where(forced, SEL_BONUS, jnp.where(valid, imp, -jnp.inf))
    j_f = j_i.astype(F32)
    taken = j_i >= n_blk
    lane_o = lax.broadcasted_iota(jnp.int32, (8, HEAD_DIM), 1)
    out = jnp.zeros((8, HEAD_DIM), F32)
    for r in range(n_sel):
        m = jnp.max(jnp.where(taken, -jnp.inf, score), axis=-1, keepdims=True)
        cand = jnp.logical_not(taken) & (score == m)
        idx = jnp.min(jnp.where(cand, j_f, float(JB)), axis=-1, keepdims=True)
        out = jnp.where(lane_o == r, idx, out)
        taken = taken | (j_f == idx)
    idx_ref[0] = out.astype(jnp.int32)


def dec_select(q8, k_cmp, v_cmp, member, *, q_pos, nc, n_blk, n_sel):
    B, _, C, _ = k_cmp.shape
    JB = member.shape[1]
    head_spec = pl.BlockSpec((1, N_KV, 8, HEAD_DIM), lambda b: (b, 0, 0, 0))
    tok_spec = pl.BlockSpec((1, N_KV, C, HEAD_DIM), lambda b: (b, 0, 0, 0))
    return pl.pallas_call(
        functools.partial(_dec_select_kernel, q_pos=q_pos, nc=nc, n_blk=n_blk, n_sel=n_sel),
        grid=(B,),
        in_specs=[head_spec, tok_spec, tok_spec, pl.BlockSpec((C, JB), lambda b: (0, 0))],
        out_specs=[head_spec, pl.BlockSpec((1, 8, HEAD_DIM), lambda b: (b, 0, 0))],
        out_shape=[
            jax.ShapeDtypeStruct((B, N_KV, 8, HEAD_DIM), F32),
            jax.ShapeDtypeStruct((B, 8, HEAD_DIM), jnp.int32),
        ],
        compiler_params=_cp(("parallel",)),
        name="dec_select",
    )(q8, k_cmp, v_cmp, member)


def _dec_attend_kernel(idx_ref, pt_ref, q_ref, gl_ref, ocmp_ref, nslc_ref, win_ref, nwin_ref, *rest,
                       q_pos, past_len, n_sel):
    del pt_ref
    blk_refs = rest[:n_sel]
    o_ref = rest[n_sel]
    b = pl.program_id(0)
    n = pl.program_id(1)
    base = (b * N_KV + n) * n_sel
    scale = HEAD_DIM ** -0.5
    past_blocks = past_len // SLC_BLOCK
    rb = SLC_BLOCK * 8
    rw = win_ref.shape[1]
    w_buf = rw // 8
    qf = q_ref[0, 0] * scale
    qb = qf.astype(BF16)
    js = [idx_ref[base + s] for s in range(n_sel)]
    has_new = js[0] == past_blocks
    for s in range(1, n_sel):
        has_new = has_new | (js[s] == past_blocks)
    new_ok = has_new & (past_len <= q_pos)

    def head_rows(x):
        slot = lax.broadcasted_iota(jnp.int32, (8, HEAD_DIM), 0)
        k = jnp.sum(jnp.where(slot == n, x, 0.0), axis=0, keepdims=True)
        v = jnp.sum(jnp.where(slot == n + N_KV, x, 0.0), axis=0, keepdims=True)
        return k, v

    def finish(m):
        return jnp.where(m == -jnp.inf, 0.0, m)

    r_s = lax.broadcasted_iota(jnp.int32, (8, rb), 1)
    t_s = r_s >> 3
    mine_s = (r_s & 7) == n

    def slc_scores(s):
        blk = blk_refs[s][0, 0].reshape(rb, HEAD_DIM)
        sc = _dot_nt(qb, blk.astype(BF16))
        ok = mine_s & (js[s] * SLC_BLOCK + t_s <= q_pos) & (js[s] < past_blocks)
        return jnp.where(ok, sc, -jnp.inf)

    k_new, v_new = head_rows(nslc_ref[0])
    s_new = jnp.where(new_ok, jnp.sum(qf * k_new, axis=-1, keepdims=True), -jnp.inf)
    scores = [slc_scores(s) for s in range(n_sel)]
    m = s_new
    for sc in scores:
        m = jnp.maximum(m, jnp.max(sc, axis=-1, keepdims=True))
    m = finish(m)
    p_new = jnp.exp(s_new - m)
    l = p_new
    acc = p_new * v_new
    for s, sc in enumerate(scores):
        p = jnp.exp(sc - m)
        l = l + jnp.sum(p, axis=-1, keepdims=True)
        acc = acc + _dot(pltpu.roll(p, N_KV, 1).astype(BF16), blk_refs[s][0, 0].reshape(rb, HEAD_DIM).astype(BF16))
    o_slc = acc / jnp.maximum(l, 1e-30)

    r_w = lax.broadcasted_iota(jnp.int32, (8, rw), 1)
    w_pos = past_len - w_buf + (r_w >> 3)
    w_ok = ((r_w & 7) == n) & (q_pos - w_pos >= 0) & (q_pos - w_pos <= WINDOW) & (w_pos >= 0)
    win = win_ref[0].astype(BF16)
    s_w = jnp.where(w_ok, _dot_nt(qb, win), -jnp.inf)
    k_nw, v_nw = head_rows(nwin_ref[0])
    nw_ok = (q_pos - past_len >= 0) and (q_pos - past_len <= WINDOW)
    s_nw = jnp.sum(qf * k_nw, axis=-1, keepdims=True) if nw_ok else jnp.full((8, 1), -jnp.inf, F32)
    m = finish(jnp.maximum(s_nw, jnp.max(s_w, axis=-1, keepdims=True)))
    p_w = jnp.exp(s_w - m)
    p_nw = jnp.exp(s_nw - m)
    l = p_nw + jnp.sum(p_w, axis=-1, keepdims=True)
    acc = p_nw * v_nw + _dot(pltpu.roll(p_w, N_KV, 1).astype(BF16), win)
    o_win = acc / jnp.maximum(l, 1e-30)

    gates = jax.nn.sigmoid(gl_ref[0, 0])
    o_ref[0, 0] = gates[:, 0:1] * ocmp_ref[0, 0] + gates[:, 1:2] * o_slc + gates[:, 2:3] * o_win


def dec_attend(idx_flat, page_table, q8, gate8, ocmp8, new_slc, win2, new_win, cache5, *, q_pos, past_len, n_sel):
    B, n_pages = page_table.shape
    rw = win2.shape[1]
    past_blocks = past_len // SLC_BLOCK
    per_page = PAGE_SIZE // SLC_BLOCK

    def blk_spec(s):
        def imap(b, n, idx, pt):
            j = jnp.minimum(idx[(b * N_KV + n) * n_sel + s], past_blocks - 1)
            return (pt[b * n_pages + j // per_page], j % per_page, 0, 1, 0)
        return pl.BlockSpec((1, 1, SLC_BLOCK, 8, HEAD_DIM), imap)

    head_spec = pl.BlockSpec((1, 1, 8, HEAD_DIM), lambda b, n, idx, pt: (b, n, 0, 0))
    tok_spec = pl.BlockSpec((1, 8, HEAD_DIM), lambda b, n, idx, pt: (b, 0, 0))
    grid_spec = pltpu.PrefetchScalarGridSpec(
        num_scalar_prefetch=2,
        grid=(B, N_KV),
        in_specs=[head_spec, head_spec, head_spec, tok_spec,
                  pl.BlockSpec((1, rw, HEAD_DIM), lambda b, n, idx, pt: (b, 0, 0)), tok_spec]
        + [blk_spec(s) for s in range(n_sel)],
        out_specs=head_spec,
    )
    return pl.pallas_call(
        functools.partial(_dec_attend_kernel, q_pos=q_pos, past_len=past_len, n_sel=n_sel),
        grid_spec=grid_spec,
        out_shape=jax.ShapeDtypeStruct((B, N_KV, 8, HEAD_DIM), F32),
        compiler_params=_cp(("parallel", "arbitrary")),
        name="dec_attend",
    )(idx_flat, page_table.reshape(-1), q8, gate8, ocmp8, new_slc, win2, new_win, *([cache5] * n_sel))


def _pad_to(a, axis, size):
    pad = [(0, 0)] * a.ndim
    pad[axis] = (0, size - a.shape[axis])
    return jnp.pad(a, pad)


def _prep_weights(w_in_a, w_in_b, b_gate, w_kv, cmp_pe, cmp_w1, cmp_w2, w_mem_kv, w_out, w_up, conv_w, conv_b,
                  w_down, hgrn_lb):
    depth = w_out.shape[0]
    n_b = w_in_b.shape[0]
    n_gate = N_MIX_HEADS * N_BRANCH
    per_kv = GROUP * N_BRANCH
    half = CMP_STRIDE * HEAD_DIM
    w1 = cmp_w1.astype(BF16)
    lb = jnp.cumsum(jax.nn.softmax(hgrn_lb.astype(F32), axis=0), axis=0)
    lb = lb - lb[0]
    layers = lambda f, n=depth: [f(l) for l in range(n)]
    gate_w = lambda j: w_in_b[j][:, MIX_DIM:MIX_DIM + n_gate].reshape(D_MODEL, N_KV, per_kv)
    return dict(
        in_a=cast_bf16(w_in_a),
        in_b=layers(lambda j: jnp.concatenate(
            [w_in_b[j][:, :MIX_DIM], w_in_b[j][:, MIX_DIM + n_gate:],
             _pad_to(gate_w(j), 2, HEAD_DIM).reshape(D_MODEL, N_KV * HEAD_DIM)], axis=-1).astype(BF16), n_b),
        b_in_b=layers(lambda j: jnp.concatenate(
            [jnp.zeros((MIX_DIM + MEM_DIM,), F32),
             _pad_to(b_gate[j].reshape(N_KV, per_kv), 1, HEAD_DIM).reshape(N_KV * HEAD_DIM)]), n_b),
        kv=cast_bf16(w_kv[None]),
        mem_kv=cast_bf16(w_mem_kv),
        out=cast_bf16(w_out),
        up=cast_split_halves(w_up, D_FF_PAD),
        conv_w=layers(lambda l: _pad_to(_pad_to(conv_w[l], 1, D_FF_PAD), 0, 8)),
        conv_b=layers(lambda l: _pad_to(conv_b[l], 0, D_FF_PAD).reshape(1, D_FF_PAD)),
        down=cast_pad_rows(w_down, D_FF_PAD),
        cmp_w1=w1,
        cmp_w1ab=jnp.concatenate([w1[:, :half], w1[:, half:]], axis=2),
        cmp_w2=cmp_w2.astype(BF16),
        cmp_pe=cmp_pe.reshape(2, 1, 2 * half),
        log_lb=jnp.log(lb),
        log1m_lb=jnp.log1p(-lb),
    )


def kernel(x_prompt, x_sample, mem_prompt, cache_nsa_kv, page_table, cache_win_kv, state_hgrn, state_conv,
           cache_mem_kv, norm_mix, norm_ffn, norm_mem, norm_kv, norm_final, w_in_a, hgrn_lb, hgrn_onorm,
           w_in_b, b_gate, w_kv, cmp_pe, cmp_w1, cmp_w2, w_mem_kv, w_out, w_up, conv_w, conv_b, w_down):
    B, L, _ = x_prompt.shape
    Bs = x_sample.shape[0]
    depth = w_out.shape[0]
    n_a = w_in_a.shape[0]
    n_mem = mem_prompt.shape[1]
    n_pool, page, _, _, _ = cache_nsa_kv.shape
    n_pages = page_table.shape[1]
    past_len = n_pages * page
    w_buf = cache_win_kv.shape[1]
    assert x_sample.shape[1] == 1 and page == PAGE_SIZE and L % 128 == 0 and L // SLC_BLOCK <= HEAD_DIM
    W = _prep_weights(w_in_a, w_in_b, b_gate, w_kv, cmp_pe, cmp_w1, cmp_w2, w_mem_kv, w_out, w_up, conv_w,
                      conv_b, w_down, hgrn_lb)
    pe_h = pe_proj(W["cmp_pe"], W["cmp_w1"])
    onorm = hgrn_onorm.reshape(n_a, 1, MIX_DIM)
    log_lb = W["log_lb"].reshape(n_a, 1, MIX_DIM)
    log1m_lb = W["log1m_lb"].reshape(n_a, 1, MIX_DIM)

    def ffn(x, l, **kw):
        return conv_ffn(x, norm_ffn[l], W["up"], W["conv_w"][l], W["conv_b"][l], W["down"], l, **kw)

    M = B * L
    mem_flat = mem_prompt.reshape(B * n_mem, D_MODEL)
    mem_kv_both = [rms_matmul_heads(mem_flat, norm_mem[l], W["mem_kv"], (l,), 2 * MEM_DIM) for l in range(depth)]
    mem_kv_p = [by_head.reshape(B, n_mem * 2 * N_MEM_HEADS, HEAD_DIM) for _, by_head in mem_kv_both]
    mem_cache = cache_mem_kv.reshape(depth, Bs, n_mem * 2 * N_MEM_HEADS, HEAD_DIM)
    x = x_prompt.reshape(M, D_MODEL)
    hgrn_p, conv_p = [], []
    for l in range(depth):
        if l < n_a:
            z3 = rms_matmul(x, norm_mix[l], W["in_a"], lead=(l,)).reshape(B, L, -1)
            s0 = jnp.zeros((B, N_MIX_HEADS, HEAD_DIM, HEAD_DIM), F32)
            o_mix, s_new = hgrn_prompt(z3, log_lb[l], log1m_lb[l], onorm[l], s0)
            hgrn_p.append(s_new)
            o_mem = mem_attention(z3, 4 * MIX_DIM // MEM_DIM, mem_kv_p[l])
        else:
            j = l - n_a
            if j == 0:
                n_row4 = 4 * N_KV * HEAD_DIM
                rows_p, rows_by_head = rms_matmul_heads(x, norm_kv, W["kv"], (0,), n_row4)
                rows3 = rows_p.reshape(B, L, -1)
                cmp_p = cmp_mlp(rows3, W["cmp_w1"], W["cmp_w2"], pe_h)
            zq3 = rms_matmul(x, norm_mix[l], W["in_b"][j], W["b_in_b"][j]).reshape(B, L, -1)
            o_mix = nsa_prompt_t(zq3, cmp_p[0], cmp_p[1], rows3)
            o_mem = mem_attention(zq3, MIX_DIM // MEM_DIM, mem_kv_p[l])
        x = out_proj(o_mix.reshape(M, MIX_DIM), o_mem.reshape(M, MEM_DIM), W["out"], l, x)
        x, a_tail = ffn(x, l, seq_len=L)
        conv_p.append(a_tail.reshape(B, -1, 8, D_FF_PAD)[:, -1, 8 - (CONV_W - 1):, :D_FF])
    y_prompt = rmsnorm_rows(x, norm_final).reshape(B, L, D_MODEL)
    w_keep = min(WINDOW, L)
    nsa_rows_prompt = rows_by_head.reshape(B, L, 4, N_KV, HEAD_DIM)
    win_prompt = rows3[:, L - w_keep:, n_row4:].reshape(B, w_keep, 2, N_KV, HEAD_DIM)
    mem_kv_prompt = jnp.stack([by_head for _, by_head in mem_kv_both]).reshape(
        depth, B, n_mem, 2, N_MEM_HEADS, HEAD_DIM)

    Ms = 16
    pad_rows = lambda a: _pad_to(a, 0, Ms)
    xs = pad_rows(x_sample.reshape(Bs, D_MODEL))
    hgrn_s, conv_s = [], []
    q_pos = past_len
    t_pad = -(-(past_len + 1) // SLC_BLOCK) * SLC_BLOCK
    n_blk = t_pad // SLC_BLOCK
    nc = t_pad // CMP_STRIDE - 1
    n_sel = min(N_SELECT, n_blk)
    for l in range(depth):
        if l < n_a:
            z = rms_matmul(xs, norm_mix[l], W["in_a"], lead=(l,))
            z3 = z[:Bs].reshape(Bs, 1, -1)
            o_mix, s_new = hgrn_step(z3, log_lb[l], log1m_lb[l], onorm[l], state_hgrn[l])
            hgrn_s.append(s_new)
            o_mix = o_mix.reshape(Bs, MIX_DIM)
            o_mem = mem_attention(z3, 4 * MIX_DIM // MEM_DIM, mem_cache, lead=(l,))
        else:
            j = l - n_a
            if j == 0:
                rows_s = rms_matmul(xs, norm_kv, W["kv"], lead=(0,))[:Bs]
                new_cmp = rows_s[:, 0:1024].reshape(Bs, 8, HEAD_DIM)
                new_slc = rows_s[:, 1024:2048].reshape(Bs, 8, HEAD_DIM)
                new_win = rows_s[:, 2048:3072].reshape(Bs, 8, HEAD_DIM)
                cache4 = cache_nsa_kv.reshape(n_pool, PAGE_SIZE, 4 * N_KV, HEAD_DIM)
                cache5 = cache_nsa_kv.reshape(n_pool, PAGE_SIZE // SLC_BLOCK, SLC_BLOCK, 4 * N_KV, HEAD_DIM)
                win2 = cache_win_kv.reshape(Bs, w_buf * 2 * N_KV, HEAD_DIM)
                k_cmp_s, v_cmp_s = dec_cmp_tokens(cache4, page_table, new_cmp, W["cmp_w1ab"], W["cmp_w2"], pe_h)
                assert k_cmp_s.shape[2] > nc + 1
                ratio = SLC_BLOCK // CMP_STRIDE
                c_of_row = jnp.arange(k_cmp_s.shape[2])[:, None] - 1
                j_of_col = jnp.arange(-(-n_blk // HEAD_DIM) * HEAD_DIM)[None, :]
                member = ((c_of_row >= ratio * j_of_col - 1) & (c_of_row <= ratio * j_of_col + ratio - 1)).astype(BF16)
            zq = rms_matmul(xs, norm_mix[l], W["in_b"][j], W["b_in_b"][j])
            gt = zq[:Bs, MIX_DIM + MEM_DIM:]
            z3 = zq[:Bs].reshape(Bs, 1, -1)
            per_head = lambda a: _pad_to(a.reshape(Bs, N_KV, GROUP, -1), 2, 8)
            q8 = per_head(zq[:Bs, :MIX_DIM])
            o_cmp, idx = dec_select(q8, k_cmp_s, v_cmp_s, member, q_pos=q_pos, nc=nc, n_blk=n_blk, n_sel=n_sel)
            idx_flat = idx[:, :N_KV, :n_sel].reshape(-1)
            gate9 = gt.reshape(Bs, N_KV, HEAD_DIM)[:, :, :GROUP * N_BRANCH]
            o_all = dec_attend(idx_flat, page_table, q8, _pad_to(per_head(gate9), 3, HEAD_DIM), o_cmp,
                               new_slc, win2, new_win, cache5, q_pos=q_pos, past_len=past_len, n_sel=n_sel)
            o_mix = o_all[:, :, :GROUP].reshape(Bs, MIX_DIM).astype(BF16)
            o_mem = mem_attention(z3, MIX_DIM // MEM_DIM, mem_cache, lead=(l,))
        xs = out_proj(pad_rows(o_mix), pad_rows(o_mem.reshape(Bs, MEM_DIM)), W["out"], l, xs)
        prev = (pad_rows(_pad_to(state_conv[l][:, 1], 1, D_FF_PAD)), pad_rows(_pad_to(state_conv[l][:, 0], 1, D_FF_PAD)))
        xs, a_new = ffn(xs, l, prev=prev)
        conv_s.append(jnp.stack([state_conv[l][:, 1], a_new[0, :Bs, :D_FF]], axis=1))
    y_sample = rmsnorm_rows(xs, norm_final)[:Bs].reshape(Bs, 1, D_MODEL)
    nsa_rows_sample = rows_s[:, :n_row4].reshape(Bs, 1, 4, N_KV, HEAD_DIM)
    win_new = rows_s[:, n_row4:].reshape(Bs, 1, 2, N_KV, HEAD_DIM).astype(cache_win_kv.dtype)
    win_sample = jnp.concatenate([cache_win_kv, win_new], axis=1)[:, 1:]

    return (y_prompt, y_sample, nsa_rows_prompt, nsa_rows_sample, win_prompt, win_sample,
            jnp.stack(hgrn_p), jnp.stack(hgrn_s), jnp.stack(conv_p), jnp.stack(conv_s), mem_kv_prompt)
```

```python
import functools

import jax
import jax.numpy as jnp
from jax import lax
from jax.experimental import pallas as pl
from jax.experimental.pallas import tpu as pltpu

F32 = jnp.float32
BF16 = jnp.bfloat16

D_MODEL = 2048
HEAD_DIM = 128
MIX_DIM = 1536
N_MIX_HEADS = 12
MEM_DIM = 512
N_MEM_HEADS = 4
N_KV = 4
GROUP = 3
N_BRANCH = 3
CMP_STRIDE = 16
CMP_HID = 256
SLC_BLOCK = 64
N_SELECT = 16
WINDOW = 512
PAGE_SIZE = 128
D_FF = 5504
D_FF_PAD = 5632
CONV_W = 3
RMS_EPS = 1e-6
SEL_BONUS = 1e9
NEG_BIG = -1e30
LOG2E = 1.4426950408889634

HGRN_T = 64
HGRN_SUB = 8
HGRN_HEADS_PER_STEP = 6
FFN_TF = 512
V7X_VMEM_BYTES = 64 * 1024 * 1024
VMEM_LIMIT = V7X_VMEM_BYTES - 8 * 1024 * 1024


def _cp(sem, vmem=VMEM_LIMIT):
    return pltpu.CompilerParams(dimension_semantics=sem, vmem_limit_bytes=vmem)


def _dot(a, b):
    return jnp.dot(a, b, preferred_element_type=F32)


def _dot_nt(a, b):
    return lax.dot_general(a, b, (((1,), (1,)), ((), ())), preferred_element_type=F32)


def _dot_tn(a, b):
    return lax.dot_general(a, b, (((0,), (0,)), ((), ())), preferred_element_type=F32)


def _pick(n, cands):
    for c in cands:
        if n % c == 0:
            return c
    return n


def _rms_matmul_kernel(x_ref, g_ref, w_ref, b_ref, o_ref, xn_ref):
    @pl.when(pl.program_id(1) == 0)
    def _():
        x = x_ref[...]
        y = x * lax.rsqrt(jnp.mean(x * x, axis=-1, keepdims=True) + RMS_EPS)
        xn_ref[...] = (y * g_ref[...]).astype(BF16)

    o_ref[...] = _dot(xn_ref[...], w_ref[...]) + b_ref[...]


def _layer_spec(lead, block, imap):
    return pl.BlockSpec((None,) * len(lead) + tuple(block), lambda *a: tuple(lead) + tuple(imap(*a)))


def rms_matmul(x, g, w, bias=None, lead=()):
    M, K = x.shape
    N = w.shape[-1]
    tm = _pick(M, (1024, 512, 256, 128))
    tn = _pick(N, (512, 256, 128))
    if bias is None:
        bias = jnp.zeros((N,), F32)
    return pl.pallas_call(
        _rms_matmul_kernel,
        grid=(M // tm, N // tn),
        in_specs=[
            pl.BlockSpec((tm, K), lambda i, j: (i, 0)),
            pl.BlockSpec((1, K), lambda i, j: (0, 0)),
            _layer_spec(lead, (K, tn), lambda i, j: (0, j)),
            pl.BlockSpec((1, tn), lambda i, j: (0, j)),
        ],
        out_specs=pl.BlockSpec((tm, tn), lambda i, j: (i, j)),
        out_shape=jax.ShapeDtypeStruct((M, N), F32),
        scratch_shapes=[pltpu.VMEM((tm, K), BF16)],
        compiler_params=_cp(("parallel", "arbitrary")),
        name="rms_matmul",
    )(x, g.reshape(1, K), w, bias.reshape(1, N))


def _rms_matmul_heads_kernel(x_ref, g_ref, w_ref, o_ref, oh_ref, xn_ref, *, n_head_steps):
    j = pl.program_id(1)

    @pl.when(j == 0)
    def _():
        x = x_ref[...]
        y = x * lax.rsqrt(jnp.mean(x * x, axis=-1, keepdims=True) + RMS_EPS)
        xn_ref[...] = (y * g_ref[...]).astype(BF16)

    y = _dot(xn_ref[...], w_ref[...])
    o_ref[...] = y

    @pl.when(j < n_head_steps)
    def _():
        oh_ref[...] = y.reshape(oh_ref.shape)


def rms_matmul_heads(x, g, w, lead, head_cols):
    M, K = x.shape
    N = w.shape[-1]
    tm = _pick(M, (512, 256, 128))
    tn = 8 * HEAD_DIM
    assert N % tn == 0 and head_cols % tn == 0
    n_head_steps = head_cols // tn
    return pl.pallas_call(
        functools.partial(_rms_matmul_heads_kernel, n_head_steps=n_head_steps),
        grid=(M // tm, N // tn),
        in_specs=[
            pl.BlockSpec((tm, K), lambda i, j: (i, 0)),
            pl.BlockSpec((1, K), lambda i, j: (0, 0)),
            _layer_spec(lead, (K, tn), lambda i, j: (0, j)),
        ],
        out_specs=[
            pl.BlockSpec((tm, tn), lambda i, j: (i, j)),
            pl.BlockSpec((tm, 8, HEAD_DIM), lambda i, j: (i, jnp.minimum(j, n_head_steps - 1), 0)),
        ],
        out_shape=[
            jax.ShapeDtypeStruct((M, N), F32),
            jax.ShapeDtypeStruct((M, head_cols // HEAD_DIM, HEAD_DIM), F32),
        ],
        scratch_shapes=[pltpu.VMEM((tm, K), BF16)],
        compiler_params=_cp(("parallel", "arbitrary")),
        name="rms_matmul_heads",
    )(x, g.reshape(1, K), w)


def _rmsnorm_kernel(x_ref, g_ref, o_ref):
    x = x_ref[...]
    y = x * lax.rsqrt(jnp.mean(x * x, axis=-1, keepdims=True) + RMS_EPS)
    o_ref[...] = y * g_ref[...]


def rmsnorm_rows(x, g):
    M, K = x.shape
    tm = _pick(M, (512, 256, 128))
    return pl.pallas_call(
        _rmsnorm_kernel,
        grid=(M // tm,),
        in_specs=[pl.BlockSpec((tm, K), lambda i: (i, 0)), pl.BlockSpec((1, K), lambda i: (0, 0))],
        out_specs=pl.BlockSpec((tm, K), lambda i: (i, 0)),
        out_shape=jax.ShapeDtypeStruct((M, K), F32),
        compiler_params=_cp(("parallel",)),
        name="final_rmsnorm",
    )(x, g.reshape(1, K))


def _cast_kernel(x_ref, o_ref):
    o_ref[...] = x_ref[...].astype(o_ref.dtype)


def cast_bf16(w):
    G, R, C = w.shape
    tr = _pick(R, (256, 128))
    return pl.pallas_call(
        _cast_kernel,
        grid=(G, R // tr),
        in_specs=[pl.BlockSpec((1, tr, C), lambda g, r: (g, r, 0))],
        out_specs=pl.BlockSpec((1, tr, C), lambda g, r: (g, r, 0)),
        out_shape=jax.ShapeDtypeStruct((G, R, C), BF16),
        compiler_params=_cp(("parallel", "parallel")),
        name="cast_bf16",
    )(w)


def _cast_halves_kernel(x_ref, o_ref):
    c = x_ref.shape[2]
    o_ref[0, 0, :, 0:c] = x_ref[0].astype(o_ref.dtype)
    o_ref[0, 0, :, c:] = jnp.zeros((o_ref.shape[2], o_ref.shape[3] - c), o_ref.dtype)


def cast_split_halves(w, cols_pad):
    G, R, C2 = w.shape
    C = C2 // 2
    tr = _pick(R, (256, 128))
    return pl.pallas_call(
        _cast_halves_kernel,
        grid=(G, 2, R // tr),
        in_specs=[pl.BlockSpec((1, tr, C), lambda g, h, r: (g, r, h))],
        out_specs=pl.BlockSpec((1, 1, tr, cols_pad), lambda g, h, r: (g, h, r, 0)),
        out_shape=jax.ShapeDtypeStruct((G, 2, R, cols_pad), BF16),
        compiler_params=_cp(("parallel", "parallel", "parallel")),
        name="cast_split_halves",
    )(w)


def _cast_pad_rows_kernel(x_ref, o_ref):
    r = x_ref.shape[1]
    o_ref[0, 0:r, :] = x_ref[0].astype(o_ref.dtype)
    o_ref[0, r:, :] = jnp.zeros((o_ref.shape[1] - r, o_ref.shape[2]), o_ref.dtype)


def cast_pad_rows(w, rows_pad):
    G, R, C = w.shape
    tc = _pick(C, (256, 128))
    assert R % 16 == 0 and rows_pad % 16 == 0
    return pl.pallas_call(
        _cast_pad_rows_kernel,
        grid=(G, C // tc),
        in_specs=[pl.BlockSpec((1, R, tc), lambda g, c: (g, 0, c))],
        out_specs=pl.BlockSpec((1, rows_pad, tc), lambda g, c: (g, 0, c)),
        out_shape=jax.ShapeDtypeStruct((G, rows_pad, C), BF16),
        compiler_params=_cp(("parallel", "parallel")),
        name="cast_pad_rows",
    )(w)


def _outproj_kernel(om_ref, oe_ref, w1_ref, w2_ref, x_ref, y_ref):
    y_ref[...] = x_ref[...] + _dot(om_ref[...], w1_ref[...]) + _dot(oe_ref[...], w2_ref[...])


def out_proj(o_mix, o_mem, w_out, layer, x):
    M = x.shape[0]
    tm = _pick(M, (1024, 512, 256, 128))
    tn = 1024
    assert MIX_DIM % MEM_DIM == 0
    return pl.pallas_call(
        _outproj_kernel,
        grid=(M // tm, D_MODEL // tn),
        in_specs=[
            pl.BlockSpec((tm, MIX_DIM), lambda i, j: (i, 0)),
            pl.BlockSpec((tm, MEM_DIM), lambda i, j: (i, 0)),
            _layer_spec((layer,), (MIX_DIM, tn), lambda i, j: (0, j)),
            _layer_spec((layer,), (MEM_DIM, tn), lambda i, j: (MIX_DIM // MEM_DIM, j)),
            pl.BlockSpec((tm, tn), lambda i, j: (i, j)),
        ],
        out_specs=pl.BlockSpec((tm, tn), lambda i, j: (i, j)),
        out_shape=jax.ShapeDtypeStruct((M, D_MODEL), F32),
        compiler_params=_cp(("parallel", "parallel")),
        name="out_proj",
    )(o_mix, o_mem, w_out, w_out, x)


def _mem_attn_kernel(q_ref, kv_ref, o_ref):
    scale = HEAD_DIM ** -0.5
    slots = 2 * N_MEM_HEADS
    n_mem = kv_ref.shape[1] // slots
    for h in range(N_MEM_HEADS):
        q = q_ref[0, :, h * HEAD_DIM:(h + 1) * HEAD_DIM].astype(BF16)
        k = kv_ref[0, pl.ds(h, n_mem, stride=slots), :].astype(BF16)
        v = kv_ref[0, pl.ds(N_MEM_HEADS + h, n_mem, stride=slots), :].astype(BF16)
        s = _dot_nt(q, k) * scale
        m = jnp.max(s, axis=-1, keepdims=True)
        p = jnp.exp(s - m)
        p = p / jnp.sum(p, axis=-1, keepdims=True)
        o_ref[0, :, h * HEAD_DIM:(h + 1) * HEAD_DIM] = _dot(p.astype(BF16), v).astype(o_ref.dtype)


def mem_attention(z3, col_block, kv, lead=()):
    B, L, _ = z3.shape
    rows = kv.shape[-2]
    tm = _pick(L, (1024, 512, 256, 128))
    return pl.pallas_call(
        _mem_attn_kernel,
        grid=(B, L // tm),
        in_specs=[
            pl.BlockSpec((1, tm, MEM_DIM), lambda b, i: (b, i, col_block)),
            _layer_spec(lead, (1, rows, HEAD_DIM), lambda b, i: (b, 0, 0)),
        ],
        out_specs=pl.BlockSpec((1, tm, MEM_DIM), lambda b, i: (b, i, 0)),
        out_shape=jax.ShapeDtypeStruct((B, L, MEM_DIM), BF16),
        compiler_params=_cp(("parallel", "parallel")),
        name="mem_attention",
    )(z3, kv)


def _ffn_kernel(*refs, blocks_per_seq, decode):
    if decode:
        (x_ref, g_ref, wa_ref, wu_ref, cw_ref, cb_ref, wd_ref, p1_ref, p2_ref,
         y_ref, at_ref, xn_ref) = refs
    else:
        (x_ref, g_ref, wa_ref, wu_ref, cw_ref, cb_ref, wd_ref,
         y_ref, at_ref, xn_ref, carry_ref) = refs
    i = pl.program_id(0)
    j = pl.program_id(1)

    @pl.when(j == 0)
    def _():
        x = x_ref[...]
        y = x * lax.rsqrt(jnp.mean(x * x, axis=-1, keepdims=True) + RMS_EPS)
        xn_ref[...] = (y * g_ref[...]).astype(BF16)
        y_ref[...] = x

    if not decode:
        @pl.when((i % blocks_per_seq) == 0)
        def _():
            carry_ref[j] = jnp.zeros(carry_ref.shape[1:], F32)

    xn = xn_ref[...]
    tm = xn.shape[0]
    n_tail = at_ref.shape[1]
    half = wa_ref.shape[1] // 2
    down = None
    for cs in (slice(0, half), slice(half, 2 * half)):
        a = _dot(xn, wa_ref[:, cs])
        u = _dot(xn, wu_ref[:, cs])
        if decode:
            a1 = p1_ref[:, cs]
            a2 = p2_ref[:, cs]
        else:
            prev = carry_ref[j, :, cs]
            row = lax.broadcasted_iota(jnp.int32, a.shape, 0)
            a1 = jnp.where(row == 0, prev[7:8], pltpu.roll(a, 1, 0))
            a2 = jnp.where(row == 0, prev[6:7], jnp.where(row == 1, prev[7:8], pltpu.roll(a, 2, 0)))
            carry_ref[j, :, cs] = a[tm - 8:tm]
        at_ref[0, :, cs] = a[tm - n_tail:tm]
        c = cb_ref[:, cs] + cw_ref[2:3, cs] * a
        c = c + cw_ref[0:1, cs] * a2
        c = c + cw_ref[1:2, cs] * a1
        h = (jax.nn.gelu(c) * u).astype(BF16)
        d = _dot(h, wd_ref[cs, :])
        down = d if down is None else down + d
    y_ref[...] += down


def conv_ffn(x, g, w_up, cw, cb, w_down, layer, *, seq_len=None, prev=None):
    M = x.shape[0]
    decode = prev is not None
    tm = M if decode else _pick(seq_len, (1024, 512, 256, 128))
    n_tail = tm if decode else 8
    nf = D_FF_PAD // FFN_TF
    in_specs = [
        pl.BlockSpec((tm, D_MODEL), lambda i, j: (i, 0)),
        pl.BlockSpec((1, D_MODEL), lambda i, j: (0, 0)),
        _layer_spec((layer, 0), (D_MODEL, FFN_TF), lambda i, j: (0, j)),
        _layer_spec((layer, 1), (D_MODEL, FFN_TF), lambda i, j: (0, j)),
        pl.BlockSpec((8, FFN_TF), lambda i, j: (0, j)),
        pl.BlockSpec((1, FFN_TF), lambda i, j: (0, j)),
        _layer_spec((layer,), (FFN_TF, D_MODEL), lambda i, j: (j, 0)),
    ]
    args = [x, g.reshape(1, D_MODEL), w_up, w_up, cw, cb, w_down]
    scratch = [pltpu.VMEM((tm, D_MODEL), BF16)]
    if decode:
        in_specs += [pl.BlockSpec((tm, FFN_TF), lambda i, j: (i, j))] * 2
        args += list(prev)
    else:
        scratch.append(pltpu.VMEM((nf, 8, FFN_TF), F32))
    return pl.pallas_call(
        functools.partial(_ffn_kernel, blocks_per_seq=(1 if decode else seq_len // tm), decode=decode),
        grid=(M // tm, nf),
        in_specs=in_specs,
        out_specs=[
            pl.BlockSpec((tm, D_MODEL), lambda i, j: (i, 0)),
            pl.BlockSpec((1, n_tail, FFN_TF), lambda i, j: (i, 0, j)),
        ],
        out_shape=[
            jax.ShapeDtypeStruct((M, D_MODEL), F32),
            jax.ShapeDtypeStruct((M // tm, n_tail, D_FF_PAD), F32),
        ],
        scratch_shapes=scratch,
        compiler_params=_cp(("arbitrary", "arbitrary")),
        name="conv_ffn",
    )(*args)


def _hgrn_gates(q, f, log_lb, log1m_lb):
    qs = jax.nn.silu(q)
    log_f = jnp.logaddexp(log_lb, log1m_lb + jax.nn.log_sigmoid(f))
    k = 1.0 - jnp.exp(log_f)
    return qs, k, log_f


def _hgrn_out(o, g, onorm):
    y = o * lax.rsqrt(jnp.mean(o * o, axis=-1, keepdims=True) + RMS_EPS)
    return (y * onorm) * jax.nn.silu(g)


def _hgrn_kernel(q_ref, f_ref, i_ref, g_ref, llb_ref, l1m_ref, on_ref, s0_ref, o_ref, s_ref, st_ref,
                 *, n_chunks):
    c = pl.program_id(2)
    T = HGRN_T
    n_heads = st_ref.shape[0]

    @pl.when(c == 0)
    def _():
        for h in range(n_heads):
            st_ref[h] = s0_ref[0, h].T

    assert HGRN_SUB == 8
    rl1 = lax.broadcasted_iota(jnp.int32, (T, 1), 0) % HGRN_SUB
    t_i = lax.broadcasted_iota(jnp.int32, (T, T), 0)
    s_i = lax.broadcasted_iota(jnp.int32, (T, T), 1)
    tril = jnp.where(s_i <= t_i, 1.0, 0.0).astype(BF16)
    pair_masks = []
    size = 2 * HGRN_SUB
    while size <= T:
        same = (t_i & -size) == (s_i & -size)
        pair_masks.append((size, same & ((t_i & (size - 1)) >= size // 2) & ((s_i & (size - 1)) < size // 2)))
        size *= 2

    def shift_rows(x, d):
        return pltpu.roll(x.reshape(T // HGRN_SUB, HGRN_SUB, HEAD_DIM), d, 1).reshape(T, HEAD_DIM)

    def one_head(q, k, b, v, h, sl):
        hs = slice(h * HEAD_DIM, (h + 1) * HEAD_DIM)
        st = st_ref[h]
        o = _dot_nt((q * jnp.exp2(b)).astype(BF16), st.astype(BF16))
        o = o + jnp.sum(q * k, axis=-1, keepdims=True) * v
        for d in range(1, HGRN_SUB):
            w = jnp.exp2(b - shift_rows(b, d))
            a = jnp.sum(q * shift_rows(k, d) * w, axis=-1, keepdims=True)
            o = o + jnp.where(rl1 >= d, a, 0.0) * shift_rows(v, d)
        att = jnp.zeros((T, T), F32)
        for size, keep in pair_masks:
            half = size // 2
            refs = [jnp.broadcast_to(b[j * size + half - 1:j * size + half], (size, HEAD_DIM))
                    for j in range(T // size)]
            r = jnp.concatenate(refs, axis=0) if len(refs) > 1 else refs[0]
            qt = q * jnp.exp2(jnp.minimum(b - r, 0.0))
            kt = k * jnp.exp2(jnp.minimum(r - b, 0.0))
            att = att + jnp.where(keep, _dot_nt(qt.astype(BF16), kt.astype(BF16)), 0.0)
        o = o + _dot(att.astype(BF16), v.astype(BF16))
        bl = b[T - 1:T]
        kt = k * jnp.exp2(bl - b)
        st_ref[h] = st * jnp.exp2(bl) + _dot_tn(v.astype(BF16), kt.astype(BF16))
        o_ref[0, sl, hs] = _hgrn_out(o, g_ref[0, sl, hs], on_ref[:, hs]).astype(o_ref.dtype)

    def chunk(ci, carry):
        sl = pl.ds(pl.multiple_of(ci * T, T), T)
        q, k, lf = _hgrn_gates(q_ref[0, sl, :], f_ref[0, sl, :], llb_ref[...], l1m_ref[...])
        b = sum(_dot(tril, part) for part in _split3(lf)) * LOG2E
        for h in range(n_heads):
            hs = slice(h * HEAD_DIM, (h + 1) * HEAD_DIM)
            one_head(q[:, hs], k[:, hs], b[:, hs], i_ref[0, sl, hs], h, sl)
        return carry

    lax.fori_loop(0, n_chunks, chunk, 0)

    @pl.when(c == pl.num_programs(2) - 1)
    def _():
        for h in range(n_heads):
            s_ref[0, h] = st_ref[h].T


def hgrn_prompt(z3, log_lb, log1m_lb, onorm, s0):
    B, L, _ = z3.shape
    H = N_MIX_HEADS
    hb = HGRN_HEADS_PER_STEP
    wb = hb * HEAD_DIM
    tc = _pick(L, (512, 256, 128, 64))
    zspec = lambda k: pl.BlockSpec((1, tc, wb), lambda b, h, c: (b, c, k * (H // hb) + h))
    vspec = pl.BlockSpec((1, wb), lambda b, h, c: (0, h))
    sspec = pl.BlockSpec((1, hb, HEAD_DIM, HEAD_DIM), lambda b, h, c: (b, h, 0, 0))
    return pl.pallas_call(
        functools.partial(_hgrn_kernel, n_chunks=tc // HGRN_T),
        grid=(B, H // hb, L // tc),
        in_specs=[zspec(0), zspec(1), zspec(2), zspec(3), vspec, vspec, vspec, sspec],
        out_specs=[pl.BlockSpec((1, tc, wb), lambda b, h, c: (b, c, h)), sspec],
        out_shape=[
            jax.ShapeDtypeStruct((B, L, MIX_DIM), BF16),
            jax.ShapeDtypeStruct((B, H, HEAD_DIM, HEAD_DIM), F32),
        ],
        scratch_shapes=[pltpu.VMEM((hb, HEAD_DIM, HEAD_DIM), F32)],
        compiler_params=_cp(("parallel", "parallel", "arbitrary")),
        name="hgrn_chunked",
    )(z3, z3, z3, z3, log_lb, log1m_lb, onorm, s0)


def _hgrn_step_kernel(z_ref, llb_ref, l1m_ref, on_ref, s0_ref, o_ref, s_ref):
    for h in range(N_MIX_HEADS):
        col = lambda k: slice((k * N_MIX_HEADS + h) * HEAD_DIM, (k * N_MIX_HEADS + h + 1) * HEAD_DIM)
        hs = slice(h * HEAD_DIM, (h + 1) * HEAD_DIM)
        q, k, lf = _hgrn_gates(z_ref[0, :, col(0)], z_ref[0, :, col(1)], llb_ref[:, hs], l1m_ref[:, hs])
        v = z_ref[0, :, col(2)]
        g = z_ref[0, :, col(3)]
        rows = jnp.concatenate([q, k, jnp.exp(lf), jnp.zeros((5, HEAD_DIM), F32)], axis=0)
        cols = rows.T
        s_new = cols[:, 2:3] * s0_ref[0, h] + cols[:, 1:2] * v
        s_ref[0, h] = s_new
        o = jnp.sum(cols[:, 0:1] * s_new, axis=0, keepdims=True)
        o_ref[0, :, hs] = _hgrn_out(o, g, on_ref[:, hs]).astype(o_ref.dtype)


def hgrn_step(z3, log_lb, log1m_lb, onorm, s0):
    B = z3.shape[0]
    W = z3.shape[2]
    H = N_MIX_HEADS
    vspec = pl.BlockSpec((1, MIX_DIM), lambda b: (0, 0))
    sspec = pl.BlockSpec((1, H, HEAD_DIM, HEAD_DIM), lambda b: (b, 0, 0, 0))
    return pl.pallas_call(
        _hgrn_step_kernel,
        grid=(B,),
        in_specs=[pl.BlockSpec((1, 1, W), lambda b: (b, 0, 0)), vspec, vspec, vspec, sspec],
        out_specs=[pl.BlockSpec((1, 1, MIX_DIM), lambda b: (b, 0, 0)), sspec],
        out_shape=[
            jax.ShapeDtypeStruct((B, 1, MIX_DIM), BF16),
            jax.ShapeDtypeStruct((B, H, HEAD_DIM, HEAD_DIM), F32),
        ],
        compiler_params=_cp(("parallel",)),
        name="hgrn_step",
    )(z3, log_lb, log1m_lb, onorm, s0)


def _pe_proj_kernel(pe_ref, w_ref, o_ref):
    pe = jnp.broadcast_to(pe_ref[0], (8, pe_ref.shape[2])).astype(BF16)
    o_ref[0] = _dot(pe, w_ref[0])


def pe_proj(pe, w1):
    K = pe.shape[2]
    return pl.pallas_call(
        _pe_proj_kernel,
        grid=(2,),
        in_specs=[pl.BlockSpec((1, 1, K), lambda i: (i, 0, 0)), pl.BlockSpec((1, K, CMP_HID), lambda i: (i, 0, 0))],
        out_specs=pl.BlockSpec((1, 8, CMP_HID), lambda i: (i, 0, 0)),
        out_shape=jax.ShapeDtypeStruct((2, 8, CMP_HID), F32),
        compiler_params=_cp(("parallel",)),
        name="cmp_pe_proj",
    )(pe, w1)


def _cmp_mlp_kernel(x_ref, w1_ref, w2_ref, pe_ref, o_ref):
    n_sub = x_ref.shape[1] // CMP_STRIDE
    half = CMP_STRIDE * HEAD_DIM
    x = jnp.concatenate([x_ref[0, pl.ds(p, n_sub, stride=CMP_STRIDE), :] for p in range(CMP_STRIDE)],
                        axis=1).astype(BF16)
    a = _dot(x, w1_ref[0, 0:half])
    bm = _dot(x, w1_ref[0, half:2 * half])
    n = a.shape[0]
    h = a + pltpu.roll(bm, n - 1, 0) + pe_ref[0, 0:1]
    o_ref[0, 0] = _dot(jax.nn.gelu(h).astype(BF16), w2_ref[0])


def cmp_mlp(rows3, w1, w2, pe_h):
    B, L, _ = rows3.shape
    n_sub = L // CMP_STRIDE
    half = CMP_STRIDE * HEAD_DIM
    return pl.pallas_call(
        _cmp_mlp_kernel,
        grid=(2, B, N_KV),
        in_specs=[
            pl.BlockSpec((1, L, HEAD_DIM), lambda k, b, n: (b, 0, k * N_KV + n)),
            pl.BlockSpec((1, 2 * half, CMP_HID), lambda k, b, n: (k, 0, 0)),
            pl.BlockSpec((1, CMP_HID, HEAD_DIM), lambda k, b, n: (k, 0, 0)),
            pl.BlockSpec((1, 8, CMP_HID), lambda k, b, n: (k, 0, 0)),
        ],
        out_specs=pl.BlockSpec((1, 1, n_sub, HEAD_DIM), lambda k, b, n: (k, b * N_KV + n, 0, 0)),
        out_shape=jax.ShapeDtypeStruct((2, B * N_KV, n_sub, HEAD_DIM), F32),
        compiler_params=_cp(("parallel", "parallel", "parallel")),
        name="cmp_mlp",
    )(rows3, w1, w2, pe_h)


def _masked_softmax_rows(s, mask, exp_fn=jnp.exp):
    s = jnp.where(mask, s, -jnp.inf)
    m = jnp.max(s, axis=-1, keepdims=True)
    m = jnp.where(m == -jnp.inf, 0.0, m)
    p = exp_fn(s - m)
    return p / jnp.maximum(jnp.sum(p, axis=-1, keepdims=True), 1e-30)


def _split3(p):
    hi = p.astype(BF16)
    r = p - hi.astype(F32)
    mid = r.astype(BF16)
    lo = (r - mid.astype(F32)).astype(BF16)
    return hi, mid, lo


def _masked_softmax_cols(s, mask, exp_fn):
    s = jnp.where(mask, s, -jnp.inf)
    m = jnp.max(s, axis=0, keepdims=True)
    m = jnp.where(m == -jnp.inf, 0.0, m)
    p = exp_fn(s - m)
    return p / jnp.maximum(jnp.sum(p, axis=0, keepdims=True), 1e-30)


def _nsa_prompt_t_kernel(q_ref, gt_ref, kc_ref, vc_ref, ks_ref, vs_ref, kw_ref, vw_ref, o_ref,
                         kaug_ref, vst_ref, kwb_ref, vwt_ref, vct_ref, score_ref, *, tq, tk, tw, seq_len):
    qi = pl.program_id(2)
    n_blk = seq_len // SLC_BLOCK
    scale = HEAD_DIM ** -0.5 * LOG2E
    cols = GROUP * tq
    nc = kc_ref.shape[1]

    @pl.when(qi == 0)
    def _():
        r = lax.broadcasted_iota(jnp.int32, (seq_len, HEAD_DIM), 0) // SLC_BLOCK
        col = lax.broadcasted_iota(jnp.int32, (seq_len, HEAD_DIM), 1)
        kaug_ref[:, 0:HEAD_DIM] = ks_ref[0].astype(BF16)
        kaug_ref[:, HEAD_DIM:2 * HEAD_DIM] = jnp.where(r == col, 1.0, 0.0).astype(BF16)
        kwb_ref[...] = kw_ref[0].astype(BF16)
        for t in range(seq_len // tk):
            vst_ref[t] = vs_ref[0, t * tk:(t + 1) * tk, :].T.astype(BF16)
        for t in range(seq_len // tw):
            vwt_ref[t] = vw_ref[0, t * tw:(t + 1) * tw, :].T.astype(BF16)
        vct_ref[...] = vc_ref[0].T.astype(BF16)

    q0 = qi * tq
    qs = q_ref[0] * scale
    qb = jnp.concatenate([qs[:, g * HEAD_DIM:(g + 1) * HEAD_DIM] for g in range(GROUP)], axis=0).astype(BF16)

    def qpos(n_keys):
        return q0 + lax.broadcasted_iota(jnp.int32, (n_keys, cols), 1) % tq

    def kidx(n_keys):
        return lax.broadcasted_iota(jnp.int32, (n_keys, cols), 0)

    cend = CMP_STRIDE * kidx(nc) + (2 * CMP_STRIDE - 1)
    p_c = _masked_softmax_cols(_dot_nt(kc_ref[0].astype(BF16), qb), cend <= qpos(nc), jnp.exp2)
    o_cmp = _dot(vct_ref[...], p_c.astype(BF16))

    psum = p_c[:, 0:tq]
    for g in range(1, GROUP):
        psum = psum + p_c[:, g * tq:(g + 1) * tq]
    ji = lax.broadcasted_iota(jnp.int32, (HEAD_DIM, nc), 0)
    ci = lax.broadcasted_iota(jnp.int32, (HEAD_DIM, nc), 1)
    ratio = SLC_BLOCK // CMP_STRIDE
    member = ((ci >= ratio * ji - 1) & (ci <= ratio * ji + ratio - 1)).astype(BF16)
    imp = sum(_dot(member, part) for part in _split3(psum))
    blk = lax.broadcasted_iota(jnp.int32, (HEAD_DIM, tq), 0)
    cur = (q0 + lax.broadcasted_iota(jnp.int32, (HEAD_DIM, tq), 1)) // SLC_BLOCK
    valid = (blk <= cur) & (blk < n_blk)
    forced = valid & ((blk == 0) | (blk == cur) | (blk == cur - 1))
    score = jnp.where(forced, SEL_BONUS, jnp.where(valid, imp, -jnp.inf))
    score = score[0:n_blk]
    score_ref[...] = score
    blk = lax.broadcasted_iota(jnp.int32, (n_blk, tq), 0)
    per_tile = tq // SLC_BLOCK

    def rank_body(it, rank):
        for u in range(per_tile):
            i = it * per_tile + u
            si = score_ref[pl.ds(i, 1), :]
            later = jnp.where(blk > i, 1, 0)
            rank = rank + jnp.where(si > score, 1, jnp.where(si == score, later, 0))
        return rank

    rank = lax.fori_loop(0, qi + 1, rank_body, jnp.zeros((n_blk, tq), jnp.int32))
    bias_t = jnp.where(rank < N_SELECT, 0.0, NEG_BIG)
    if n_blk < HEAD_DIM:
        bias_t = jnp.concatenate([bias_t, jnp.zeros((HEAD_DIM - n_blk, tq), F32)], axis=0)
    bias = bias_t.T.astype(BF16)
    qaug = jnp.concatenate([qb, jnp.concatenate([bias] * GROUP, axis=0)], axis=1)

    qpos_k = qpos(tk)
    krow = kidx(tk)

    def slc_body(kt, carry, causal):
        m, l, acc = carry
        s = _dot_nt(kaug_ref[pl.ds(pl.multiple_of(kt * tk, tk), tk), :], qaug)
        if causal:
            s = jnp.where(kt * tk + krow <= qpos_k, s, -jnp.inf)
        m_new = jnp.maximum(m, jnp.max(s, axis=0, keepdims=True))
        alpha = jnp.exp2(m - m_new)
        p = jnp.exp2(s - m_new)
        l = alpha * l + jnp.sum(p, axis=0, keepdims=True)
        acc = alpha * acc + _dot(vst_ref[kt], p.astype(BF16))
        return m_new, l, acc

    init = (jnp.full((1, cols), -jnp.inf, F32), jnp.zeros((1, cols), F32), jnp.zeros((HEAD_DIM, cols), F32))
    n_full = q0 // tk

    def two_tiles(i, carry):
        return slc_body(2 * i + 1, slc_body(2 * i, carry, False), False)

    carry = lax.fori_loop(0, n_full // 2, two_tiles, init)
    carry = lax.fori_loop(n_full - n_full % 2, n_full, functools.partial(slc_body, causal=False), carry)
    _, l_s, acc_s = slc_body(n_full, carry, True)
    o_slc = acc_s / jnp.maximum(l_s, 1e-30)

    wk = min(WINDOW + tw, seq_len)
    d_rel = (lax.broadcasted_iota(jnp.int32, (wk, GROUP * tw), 1) % tw
             - lax.broadcasted_iota(jnp.int32, (wk, GROUP * tw), 0))
    win_parts = []
    for h in range(tq // tw):
        qh = q0 + h * tw
        k0 = jnp.minimum(jnp.maximum(qh - WINDOW, 0), seq_len - wk)
        qb_h = jnp.concatenate([qb[g * tq + h * tw:g * tq + (h + 1) * tw] for g in range(GROUP)], axis=0)
        d = d_rel + (qh - k0)
        s_w = _dot_nt(kwb_ref[pl.ds(pl.multiple_of(k0, tw), wk), :], qb_h)
        s_w = jnp.where((d >= 0) & (d <= WINDOW), s_w, -jnp.inf)
        p_w = jnp.exp2(s_w - jnp.max(s_w, axis=0, keepdims=True))
        l_w = jnp.sum(p_w, axis=0, keepdims=True)
        p_wb = p_w.astype(BF16)
        t0 = k0 // tw
        o_h = sum(_dot(vwt_ref[t0 + i], p_wb[i * tw:(i + 1) * tw]) for i in range(wk // tw))
        win_parts.append(o_h / jnp.maximum(l_w, 1e-30))
    o_win = jnp.concatenate([win_parts[h][:, g * tw:(g + 1) * tw]
                             for g in range(GROUP) for h in range(tq // tw)], axis=1)

    gates = jax.nn.sigmoid(gt_ref[0]).T
    for g in range(GROUP):
        cs = slice(g * tq, (g + 1) * tq)
        c0 = g * N_BRANCH
        o = (gates[c0:c0 + 1] * o_cmp[:, cs] + gates[c0 + 1:c0 + 2] * o_slc[:, cs]
             + gates[c0 + 2:c0 + 3] * o_win[:, cs])
        o_ref[0, :, g * HEAD_DIM:(g + 1) * HEAD_DIM] = o.T.astype(o_ref.dtype)


def nsa_prompt_t(zq3, k_cmp, v_cmp, rows3):
    B, L, _ = zq3.shape
    gate_block0 = (MIX_DIM + MEM_DIM) // HEAD_DIM
    tq = _pick(L, (512, 256, 128))
    tk = _pick(L, (512, 256, 128))
    tw = min(tq, 256)
    assert WINDOW % tw == 0 and tk % tq == 0 and tq % tw == 0
    n_sub = k_cmp.shape[1]
    qw = GROUP * HEAD_DIM
    rspec = lambda kind: pl.BlockSpec((1, L, HEAD_DIM), lambda b, n, i: (b, 0, kind * N_KV + n))
    cspec = pl.BlockSpec((1, n_sub, HEAD_DIM), lambda b, n, i: (b * N_KV + n, 0, 0))
    return pl.pallas_call(
        functools.partial(_nsa_prompt_t_kernel, tq=tq, tk=tk, tw=tw, seq_len=L),
        grid=(B, N_KV, L // tq),
        in_specs=[
            pl.BlockSpec((1, tq, qw), lambda b, n, i: (b, i, n)),
            pl.BlockSpec((1, tq, HEAD_DIM), lambda b, n, i: (b, i, gate_block0 + n)),
            cspec, cspec, rspec(2), rspec(3), rspec(4), rspec(5),
        ],
        out_specs=pl.BlockSpec((1, tq, qw), lambda b, n, i: (b, i, n)),
        out_shape=jax.ShapeDtypeStruct((B, L, MIX_DIM), BF16),
        scratch_shapes=[
            pltpu.VMEM((L, 2 * HEAD_DIM), BF16),
            pltpu.VMEM((L // tk, HEAD_DIM, tk), BF16),
            pltpu.VMEM((L, HEAD_DIM), BF16),
            pltpu.VMEM((L // tw, HEAD_DIM, tw), BF16),
            pltpu.VMEM((HEAD_DIM, n_sub), BF16),
            pltpu.VMEM((L // SLC_BLOCK, tq), F32),
        ],
        compiler_params=_cp(("parallel", "parallel", "arbitrary")),
        name="nsa_prompt",
    )(zq3, zq3, k_cmp, v_cmp, rows3, rows3, rows3, rows3)


def _dec_cmp_kernel(pt_ref, *refs, n_pg, n_groups):
    del pt_ref
    pg_refs = refs[:n_pg]
    new_ref, w1_ref, w2_ref, pe_ref, kc_ref, vc_ref, carry_ref = refs[n_pg:]
    g = pl.program_id(1)
    sub_pg = PAGE_SIZE // CMP_STRIDE
    n_sub = n_pg * sub_pg
    pos_per_chunk = 4
    kw = pos_per_chunk * HEAD_DIM

    @pl.when(g == 0)
    def _():
        carry_ref[...] = jnp.zeros_like(carry_ref)

    sub_i = lax.broadcasted_iota(jnp.int32, (8, n_sub, kw), 1)
    new_row = jnp.concatenate([new_ref[0], jnp.zeros((8, kw - HEAD_DIM), F32)], axis=1)[:, None, :]
    ys = [None, None]
    for pc in range(CMP_STRIDE // pos_per_chunk):
        x = jnp.concatenate(
            [jnp.concatenate([jnp.swapaxes(pg_refs[r][0, pl.ds(p, sub_pg, stride=CMP_STRIDE), :, :], 0, 1)
                              for r in range(n_pg)], axis=1)
             for p in range(pc * pos_per_chunk, (pc + 1) * pos_per_chunk)], axis=2)
        x = jnp.where(g < n_groups, x, 0.0)
        if pc == 0:
            x = jnp.where((g == n_groups) & (sub_i == 0), new_row, x)
        xb = x.astype(BF16)
        for kind in range(2):
            part = _dot(xb[kind * N_KV:(kind + 1) * N_KV].reshape(N_KV * n_sub, kw),
                        w1_ref[kind, pc * kw:(pc + 1) * kw, :])
            ys[kind] = part if ys[kind] is None else ys[kind] + part

    row = lax.broadcasted_iota(jnp.int32, (n_sub, CMP_HID), 0)
    for kind, out_ref in enumerate((kc_ref, vc_ref)):
        y = ys[kind]
        for n in range(N_KV):
            slot = kind * N_KV + n
            a = y[n * n_sub:(n + 1) * n_sub, 0:CMP_HID]
            bm = y[n * n_sub:(n + 1) * n_sub, CMP_HID:2 * CMP_HID]
            a_prev = jnp.where(row == 0, carry_ref[slot:slot + 1, :], pltpu.roll(a, 1, 0))
            carry_ref[slot:slot + 1, :] = a[n_sub - 1:n_sub]
            h = a_prev + bm + pe_ref[kind, 0:1]
            out_ref[0, n] = _dot(jax.nn.gelu(h).astype(BF16), w2_ref[kind])


def dec_cmp_tokens(cache4, page_table, new_cmp, w1ab, w2, pe_h):
    B, n_pages = page_table.shape
    n_pg = _pick(n_pages, (16, 8, 4, 2))
    assert n_pages % n_pg == 0
    n_groups = n_pages // n_pg
    sub_pg = PAGE_SIZE // CMP_STRIDE
    n_sub = n_pg * sub_pg
    n_steps = n_groups + 1
    while (n_steps * n_sub) % HEAD_DIM:
        n_steps += 1
    half = CMP_STRIDE * HEAD_DIM

    def page_spec(r):
        def imap(b, g, pt):
            return (pt[b * n_pages + jnp.minimum(g * n_pg + r, n_pages - 1)], 0, 0, 0)
        return pl.BlockSpec((1, PAGE_SIZE, 8, HEAD_DIM), imap)

    out_spec = pl.BlockSpec((1, N_KV, n_sub, HEAD_DIM), lambda b, g, pt: (b, 0, g, 0))
    grid_spec = pltpu.PrefetchScalarGridSpec(
        num_scalar_prefetch=1,
        grid=(B, n_steps),
        in_specs=[page_spec(r) for r in range(n_pg)] + [
            pl.BlockSpec((1, 8, HEAD_DIM), lambda b, g, pt: (b, 0, 0)),
            pl.BlockSpec((2, half, 2 * CMP_HID), lambda b, g, pt: (0, 0, 0)),
            pl.BlockSpec((2, CMP_HID, HEAD_DIM), lambda b, g, pt: (0, 0, 0)),
            pl.BlockSpec((2, 8, CMP_HID), lambda b, g, pt: (0, 0, 0)),
        ],
        out_specs=[out_spec, out_spec],
        scratch_shapes=[pltpu.VMEM((8, CMP_HID), F32)],
    )
    tok = jax.ShapeDtypeStruct((B, N_KV, n_steps * n_sub, HEAD_DIM), F32)
    return pl.pallas_call(
        functools.partial(_dec_cmp_kernel, n_pg=n_pg, n_groups=n_groups),
        grid_spec=grid_spec,
        out_shape=[tok, tok],
        compiler_params=_cp(("parallel", "arbitrary")),
        name="dec_cmp_tokens",
    )(page_table.reshape(-1), *([cache4] * n_pg), new_cmp, w1ab, w2, pe_h)


def _dec_select_kernel(q_ref, kc_ref, vc_ref, mem_ref, ocmp_ref, idx_ref, *, q_pos, nc, n_blk, n_sel):
    C = kc_ref.shape[2]
    JB = mem_ref.shape[1]
    scale = HEAD_DIM ** -0.5
    c_i = lax.broadcasted_iota(jnp.int32, (8, C), 1) - 1
    g_i = lax.broadcasted_iota(jnp.int32, (8, C), 0)
    ok = (c_i >= 0) & (c_i < nc) & (CMP_STRIDE * c_i + (2 * CMP_STRIDE - 1) <= q_pos)
    psums = []
    for n in range(N_KV):
        qb = (q_ref[0, n] * scale).astype(BF16)
        p = _masked_softmax_rows(_dot_nt(qb, kc_ref[0, n].astype(BF16)), ok)
        ocmp_ref[0, n] = _dot(p.astype(BF16), vc_ref[0, n].astype(BF16))
        psums.append(jnp.sum(jnp.where(g_i < GROUP, p, 0.0), axis=0, keepdims=True))
    psum = jnp.concatenate(psums + [jnp.zeros((8 - N_KV, C), F32)], axis=0)
    imp = sum(_dot(part, mem_ref[...]) for part in _split3(psum))
    j_i = lax.broadcasted_iota(jnp.int32, (8, JB), 1)
    cur = q_pos // SLC_BLOCK
    valid = (j_i <= cur) & (j_i < n_blk)
    forced = valid & ((j_i == 0) | (j_i == cur) | (j_i == cur - 1))
    score = jnp.where(forced, SEL_BONUS, jnp.where(valid, imp, -jnp.inf))
    j_f = j_i.astype(F32)
    taken = j_i >= n_blk
    lane_o = lax.broadcasted_iota(jnp.int32, (8, HEAD_DIM), 1)
    out = jnp.zeros((8, HEAD_DIM), F32)
    for r in range(n_sel):
        m = jnp.max(jnp.where(taken, -jnp.inf, score), axis=-1, keepdims=True)
        cand = jnp.logical_not(taken) & (score == m)
        idx = jnp.min(jnp.where(cand, j_f, float(JB)), axis=-1, keepdims=True)
        out = jnp.where(lane_o == r, idx, out)
        taken = taken | (j_f == idx)
    idx_ref[0] = out.astype(jnp.int32)


def dec_select(q8, k_cmp, v_cmp, member, *, q_pos, nc, n_blk, n_sel):
    B, _, C, _ = k_cmp.shape
    JB = member.shape[1]
    head_spec = pl.BlockSpec((1, N_KV, 8, HEAD_DIM), lambda b: (b, 0, 0, 0))
    tok_spec = pl.BlockSpec((1, N_KV, C, HEAD_DIM), lambda b: (b, 0, 0, 0))
    return pl.pallas_call(
        functools.partial(_dec_select_kernel, q_pos=q_pos, nc=nc, n_blk=n_blk, n_sel=n_sel),
        grid=(B,),
        in_specs=[head_spec, tok_spec, tok_spec, pl.BlockSpec((C, JB), lambda b: (0, 0))],
        out_specs=[head_spec, pl.BlockSpec((1, 8, HEAD_DIM), lambda b: (b, 0, 0))],
        out_shape=[
            jax.ShapeDtypeStruct((B, N_KV, 8, HEAD_DIM), F32),
            jax.ShapeDtypeStruct((B, 8, HEAD_DIM), jnp.int32),
        ],
        compiler_params=_cp(("parallel",)),
        name="dec_select",
    )(q8, k_cmp, v_cmp, member)


def _dec_attend_kernel(idx_ref, pt_ref, q_ref, gl_ref, ocmp_ref, nslc_ref, win_ref, nwin_ref, *rest,
                       q_pos, past_len, n_sel):
    del pt_ref
    blk_refs = rest[:n_sel]
    o_ref = rest[n_sel]
    b = pl.program_id(0)
    n = pl.program_id(1)
    base = (b * N_KV + n) * n_sel
    scale = HEAD_DIM ** -0.5
    past_blocks = past_len // SLC_BLOCK
    rb = SLC_BLOCK * 8
    rw = win_ref.shape[1]
    w_buf = rw // 8
    qf = q_ref[0, 0] * scale
    qb = qf.astype(BF16)
    js = [idx_ref[base + s] for s in range(n_sel)]
    has_new = js[0] == past_blocks
    for s in range(1, n_sel):
        has_new = has_new | (js[s] == past_blocks)
    new_ok = has_new & (past_len <= q_pos)

    def head_rows(x):
        slot = lax.broadcasted_iota(jnp.int32, (8, HEAD_DIM), 0)
        k = jnp.sum(jnp.where(slot == n, x, 0.0), axis=0, keepdims=True)
        v = jnp.sum(jnp.where(slot == n + N_KV, x, 0.0), axis=0, keepdims=True)
        return k, v

    def finish(m):
        return jnp.where(m == -jnp.inf, 0.0, m)

    r_s = lax.broadcasted_iota(jnp.int32, (8, rb), 1)
    t_s = r_s >> 3
    mine_s = (r_s & 7) == n

    def slc_scores(s):
        blk = blk_refs[s][0, 0].reshape(rb, HEAD_DIM)
        sc = _dot_nt(qb, blk.astype(BF16))
        ok = mine_s & (js[s] * SLC_BLOCK + t_s <= q_pos) & (js[s] < past_blocks)
        return jnp.where(ok, sc, -jnp.inf)

    k_new, v_new = head_rows(nslc_ref[0])
    s_new = jnp.where(new_ok, jnp.sum(qf * k_new, axis=-1, keepdims=True), -jnp.inf)
    scores = [slc_scores(s) for s in range(n_sel)]
    m = s_new
    for sc in scores:
        m = jnp.maximum(m, jnp.max(sc, axis=-1, keepdims=True))
    m = finish(m)
    p_new = jnp.exp(s_new - m)
    l = p_new
    acc = p_new * v_new
    for s, sc in enumerate(scores):
        p = jnp.exp(sc - m)
        l = l + jnp.sum(p, axis=-1, keepdims=True)
        acc = acc + _dot(pltpu.roll(p, N_KV, 1).astype(BF16), blk_refs[s][0, 0].reshape(rb, HEAD_DIM).astype(BF16))
    o_slc = acc / jnp.maximum(l, 1e-30)

    r_w = lax.broadcasted_iota(jnp.int32, (8, rw), 1)
    w_pos = past_len - w_buf + (r_w >> 3)
    w_ok = ((r_w & 7) == n) & (q_pos - w_pos >= 0) & (q_pos - w_pos <= WINDOW) & (w_pos >= 0)
    win = win_ref[0].astype(BF16)
    s_w = jnp.where(w_ok, _dot_nt(qb, win), -jnp.inf)
    k_nw, v_nw = head_rows(nwin_ref[0])
    nw_ok = (q_pos - past_len >= 0) and (q_pos - past_len <= WINDOW)
    s_nw = jnp.sum(qf * k_nw, axis=-1, keepdims=True) if nw_ok else jnp.full((8, 1), -jnp.inf, F32)
    m = finish(jnp.maximum(s_nw, jnp.max(s_w, axis=-1, keepdims=True)))
    p_w = jnp.exp(s_w - m)
    p_nw = jnp.exp(s_nw - m)
    l = p_nw + jnp.sum(p_w, axis=-1, keepdims=True)
    acc = p_nw * v_nw + _dot(pltpu.roll(p_w, N_KV, 1).astype(BF16), win)
    o_win = acc / jnp.maximum(l, 1e-30)

    gates = jax.nn.sigmoid(gl_ref[0, 0])
    o_ref[0, 0] = gates[:, 0:1] * ocmp_ref[0, 0] + gates[:, 1:2] * o_slc + gates[:, 2:3] * o_win


def dec_attend(idx_flat, page_table, q8, gate8, ocmp8, new_slc, win2, new_win, cache5, *, q_pos, past_len, n_sel):
    B, n_pages = page_table.shape
    rw = win2.shape[1]
    past_blocks = past_len // SLC_BLOCK
    per_page = PAGE_SIZE // SLC_BLOCK

    def blk_spec(s):
        def imap(b, n, idx, pt):
            j = jnp.minimum(idx[(b * N_KV + n) * n_sel + s], past_blocks - 1)
            return (pt[b * n_pages + j // per_page], j % per_page, 0, 1, 0)
        return pl.BlockSpec((1, 1, SLC_BLOCK, 8, HEAD_DIM), imap)

    head_spec = pl.BlockSpec((1, 1, 8, HEAD_DIM), lambda b, n, idx, pt: (b, n, 0, 0))
    tok_spec = pl.BlockSpec((1, 8, HEAD_DIM), lambda b, n, idx, pt: (b, 0, 0))
    grid_spec = pltpu.PrefetchScalarGridSpec(
        num_scalar_prefetch=2,
        grid=(B, N_KV),
        in_specs=[head_spec, head_spec, head_spec, tok_spec,
                  pl.BlockSpec((1, rw, HEAD_DIM), lambda b, n, idx, pt: (b, 0, 0)), tok_spec]
        + [blk_spec(s) for s in range(n_sel)],
        out_specs=head_spec,
    )
    return pl.pallas_call(
        functools.partial(_dec_attend_kernel, q_pos=q_pos, past_len=past_len, n_sel=n_sel),
        grid_spec=grid_spec,
        out_shape=jax.ShapeDtypeStruct((B, N_KV, 8, HEAD_DIM), F32),
        compiler_params=_cp(("parallel", "arbitrary")),
        name="dec_attend",
    )(idx_flat, page_table.reshape(-1), q8, gate8, ocmp8, new_slc, win2, new_win, *([cache5] * n_sel))


def _pad_to(a, axis, size):
    pad = [(0, 0)] * a.ndim
    pad[axis] = (0, size - a.shape[axis])
    return jnp.pad(a, pad)


def _prep_weights(w_in_a, w_in_b, b_gate, w_kv, cmp_pe, cmp_w1, cmp_w2, w_mem_kv, w_out, w_up, conv_w, conv_b,
                  w_down, hgrn_lb):
    depth = w_out.shape[0]
    n_b = w_in_b.shape[0]
    n_gate = N_MIX_HEADS * N_BRANCH
    per_kv = GROUP * N_BRANCH
    half = CMP_STRIDE * HEAD_DIM
    w1 = cmp_w1.astype(BF16)
    lb = jnp.cumsum(jax.nn.softmax(hgrn_lb.astype(F32), axis=0), axis=0)
    lb = lb - lb[0]
    layers = lambda f, n=depth: [f(l) for l in range(n)]
    gate_w = lambda j: w_in_b[j][:, MIX_DIM:MIX_DIM + n_gate].reshape(D_MODEL, N_KV, per_kv)
    return dict(
        in_a=cast_bf16(w_in_a),
        in_b=layers(lambda j: jnp.concatenate(
            [w_in_b[j][:, :MIX_DIM], w_in_b[j][:, MIX_DIM + n_gate:],
             _pad_to(gate_w(j), 2, HEAD_DIM).reshape(D_MODEL, N_KV * HEAD_DIM)], axis=-1).astype(BF16), n_b),
        b_in_b=layers(lambda j: jnp.concatenate(
            [jnp.zeros((MIX_DIM + MEM_DIM,), F32),
             _pad_to(b_gate[j].reshape(N_KV, per_kv), 1, HEAD_DIM).reshape(N_KV * HEAD_DIM)]), n_b),
        kv=cast_bf16(w_kv[None]),
        mem_kv=cast_bf16(w_mem_kv),
        out=cast_bf16(w_out),
        up=cast_split_halves(w_up, D_FF_PAD),
        conv_w=layers(lambda l: _pad_to(_pad_to(conv_w[l], 1, D_FF_PAD), 0, 8)),
        conv_b=layers(lambda l: _pad_to(conv_b[l], 0, D_FF_PAD).reshape(1, D_FF_PAD)),
        down=cast_pad_rows(w_down, D_FF_PAD),
        cmp_w1=w1,
        cmp_w1ab=jnp.concatenate([w1[:, :half], w1[:, half:]], axis=2),
        cmp_w2=cmp_w2.astype(BF16),
        cmp_pe=cmp_pe.reshape(2, 1, 2 * half),
        log_lb=jnp.log(lb),
        log1m_lb=jnp.log1p(-lb),
    )


def kernel(x_prompt, x_sample, mem_prompt, cache_nsa_kv, page_table, cache_win_kv, state_hgrn, state_conv,
           cache_mem_kv, norm_mix, norm_ffn, norm_mem, norm_kv, norm_final, w_in_a, hgrn_lb, hgrn_onorm,
           w_in_b, b_gate, w_kv, cmp_pe, cmp_w1, cmp_w2, w_mem_kv, w_out, w_up, conv_w, conv_b, w_down):
    B, L, _ = x_prompt.shape
    Bs = x_sample.shape[0]
    depth = w_out.shape[0]
    n_a = w_in_a.shape[0]
    n_mem = mem_prompt.shape[1]
    n_pool, page, _, _, _ = cache_nsa_kv.shape
    n_pages = page_table.shape[1]
    past_len = n_pages * page
    w_buf = cache_win_kv.shape[1]
    assert x_sample.shape[1] == 1 and page == PAGE_SIZE and L % 128 == 0 and L // SLC_BLOCK <= HEAD_DIM
    W = _prep_weights(w_in_a, w_in_b, b_gate, w_kv, cmp_pe, cmp_w1, cmp_w2, w_mem_kv, w_out, w_up, conv_w,
                      conv_b, w_down, hgrn_lb)
    pe_h = pe_proj(W["cmp_pe"], W["cmp_w1"])
    onorm = hgrn_onorm.reshape(n_a, 1, MIX_DIM)
    log_lb = W["log_lb"].reshape(n_a, 1, MIX_DIM)
    log1m_lb = W["log1m_lb"].reshape(n_a, 1, MIX_DIM)

    def ffn(x, l, **kw):
        return conv_ffn(x, norm_ffn[l], W["up"], W["conv_w"][l], W["conv_b"][l], W["down"], l, **kw)

    M = B * L
    mem_flat = mem_prompt.reshape(B * n_mem, D_MODEL)
    mem_kv_both = [rms_matmul_heads(mem_flat, norm_mem[l], W["mem_kv"], (l,), 2 * MEM_DIM) for l in range(depth)]
    mem_kv_p = [by_head.reshape(B, n_mem * 2 * N_MEM_HEADS, HEAD_DIM) for _, by_head in mem_kv_both]
    mem_cache = cache_mem_kv.reshape(depth, Bs, n_mem * 2 * N_MEM_HEADS, HEAD_DIM)
    x = x_prompt.reshape(M, D_MODEL)
    hgrn_p, conv_p = [], []
    for l in range(depth):
        if l < n_a:
            z3 = rms_matmul(x, norm_mix[l], W["in_a"], lead=(l,)).reshape(B, L, -1)
            s0 = jnp.zeros((B, N_MIX_HEADS, HEAD_DIM, HEAD_DIM), F32)
            o_mix, s_new = hgrn_prompt(z3, log_lb[l], log1m_lb[l], onorm[l], s0)
            hgrn_p.append(s_new)
            o_mem = mem_attention(z3, 4 * MIX_DIM // MEM_DIM, mem_kv_p[l])
        else:
            j = l - n_a
            if j == 0:
                n_row4 = 4 * N_KV * HEAD_DIM
                rows_p, rows_by_head = rms_matmul_heads(x, norm_kv, W["kv"], (0,), n_row4)
                rows3 = rows_p.reshape(B, L, -1)
                cmp_p = cmp_mlp(rows3, W["cmp_w1"], W["cmp_w2"], pe_h)
            zq3 = rms_matmul(x, norm_mix[l], W["in_b"][j], W["b_in_b"][j]).reshape(B, L, -1)
            o_mix = nsa_prompt_t(zq3, cmp_p[0], cmp_p[1], rows3)
            o_mem = mem_attention(zq3, MIX_DIM // MEM_DIM, mem_kv_p[l])
        x = out_proj(o_mix.reshape(M, MIX_DIM), o_mem.reshape(M, MEM_DIM), W["out"], l, x)
        x, a_tail = ffn(x, l, seq_len=L)
        conv_p.append(a_tail.reshape(B, -1, 8, D_FF_PAD)[:, -1, 8 - (CONV_W - 1):, :D_FF])
    y_prompt = rmsnorm_rows(x, norm_final).reshape(B, L, D_MODEL)
    w_keep = min(WINDOW, L)
    nsa_rows_prompt = rows_by_head.reshape(B, L, 4, N_KV, HEAD_DIM)
    win_prompt = rows3[:, L - w_keep:, n_row4:].reshape(B, w_keep, 2, N_KV, HEAD_DIM)
    mem_kv_prompt = jnp.stack([by_head for _, by_head in mem_kv_both]).reshape(
        depth, B, n_mem, 2, N_MEM_HEADS, HEAD_DIM)

    Ms = 16
    pad_rows = lambda a: _pad_to(a, 0, Ms)
    xs = pad_rows(x_sample.reshape(Bs, D_MODEL))
    hgrn_s, conv_s = [], []
    q_pos = past_len
    t_pad = -(-(past_len + 1) // SLC_BLOCK) * SLC_BLOCK
    n_blk = t_pad // SLC_BLOCK
    nc = t_pad // CMP_STRIDE - 1
    n_sel = min(N_SELECT, n_blk)
    for l in range(depth):
        if l < n_a:
            z = rms_matmul(xs, norm_mix[l], W["in_a"], lead=(l,))
            z3 = z[:Bs].reshape(Bs, 1, -1)
            o_mix, s_new = hgrn_step(z3, log_lb[l], log1m_lb[l], onorm[l], state_hgrn[l])
            hgrn_s.append(s_new)
            o_mix = o_mix.reshape(Bs, MIX_DIM)
            o_mem = mem_attention(z3, 4 * MIX_DIM // MEM_DIM, mem_cache, lead=(l,))
        else:
            j = l - n_a
            if j == 0:
                rows_s = rms_matmul(xs, norm_kv, W["kv"], lead=(0,))[:Bs]
                new_cmp = rows_s[:, 0:1024].reshape(Bs, 8, HEAD_DIM)
                new_slc = rows_s[:, 1024:2048].reshape(Bs, 8, HEAD_DIM)
                new_win = rows_s[:, 2048:3072].reshape(Bs, 8, HEAD_DIM)
                cache4 = cache_nsa_kv.reshape(n_pool, PAGE_SIZE, 4 * N_KV, HEAD_DIM)
                cache5 = cache_nsa_kv.reshape(n_pool, PAGE_SIZE // SLC_BLOCK, SLC_BLOCK, 4 * N_KV, HEAD_DIM)
                win2 = cache_win_kv.reshape(Bs, w_buf * 2 * N_KV, HEAD_DIM)
                k_cmp_s, v_cmp_s = dec_cmp_tokens(cache4, page_table, new_cmp, W["cmp_w1ab"], W["cmp_w2"], pe_h)
                assert k_cmp_s.shape[2] > nc + 1
                ratio = SLC_BLOCK // CMP_STRIDE
                c_of_row = jnp.arange(k_cmp_s.shape[2])[:, None] - 1
                j_of_col = jnp.arange(-(-n_blk // HEAD_DIM) * HEAD_DIM)[None, :]
                member = ((c_of_row >= ratio * j_of_col - 1) & (c_of_row <= ratio * j_of_col + ratio - 1)).astype(BF16)
            zq = rms_matmul(xs, norm_mix[l], W["in_b"][j], W["b_in_b"][j])
            gt = zq[:Bs, MIX_DIM + MEM_DIM:]
            z3 = zq[:Bs].reshape(Bs, 1, -1)
            per_head = lambda a: _pad_to(a.reshape(Bs, N_KV, GROUP, -1), 2, 8)
            q8 = per_head(zq[:Bs, :MIX_DIM])
            o_cmp, idx = dec_select(q8, k_cmp_s, v_cmp_s, member, q_pos=q_pos, nc=nc, n_blk=n_blk, n_sel=n_sel)
            idx_flat = idx[:, :N_KV, :n_sel].reshape(-1)
            gate9 = gt.reshape(Bs, N_KV, HEAD_DIM)[:, :, :GROUP * N_BRANCH]
            o_all = dec_attend(idx_flat, page_table, q8, _pad_to(per_head(gate9), 3, HEAD_DIM), o_cmp,
                               new_slc, win2, new_win, cache5, q_pos=q_pos, past_len=past_len, n_sel=n_sel)
            o_mix = o_all[:, :, :GROUP].reshape(Bs, MIX_DIM).astype(BF16)
            o_mem = mem_attention(z3, MIX_DIM // MEM_DIM, mem_cache, lead=(l,))
        xs = out_proj(pad_rows(o_mix), pad_rows(o_mem.reshape(Bs, MEM_DIM)), W["out"], l, xs)
        prev = (pad_rows(_pad_to(state_conv[l][:, 1], 1, D_FF_PAD)), pad_rows(_pad_to(state_conv[l][:, 0], 1, D_FF_PAD)))
        xs, a_new = ffn(xs, l, prev=prev)
        conv_s.append(jnp.stack([state_conv[l][:, 1], a_new[0, :Bs, :D_FF]], axis=1))
    y_sample = rmsnorm_rows(xs, norm_final)[:Bs].reshape(Bs, 1, D_MODEL)
    nsa_rows_sample = rows_s[:, :n_row4].reshape(Bs, 1, 4, N_KV, HEAD_DIM)
    win_new = rows_s[:, n_row4:].reshape(Bs, 1, 2, N_KV, HEAD_DIM).astype(cache_win_kv.dtype)
    win_sample = jnp.concatenate([cache_win_kv, win_new], axis=1)[:, 1:]

    return (y_prompt, y_sample, nsa_rows_prompt, nsa_rows_sample, win_prompt, win_sample,
            jnp.stack(hgrn_p), jnp.stack(hgrn_s), jnp.stack(conv_p), jnp.stack(conv_s), mem_kv_prompt)
```

```python
import functools

import jax
import jax.numpy as jnp
from jax import lax
from jax.experimental import pallas as pl
from jax.experimental.pallas import tpu as pltpu

F32 = jnp.float32
BF16 = jnp.bfloat16

D_MODEL = 2048
HEAD_DIM = 128
MIX_DIM = 1536
N_MIX_HEADS = 12
MEM_DIM = 512
N_MEM_HEADS = 4
N_KV = 4
GROUP = 3
N_BRANCH = 3
CMP_STRIDE = 16
CMP_HID = 256
SLC_BLOCK = 64
N_SELECT = 16
WINDOW = 512
PAGE_SIZE = 128
D_FF = 5504
D_FF_PAD = 5632
CONV_W = 3
RMS_EPS = 1e-6
SEL_BONUS = 1e9
NEG_BIG = -1e30
LOG2E = 1.4426950408889634

HGRN_T = 64
HGRN_SUB = 8
HGRN_HEADS_PER_STEP = 6
FFN_TF = 512
V7X_VMEM_BYTES = 64 * 1024 * 1024
VMEM_LIMIT = V7X_VMEM_BYTES - 8 * 1024 * 1024


def _cp(sem, vmem=VMEM_LIMIT):
    return pltpu.CompilerParams(dimension_semantics=sem, vmem_limit_bytes=vmem)


def _dot(a, b):
    return jnp.dot(a, b, preferred_element_type=F32)


def _dot_nt(a, b):
    return lax.dot_general(a, b, (((1,), (1,)), ((), ())), preferred_element_type=F32)


def _dot_tn(a, b):
    return lax.dot_general(a, b, (((0,), (0,)), ((), ())), preferred_element_type=F32)


def _pick(n, cands):
    for c in cands:
        if n % c == 0:
            return c
    return n


def _rms_matmul_kernel(x_ref, g_ref, w_ref, b_ref, o_ref, xn_ref):
    @pl.when(pl.program_id(1) == 0)
    def _():
        x = x_ref[...]
        y = x * lax.rsqrt(jnp.mean(x * x, axis=-1, keepdims=True) + RMS_EPS)
        xn_ref[...] = (y * g_ref[...]).astype(BF16)

    o_ref[...] = _dot(xn_ref[...], w_ref[...]) + b_ref[...]


def _layer_spec(lead, block, imap):
    return pl.BlockSpec((None,) * len(lead) + tuple(block), lambda *a: tuple(lead) + tuple(imap(*a)))


def rms_matmul(x, g, w, bias=None, lead=()):
    M, K = x.shape
    N = w.shape[-1]
    tm = _pick(M, (1024, 512, 256, 128))
    tn = _pick(N, (512, 256, 128))
    if bias is None:
        bias = jnp.zeros((N,), F32)
    return pl.pallas_call(
        _rms_matmul_kernel,
        grid=(M // tm, N // tn),
        in_specs=[
            pl.BlockSpec((tm, K), lambda i, j: (i, 0)),
            pl.BlockSpec((1, K), lambda i, j: (0, 0)),
            _layer_spec(lead, (K, tn), lambda i, j: (0, j)),
            pl.BlockSpec((1, tn), lambda i, j: (0, j)),
        ],
        out_specs=pl.BlockSpec((tm, tn), lambda i, j: (i, j)),
        out_shape=jax.ShapeDtypeStruct((M, N), F32),
        scratch_shapes=[pltpu.VMEM((tm, K), BF16)],
        compiler_params=_cp(("parallel", "arbitrary")),
        name="rms_matmul",
    )(x, g.reshape(1, K), w, bias.reshape(1, N))


def _rms_matmul_heads_kernel(x_ref, g_ref, w_ref, o_ref, oh_ref, xn_ref, *, n_head_steps):
    j = pl.program_id(1)

    @pl.when(j == 0)
    def _():
        x = x_ref[...]
        y = x * lax.rsqrt(jnp.mean(x * x, axis=-1, keepdims=True) + RMS_EPS)
        xn_ref[...] = (y * g_ref[...]).astype(BF16)

    y = _dot(xn_ref[...], w_ref[...])
    o_ref[...] = y

    @pl.when(j < n_head_steps)
    def _():
        oh_ref[...] = y.reshape(oh_ref.shape)


def rms_matmul_heads(x, g, w, lead, head_cols):
    M, K = x.shape
    N = w.shape[-1]
    tm = _pick(M, (512, 256, 128))
    tn = 8 * HEAD_DIM
    assert N % tn == 0 and head_cols % tn == 0
    n_head_steps = head_cols // tn
    return pl.pallas_call(
        functools.partial(_rms_matmul_heads_kernel, n_head_steps=n_head_steps),
        grid=(M // tm, N // tn),
        in_specs=[
            pl.BlockSpec((tm, K), lambda i, j: (i, 0)),
            pl.BlockSpec((1, K), lambda i, j: (0, 0)),
            _layer_spec(lead, (K, tn), lambda i, j: (0, j)),
        ],
        out_specs=[
            pl.BlockSpec((tm, tn), lambda i, j: (i, j)),
            pl.BlockSpec((tm, 8, HEAD_DIM), lambda i, j: (i, jnp.minimum(j, n_head_steps - 1), 0)),
        ],
        out_shape=[
            jax.ShapeDtypeStruct((M, N), F32),
            jax.ShapeDtypeStruct((M, head_cols // HEAD_DIM, HEAD_DIM), F32),
        ],
        scratch_shapes=[pltpu.VMEM((tm, K), BF16)],
        compiler_params=_cp(("parallel", "arbitrary")),
        name="rms_matmul_heads",
    )(x, g.reshape(1, K), w)


def _rmsnorm_kernel(x_ref, g_ref, o_ref):
    x = x_ref[...]
    y = x * lax.rsqrt(jnp.mean(x * x, axis=-1, keepdims=True) + RMS_EPS)
    o_ref[...] = y * g_ref[...]


def rmsnorm_rows(x, g):
    M, K = x.shape
    tm = _pick(M, (512, 256, 128))
    return pl.pallas_call(
        _rmsnorm_kernel,
        grid=(M // tm,),
        in_specs=[pl.BlockSpec((tm, K), lambda i: (i, 0)), pl.BlockSpec((1, K), lambda i: (0, 0))],
        out_specs=pl.BlockSpec((tm, K), lambda i: (i, 0)),
        out_shape=jax.ShapeDtypeStruct((M, K), F32),
        compiler_params=_cp(("parallel",)),
        name="final_rmsnorm",
    )(x, g.reshape(1, K))


def _cast_kernel(x_ref, o_ref):
    o_ref[...] = x_ref[...].astype(o_ref.dtype)


def cast_bf16(w):
    G, R, C = w.shape
    tr = _pick(R, (256, 128))
    return pl.pallas_call(
        _cast_kernel,
        grid=(G, R // tr),
        in_specs=[pl.BlockSpec((1, tr, C), lambda g, r: (g, r, 0))],
        out_specs=pl.BlockSpec((1, tr, C), lambda g, r: (g, r, 0)),
        out_shape=jax.ShapeDtypeStruct((G, R, C), BF16),
        compiler_params=_cp(("parallel", "parallel")),
        name="cast_bf16",
    )(w)


def _cast_halves_kernel(x_ref, o_ref):
    c = x_ref.shape[2]
    o_ref[0, 0, :, 0:c] = x_ref[0].astype(o_ref.dtype)
    o_ref[0, 0, :, c:] = jnp.zeros((o_ref.shape[2], o_ref.shape[3] - c), o_ref.dtype)


def cast_split_halves(w, cols_pad):
    G, R, C2 = w.shape
    C = C2 // 2
    tr = _pick(R, (256, 128))
    return pl.pallas_call(
        _cast_halves_kernel,
        grid=(G, 2, R // tr),
        in_specs=[pl.BlockSpec((1, tr, C), lambda g, h, r: (g, r, h))],
        out_specs=pl.BlockSpec((1, 1, tr, cols_pad), lambda g, h, r: (g, h, r, 0)),
        out_shape=jax.ShapeDtypeStruct((G, 2, R, cols_pad), BF16),
        compiler_params=_cp(("parallel", "parallel", "parallel")),
        name="cast_split_halves",
    )(w)


def _cast_pad_rows_kernel(x_ref, o_ref):
    r = x_ref.shape[1]
    o_ref[0, 0:r, :] = x_ref[0].astype(o_ref.dtype)
    o_ref[0, r:, :] = jnp.zeros((o_ref.shape[1] - r, o_ref.shape[2]), o_ref.dtype)


def cast_pad_rows(w, rows_pad):
    G, R, C = w.shape
    tc = _pick(C, (256, 128))
    assert R % 16 == 0 and rows_pad % 16 == 0
    return pl.pallas_call(
        _cast_pad_rows_kernel,
        grid=(G, C // tc),
        in_specs=[pl.BlockSpec((1, R, tc), lambda g, c: (g, 0, c))],
        out_specs=pl.BlockSpec((1, rows_pad, tc), lambda g, c: (g, 0, c)),
        out_shape=jax.ShapeDtypeStruct((G, rows_pad, C), BF16),
        compiler_params=_cp(("parallel", "parallel")),
        name="cast_pad_rows",
    )(w)


def _outproj_kernel(om_ref, oe_ref, w1_ref, w2_ref, x_ref, y_ref):
    y_ref[...] = x_ref[...] + _dot(om_ref[...], w1_ref[...]) + _dot(oe_ref[...], w2_ref[...])


def out_proj(o_mix, o_mem, w_out, layer, x):
    M = x.shape[0]
    tm = _pick(M, (1024, 512, 256, 128))
    tn = 1024
    assert MIX_DIM % MEM_DIM == 0
    return pl.pallas_call(
        _outproj_kernel,
        grid=(M // tm, D_MODEL // tn),
        in_specs=[
            pl.BlockSpec((tm, MIX_DIM), lambda i, j: (i, 0)),
            pl.BlockSpec((tm, MEM_DIM), lambda i, j: (i, 0)),
            _layer_spec((layer,), (MIX_DIM, tn), lambda i, j: (0, j)),
            _layer_spec((layer,), (MEM_DIM, tn), lambda i, j: (MIX_DIM // MEM_DIM, j)),
            pl.BlockSpec((tm, tn), lambda i, j: (i, j)),
        ],
        out_specs=pl.BlockSpec((tm, tn), lambda i, j: (i, j)),
        out_shape=jax.ShapeDtypeStruct((M, D_MODEL), F32),
        compiler_params=_cp(("parallel", "parallel")),
        name="out_proj",
    )(o_mix, o_mem, w_out, w_out, x)


def _mem_attend(q, kv_ref, store):
    scale = HEAD_DIM ** -0.5
    slots = 2 * N_MEM_HEADS
    n_mem = kv_ref.shape[1] // slots
    for h in range(N_MEM_HEADS):
        k = kv_ref[0, pl.ds(h, n_mem, stride=slots), :].astype(BF16)
        v = kv_ref[0, pl.ds(N_MEM_HEADS + h, n_mem, stride=slots), :].astype(BF16)
        s = _dot_nt(q[:, h * HEAD_DIM:(h + 1) * HEAD_DIM].astype(BF16), k) * scale
        m = jnp.max(s, axis=-1, keepdims=True)
        p = jnp.exp(s - m)
        p = p / jnp.sum(p, axis=-1, keepdims=True)
        store(h, _dot(p.astype(BF16), v))


def _mem_attn_kernel(q_ref, kv_ref, o_ref):
    def store(h, o):
        o_ref[0, :, h * HEAD_DIM:(h + 1) * HEAD_DIM] = o.astype(o_ref.dtype)

    _mem_attend(q_ref[0], kv_ref, store)


def _outproj_mem_kernel(om_ref, q_ref, kv_ref, w1_ref, w2_ref, x_ref, y_ref, oe_ref):
    @pl.when(pl.program_id(1) == 0)
    def _():
        def store(h, o):
            oe_ref[:, h * HEAD_DIM:(h + 1) * HEAD_DIM] = o.astype(oe_ref.dtype)

        _mem_attend(q_ref[...], kv_ref, store)

    y_ref[...] = x_ref[...] + _dot(om_ref[...], w1_ref[...]) + _dot(oe_ref[...], w2_ref[...])


def out_proj_mem(o_mix, z, col_block, kv, seq_len, w_out, layer, x):
    M = x.shape[0]
    tm = _pick(seq_len, (1024, 512, 256, 128))
    tn = 1024
    rows = kv.shape[1]
    return pl.pallas_call(
        _outproj_mem_kernel,
        grid=(M // tm, D_MODEL // tn),
        in_specs=[
            pl.BlockSpec((tm, MIX_DIM), lambda i, j: (i, 0)),
            pl.BlockSpec((tm, MEM_DIM), lambda i, j: (i, col_block)),
            pl.BlockSpec((1, rows, HEAD_DIM), lambda i, j: ((i * tm) // seq_len, 0, 0)),
            _layer_spec((layer,), (MIX_DIM, tn), lambda i, j: (0, j)),
            _layer_spec((layer,), (MEM_DIM, tn), lambda i, j: (MIX_DIM // MEM_DIM, j)),
            pl.BlockSpec((tm, tn), lambda i, j: (i, j)),
        ],
        out_specs=pl.BlockSpec((tm, tn), lambda i, j: (i, j)),
        out_shape=jax.ShapeDtypeStruct((M, D_MODEL), F32),
        scratch_shapes=[pltpu.VMEM((tm, MEM_DIM), BF16)],
        compiler_params=_cp(("parallel", "arbitrary")),
        name="out_proj_mem",
    )(o_mix, z, kv, w_out, w_out, x)


def mem_attention(z3, col_block, kv, lead=()):
    B, L, _ = z3.shape
    rows = kv.shape[-2]
    tm = _pick(L, (1024, 512, 256, 128))
    return pl.pallas_call(
        _mem_attn_kernel,
        grid=(B, L // tm),
        in_specs=[
            pl.BlockSpec((1, tm, MEM_DIM), lambda b, i: (b, i, col_block)),
            _layer_spec(lead, (1, rows, HEAD_DIM), lambda b, i: (b, 0, 0)),
        ],
        out_specs=pl.BlockSpec((1, tm, MEM_DIM), lambda b, i: (b, i, 0)),
        out_shape=jax.ShapeDtypeStruct((B, L, MEM_DIM), BF16),
        compiler_params=_cp(("parallel", "parallel")),
        name="mem_attention",
    )(z3, kv)


def _ffn_kernel(*refs, blocks_per_seq, decode):
    if decode:
        (x_ref, g_ref, wa_ref, wu_ref, cw_ref, cb_ref, wd_ref, p1_ref, p2_ref,
         y_ref, at_ref, xn_ref) = refs
    else:
        (x_ref, g_ref, wa_ref, wu_ref, cw_ref, cb_ref, wd_ref,
         y_ref, at_ref, xn_ref, carry_ref) = refs
    i = pl.program_id(0)
    j = pl.program_id(1)

    @pl.when(j == 0)
    def _():
        x = x_ref[...]
        y = x * lax.rsqrt(jnp.mean(x * x, axis=-1, keepdims=True) + RMS_EPS)
        xn_ref[...] = (y * g_ref[...]).astype(BF16)
        y_ref[...] = x

    if not decode:
        @pl.when((i % blocks_per_seq) == 0)
        def _():
            carry_ref[j] = jnp.zeros(carry_ref.shape[1:], F32)

    xn = xn_ref[...]
    tm = xn.shape[0]
    n_tail = at_ref.shape[1]
    half = wa_ref.shape[1] // 2
    down = None
    for cs in (slice(0, half), slice(half, 2 * half)):
        a = _dot(xn, wa_ref[:, cs])
        u = _dot(xn, wu_ref[:, cs])
        if decode:
            a1 = p1_ref[:, cs]
            a2 = p2_ref[:, cs]
        else:
            prev = carry_ref[j, :, cs]
            row = lax.broadcasted_iota(jnp.int32, a.shape, 0)
            a1 = jnp.where(row == 0, prev[7:8], pltpu.roll(a, 1, 0))
            a2 = jnp.where(row == 0, prev[6:7], jnp.where(row == 1, prev[7:8], pltpu.roll(a, 2, 0)))
            carry_ref[j, :, cs] = a[tm - 8:tm]
        at_ref[0, :, cs] = a[tm - n_tail:tm]
        c = cb_ref[:, cs] + cw_ref[2:3, cs] * a
        c = c + cw_ref[0:1, cs] * a2
        c = c + cw_ref[1:2, cs] * a1
        h = (jax.nn.gelu(c) * u).astype(BF16)
        d = _dot(h, wd_ref[cs, :])
        down = d if down is None else down + d
    y_ref[...] += down


def conv_ffn(x, g, w_up, cw, cb, w_down, layer, *, seq_len=None, prev=None):
    M = x.shape[0]
    decode = prev is not None
    tm = M if decode else _pick(seq_len, (1024, 512, 256, 128))
    n_tail = tm if decode else 8
    nf = D_FF_PAD // FFN_TF
    in_specs = [
        pl.BlockSpec((tm, D_MODEL), lambda i, j: (i, 0)),
        pl.BlockSpec((1, D_MODEL), lambda i, j: (0, 0)),
        _layer_spec((layer, 0), (D_MODEL, FFN_TF), lambda i, j: (0, j)),
        _layer_spec((layer, 1), (D_MODEL, FFN_TF), lambda i, j: (0, j)),
        pl.BlockSpec((8, FFN_TF), lambda i, j: (0, j)),
        pl.BlockSpec((1, FFN_TF), lambda i, j: (0, j)),
        _layer_spec((layer,), (FFN_TF, D_MODEL), lambda i, j: (j, 0)),
    ]
    args = [x, g.reshape(1, D_MODEL), w_up, w_up, cw, cb, w_down]
    scratch = [pltpu.VMEM((tm, D_MODEL), BF16)]
    if decode:
        in_specs += [pl.BlockSpec((tm, FFN_TF), lambda i, j: (i, j))] * 2
        args += list(prev)
    else:
        scratch.append(pltpu.VMEM((nf, 8, FFN_TF), F32))
    return pl.pallas_call(
        functools.partial(_ffn_kernel, blocks_per_seq=(1 if decode else seq_len // tm), decode=decode),
        grid=(M // tm, nf),
        in_specs=in_specs,
        out_specs=[
            pl.BlockSpec((tm, D_MODEL), lambda i, j: (i, 0)),
            pl.BlockSpec((1, n_tail, FFN_TF), lambda i, j: (i, 0, j)),
        ],
        out_shape=[
            jax.ShapeDtypeStruct((M, D_MODEL), F32),
            jax.ShapeDtypeStruct((M // tm, n_tail, D_FF_PAD), F32),
        ],
        scratch_shapes=scratch,
        compiler_params=_cp(("arbitrary", "arbitrary")),
        name="conv_ffn",
    )(*args)


def _hgrn_gates(q, f, log_lb, log1m_lb):
    qs = jax.nn.silu(q)
    log_f = jnp.logaddexp(log_lb, log1m_lb + jax.nn.log_sigmoid(f))
    k = 1.0 - jnp.exp(log_f)
    return qs, k, log_f


def _hgrn_out(o, g, onorm):
    y = o * lax.rsqrt(jnp.mean(o * o, axis=-1, keepdims=True) + RMS_EPS)
    return (y * onorm) * jax.nn.silu(g)


def _hgrn_kernel(q_ref, f_ref, i_ref, g_ref, llb_ref, l1m_ref, on_ref, s0_ref, o_ref, s_ref, st_ref,
                 *, n_chunks):
    c = pl.program_id(2)
    T = HGRN_T
    n_heads = st_ref.shape[0]

    @pl.when(c == 0)
    def _():
        for h in range(n_heads):
            st_ref[h] = s0_ref[0, h].T

    assert HGRN_SUB == 8
    rl1 = lax.broadcasted_iota(jnp.int32, (T, 1), 0) % HGRN_SUB
    t_i = lax.broadcasted_iota(jnp.int32, (T, T), 0)
    s_i = lax.broadcasted_iota(jnp.int32, (T, T), 1)
    tril = jnp.where(s_i <= t_i, 1.0, 0.0).astype(BF16)
    pair_masks = []
    size = 2 * HGRN_SUB
    while size <= T:
        same = (t_i & -size) == (s_i & -size)
        pair_masks.append((size, same & ((t_i & (size - 1)) >= size // 2) & ((s_i & (size - 1)) < size // 2)))
        size *= 2

    def shift_rows(x, d):
        return pltpu.roll(x.reshape(T // HGRN_SUB, HGRN_SUB, HEAD_DIM), d, 1).reshape(T, HEAD_DIM)

    def one_head(q, k, b, v, h, sl):
        hs = slice(h * HEAD_DIM, (h + 1) * HEAD_DIM)
        st = st_ref[h]
        o = _dot_nt((q * jnp.exp2(b)).astype(BF16), st.astype(BF16))
        o = o + jnp.sum(q * k, axis=-1, keepdims=True) * v
        for d in range(1, HGRN_SUB):
            w = jnp.exp2(b - shift_rows(b, d))
            a = jnp.sum(q * shift_rows(k, d) * w, axis=-1, keepdims=True)
            o = o + jnp.where(rl1 >= d, a, 0.0) * shift_rows(v, d)
        att = jnp.zeros((T, T), F32)
        for size, keep in pair_masks:
            half = size // 2
            refs = [jnp.broadcast_to(b[j * size + half - 1:j * size + half], (size, HEAD_DIM))
                    for j in range(T // size)]
            r = jnp.concatenate(refs, axis=0) if len(refs) > 1 else refs[0]
            qt = q * jnp.exp2(jnp.minimum(b - r, 0.0))
            kt = k * jnp.exp2(jnp.minimum(r - b, 0.0))
            att = att + jnp.where(keep, _dot_nt(qt.astype(BF16), kt.astype(BF16)), 0.0)
        o = o + _dot(att.astype(BF16), v.astype(BF16))
        bl = b[T - 1:T]
        kt = k * jnp.exp2(bl - b)
        st_ref[h] = st * jnp.exp2(bl) + _dot_tn(v.astype(BF16), kt.astype(BF16))
        o_ref[0, sl, hs] = _hgrn_out(o, g_ref[0, sl, hs], on_ref[:, hs]).astype(o_ref.dtype)

    def chunk(ci, carry):
        sl = pl.ds(pl.multiple_of(ci * T, T), T)
        q, k, lf = _hgrn_gates(q_ref[0, sl, :], f_ref[0, sl, :], llb_ref[...], l1m_ref[...])
        b = sum(_dot(tril, part) for part in _split3(lf)) * LOG2E
        for h in range(n_heads):
            hs = slice(h * HEAD_DIM, (h + 1) * HEAD_DIM)
            one_head(q[:, hs], k[:, hs], b[:, hs], i_ref[0, sl, hs], h, sl)
        return carry

    lax.fori_loop(0, n_chunks, chunk, 0)

    @pl.when(c == pl.num_programs(2) - 1)
    def _():
        for h in range(n_heads):
            s_ref[0, h] = st_ref[h].T


def hgrn_prompt(z3, log_lb, log1m_lb, onorm, s0):
    B, L, _ = z3.shape
    H = N_MIX_HEADS
    hb = HGRN_HEADS_PER_STEP
    wb = hb * HEAD_DIM
    tc = _pick(L, (512, 256, 128, 64))
    zspec = lambda k: pl.BlockSpec((1, tc, wb), lambda b, h, c: (b, c, k * (H // hb) + h))
    vspec = pl.BlockSpec((1, wb), lambda b, h, c: (0, h))
    sspec = pl.BlockSpec((1, hb, HEAD_DIM, HEAD_DIM), lambda b, h, c: (b, h, 0, 0))
    return pl.pallas_call(
        functools.partial(_hgrn_kernel, n_chunks=tc // HGRN_T),
        grid=(B, H // hb, L // tc),
        in_specs=[zspec(0), zspec(1), zspec(2), zspec(3), vspec, vspec, vspec, sspec],
        out_specs=[pl.BlockSpec((1, tc, wb), lambda b, h, c: (b, c, h)), sspec],
        out_shape=[
            jax.ShapeDtypeStruct((B, L, MIX_DIM), BF16),
            jax.ShapeDtypeStruct((B, H, HEAD_DIM, HEAD_DIM), F32),
        ],
        scratch_shapes=[pltpu.VMEM((hb, HEAD_DIM, HEAD_DIM), F32)],
        compiler_params=_cp(("parallel", "parallel", "arbitrary")),
        name="hgrn_chunked",
    )(z3, z3, z3, z3, log_lb, log1m_lb, onorm, s0)


def _hgrn_step_kernel(z_ref, llb_ref, l1m_ref, on_ref, s0_ref, o_ref, s_ref):
    for h in range(N_MIX_HEADS):
        col = lambda k: slice((k * N_MIX_HEADS + h) * HEAD_DIM, (k * N_MIX_HEADS + h + 1) * HEAD_DIM)
        hs = slice(h * HEAD_DIM, (h + 1) * HEAD_DIM)
        q, k, lf = _hgrn_gates(z_ref[0, :, col(0)], z_ref[0, :, col(1)], llb_ref[:, hs], l1m_ref[:, hs])
        v = z_ref[0, :, col(2)]
        g = z_ref[0, :, col(3)]
        rows = jnp.concatenate([q, k, jnp.exp(lf), jnp.zeros((5, HEAD_DIM), F32)], axis=0)
        cols = rows.T
        s_new = cols[:, 2:3] * s0_ref[0, h] + cols[:, 1:2] * v
        s_ref[0, h] = s_new
        o = jnp.sum(cols[:, 0:1] * s_new, axis=0, keepdims=True)
        o_ref[0, :, hs] = _hgrn_out(o, g, on_ref[:, hs]).astype(o_ref.dtype)


def hgrn_step(z3, log_lb, log1m_lb, onorm, s0):
    B = z3.shape[0]
    W = z3.shape[2]
    H = N_MIX_HEADS
    vspec = pl.BlockSpec((1, MIX_DIM), lambda b: (0, 0))
    sspec = pl.BlockSpec((1, H, HEAD_DIM, HEAD_DIM), lambda b: (b, 0, 0, 0))
    return pl.pallas_call(
        _hgrn_step_kernel,
        grid=(B,),
        in_specs=[pl.BlockSpec((1, 1, W), lambda b: (b, 0, 0)), vspec, vspec, vspec, sspec],
        out_specs=[pl.BlockSpec((1, 1, MIX_DIM), lambda b: (b, 0, 0)), sspec],
        out_shape=[
            jax.ShapeDtypeStruct((B, 1, MIX_DIM), BF16),
            jax.ShapeDtypeStruct((B, H, HEAD_DIM, HEAD_DIM), F32),
        ],
        compiler_params=_cp(("parallel",)),
        name="hgrn_step",
    )(z3, log_lb, log1m_lb, onorm, s0)


def _pe_proj_kernel(pe_ref, w_ref, o_ref):
    pe = jnp.broadcast_to(pe_ref[0], (8, pe_ref.shape[2])).astype(BF16)
    o_ref[0] = _dot(pe, w_ref[0])


def pe_proj(pe, w1):
    K = pe.shape[2]
    return pl.pallas_call(
        _pe_proj_kernel,
        grid=(2,),
        in_specs=[pl.BlockSpec((1, 1, K), lambda i: (i, 0, 0)), pl.BlockSpec((1, K, CMP_HID), lambda i: (i, 0, 0))],
        out_specs=pl.BlockSpec((1, 8, CMP_HID), lambda i: (i, 0, 0)),
        out_shape=jax.ShapeDtypeStruct((2, 8, CMP_HID), F32),
        compiler_params=_cp(("parallel",)),
        name="cmp_pe_proj",
    )(pe, w1)


def _cmp_mlp_kernel(x_ref, w1_ref, w2_ref, pe_ref, o_ref):
    n_sub = x_ref.shape[1] // CMP_STRIDE
    half = CMP_STRIDE * HEAD_DIM
    x = jnp.concatenate([x_ref[0, pl.ds(p, n_sub, stride=CMP_STRIDE), :] for p in range(CMP_STRIDE)],
                        axis=1).astype(BF16)
    a = _dot(x, w1_ref[0, 0:half])
    bm = _dot(x, w1_ref[0, half:2 * half])
    n = a.shape[0]
    h = a + pltpu.roll(bm, n - 1, 0) + pe_ref[0, 0:1]
    o_ref[0, 0] = _dot(jax.nn.gelu(h).astype(BF16), w2_ref[0])


def cmp_mlp(rows3, w1, w2, pe_h):
    B, L, _ = rows3.shape
    n_sub = L // CMP_STRIDE
    half = CMP_STRIDE * HEAD_DIM
    return pl.pallas_call(
        _cmp_mlp_kernel,
        grid=(2, B, N_KV),
        in_specs=[
            pl.BlockSpec((1, L, HEAD_DIM), lambda k, b, n: (b, 0, k * N_KV + n)),
            pl.BlockSpec((1, 2 * half, CMP_HID), lambda k, b, n: (k, 0, 0)),
            pl.BlockSpec((1, CMP_HID, HEAD_DIM), lambda k, b, n: (k, 0, 0)),
            pl.BlockSpec((1, 8, CMP_HID), lambda k, b, n: (k, 0, 0)),
        ],
        out_specs=pl.BlockSpec((1, 1, n_sub, HEAD_DIM), lambda k, b, n: (k, b * N_KV + n, 0, 0)),
        out_shape=jax.ShapeDtypeStruct((2, B * N_KV, n_sub, HEAD_DIM), F32),
        compiler_params=_cp(("parallel", "parallel", "parallel")),
        name="cmp_mlp",
    )(rows3, w1, w2, pe_h)


def _masked_softmax_rows(s, mask, exp_fn=jnp.exp):
    s = jnp.where(mask, s, -jnp.inf)
    m = jnp.max(s, axis=-1, keepdims=True)
    m = jnp.where(m == -jnp.inf, 0.0, m)
    p = exp_fn(s - m)
    return p / jnp.maximum(jnp.sum(p, axis=-1, keepdims=True), 1e-30)


def _split3(p):
    hi = p.astype(BF16)
    r = p - hi.astype(F32)
    mid = r.astype(BF16)
    lo = (r - mid.astype(F32)).astype(BF16)
    return hi, mid, lo


def _masked_softmax_cols(s, mask, exp_fn):
    s = jnp.where(mask, s, -jnp.inf)
    m = jnp.max(s, axis=0, keepdims=True)
    m = jnp.where(m == -jnp.inf, 0.0, m)
    p = exp_fn(s - m)
    return p / jnp.maximum(jnp.sum(p, axis=0, keepdims=True), 1e-30)


def _nsa_prompt_t_kernel(q_ref, gt_ref, kc_ref, vc_ref, ks_ref, vs_ref, kw_ref, vw_ref, o_ref,
                         kaug_ref, vst_ref, kwb_ref, vwt_ref, vct_ref, score_ref, *, tq, tk, tw, seq_len):
    qi = pl.program_id(2)
    n_blk = seq_len // SLC_BLOCK
    scale = HEAD_DIM ** -0.5 * LOG2E
    cols = GROUP * tq
    nc = kc_ref.shape[1]

    @pl.when(qi == 0)
    def _():
        r = lax.broadcasted_iota(jnp.int32, (seq_len, HEAD_DIM), 0) // SLC_BLOCK
        col = lax.broadcasted_iota(jnp.int32, (seq_len, HEAD_DIM), 1)
        kaug_ref[:, 0:HEAD_DIM] = ks_ref[0].astype(BF16)
        kaug_ref[:, HEAD_DIM:2 * HEAD_DIM] = jnp.where(r == col, 1.0, 0.0).astype(BF16)
        kwb_ref[...] = kw_ref[0].astype(BF16)
        for t in range(seq_len // tk):
            vst_ref[t] = vs_ref[0, t * tk:(t + 1) * tk, :].T.astype(BF16)
        for t in range(seq_len // tw):
            vwt_ref[t] = vw_ref[0, t * tw:(t + 1) * tw, :].T.astype(BF16)
        vct_ref[...] = vc_ref[0].T.astype(BF16)

    q0 = qi * tq
    qs = q_ref[0] * scale
    qb = jnp.concatenate([qs[:, g * HEAD_DIM:(g + 1) * HEAD_DIM] for g in range(GROUP)], axis=0).astype(BF16)

    def qpos(n_keys):
        return q0 + lax.broadcasted_iota(jnp.int32, (n_keys, cols), 1) % tq

    def kidx(n_keys):
        return lax.broadcasted_iota(jnp.int32, (n_keys, cols), 0)

    cend = CMP_STRIDE * kidx(nc) + (2 * CMP_STRIDE - 1)
    p_c = _masked_softmax_cols(_dot_nt(kc_ref[0].astype(BF16), qb), cend <= qpos(nc), jnp.exp2)
    o_cmp = _dot(vct_ref[...], p_c.astype(BF16))

    psum = p_c[:, 0:tq]
    for g in range(1, GROUP):
        psum = psum + p_c[:, g * tq:(g + 1) * tq]
    ji = lax.broadcasted_iota(jnp.int32, (HEAD_DIM, nc), 0)
    ci = lax.broadcasted_iota(jnp.int32, (HEAD_DIM, nc), 1)
    ratio = SLC_BLOCK // CMP_STRIDE
    member = ((ci >= ratio * ji - 1) & (ci <= ratio * ji + ratio - 1)).astype(BF16)
    imp = sum(_dot(member, part) for part in _split3(psum))
    blk = lax.broadcasted_iota(jnp.int32, (HEAD_DIM, tq), 0)
    cur = (q0 + lax.broadcasted_iota(jnp.int32, (HEAD_DIM, tq), 1)) // SLC_BLOCK
    valid = (blk <= cur) & (blk < n_blk)
    forced = valid & ((blk == 0) | (blk == cur) | (blk == cur - 1))
    score = jnp.where(forced, SEL_BONUS, jnp.where(valid, imp, -jnp.inf))
    score = score[0:n_blk]
    score_ref[...] = score
    blk = lax.broadcasted_iota(jnp.int32, (n_blk, tq), 0)
    per_tile = tq // SLC_BLOCK

    def rank_body(it, rank):
        for u in range(per_tile):
            i = it * per_tile + u
            si = score_ref[pl.ds(i, 1), :]
            later = jnp.where(blk > i, 1, 0)
            rank = rank + jnp.where(si > score, 1, jnp.where(si == score, later, 0))
        return rank

    rank = lax.fori_loop(0, qi + 1, rank_body, jnp.zeros((n_blk, tq), jnp.int32))
    bias_t = jnp.where(rank < N_SELECT, 0.0, NEG_BIG)
    if n_blk < HEAD_DIM:
        bias_t = jnp.concatenate([bias_t, jnp.zeros((HEAD_DIM - n_blk, tq), F32)], axis=0)
    bias = bias_t.T.astype(BF16)
    qaug = jnp.concatenate([qb, jnp.concatenate([bias] * GROUP, axis=0)], axis=1)

    qpos_k = qpos(tk)
    krow = kidx(tk)

    def slc_body(kt, carry, causal):
        m, l, acc = carry
        s = _dot_nt(kaug_ref[pl.ds(pl.multiple_of(kt * tk, tk), tk), :], qaug)
        if causal:
            s = jnp.where(kt * tk + krow <= qpos_k, s, -jnp.inf)
        m_new = jnp.maximum(m, jnp.max(s, axis=0, keepdims=True))
        alpha = jnp.exp2(m - m_new)
        p = jnp.exp2(s - m_new)
        l = alpha * l + jnp.sum(p, axis=0, keepdims=True)
        acc = alpha * acc + _dot(vst_ref[kt], p.astype(BF16))
        return m_new, l, acc

    init = (jnp.full((1, cols), -jnp.inf, F32), jnp.zeros((1, cols), F32), jnp.zeros((HEAD_DIM, cols), F32))
    n_full = q0 // tk

    def two_tiles(i, carry):
        return slc_body(2 * i + 1, slc_body(2 * i, carry, False), False)

    carry = lax.fori_loop(0, n_full // 2, two_tiles, init)
    carry = lax.fori_loop(n_full - n_full % 2, n_full, functools.partial(slc_body, causal=False), carry)
    _, l_s, acc_s = slc_body(n_full, carry, True)
    o_slc = acc_s / jnp.maximum(l_s, 1e-30)

    wk = min(WINDOW + tw, seq_len)
    d_rel = (lax.broadcasted_iota(jnp.int32, (wk, GROUP * tw), 1) % tw
             - lax.broadcasted_iota(jnp.int32, (wk, GROUP * tw), 0))
    win_parts = []
    for h in range(tq // tw):
        qh = q0 + h * tw
        k0 = jnp.minimum(jnp.maximum(qh - WINDOW, 0), seq_len - wk)
        qb_h = jnp.concatenate([qb[g * tq + h * tw:g * tq + (h + 1) * tw] for g in range(GROUP)], axis=0)
        d = d_rel + (qh - k0)
        s_w = _dot_nt(kwb_ref[pl.ds(pl.multiple_of(k0, tw), wk), :], qb_h)
        s_w = jnp.where((d >= 0) & (d <= WINDOW), s_w, -jnp.inf)
        p_w = jnp.exp2(s_w - jnp.max(s_w, axis=0, keepdims=True))
        l_w = jnp.sum(p_w, axis=0, keepdims=True)
        p_wb = p_w.astype(BF16)
        t0 = k0 // tw
        o_h = sum(_dot(vwt_ref[t0 + i], p_wb[i * tw:(i + 1) * tw]) for i in range(wk // tw))
        win_parts.append(o_h / jnp.maximum(l_w, 1e-30))
    o_win = jnp.concatenate([win_parts[h][:, g * tw:(g + 1) * tw]
                             for g in range(GROUP) for h in range(tq // tw)], axis=1)

    gates = jax.nn.sigmoid(gt_ref[0]).T
    for g in range(GROUP):
        cs = slice(g * tq, (g + 1) * tq)
        c0 = g * N_BRANCH
        o = (gates[c0:c0 + 1] * o_cmp[:, cs] + gates[c0 + 1:c0 + 2] * o_slc[:, cs]
             + gates[c0 + 2:c0 + 3] * o_win[:, cs])
        o_ref[0, :, g * HEAD_DIM:(g + 1) * HEAD_DIM] = o.T.astype(o_ref.dtype)


def nsa_prompt_t(zq3, k_cmp, v_cmp, rows3):
    B, L, _ = zq3.shape
    gate_block0 = (MIX_DIM + MEM_DIM) // HEAD_DIM
    tq = _pick(L, (512, 256, 128))
    tk = _pick(L, (512, 256, 128))
    tw = min(tq, 256)
    assert WINDOW % tw == 0 and tk % tq == 0 and tq % tw == 0
    n_sub = k_cmp.shape[1]
    qw = GROUP * HEAD_DIM
    rspec = lambda kind: pl.BlockSpec((1, L, HEAD_DIM), lambda b, n, i: (b, 0, kind * N_KV + n))
    cspec = pl.BlockSpec((1, n_sub, HEAD_DIM), lambda b, n, i: (b * N_KV + n, 0, 0))
    return pl.pallas_call(
        functools.partial(_nsa_prompt_t_kernel, tq=tq, tk=tk, tw=tw, seq_len=L),
        grid=(B, N_KV, L // tq),
        in_specs=[
            pl.BlockSpec((1, tq, qw), lambda b, n, i: (b, i, n)),
            pl.BlockSpec((1, tq, HEAD_DIM), lambda b, n, i: (b, i, gate_block0 + n)),
            cspec, cspec, rspec(2), rspec(3), rspec(4), rspec(5),
        ],
        out_specs=pl.BlockSpec((1, tq, qw), lambda b, n, i: (b, i, n)),
        out_shape=jax.ShapeDtypeStruct((B, L, MIX_DIM), BF16),
        scratch_shapes=[
            pltpu.VMEM((L, 2 * HEAD_DIM), BF16),
            pltpu.VMEM((L // tk, HEAD_DIM, tk), BF16),
            pltpu.VMEM((L, HEAD_DIM), BF16),
            pltpu.VMEM((L // tw, HEAD_DIM, tw), BF16),
            pltpu.VMEM((HEAD_DIM, n_sub), BF16),
            pltpu.VMEM((L // SLC_BLOCK, tq), F32),
        ],
        compiler_params=_cp(("parallel", "parallel", "arbitrary")),
        name="nsa_prompt",
    )(zq3, zq3, k_cmp, v_cmp, rows3, rows3, rows3, rows3)


def _dec_cmp_kernel(pt_ref, *refs, n_pg, n_groups):
    del pt_ref
    pg_refs = refs[:n_pg]
    new_ref, w1_ref, w2_ref, pe_ref, kc_ref, vc_ref, carry_ref = refs[n_pg:]
    g = pl.program_id(1)
    sub_pg = PAGE_SIZE // CMP_STRIDE
    n_sub = n_pg * sub_pg
    pos_per_chunk = 4
    kw = pos_per_chunk * HEAD_DIM

    @pl.when(g == 0)
    def _():
        carry_ref[...] = jnp.zeros_like(carry_ref)

    sub_i = lax.broadcasted_iota(jnp.int32, (8, n_sub, kw), 1)
    new_row = jnp.concatenate([new_ref[0], jnp.zeros((8, kw - HEAD_DIM), F32)], axis=1)[:, None, :]
    ys = [None, None]
    for pc in range(CMP_STRIDE // pos_per_chunk):
        x = jnp.concatenate(
            [jnp.concatenate([jnp.swapaxes(pg_refs[r][0, pl.ds(p, sub_pg, stride=CMP_STRIDE), :, :], 0, 1)
                              for r in range(n_pg)], axis=1)
             for p in range(pc * pos_per_chunk, (pc + 1) * pos_per_chunk)], axis=2)
        x = jnp.where(g < n_groups, x, 0.0)
        if pc == 0:
            x = jnp.where((g == n_groups) & (sub_i == 0), new_row, x)
        xb = x.astype(BF16)
        for kind in range(2):
            part = _dot(xb[kind * N_KV:(kind + 1) * N_KV].reshape(N_KV * n_sub, kw),
                        w1_ref[kind, pc * kw:(pc + 1) * kw, :])
            ys[kind] = part if ys[kind] is None else ys[kind] + part

    row = lax.broadcasted_iota(jnp.int32, (n_sub, CMP_HID), 0)
    for kind, out_ref in enumerate((kc_ref, vc_ref)):
        y = ys[kind]
        for n in range(N_KV):
            slot = kind * N_KV + n
            a = y[n * n_sub:(n + 1) * n_sub, 0:CMP_HID]
            bm = y[n * n_sub:(n + 1) * n_sub, CMP_HID:2 * CMP_HID]
            a_prev = jnp.where(row == 0, carry_ref[slot:slot + 1, :], pltpu.roll(a, 1, 0))
            carry_ref[slot:slot + 1, :] = a[n_sub - 1:n_sub]
            h = a_prev + bm + pe_ref[kind, 0:1]
            out_ref[0, n] = _dot(jax.nn.gelu(h).astype(BF16), w2_ref[kind])


def dec_cmp_tokens(cache4, page_table, new_cmp, w1ab, w2, pe_h):
    B, n_pages = page_table.shape
    n_pg = _pick(n_pages, (16, 8, 4, 2))
    assert n_pages % n_pg == 0
    n_groups = n_pages // n_pg
    sub_pg = PAGE_SIZE // CMP_STRIDE
    n_sub = n_pg * sub_pg
    n_steps = n_groups + 1
    while (n_steps * n_sub) % HEAD_DIM:
        n_steps += 1
    half = CMP_STRIDE * HEAD_DIM

    def page_spec(r):
        def imap(b, g, pt):
            return (pt[b * n_pages + jnp.minimum(g * n_pg + r, n_pages - 1)], 0, 0, 0)
        return pl.BlockSpec((1, PAGE_SIZE, 8, HEAD_DIM), imap)

    out_spec = pl.BlockSpec((1, N_KV, n_sub, HEAD_DIM), lambda b, g, pt: (b, 0, g, 0))
    grid_spec = pltpu.PrefetchScalarGridSpec(
        num_scalar_prefetch=1,
        grid=(B, n_steps),
        in_specs=[page_spec(r) for r in range(n_pg)] + [
            pl.BlockSpec((1, 8, HEAD_DIM), lambda b, g, pt: (b, 0, 0)),
            pl.BlockSpec((2, half, 2 * CMP_HID), lambda b, g, pt: (0, 0, 0)),
            pl.BlockSpec((2, CMP_HID, HEAD_DIM), lambda b, g, pt: (0, 0, 0)),
            pl.BlockSpec((2, 8, CMP_HID), lambda b, g, pt: (0, 0, 0)),
        ],
        out_specs=[out_spec, out_spec],
        scratch_shapes=[pltpu.VMEM((8, CMP_HID), F32)],
    )
    tok = jax.ShapeDtypeStruct((B, N_KV, n_steps * n_sub, HEAD_DIM), F32)
    return pl.pallas_call(
        functools.partial(_dec_cmp_kernel, n_pg=n_pg, n_groups=n_groups),
        grid_spec=grid_spec,
        out_shape=[tok, tok],
        compiler_params=_cp(("parallel", "arbitrary")),
        name="dec_cmp_tokens",
    )(page_table.reshape(-1), *([cache4] * n_pg), new_cmp, w1ab, w2, pe_h)


def _dec_select_kernel(q_ref, kc_ref, vc_ref, mem_ref, ocmp_ref, idx_ref, *, q_pos, nc, n_blk, n_sel):
    C = kc_ref.shape[2]
    JB = mem_ref.shape[1]
    scale = HEAD_DIM ** -0.5
    c_i = lax.broadcasted_iota(jnp.int32, (8, C), 1) - 1
    g_i = lax.broadcasted_iota(jnp.int32, (8, C), 0)
    ok = (c_i >= 0) & (c_i < nc) & (CMP_STRIDE * c_i + (2 * CMP_STRIDE - 1) <= q_pos)
    psums = []
    for n in range(N_KV):
        qb = (q_ref[0, n] * scale).astype(BF16)
        p = _masked_softmax_rows(_dot_nt(qb, kc_ref[0, n].astype(BF16)), ok)
        ocmp_ref[0, n] = _dot(p.astype(BF16), vc_ref[0, n].astype(BF16))
        psums.append(jnp.sum(jnp.where(g_i < GROUP, p, 0.0), axis=0, keepdims=True))
    psum = jnp.concatenate(psums + [jnp.zeros((8 - N_KV, C), F32)], axis=0)
    imp = sum(_dot(part, mem_ref[...]) for part in _split3(psum))
    j_i = lax.broadcasted_iota(jnp.int32, (8, JB), 1)
    cur = q_pos // SLC_BLOCK
    valid = (j_i <= cur) & (j_i < n_blk)
    forced = valid & ((j_i == 0) | (j_i == cur) | (j_i == cur - 1))
    score = jnp.where(forced, SEL_BONUS, jnp.where(valid, imp, -jnp.inf))
    j_f = j_i.astype(F32)
    taken = j_i >= n_blk
    lane_o = lax.broadcasted_iota(jnp.int32, (8, HEAD_DIM), 1)
    out = jnp.zeros((8, HEAD_DIM), F32)
    for r in range(n_sel):
        m = jnp.max(jnp.where(taken, -jnp.inf, score), axis=-1, keepdims=True)
        cand = jnp.logical_not(taken) & (score == m)
        idx = jnp.min(jnp.where(cand, j_f, float(JB)), axis=-1, keepdims=True)
        out = jnp.where(lane_o == r, idx, out)
        taken = taken | (j_f == idx)
    idx_ref[0] = out.astype(jnp.int32)


def dec_select(q8, k_cmp, v_cmp, member, *, q_pos, nc, n_blk, n_sel):
    B, _, C, _ = k_cmp.shape
    JB = member.shape[1]
    head_spec = pl.BlockSpec((1, N_KV, 8, HEAD_DIM), lambda b: (b, 0, 0, 0))
    tok_spec = pl.BlockSpec((1, N_KV, C, HEAD_DIM), lambda b: (b, 0, 0, 0))
    return pl.pallas_call(
        functools.partial(_dec_select_kernel, q_pos=q_pos, nc=nc, n_blk=n_blk, n_sel=n_sel),
        grid=(B,),
        in_specs=[head_spec, tok_spec, tok_spec, pl.BlockSpec((C, JB), lambda b: (0, 0))],
        out_specs=[head_spec, pl.BlockSpec((1, 8, HEAD_DIM), lambda b: (b, 0, 0))],
        out_shape=[
            jax.ShapeDtypeStruct((B, N_KV, 8, HEAD_DIM), F32),
            jax.ShapeDtypeStruct((B, 8, HEAD_DIM), jnp.int32),
        ],
        compiler_params=_cp(("parallel",)),
        name="dec_select",
    )(q8, k_cmp, v_cmp, member)


def _dec_attend_kernel(idx_ref, pt_ref, q_ref, gl_ref, ocmp_ref, nslc_ref, win_ref, nwin_ref, *rest,
                       q_pos, past_len, n_sel):
    del pt_ref
    blk_refs = rest[:n_sel]
    o_ref = rest[n_sel]
    b = pl.program_id(0)
    n = pl.program_id(1)
    base = (b * N_KV + n) * n_sel
    scale = HEAD_DIM ** -0.5
    past_blocks = past_len // SLC_BLOCK
    rb = SLC_BLOCK * 8
    rw = win_ref.shape[1]
    w_buf = rw // 8
    qf = q_ref[0, 0] * scale
    qb = qf.astype(BF16)
    js = [idx_ref[base + s] for s in range(n_sel)]
    has_new = js[0] == past_blocks
    for s in range(1, n_sel):
        has_new = has_new | (js[s] == past_blocks)
    new_ok = has_new & (past_len <= q_pos)

    def head_rows(x):
        slot = lax.broadcasted_iota(jnp.int32, (8, HEAD_DIM), 0)
        k = jnp.sum(jnp.where(slot == n, x, 0.0), axis=0, keepdims=True)
        v = jnp.sum(jnp.where(slot == n + N_KV, x, 0.0), axis=0, keepdims=True)
        return k, v

    def finish(m):
        return jnp.where(m == -jnp.inf, 0.0, m)

    r_s = lax.broadcasted_iota(jnp.int32, (8, rb), 1)
    t_s = r_s >> 3
    mine_s = (r_s & 7) == n

    def slc_scores(s):
        blk = blk_refs[s][0, 0].reshape(rb, HEAD_DIM)
        sc = _dot_nt(qb, blk.astype(BF16))
        ok = mine_s & (js[s] * SLC_BLOCK + t_s <= q_pos) & (js[s] < past_blocks)
        return jnp.where(ok, sc, -jnp.inf)

    k_new, v_new = head_rows(nslc_ref[0])
    s_new = jnp.where(new_ok, jnp.sum(qf * k_new, axis=-1, keepdims=True), -jnp.inf)
    scores = [slc_scores(s) for s in range(n_sel)]
    m = s_new
    for sc in scores:
        m = jnp.maximum(m, jnp.max(sc, axis=-1, keepdims=True))
    m = finish(m)
    p_new = jnp.exp(s_new - m)
    l = p_new
    acc = p_new * v_new
    for s, sc in enumerate(scores):
        p = jnp.exp(sc - m)
        l = l + jnp.sum(p, axis=-1, keepdims=True)
        acc = acc + _dot(pltpu.roll(p, N_KV, 1).astype(BF16), blk_refs[s][0, 0].reshape(rb, HEAD_DIM).astype(BF16))
    o_slc = acc / jnp.maximum(l, 1e-30)

    r_w = lax.broadcasted_iota(jnp.int32, (8, rw), 1)
    w_pos = past_len - w_buf + (r_w >> 3)
    w_ok = ((r_w & 7) == n) & (q_pos - w_pos >= 0) & (q_pos - w_pos <= WINDOW) & (w_pos >= 0)
    win = win_ref[0].astype(BF16)
    s_w = jnp.where(w_ok, _dot_nt(qb, win), -jnp.inf)
    k_nw, v_nw = head_rows(nwin_ref[0])
    nw_ok = (q_pos - past_len >= 0) and (q_pos - past_len <= WINDOW)
    s_nw = jnp.sum(qf * k_nw, axis=-1, keepdims=True) if nw_ok else jnp.full((8, 1), -jnp.inf, F32)
    m = finish(jnp.maximum(s_nw, jnp.max(s_w, axis=-1, keepdims=True)))
    p_w = jnp.exp(s_w - m)
    p_nw = jnp.exp(s_nw - m)
    l = p_nw + jnp.sum(p_w, axis=-1, keepdims=True)
    acc = p_nw * v_nw + _dot(pltpu.roll(p_w, N_KV, 1).astype(BF16), win)
    o_win = acc / jnp.maximum(l, 1e-30)

    gates = jax.nn.sigmoid(gl_ref[0, 0])
    o_ref[0, 0] = gates[:, 0:1] * ocmp_ref[0, 0] + gates[:, 1:2] * o_slc + gates[:, 2:3] * o_win


def dec_attend(idx_flat, page_table, q8, gate8, ocmp8, new_slc, win2, new_win, cache5, *, q_pos, past_len, n_sel):
    B, n_pages = page_table.shape
    rw = win2.shape[1]
    past_blocks = past_len // SLC_BLOCK
    per_page = PAGE_SIZE // SLC_BLOCK

    def blk_spec(s):
        def imap(b, n, idx, pt):
            j = jnp.minimum(idx[(b * N_KV + n) * n_sel + s], past_blocks - 1)
            return (pt[b * n_pages + j // per_page], j % per_page, 0, 1, 0)
        return pl.BlockSpec((1, 1, SLC_BLOCK, 8, HEAD_DIM), imap)

    head_spec = pl.BlockSpec((1, 1, 8, HEAD_DIM), lambda b, n, idx, pt: (b, n, 0, 0))
    tok_spec = pl.BlockSpec((1, 8, HEAD_DIM), lambda b, n, idx, pt: (b, 0, 0))
    grid_spec = pltpu.PrefetchScalarGridSpec(
        num_scalar_prefetch=2,
        grid=(B, N_KV),
        in_specs=[head_spec, head_spec, head_spec, tok_spec,
                  pl.BlockSpec((1, rw, HEAD_DIM), lambda b, n, idx, pt: (b, 0, 0)), tok_spec]
        + [blk_spec(s) for s in range(n_sel)],
        out_specs=head_spec,
    )
    return pl.pallas_call(
        functools.partial(_dec_attend_kernel, q_pos=q_pos, past_len=past_len, n_sel=n_sel),
        grid_spec=grid_spec,
        out_shape=jax.ShapeDtypeStruct((B, N_KV, 8, HEAD_DIM), F32),
        compiler_params=_cp(("parallel", "arbitrary")),
        name="dec_attend",
    )(idx_flat, page_table.reshape(-1), q8, gate8, ocmp8, new_slc, win2, new_win, *([cache5] * n_sel))


def _pad_to(a, axis, size):
    pad = [(0, 0)] * a.ndim
    pad[axis] = (0, size - a.shape[axis])
    return jnp.pad(a, pad)


def _prep_weights(w_in_a, w_in_b, b_gate, w_kv, cmp_pe, cmp_w1, cmp_w2, w_mem_kv, w_out, w_up, conv_w, conv_b,
                  w_down, hgrn_lb):
    depth = w_out.shape[0]
    n_b = w_in_b.shape[0]
    n_gate = N_MIX_HEADS * N_BRANCH
    per_kv = GROUP * N_BRANCH
    half = CMP_STRIDE * HEAD_DIM
    w1 = cmp_w1.astype(BF16)
    lb = jnp.cumsum(jax.nn.softmax(hgrn_lb.astype(F32), axis=0), axis=0)
    lb = lb - lb[0]
    layers = lambda f, n=depth: [f(l) for l in range(n)]
    gate_w = lambda j: w_in_b[j][:, MIX_DIM:MIX_DIM + n_gate].reshape(D_MODEL, N_KV, per_kv)
    return dict(
        in_a=cast_bf16(w_in_a),
        in_b=layers(lambda j: jnp.concatenate(
            [w_in_b[j][:, :MIX_DIM], w_in_b[j][:, MIX_DIM + n_gate:],
             _pad_to(gate_w(j), 2, HEAD_DIM).reshape(D_MODEL, N_KV * HEAD_DIM)], axis=-1).astype(BF16), n_b),
        b_in_b=layers(lambda j: jnp.concatenate(
            [jnp.zeros((MIX_DIM + MEM_DIM,), F32),
             _pad_to(b_gate[j].reshape(N_KV, per_kv), 1, HEAD_DIM).reshape(N_KV * HEAD_DIM)]), n_b),
        kv=cast_bf16(w_kv[None]),
        mem_kv=cast_bf16(w_mem_kv),
        out=cast_bf16(w_out),
        up=cast_split_halves(w_up, D_FF_PAD),
        conv_w=layers(lambda l: _pad_to(_pad_to(conv_w[l], 1, D_FF_PAD), 0, 8)),
        conv_b=layers(lambda l: _pad_to(conv_b[l], 0, D_FF_PAD).reshape(1, D_FF_PAD)),
        down=cast_pad_rows(w_down, D_FF_PAD),
        cmp_w1=w1,
        cmp_w1ab=jnp.concatenate([w1[:, :half], w1[:, half:]], axis=2),
        cmp_w2=cmp_w2.astype(BF16),
        cmp_pe=cmp_pe.reshape(2, 1, 2 * half),
        log_lb=jnp.log(lb),
        log1m_lb=jnp.log1p(-lb),
    )


def kernel(x_prompt, x_sample, mem_prompt, cache_nsa_kv, page_table, cache_win_kv, state_hgrn, state_conv,
           cache_mem_kv, norm_mix, norm_ffn, norm_mem, norm_kv, norm_final, w_in_a, hgrn_lb, hgrn_onorm,
           w_in_b, b_gate, w_kv, cmp_pe, cmp_w1, cmp_w2, w_mem_kv, w_out, w_up, conv_w, conv_b, w_down):
    B, L, _ = x_prompt.shape
    Bs = x_sample.shape[0]
    depth = w_out.shape[0]
    n_a = w_in_a.shape[0]
    n_mem = mem_prompt.shape[1]
    n_pool, page, _, _, _ = cache_nsa_kv.shape
    n_pages = page_table.shape[1]
    past_len = n_pages * page
    w_buf = cache_win_kv.shape[1]
    assert x_sample.shape[1] == 1 and page == PAGE_SIZE and L % 128 == 0 and L // SLC_BLOCK <= HEAD_DIM
    W = _prep_weights(w_in_a, w_in_b, b_gate, w_kv, cmp_pe, cmp_w1, cmp_w2, w_mem_kv, w_out, w_up, conv_w,
                      conv_b, w_down, hgrn_lb)
    pe_h = pe_proj(W["cmp_pe"], W["cmp_w1"])
    onorm = hgrn_onorm.reshape(n_a, 1, MIX_DIM)
    log_lb = W["log_lb"].reshape(n_a, 1, MIX_DIM)
    log1m_lb = W["log1m_lb"].reshape(n_a, 1, MIX_DIM)

    def ffn(x, l, **kw):
        return conv_ffn(x, norm_ffn[l], W["up"], W["conv_w"][l], W["conv_b"][l], W["down"], l, **kw)

    M = B * L
    mem_flat = mem_prompt.reshape(B * n_mem, D_MODEL)
    mem_kv_both = [rms_matmul_heads(mem_flat, norm_mem[l], W["mem_kv"], (l,), 2 * MEM_DIM) for l in range(depth)]
    mem_kv_p = [by_head.reshape(B, n_mem * 2 * N_MEM_HEADS, HEAD_DIM) for _, by_head in mem_kv_both]
    mem_cache = cache_mem_kv.reshape(depth, Bs, n_mem * 2 * N_MEM_HEADS, HEAD_DIM)
    x = x_prompt.reshape(M, D_MODEL)
    hgrn_p, conv_p = [], []
    for l in range(depth):
        if l < n_a:
            z3 = rms_matmul(x, norm_mix[l], W["in_a"], lead=(l,)).reshape(B, L, -1)
            s0 = jnp.zeros((B, N_MIX_HEADS, HEAD_DIM, HEAD_DIM), F32)
            o_mix, s_new = hgrn_prompt(z3, log_lb[l], log1m_lb[l], onorm[l], s0)
            hgrn_p.append(s_new)
            z_mem, mem_col = z3, 4 * MIX_DIM // MEM_DIM
        else:
            j = l - n_a
            if j == 0:
                n_row4 = 4 * N_KV * HEAD_DIM
                rows_p, rows_by_head = rms_matmul_heads(x, norm_kv, W["kv"], (0,), n_row4)
                rows3 = rows_p.reshape(B, L, -1)
                cmp_p = cmp_mlp(rows3, W["cmp_w1"], W["cmp_w2"], pe_h)
            zq3 = rms_matmul(x, norm_mix[l], W["in_b"][j], W["b_in_b"][j]).reshape(B, L, -1)
            o_mix = nsa_prompt_t(zq3, cmp_p[0], cmp_p[1], rows3)
            z_mem, mem_col = zq3, MIX_DIM // MEM_DIM
        x = out_proj_mem(o_mix.reshape(M, MIX_DIM), z_mem.reshape(M, -1), mem_col, mem_kv_p[l], L, W["out"], l, x)
        x, a_tail = ffn(x, l, seq_len=L)
        conv_p.append(a_tail.reshape(B, -1, 8, D_FF_PAD)[:, -1, 8 - (CONV_W - 1):, :D_FF])
    y_prompt = rmsnorm_rows(x, norm_final).reshape(B, L, D_MODEL)
    w_keep = min(WINDOW, L)
    nsa_rows_prompt = rows_by_head.reshape(B, L, 4, N_KV, HEAD_DIM)
    win_prompt = rows3[:, L - w_keep:, n_row4:].reshape(B, w_keep, 2, N_KV, HEAD_DIM)
    mem_kv_prompt = jnp.stack([by_head for _, by_head in mem_kv_both]).reshape(
        depth, B, n_mem, 2, N_MEM_HEADS, HEAD_DIM)

    Ms = 16
    pad_rows = lambda a: _pad_to(a, 0, Ms)
    xs = pad_rows(x_sample.reshape(Bs, D_MODEL))
    hgrn_s, conv_s = [], []
    q_pos = past_len
    t_pad = -(-(past_len + 1) // SLC_BLOCK) * SLC_BLOCK
    n_blk = t_pad // SLC_BLOCK
    nc = t_pad // CMP_STRIDE - 1
    n_sel = min(N_SELECT, n_blk)
    for l in range(depth):
        if l < n_a:
            z = rms_matmul(xs, norm_mix[l], W["in_a"], lead=(l,))
            z3 = z[:Bs].reshape(Bs, 1, -1)
            o_mix, s_new = hgrn_step(z3, log_lb[l], log1m_lb[l], onorm[l], state_hgrn[l])
            hgrn_s.append(s_new)
            o_mix = o_mix.reshape(Bs, MIX_DIM)
            o_mem = mem_attention(z3, 4 * MIX_DIM // MEM_DIM, mem_cache, lead=(l,))
        else:
            j = l - n_a
            if j == 0:
                rows_s = rms_matmul(xs, norm_kv, W["kv"], lead=(0,))[:Bs]
                new_cmp = rows_s[:, 0:1024].reshape(Bs, 8, HEAD_DIM)
                new_slc = rows_s[:, 1024:2048].reshape(Bs, 8, HEAD_DIM)
                new_win = rows_s[:, 2048:3072].reshape(Bs, 8, HEAD_DIM)
                cache4 = cache_nsa_kv.reshape(n_pool, PAGE_SIZE, 4 * N_KV, HEAD_DIM)
                cache5 = cache_nsa_kv.reshape(n_pool, PAGE_SIZE // SLC_BLOCK, SLC_BLOCK, 4 * N_KV, HEAD_DIM)
                win2 = cache_win_kv.reshape(Bs, w_buf * 2 * N_KV, HEAD_DIM)
                k_cmp_s, v_cmp_s = dec_cmp_tokens(cache4, page_table, new_cmp, W["cmp_w1ab"], W["cmp_w2"], pe_h)
                assert k_cmp_s.shape[2] > nc + 1
                ratio = SLC_BLOCK // CMP_STRIDE
                c_of_row = jnp.arange(k_cmp_s.shape[2])[:, None] - 1
                j_of_col = jnp.arange(-(-n_blk // HEAD_DIM) * HEAD_DIM)[None, :]
                member = ((c_of_row >= ratio * j_of_col - 1) & (c_of_row <= ratio * j_of_col + ratio - 1)).astype(BF16)
            zq = rms_matmul(xs, norm_mix[l], W["in_b"][j], W["b_in_b"][j])
            gt = zq[:Bs, MIX_DIM + MEM_DIM:]
            z3 = zq[:Bs].reshape(Bs, 1, -1)
            per_head = lambda a: _pad_to(a.reshape(Bs, N_KV, GROUP, -1), 2, 8)
            q8 = per_head(zq[:Bs, :MIX_DIM])
            o_cmp, idx = dec_select(q8, k_cmp_s, v_cmp_s, member, q_pos=q_pos, nc=nc, n_blk=n_blk, n_sel=n_sel)
            idx_flat = idx[:, :N_KV, :n_sel].reshape(-1)
            gate9 = gt.reshape(Bs, N_KV, HEAD_DIM)[:, :, :GROUP * N_BRANCH]
            o_all = dec_attend(idx_flat, page_table, q8, _pad_to(per_head(gate9), 3, HEAD_DIM), o_cmp,
                               new_slc, win2, new_win, cache5, q_pos=q_pos, past_len=past_len, n_sel=n_sel)
            o_mix = o_all[:, :, :GROUP].reshape(Bs, MIX_DIM).astype(BF16)
            o_mem = mem_attention(z3, MIX_DIM // MEM_DIM, mem_cache, lead=(l,))
        xs = out_proj(pad_rows(o_mix), pad_rows(o_mem.reshape(Bs, MEM_DIM)), W["out"], l, xs)
        prev = (pad_rows(_pad_to(state_conv[l][:, 1], 1, D_FF_PAD)), pad_rows(_pad_to(state_conv[l][:, 0], 1, D_FF_PAD)))
        xs, a_new = ffn(xs, l, prev=prev)
        conv_s.append(jnp.stack([state_conv[l][:, 1], a_new[0, :Bs, :D_FF]], axis=1))
    y_sample = rmsnorm_rows(xs, norm_final)[:Bs].reshape(Bs, 1, D_MODEL)
    nsa_rows_sample = rows_s[:, :n_row4].reshape(Bs, 1, 4, N_KV, HEAD_DIM)
    win_new = rows_s[:, n_row4:].reshape(Bs, 1, 2, N_KV, HEAD_DIM).astype(cache_win_kv.dtype)
    win_sample = jnp.concatenate([cache_win_kv, win_new], axis=1)[:, 1:]

    return (y_prompt, y_sample, nsa_rows_prompt, nsa_rows_sample, win_prompt, win_sample,
            jnp.stack(hgrn_p), jnp.stack(hgrn_s), jnp.stack(conv_p), jnp.stack(conv_s), mem_kv_prompt)
```

```python
import functools

import jax
import jax.numpy as jnp
from jax import lax
from jax.experimental import pallas as pl
from jax.experimental.pallas import tpu as pltpu

F32 = jnp.float32
BF16 = jnp.bfloat16

D_MODEL = 2048
HEAD_DIM = 128
MIX_DIM = 1536
N_MIX_HEADS = 12
MEM_DIM = 512
N_MEM_HEADS = 4
N_KV = 4
GROUP = 3
N_BRANCH = 3
CMP_STRIDE = 16
CMP_HID = 256
SLC_BLOCK = 64
N_SELECT = 16
WINDOW = 512
PAGE_SIZE = 128
D_FF = 5504
D_FF_PAD = 5632
CONV_W = 3
RMS_EPS = 1e-6
SEL_BONUS = 1e9
NEG_BIG = -1e30
LOG2E = 1.4426950408889634

HGRN_T = 64
HGRN_SUB = 8
HGRN_HEADS_PER_STEP = 12
FFN_TF = 512
V7X_VMEM_BYTES = 64 * 1024 * 1024
VMEM_LIMIT = V7X_VMEM_BYTES - 8 * 1024 * 1024


def _cp(sem, vmem=VMEM_LIMIT):
    return pltpu.CompilerParams(dimension_semantics=sem, vmem_limit_bytes=vmem)


def _dot(a, b):
    return jnp.dot(a, b, preferred_element_type=F32)


def _dot_nt(a, b):
    return lax.dot_general(a, b, (((1,), (1,)), ((), ())), preferred_element_type=F32)


def _dot_tn(a, b):
    return lax.dot_general(a, b, (((0,), (0,)), ((), ())), preferred_element_type=F32)


def _pick(n, cands):
    for c in cands:
        if n % c == 0:
            return c
    return n


def _rms_matmul_kernel(x_ref, g_ref, w_ref, b_ref, o_ref, xn_ref):
    @pl.when(pl.program_id(1) == 0)
    def _():
        x = x_ref[...]
        y = x * lax.rsqrt(jnp.mean(x * x, axis=-1, keepdims=True) + RMS_EPS)
        xn_ref[...] = (y * g_ref[...]).astype(BF16)

    o_ref[...] = _dot(xn_ref[...], w_ref[...]) + b_ref[...]


def _layer_spec(lead, block, imap):
    return pl.BlockSpec((None,) * len(lead) + tuple(block), lambda *a: tuple(lead) + tuple(imap(*a)))


def rms_matmul(x, g, w, bias=None, lead=()):
    M, K = x.shape
    N = w.shape[-1]
    tm = _pick(M, (1024, 512, 256, 128))
    tn = _pick(N, (512, 256, 128))
    if bias is None:
        bias = jnp.zeros((N,), F32)
    return pl.pallas_call(
        _rms_matmul_kernel,
        grid=(M // tm, N // tn),
        in_specs=[
            pl.BlockSpec((tm, K), lambda i, j: (i, 0)),
            pl.BlockSpec((1, K), lambda i, j: (0, 0)),
            _layer_spec(lead, (K, tn), lambda i, j: (0, j)),
            pl.BlockSpec((1, tn), lambda i, j: (0, j)),
        ],
        out_specs=pl.BlockSpec((tm, tn), lambda i, j: (i, j)),
        out_shape=jax.ShapeDtypeStruct((M, N), F32),
        scratch_shapes=[pltpu.VMEM((tm, K), BF16)],
        compiler_params=_cp(("parallel", "arbitrary")),
        name="rms_matmul",
    )(x, g.reshape(1, K), w, bias.reshape(1, N))


def _rms_matmul_heads_kernel(x_ref, g_ref, w_ref, o_ref, oh_ref, xn_ref, *, n_head_steps):
    j = pl.program_id(1)

    @pl.when(j == 0)
    def _():
        x = x_ref[...]
        y = x * lax.rsqrt(jnp.mean(x * x, axis=-1, keepdims=True) + RMS_EPS)
        xn_ref[...] = (y * g_ref[...]).astype(BF16)

    y = _dot(xn_ref[...], w_ref[...])
    o_ref[...] = y

    @pl.when(j < n_head_steps)
    def _():
        oh_ref[...] = y.reshape(oh_ref.shape)


def rms_matmul_heads(x, g, w, lead, head_cols):
    M, K = x.shape
    N = w.shape[-1]
    tm = _pick(M, (512, 256, 128))
    tn = 8 * HEAD_DIM
    assert N % tn == 0 and head_cols % tn == 0
    n_head_steps = head_cols // tn
    return pl.pallas_call(
        functools.partial(_rms_matmul_heads_kernel, n_head_steps=n_head_steps),
        grid=(M // tm, N // tn),
        in_specs=[
            pl.BlockSpec((tm, K), lambda i, j: (i, 0)),
            pl.BlockSpec((1, K), lambda i, j: (0, 0)),
            _layer_spec(lead, (K, tn), lambda i, j: (0, j)),
        ],
        out_specs=[
            pl.BlockSpec((tm, tn), lambda i, j: (i, j)),
            pl.BlockSpec((tm, 8, HEAD_DIM), lambda i, j: (i, jnp.minimum(j, n_head_steps - 1), 0)),
        ],
        out_shape=[
            jax.ShapeDtypeStruct((M, N), F32),
            jax.ShapeDtypeStruct((M, head_cols // HEAD_DIM, HEAD_DIM), F32),
        ],
        scratch_shapes=[pltpu.VMEM((tm, K), BF16)],
        compiler_params=_cp(("parallel", "arbitrary")),
        name="rms_matmul_heads",
    )(x, g.reshape(1, K), w)


def _rmsnorm_kernel(x_ref, g_ref, o_ref):
    x = x_ref[...]
    y = x * lax.rsqrt(jnp.mean(x * x, axis=-1, keepdims=True) + RMS_EPS)
    o_ref[...] = y * g_ref[...]


def rmsnorm_rows(x, g):
    M, K = x.shape
    tm = _pick(M, (512, 256, 128))
    return pl.pallas_call(
        _rmsnorm_kernel,
        grid=(M // tm,),
        in_specs=[pl.BlockSpec((tm, K), lambda i: (i, 0)), pl.BlockSpec((1, K), lambda i: (0, 0))],
        out_specs=pl.BlockSpec((tm, K), lambda i: (i, 0)),
        out_shape=jax.ShapeDtypeStruct((M, K), F32),
        compiler_params=_cp(("parallel",)),
        name="final_rmsnorm",
    )(x, g.reshape(1, K))


def _cast_kernel(x_ref, o_ref):
    o_ref[...] = x_ref[...].astype(o_ref.dtype)


def cast_bf16(w):
    G, R, C = w.shape
    tr = _pick(R, (256, 128))
    return pl.pallas_call(
        _cast_kernel,
        grid=(G, R // tr),
        in_specs=[pl.BlockSpec((1, tr, C), lambda g, r: (g, r, 0))],
        out_specs=pl.BlockSpec((1, tr, C), lambda g, r: (g, r, 0)),
        out_shape=jax.ShapeDtypeStruct((G, R, C), BF16),
        compiler_params=_cp(("parallel", "parallel")),
        name="cast_bf16",
    )(w)


def _cast_halves_kernel(x_ref, o_ref):
    c = x_ref.shape[2]
    o_ref[0, 0, :, 0:c] = x_ref[0].astype(o_ref.dtype)
    o_ref[0, 0, :, c:] = jnp.zeros((o_ref.shape[2], o_ref.shape[3] - c), o_ref.dtype)


def cast_split_halves(w, cols_pad):
    G, R, C2 = w.shape
    C = C2 // 2
    tr = _pick(R, (256, 128))
    return pl.pallas_call(
        _cast_halves_kernel,
        grid=(G, 2, R // tr),
        in_specs=[pl.BlockSpec((1, tr, C), lambda g, h, r: (g, r, h))],
        out_specs=pl.BlockSpec((1, 1, tr, cols_pad), lambda g, h, r: (g, h, r, 0)),
        out_shape=jax.ShapeDtypeStruct((G, 2, R, cols_pad), BF16),
        compiler_params=_cp(("parallel", "parallel", "parallel")),
        name="cast_split_halves",
    )(w)


def _cast_pad_rows_kernel(x_ref, o_ref):
    r = x_ref.shape[1]
    o_ref[0, 0:r, :] = x_ref[0].astype(o_ref.dtype)
    o_ref[0, r:, :] = jnp.zeros((o_ref.shape[1] - r, o_ref.shape[2]), o_ref.dtype)


def cast_pad_rows(w, rows_pad):
    G, R, C = w.shape
    tc = _pick(C, (256, 128))
    assert R % 16 == 0 and rows_pad % 16 == 0
    return pl.pallas_call(
        _cast_pad_rows_kernel,
        grid=(G, C // tc),
        in_specs=[pl.BlockSpec((1, R, tc), lambda g, c: (g, 0, c))],
        out_specs=pl.BlockSpec((1, rows_pad, tc), lambda g, c: (g, 0, c)),
        out_shape=jax.ShapeDtypeStruct((G, rows_pad, C), BF16),
        compiler_params=_cp(("parallel", "parallel")),
        name="cast_pad_rows",
    )(w)


def _outproj_kernel(om_ref, oe_ref, w1_ref, w2_ref, x_ref, y_ref):
    y_ref[...] = x_ref[...] + _dot(om_ref[...], w1_ref[...]) + _dot(oe_ref[...], w2_ref[...])


def out_proj(o_mix, o_mem, w_out, layer, x):
    M = x.shape[0]
    tm = _pick(M, (1024, 512, 256, 128))
    tn = 1024
    assert MIX_DIM % MEM_DIM == 0
    return pl.pallas_call(
        _outproj_kernel,
        grid=(M // tm, D_MODEL // tn),
        in_specs=[
            pl.BlockSpec((tm, MIX_DIM), lambda i, j: (i, 0)),
            pl.BlockSpec((tm, MEM_DIM), lambda i, j: (i, 0)),
            _layer_spec((layer,), (MIX_DIM, tn), lambda i, j: (0, j)),
            _layer_spec((layer,), (MEM_DIM, tn), lambda i, j: (MIX_DIM // MEM_DIM, j)),
            pl.BlockSpec((tm, tn), lambda i, j: (i, j)),
        ],
        out_specs=pl.BlockSpec((tm, tn), lambda i, j: (i, j)),
        out_shape=jax.ShapeDtypeStruct((M, D_MODEL), F32),
        compiler_params=_cp(("parallel", "parallel")),
        name="out_proj",
    )(o_mix, o_mem, w_out, w_out, x)


def _mem_attn_kernel(q_ref, kv_ref, o_ref):
    scale = HEAD_DIM ** -0.5
    slots = 2 * N_MEM_HEADS
    n_mem = kv_ref.shape[1] // slots
    for h in range(N_MEM_HEADS):
        q = q_ref[0, :, h * HEAD_DIM:(h + 1) * HEAD_DIM].astype(BF16)
        k = kv_ref[0, pl.ds(h, n_mem, stride=slots), :].astype(BF16)
        v = kv_ref[0, pl.ds(N_MEM_HEADS + h, n_mem, stride=slots), :].astype(BF16)
        s = _dot_nt(q, k) * scale
        m = jnp.max(s, axis=-1, keepdims=True)
        p = jnp.exp(s - m)
        p = p / jnp.sum(p, axis=-1, keepdims=True)
        o_ref[0, :, h * HEAD_DIM:(h + 1) * HEAD_DIM] = _dot(p.astype(BF16), v).astype(o_ref.dtype)


def mem_attention(z3, col_block, kv, lead=()):
    B, L, _ = z3.shape
    rows = kv.shape[-2]
    tm = _pick(L, (1024, 512, 256, 128))
    return pl.pallas_call(
        _mem_attn_kernel,
        grid=(B, L // tm),
        in_specs=[
            pl.BlockSpec((1, tm, MEM_DIM), lambda b, i: (b, i, col_block)),
            _layer_spec(lead, (1, rows, HEAD_DIM), lambda b, i: (b, 0, 0)),
        ],
        out_specs=pl.BlockSpec((1, tm, MEM_DIM), lambda b, i: (b, i, 0)),
        out_shape=jax.ShapeDtypeStruct((B, L, MEM_DIM), BF16),
        compiler_params=_cp(("parallel", "parallel")),
        name="mem_attention",
    )(z3, kv)


def _ffn_kernel(*refs, blocks_per_seq, decode):
    if decode:
        (x_ref, g_ref, wa_ref, wu_ref, cw_ref, cb_ref, wd_ref, p1_ref, p2_ref,
         y_ref, at_ref, xn_ref) = refs
    else:
        (x_ref, g_ref, wa_ref, wu_ref, cw_ref, cb_ref, wd_ref,
         y_ref, at_ref, xn_ref, carry_ref) = refs
    i = pl.program_id(0)
    j = pl.program_id(1)

    @pl.when(j == 0)
    def _():
        x = x_ref[...]
        y = x * lax.rsqrt(jnp.mean(x * x, axis=-1, keepdims=True) + RMS_EPS)
        xn_ref[...] = (y * g_ref[...]).astype(BF16)
        y_ref[...] = x

    if not decode:
        @pl.when((i % blocks_per_seq) == 0)
        def _():
            carry_ref[j] = jnp.zeros(carry_ref.shape[1:], F32)

    xn = xn_ref[...]
    tm = xn.shape[0]
    n_tail = at_ref.shape[1]
    half = wa_ref.shape[1] // 2
    down = None
    for cs in (slice(0, half), slice(half, 2 * half)):
        a = _dot(xn, wa_ref[:, cs])
        u = _dot(xn, wu_ref[:, cs])
        if decode:
            a1 = p1_ref[:, cs]
            a2 = p2_ref[:, cs]
        else:
            prev = carry_ref[j, :, cs]
            row = lax.broadcasted_iota(jnp.int32, a.shape, 0)
            a1 = jnp.where(row == 0, prev[7:8], pltpu.roll(a, 1, 0))
            a2 = jnp.where(row == 0, prev[6:7], jnp.where(row == 1, prev[7:8], pltpu.roll(a, 2, 0)))
            carry_ref[j, :, cs] = a[tm - 8:tm]
        at_ref[0, :, cs] = a[tm - n_tail:tm]
        c = cb_ref[:, cs] + cw_ref[2:3, cs] * a
        c = c + cw_ref[0:1, cs] * a2
        c = c + cw_ref[1:2, cs] * a1
        h = (jax.nn.gelu(c) * u).astype(BF16)
        d = _dot(h, wd_ref[cs, :])
        down = d if down is None else down + d
    y_ref[...] += down


def conv_ffn(x, g, w_up, cw, cb, w_down, layer, *, seq_len=None, prev=None):
    M = x.shape[0]
    decode = prev is not None
    tm = M if decode else _pick(seq_len, (1024, 512, 256, 128))
    n_tail = tm if decode else 8
    nf = D_FF_PAD // FFN_TF
    in_specs = [
        pl.BlockSpec((tm, D_MODEL), lambda i, j: (i, 0)),
        pl.BlockSpec((1, D_MODEL), lambda i, j: (0, 0)),
        _layer_spec((layer, 0), (D_MODEL, FFN_TF), lambda i, j: (0, j)),
        _layer_spec((layer, 1), (D_MODEL, FFN_TF), lambda i, j: (0, j)),
        pl.BlockSpec((8, FFN_TF), lambda i, j: (0, j)),
        pl.BlockSpec((1, FFN_TF), lambda i, j: (0, j)),
        _layer_spec((layer,), (FFN_TF, D_MODEL), lambda i, j: (j, 0)),
    ]
    args = [x, g.reshape(1, D_MODEL), w_up, w_up, cw, cb, w_down]
    scratch = [pltpu.VMEM((tm, D_MODEL), BF16)]
    if decode:
        in_specs += [pl.BlockSpec((tm, FFN_TF), lambda i, j: (i, j))] * 2
        args += list(prev)
    else:
        scratch.append(pltpu.VMEM((nf, 8, FFN_TF), F32))
    return pl.pallas_call(
        functools.partial(_ffn_kernel, blocks_per_seq=(1 if decode else seq_len // tm), decode=decode),
        grid=(M // tm, nf),
        in_specs=in_specs,
        out_specs=[
            pl.BlockSpec((tm, D_MODEL), lambda i, j: (i, 0)),
            pl.BlockSpec((1, n_tail, FFN_TF), lambda i, j: (i, 0, j)),
        ],
        out_shape=[
            jax.ShapeDtypeStruct((M, D_MODEL), F32),
            jax.ShapeDtypeStruct((M // tm, n_tail, D_FF_PAD), F32),
        ],
        scratch_shapes=scratch,
        compiler_params=_cp(("arbitrary", "arbitrary")),
        name="conv_ffn",
    )(*args)


def _hgrn_gates(q, f, log_lb, log1m_lb):
    qs = jax.nn.silu(q)
    log_f = jnp.logaddexp(log_lb, log1m_lb + jax.nn.log_sigmoid(f))
    k = 1.0 - jnp.exp(log_f)
    return qs, k, log_f


def _hgrn_out(o, g, onorm):
    y = o * lax.rsqrt(jnp.mean(o * o, axis=-1, keepdims=True) + RMS_EPS)
    return (y * onorm) * jax.nn.silu(g)


def _hgrn_kernel(q_ref, f_ref, i_ref, g_ref, llb_ref, l1m_ref, on_ref, s0_ref, o_ref, s_ref, st_ref,
                 *, n_chunks):
    c = pl.program_id(2)
    T = HGRN_T
    n_heads = st_ref.shape[0]

    @pl.when(c == 0)
    def _():
        for h in range(n_heads):
            st_ref[h] = s0_ref[0, h].T

    assert HGRN_SUB == 8
    rl1 = lax.broadcasted_iota(jnp.int32, (T, 1), 0) % HGRN_SUB
    t_i = lax.broadcasted_iota(jnp.int32, (T, T), 0)
    s_i = lax.broadcasted_iota(jnp.int32, (T, T), 1)
    tril = jnp.where(s_i <= t_i, 1.0, 0.0).astype(BF16)
    pair_masks = []
    size = 2 * HGRN_SUB
    while size <= T:
        same = (t_i & -size) == (s_i & -size)
        pair_masks.append((size, same & ((t_i & (size - 1)) >= size // 2) & ((s_i & (size - 1)) < size // 2)))
        size *= 2

    def shift_rows(x, d):
        return pltpu.roll(x.reshape(T // HGRN_SUB, HGRN_SUB, HEAD_DIM), d, 1).reshape(T, HEAD_DIM)

    def one_head(q, k, b, v, h, sl):
        hs = slice(h * HEAD_DIM, (h + 1) * HEAD_DIM)
        st = st_ref[h]
        o = _dot_nt((q * jnp.exp2(b)).astype(BF16), st.astype(BF16))
        o = o + jnp.sum(q * k, axis=-1, keepdims=True) * v
        for d in range(1, HGRN_SUB):
            w = jnp.exp2(b - shift_rows(b, d))
            a = jnp.sum(q * shift_rows(k, d) * w, axis=-1, keepdims=True)
            o = o + jnp.where(rl1 >= d, a, 0.0) * shift_rows(v, d)
        att = jnp.zeros((T, T), F32)
        for size, keep in pair_masks:
            half = size // 2
            refs = [jnp.broadcast_to(b[j * size + half - 1:j * size + half], (size, HEAD_DIM))
                    for j in range(T // size)]
            r = jnp.concatenate(refs, axis=0) if len(refs) > 1 else refs[0]
            qt = q * jnp.exp2(jnp.minimum(b - r, 0.0))
            kt = k * jnp.exp2(jnp.minimum(r - b, 0.0))
            att = att + jnp.where(keep, _dot_nt(qt.astype(BF16), kt.astype(BF16)), 0.0)
        o = o + _dot(att.astype(BF16), v.astype(BF16))
        bl = b[T - 1:T]
        kt = k * jnp.exp2(bl - b)
        st_ref[h] = st * jnp.exp2(bl) + _dot_tn(v.astype(BF16), kt.astype(BF16))
        o_ref[0, sl, hs] = _hgrn_out(o, g_ref[0, sl, hs], on_ref[:, hs]).astype(o_ref.dtype)

    def chunk(ci, carry):
        sl = pl.ds(pl.multiple_of(ci * T, T), T)
        q, k, lf = _hgrn_gates(q_ref[0, sl, :], f_ref[0, sl, :], llb_ref[...], l1m_ref[...])
        b = sum(_dot(tril, part) for part in _split3(lf)) * LOG2E
        for h in range(n_heads):
            hs = slice(h * HEAD_DIM, (h + 1) * HEAD_DIM)
            one_head(q[:, hs], k[:, hs], b[:, hs], i_ref[0, sl, hs], h, sl)
        return carry

    lax.fori_loop(0, n_chunks, chunk, 0)

    @pl.when(c == pl.num_programs(2) - 1)
    def _():
        for h in range(n_heads):
            s_ref[0, h] = st_ref[h].T


def hgrn_prompt(z3, log_lb, log1m_lb, onorm, s0):
    B, L, _ = z3.shape
    H = N_MIX_HEADS
    hb = HGRN_HEADS_PER_STEP
    wb = hb * HEAD_DIM
    tc = _pick(L, (512, 256, 128, 64))
    zspec = lambda k: pl.BlockSpec((1, tc, wb), lambda b, h, c: (b, c, k * (H // hb) + h))
    vspec = pl.BlockSpec((1, wb), lambda b, h, c: (0, h))
    sspec = pl.BlockSpec((1, hb, HEAD_DIM, HEAD_DIM), lambda b, h, c: (b, h, 0, 0))
    return pl.pallas_call(
        functools.partial(_hgrn_kernel, n_chunks=tc // HGRN_T),
        grid=(B, H // hb, L // tc),
        in_specs=[zspec(0), zspec(1), zspec(2), zspec(3), vspec, vspec, vspec, sspec],
        out_specs=[pl.BlockSpec((1, tc, wb), lambda b, h, c: (b, c, h)), sspec],
        out_shape=[
            jax.ShapeDtypeStruct((B, L, MIX_DIM), BF16),
            jax.ShapeDtypeStruct((B, H, HEAD_DIM, HEAD_DIM), F32),
        ],
        scratch_shapes=[pltpu.VMEM((hb, HEAD_DIM, HEAD_DIM), F32)],
        compiler_params=_cp(("parallel", "parallel", "arbitrary")),
        name="hgrn_chunked",
    )(z3, z3, z3, z3, log_lb, log1m_lb, onorm, s0)


def _hgrn_step_kernel(z_ref, llb_ref, l1m_ref, on_ref, s0_ref, o_ref, s_ref):
    for h in range(N_MIX_HEADS):
        col = lambda k: slice((k * N_MIX_HEADS + h) * HEAD_DIM, (k * N_MIX_HEADS + h + 1) * HEAD_DIM)
        hs = slice(h * HEAD_DIM, (h + 1) * HEAD_DIM)
        q, k, lf = _hgrn_gates(z_ref[0, :, col(0)], z_ref[0, :, col(1)], llb_ref[:, hs], l1m_ref[:, hs])
        v = z_ref[0, :, col(2)]
        g = z_ref[0, :, col(3)]
        rows = jnp.concatenate([q, k, jnp.exp(lf), jnp.zeros((5, HEAD_DIM), F32)], axis=0)
        cols = rows.T
        s_new = cols[:, 2:3] * s0_ref[0, h] + cols[:, 1:2] * v
        s_ref[0, h] = s_new
        o = jnp.sum(cols[:, 0:1] * s_new, axis=0, keepdims=True)
        o_ref[0, :, hs] = _hgrn_out(o, g, on_ref[:, hs]).astype(o_ref.dtype)


def hgrn_step(z3, log_lb, log1m_lb, onorm, s0):
    B = z3.shape[0]
    W = z3.shape[2]
    H = N_MIX_HEADS
    vspec = pl.BlockSpec((1, MIX_DIM), lambda b: (0, 0))
    sspec = pl.BlockSpec((1, H, HEAD_DIM, HEAD_DIM), lambda b: (b, 0, 0, 0))
    return pl.pallas_call(
        _hgrn_step_kernel,
        grid=(B,),
        in_specs=[pl.BlockSpec((1, 1, W), lambda b: (b, 0, 0)), vspec, vspec, vspec, sspec],
        out_specs=[pl.BlockSpec((1, 1, MIX_DIM), lambda b: (b, 0, 0)), sspec],
        out_shape=[
            jax.ShapeDtypeStruct((B, 1, MIX_DIM), BF16),
            jax.ShapeDtypeStruct((B, H, HEAD_DIM, HEAD_DIM), F32),
        ],
        compiler_params=_cp(("parallel",)),
        name="hgrn_step",
    )(z3, log_lb, log1m_lb, onorm, s0)


def _pe_proj_kernel(pe_ref, w_ref, o_ref):
    pe = jnp.broadcast_to(pe_ref[0], (8, pe_ref.shape[2])).astype(BF16)
    o_ref[0] = _dot(pe, w_ref[0])


def pe_proj(pe, w1):
    K = pe.shape[2]
    return pl.pallas_call(
        _pe_proj_kernel,
        grid=(2,),
        in_specs=[pl.BlockSpec((1, 1, K), lambda i: (i, 0, 0)), pl.BlockSpec((1, K, CMP_HID), lambda i: (i, 0, 0))],
        out_specs=pl.BlockSpec((1, 8, CMP_HID), lambda i: (i, 0, 0)),
        out_shape=jax.ShapeDtypeStruct((2, 8, CMP_HID), F32),
        compiler_params=_cp(("parallel",)),
        name="cmp_pe_proj",
    )(pe, w1)


def _cmp_mlp_kernel(x_ref, w1_ref, w2_ref, pe_ref, o_ref):
    n_sub = x_ref.shape[1] // CMP_STRIDE
    half = CMP_STRIDE * HEAD_DIM
    x = jnp.concatenate([x_ref[0, pl.ds(p, n_sub, stride=CMP_STRIDE), :] for p in range(CMP_STRIDE)],
                        axis=1).astype(BF16)
    a = _dot(x, w1_ref[0, 0:half])
    bm = _dot(x, w1_ref[0, half:2 * half])
    n = a.shape[0]
    h = a + pltpu.roll(bm, n - 1, 0) + pe_ref[0, 0:1]
    o_ref[0, 0] = _dot(jax.nn.gelu(h).astype(BF16), w2_ref[0])


def cmp_mlp(rows3, w1, w2, pe_h):
    B, L, _ = rows3.shape
    n_sub = L // CMP_STRIDE
    half = CMP_STRIDE * HEAD_DIM
    return pl.pallas_call(
        _cmp_mlp_kernel,
        grid=(2, B, N_KV),
        in_specs=[
            pl.BlockSpec((1, L, HEAD_DIM), lambda k, b, n: (b, 0, k * N_KV + n)),
            pl.BlockSpec((1, 2 * half, CMP_HID), lambda k, b, n: (k, 0, 0)),
            pl.BlockSpec((1, CMP_HID, HEAD_DIM), lambda k, b, n: (k, 0, 0)),
            pl.BlockSpec((1, 8, CMP_HID), lambda k, b, n: (k, 0, 0)),
        ],
        out_specs=pl.BlockSpec((1, 1, n_sub, HEAD_DIM), lambda k, b, n: (k, b * N_KV + n, 0, 0)),
        out_shape=jax.ShapeDtypeStruct((2, B * N_KV, n_sub, HEAD_DIM), F32),
        compiler_params=_cp(("parallel", "parallel", "parallel")),
        name="cmp_mlp",
    )(rows3, w1, w2, pe_h)


def _masked_softmax_rows(s, mask, exp_fn=jnp.exp):
    s = jnp.where(mask, s, -jnp.inf)
    m = jnp.max(s, axis=-1, keepdims=True)
    m = jnp.where(m == -jnp.inf, 0.0, m)
    p = exp_fn(s - m)
    return p / jnp.maximum(jnp.sum(p, axis=-1, keepdims=True), 1e-30)


def _split3(p):
    hi = p.astype(BF16)
    r = p - hi.astype(F32)
    mid = r.astype(BF16)
    lo = (r - mid.astype(F32)).astype(BF16)
    return hi, mid, lo


def _masked_softmax_cols(s, mask, exp_fn):
    s = jnp.where(mask, s, -jnp.inf)
    m = jnp.max(s, axis=0, keepdims=True)
    m = jnp.where(m == -jnp.inf, 0.0, m)
    p = exp_fn(s - m)
    return p / jnp.maximum(jnp.sum(p, axis=0, keepdims=True), 1e-30)


def _nsa_prompt_t_kernel(q_ref, gt_ref, kc_ref, vc_ref, ks_ref, vs_ref, kw_ref, vw_ref, o_ref,
                         kaug_ref, vst_ref, kwb_ref, vwt_ref, vct_ref, score_ref, *, tq, tk, tw, seq_len):
    qi = pl.program_id(2)
    n_blk = seq_len // SLC_BLOCK
    scale = HEAD_DIM ** -0.5 * LOG2E
    cols = GROUP * tq
    nc = kc_ref.shape[1]

    @pl.when(qi == 0)
    def _():
        r = lax.broadcasted_iota(jnp.int32, (seq_len, HEAD_DIM), 0) // SLC_BLOCK
        col = lax.broadcasted_iota(jnp.int32, (seq_len, HEAD_DIM), 1)
        kaug_ref[:, 0:HEAD_DIM] = ks_ref[0].astype(BF16)
        kaug_ref[:, HEAD_DIM:2 * HEAD_DIM] = jnp.where(r == col, 1.0, 0.0).astype(BF16)
        kwb_ref[...] = kw_ref[0].astype(BF16)
        for t in range(seq_len // tk):
            vst_ref[t] = vs_ref[0, t * tk:(t + 1) * tk, :].T.astype(BF16)
        for t in range(seq_len // tw):
            vwt_ref[t] = vw_ref[0, t * tw:(t + 1) * tw, :].T.astype(BF16)
        vct_ref[...] = vc_ref[0].T.astype(BF16)

    q0 = qi * tq
    qs = q_ref[0] * scale
    qb = jnp.concatenate([qs[:, g * HEAD_DIM:(g + 1) * HEAD_DIM] for g in range(GROUP)], axis=0).astype(BF16)

    def qpos(n_keys):
        return q0 + lax.broadcasted_iota(jnp.int32, (n_keys, cols), 1) % tq

    def kidx(n_keys):
        return lax.broadcasted_iota(jnp.int32, (n_keys, cols), 0)

    cend = CMP_STRIDE * kidx(nc) + (2 * CMP_STRIDE - 1)
    p_c = _masked_softmax_cols(_dot_nt(kc_ref[0].astype(BF16), qb), cend <= qpos(nc), jnp.exp2)
    o_cmp = _dot(vct_ref[...], p_c.astype(BF16))

    psum = p_c[:, 0:tq]
    for g in range(1, GROUP):
        psum = psum + p_c[:, g * tq:(g + 1) * tq]
    ji = lax.broadcasted_iota(jnp.int32, (HEAD_DIM, nc), 0)
    ci = lax.broadcasted_iota(jnp.int32, (HEAD_DIM, nc), 1)
    ratio = SLC_BLOCK // CMP_STRIDE
    member = ((ci >= ratio * ji - 1) & (ci <= ratio * ji + ratio - 1)).astype(BF16)
    imp = sum(_dot(member, part) for part in _split3(psum))
    blk = lax.broadcasted_iota(jnp.int32, (HEAD_DIM, tq), 0)
    cur = (q0 + lax.broadcasted_iota(jnp.int32, (HEAD_DIM, tq), 1)) // SLC_BLOCK
    valid = (blk <= cur) & (blk < n_blk)
    forced = valid & ((blk == 0) | (blk == cur) | (blk == cur - 1))
    score = jnp.where(forced, SEL_BONUS, jnp.where(valid, imp, -jnp.inf))
    score = score[0:n_blk]
    score_ref[...] = score
    blk = lax.broadcasted_iota(jnp.int32, (n_blk, tq), 0)
    per_tile = tq // SLC_BLOCK

    def rank_body(it, rank):
        for u in range(per_tile):
            i = it * per_tile + u
            si = score_ref[pl.ds(i, 1), :]
            later = jnp.where(blk > i, 1, 0)
            rank = rank + jnp.where(si > score, 1, jnp.where(si == score, later, 0))
        return rank

    rank = lax.fori_loop(0, qi + 1, rank_body, jnp.zeros((n_blk, tq), jnp.int32))
    bias_t = jnp.where(rank < N_SELECT, 0.0, NEG_BIG)
    if n_blk < HEAD_DIM:
        bias_t = jnp.concatenate([bias_t, jnp.zeros((HEAD_DIM - n_blk, tq), F32)], axis=0)
    bias = bias_t.T.astype(BF16)
    qaug = jnp.concatenate([qb, jnp.concatenate([bias] * GROUP, axis=0)], axis=1)

    qpos_k = qpos(tk)
    krow = kidx(tk)

    def slc_body(kt, carry, causal):
        m, l, acc = carry
        s = _dot_nt(kaug_ref[pl.ds(pl.multiple_of(kt * tk, tk), tk), :], qaug)
        if causal:
            s = jnp.where(kt * tk + krow <= qpos_k, s, -jnp.inf)
        m_new = jnp.maximum(m, jnp.max(s, axis=0, keepdims=True))
        alpha = jnp.exp2(m - m_new)
        p = jnp.exp2(s - m_new)
        l = alpha * l + jnp.sum(p, axis=0, keepdims=True)
        acc = alpha * acc + _dot(vst_ref[kt], p.astype(BF16))
        return m_new, l, acc

    init = (jnp.full((1, cols), -jnp.inf, F32), jnp.zeros((1, cols), F32), jnp.zeros((HEAD_DIM, cols), F32))
    n_full = q0 // tk

    def two_tiles(i, carry):
        return slc_body(2 * i + 1, slc_body(2 * i, carry, False), False)

    carry = lax.fori_loop(0, n_full // 2, two_tiles, init)
    carry = lax.fori_loop(n_full - n_full % 2, n_full, functools.partial(slc_body, causal=False), carry)
    _, l_s, acc_s = slc_body(n_full, carry, True)
    o_slc = acc_s / jnp.maximum(l_s, 1e-30)

    wk = min(WINDOW + tw, seq_len)
    d_rel = (lax.broadcasted_iota(jnp.int32, (wk, GROUP * tw), 1) % tw
             - lax.broadcasted_iota(jnp.int32, (wk, GROUP * tw), 0))
    win_parts = []
    for h in range(tq // tw):
        qh = q0 + h * tw
        k0 = jnp.minimum(jnp.maximum(qh - WINDOW, 0), seq_len - wk)
        qb_h = jnp.concatenate([qb[g * tq + h * tw:g * tq + (h + 1) * tw] for g in range(GROUP)], axis=0)
        d = d_rel + (qh - k0)
        s_w = _dot_nt(kwb_ref[pl.ds(pl.multiple_of(k0, tw), wk), :], qb_h)
        s_w = jnp.where((d >= 0) & (d <= WINDOW), s_w, -jnp.inf)
        p_w = jnp.exp2(s_w - jnp.max(s_w, axis=0, keepdims=True))
        l_w = jnp.sum(p_w, axis=0, keepdims=True)
        p_wb = p_w.astype(BF16)
        t0 = k0 // tw
        o_h = sum(_dot(vwt_ref[t0 + i], p_wb[i * tw:(i + 1) * tw]) for i in range(wk // tw))
        win_parts.append(o_h / jnp.maximum(l_w, 1e-30))
    o_win = jnp.concatenate([win_parts[h][:, g * tw:(g + 1) * tw]
                             for g in range(GROUP) for h in range(tq // tw)], axis=1)

    gates = jax.nn.sigmoid(gt_ref[0]).T
    for g in range(GROUP):
        cs = slice(g * tq, (g + 1) * tq)
        c0 = g * N_BRANCH
        o = (gates[c0:c0 + 1] * o_cmp[:, cs] + gates[c0 + 1:c0 + 2] * o_slc[:, cs]
             + gates[c0 + 2:c0 + 3] * o_win[:, cs])
        o_ref[0, :, g * HEAD_DIM:(g + 1) * HEAD_DIM] = o.T.astype(o_ref.dtype)


def nsa_prompt_t(zq3, k_cmp, v_cmp, rows3):
    B, L, _ = zq3.shape
    gate_block0 = (MIX_DIM + MEM_DIM) // HEAD_DIM
    tq = _pick(L, (512, 256, 128))
    tk = _pick(L, (512, 256, 128))
    tw = min(tq, 256)
    assert WINDOW % tw == 0 and tk % tq == 0 and tq % tw == 0
    n_sub = k_cmp.shape[1]
    qw = GROUP * HEAD_DIM
    rspec = lambda kind: pl.BlockSpec((1, L, HEAD_DIM), lambda b, n, i: (b, 0, kind * N_KV + n))
    cspec = pl.BlockSpec((1, n_sub, HEAD_DIM), lambda b, n, i: (b * N_KV + n, 0, 0))
    return pl.pallas_call(
        functools.partial(_nsa_prompt_t_kernel, tq=tq, tk=tk, tw=tw, seq_len=L),
        grid=(B, N_KV, L // tq),
        in_specs=[
            pl.BlockSpec((1, tq, qw), lambda b, n, i: (b, i, n)),
            pl.BlockSpec((1, tq, HEAD_DIM), lambda b, n, i: (b, i, gate_block0 + n)),
            cspec, cspec, rspec(2), rspec(3), rspec(4), rspec(5),
        ],
        out_specs=pl.BlockSpec((1, tq, qw), lambda b, n, i: (b, i, n)),
        out_shape=jax.ShapeDtypeStruct((B, L, MIX_DIM), BF16),
        scratch_shapes=[
            pltpu.VMEM((L, 2 * HEAD_DIM), BF16),
            pltpu.VMEM((L // tk, HEAD_DIM, tk), BF16),
            pltpu.VMEM((L, HEAD_DIM), BF16),
            pltpu.VMEM((L // tw, HEAD_DIM, tw), BF16),
            pltpu.VMEM((HEAD_DIM, n_sub), BF16),
            pltpu.VMEM((L // SLC_BLOCK, tq), F32),
        ],
        compiler_params=_cp(("parallel", "parallel", "arbitrary")),
        name="nsa_prompt",
    )(zq3, zq3, k_cmp, v_cmp, rows3, rows3, rows3, rows3)


def _dec_cmp_kernel(pt_ref, *refs, n_pg, n_groups):
    del pt_ref
    pg_refs = refs[:n_pg]
    new_ref, w1_ref, w2_ref, pe_ref, kc_ref, vc_ref, carry_ref = refs[n_pg:]
    g = pl.program_id(1)
    sub_pg = PAGE_SIZE // CMP_STRIDE
    n_sub = n_pg * sub_pg
    pos_per_chunk = 4
    kw = pos_per_chunk * HEAD_DIM

    @pl.when(g == 0)
    def _():
        carry_ref[...] = jnp.zeros_like(carry_ref)

    sub_i = lax.broadcasted_iota(jnp.int32, (8, n_sub, kw), 1)
    new_row = jnp.concatenate([new_ref[0], jnp.zeros((8, kw - HEAD_DIM), F32)], axis=1)[:, None, :]
    ys = [None, None]
    for pc in range(CMP_STRIDE // pos_per_chunk):
        x = jnp.concatenate(
            [jnp.concatenate([jnp.swapaxes(pg_refs[r][0, pl.ds(p, sub_pg, stride=CMP_STRIDE), :, :], 0, 1)
                              for r in range(n_pg)], axis=1)
             for p in range(pc * pos_per_chunk, (pc + 1) * pos_per_chunk)], axis=2)
        x = jnp.where(g < n_groups, x, 0.0)
        if pc == 0:
            x = jnp.where((g == n_groups) & (sub_i == 0), new_row, x)
        xb = x.astype(BF16)
        for kind in range(2):
            part = _dot(xb[kind * N_KV:(kind + 1) * N_KV].reshape(N_KV * n_sub, kw),
                        w1_ref[kind, pc * kw:(pc + 1) * kw, :])
            ys[kind] = part if ys[kind] is None else ys[kind] + part

    row = lax.broadcasted_iota(jnp.int32, (n_sub, CMP_HID), 0)
    for kind, out_ref in enumerate((kc_ref, vc_ref)):
        y = ys[kind]
        for n in range(N_KV):
            slot = kind * N_KV + n
            a = y[n * n_sub:(n + 1) * n_sub, 0:CMP_HID]
            bm = y[n * n_sub:(n + 1) * n_sub, CMP_HID:2 * CMP_HID]
            a_prev = jnp.where(row == 0, carry_ref[slot:slot + 1, :], pltpu.roll(a, 1, 0))
            carry_ref[slot:slot + 1, :] = a[n_sub - 1:n_sub]
            h = a_prev + bm + pe_ref[kind, 0:1]
            out_ref[0, n] = _dot(jax.nn.gelu(h).astype(BF16), w2_ref[kind])


def dec_cmp_tokens(cache4, page_table, new_cmp, w1ab, w2, pe_h):
    B, n_pages = page_table.shape
    n_pg = _pick(n_pages, (16, 8, 4, 2))
    assert n_pages % n_pg == 0
    n_groups = n_pages // n_pg
    sub_pg = PAGE_SIZE // CMP_STRIDE
    n_sub = n_pg * sub_pg
    n_steps = n_groups + 1
    while (n_steps * n_sub) % HEAD_DIM:
        n_steps += 1
    half = CMP_STRIDE * HEAD_DIM

    def page_spec(r):
        def imap(b, g, pt):
            return (pt[b * n_pages + jnp.minimum(g * n_pg + r, n_pages - 1)], 0, 0, 0)
        return pl.BlockSpec((1, PAGE_SIZE, 8, HEAD_DIM), imap)

    out_spec = pl.BlockSpec((1, N_KV, n_sub, HEAD_DIM), lambda b, g, pt: (b, 0, g, 0))
    grid_spec = pltpu.PrefetchScalarGridSpec(
        num_scalar_prefetch=1,
        grid=(B, n_steps),
        in_specs=[page_spec(r) for r in range(n_pg)] + [
            pl.BlockSpec((1, 8, HEAD_DIM), lambda b, g, pt: (b, 0, 0)),
            pl.BlockSpec((2, half, 2 * CMP_HID), lambda b, g, pt: (0, 0, 0)),
            pl.BlockSpec((2, CMP_HID, HEAD_DIM), lambda b, g, pt: (0, 0, 0)),
            pl.BlockSpec((2, 8, CMP_HID), lambda b, g, pt: (0, 0, 0)),
        ],
        out_specs=[out_spec, out_spec],
        scratch_shapes=[pltpu.VMEM((8, CMP_HID), F32)],
    )
    tok = jax.ShapeDtypeStruct((B, N_KV, n_steps * n_sub, HEAD_DIM), F32)
    return pl.pallas_call(
        functools.partial(_dec_cmp_kernel, n_pg=n_pg, n_groups=n_groups),
        grid_spec=grid_spec,
        out_shape=[tok, tok],
        compiler_params=_cp(("parallel", "arbitrary")),
        name="dec_cmp_tokens",
    )(page_table.reshape(-1), *([cache4] * n_pg), new_cmp, w1ab, w2, pe_h)


def _dec_select_kernel(q_ref, kc_ref, vc_ref, mem_ref, ocmp_ref, idx_ref, *, q_pos, nc, n_blk, n_sel):
    C = kc_ref.shape[2]
    JB = mem_ref.shape[1]
    scale = HEAD_DIM ** -0.5
    c_i = lax.broadcasted_iota(jnp.int32, (8, C), 1) - 1
    g_i = lax.broadcasted_iota(jnp.int32, (8, C), 0)
    ok = (c_i >= 0) & (c_i < nc) & (CMP_STRIDE * c_i + (2 * CMP_STRIDE - 1) <= q_pos)
    psums = []
    for n in range(N_KV):
        qb = (q_ref[0, n] * scale).astype(BF16)
        p = _masked_softmax_rows(_dot_nt(qb, kc_ref[0, n].astype(BF16)), ok)
        ocmp_ref[0, n] = _dot(p.astype(BF16), vc_ref[0, n].astype(BF16))
        psums.append(jnp.sum(jnp.where(g_i < GROUP, p, 0.0), axis=0, keepdims=True))
    psum = jnp.concatenate(psums + [jnp.zeros((8 - N_KV, C), F32)], axis=0)
    imp = sum(_dot(part, mem_ref[...]) for part in _split3(psum))
    j_i = lax.broadcasted_iota(jnp.int32, (8, JB), 1)
    cur = q_pos // SLC_BLOCK
    valid = (j_i <= cur) & (j_i < n_blk)
    forced = valid & ((j_i == 0) | (j_i == cur) | (j_i == cur - 1))
    score = jnp.where(forced, SEL_BONUS, jnp.where(valid, imp, -jnp.inf))
    j_f = j_i.astype(F32)
    taken = j_i >= n_blk
    lane_o = lax.broadcasted_iota(jnp.int32, (8, HEAD_DIM), 1)
    out = jnp.zeros((8, HEAD_DIM), F32)
    for r in range(n_sel):
        m = jnp.max(jnp.where(taken, -jnp.inf, score), axis=-1, keepdims=True)
        cand = jnp.logical_not(taken) & (score == m)
        idx = jnp.min(jnp.where(cand, j_f, float(JB)), axis=-1, keepdims=True)
        out = jnp.where(lane_o == r, idx, out)
        taken = taken | (j_f == idx)
    idx_ref[0] = out.astype(jnp.int32)


def dec_select(q8, k_cmp, v_cmp, member, *, q_pos, nc, n_blk, n_sel):
    B, _, C, _ = k_cmp.shape
    JB = member.shape[1]
    head_spec = pl.BlockSpec((1, N_KV, 8, HEAD_DIM), lambda b: (b, 0, 0, 0))
    tok_spec = pl.BlockSpec((1, N_KV, C, HEAD_DIM), lambda b: (b, 0, 0, 0))
    return pl.pallas_call(
        functools.partial(_dec_select_kernel, q_pos=q_pos, nc=nc, n_blk=n_blk, n_sel=n_sel),
        grid=(B,),
        in_specs=[head_spec, tok_spec, tok_spec, pl.BlockSpec((C, JB), lambda b: (0, 0))],
        out_specs=[head_spec, pl.BlockSpec((1, 8, HEAD_DIM), lambda b: (b, 0, 0))],
        out_shape=[
            jax.ShapeDtypeStruct((B, N_KV, 8, HEAD_DIM), F32),
            jax.ShapeDtypeStruct((B, 8, HEAD_DIM), jnp.int32),
        ],
        compiler_params=_cp(("parallel",)),
        name="dec_select",
    )(q8, k_cmp, v_cmp, member)


def _dec_attend_kernel(idx_ref, pt_ref, q_ref, gl_ref, ocmp_ref, nslc_ref, win_ref, nwin_ref, *rest,
                       q_pos, past_len, n_sel):
    del pt_ref
    blk_refs = rest[:n_sel]
    o_ref = rest[n_sel]
    b = pl.program_id(0)
    n = pl.program_id(1)
    base = (b * N_KV + n) * n_sel
    scale = HEAD_DIM ** -0.5
    past_blocks = past_len // SLC_BLOCK
    rb = SLC_BLOCK * 8
    rw = win_ref.shape[1]
    w_buf = rw // 8
    qf = q_ref[0, 0] * scale
    qb = qf.astype(BF16)
    js = [idx_ref[base + s] for s in range(n_sel)]
    has_new = js[0] == past_blocks
    for s in range(1, n_sel):
        has_new = has_new | (js[s] == past_blocks)
    new_ok = has_new & (past_len <= q_pos)

    def head_rows(x):
        slot = lax.broadcasted_iota(jnp.int32, (8, HEAD_DIM), 0)
        k = jnp.sum(jnp.where(slot == n, x, 0.0), axis=0, keepdims=True)
        v = jnp.sum(jnp.where(slot == n + N_KV, x, 0.0), axis=0, keepdims=True)
        return k, v

    def finish(m):
        return jnp.where(m == -jnp.inf, 0.0, m)

    r_s = lax.broadcasted_iota(jnp.int32, (8, rb), 1)
    t_s = r_s >> 3
    mine_s = (r_s & 7) == n

    def slc_scores(s):
        blk = blk_refs[s][0, 0].reshape(rb, HEAD_DIM)
        sc = _dot_nt(qb, blk.astype(BF16))
        ok = mine_s & (js[s] * SLC_BLOCK + t_s <= q_pos) & (js[s] < past_blocks)
        return jnp.where(ok, sc, -jnp.inf)

    k_new, v_new = head_rows(nslc_ref[0])
    s_new = jnp.where(new_ok, jnp.sum(qf * k_new, axis=-1, keepdims=True), -jnp.inf)
    scores = [slc_scores(s) for s in range(n_sel)]
    m = s_new
    for sc in scores:
        m = jnp.maximum(m, jnp.max(sc, axis=-1, keepdims=True))
    m = finish(m)
    p_new = jnp.exp(s_new - m)
    l = p_new
    acc = p_new * v_new
    for s, sc in enumerate(scores):
        p = jnp.exp(sc - m)
        l = l + jnp.sum(p, axis=-1, keepdims=True)
        acc = acc + _dot(pltpu.roll(p, N_KV, 1).astype(BF16), blk_refs[s][0, 0].reshape(rb, HEAD_DIM).astype(BF16))
    o_slc = acc / jnp.maximum(l, 1e-30)

    r_w = lax.broadcasted_iota(jnp.int32, (8, rw), 1)
    w_pos = past_len - w_buf + (r_w >> 3)
    w_ok = ((r_w & 7) == n) & (q_pos - w_pos >= 0) & (q_pos - w_pos <= WINDOW) & (w_pos >= 0)
    win = win_ref[0].astype(BF16)
    s_w = jnp.where(w_ok, _dot_nt(qb, win), -jnp.inf)
    k_nw, v_nw = head_rows(nwin_ref[0])
    nw_ok = (q_pos - past_len >= 0) and (q_pos - past_len <= WINDOW)
    s_nw = jnp.sum(qf * k_nw, axis=-1, keepdims=True) if nw_ok else jnp.full((8, 1), -jnp.inf, F32)
    m = finish(jnp.maximum(s_nw, jnp.max(s_w, axis=-1, keepdims=True)))
    p_w = jnp.exp(s_w - m)
    p_nw = jnp.exp(s_nw - m)
    l = p_nw + jnp.sum(p_w, axis=-1, keepdims=True)
    acc = p_nw * v_nw + _dot(pltpu.roll(p_w, N_KV, 1).astype(BF16), win)
    o_win = acc / jnp.maximum(l, 1e-30)

    gates = jax.nn.sigmoid(gl_ref[0, 0])
    o_ref[0, 0] = gates[:, 0:1] * ocmp_ref[0, 0] + gates[:, 1:2] * o_slc + gates[:, 2:3] * o_win


def dec_attend(idx_flat, page_table, q8, gate8, ocmp8, new_slc, win2, new_win, cache5, *, q_pos, past_len, n_sel):
    B, n_pages = page_table.shape
    rw = win2.shape[1]
    past_blocks = past_len // SLC_BLOCK
    per_page = PAGE_SIZE // SLC_BLOCK

    def blk_spec(s):
        def imap(b, n, idx, pt):
            j = jnp.minimum(idx[(b * N_KV + n) * n_sel + s], past_blocks - 1)
            return (pt[b * n_pages + j // per_page], j % per_page, 0, 1, 0)
        return pl.BlockSpec((1, 1, SLC_BLOCK, 8, HEAD_DIM), imap)

    head_spec = pl.BlockSpec((1, 1, 8, HEAD_DIM), lambda b, n, idx, pt: (b, n, 0, 0))
    tok_spec = pl.BlockSpec((1, 8, HEAD_DIM), lambda b, n, idx, pt: (b, 0, 0))
    grid_spec = pltpu.PrefetchScalarGridSpec(
        num_scalar_prefetch=2,
        grid=(B, N_KV),
        in_specs=[head_spec, head_spec, head_spec, tok_spec,
                  pl.BlockSpec((1, rw, HEAD_DIM), lambda b, n, idx, pt: (b, 0, 0)), tok_spec]
        + [blk_spec(s) for s in range(n_sel)],
        out_specs=head_spec,
    )
    return pl.pallas_call(
        functools.partial(_dec_attend_kernel, q_pos=q_pos, past_len=past_len, n_sel=n_sel),
        grid_spec=grid_spec,
        out_shape=jax.ShapeDtypeStruct((B, N_KV, 8, HEAD_DIM), F32),
        compiler_params=_cp(("parallel", "arbitrary")),
        name="dec_attend",
    )(idx_flat, page_table.reshape(-1), q8, gate8, ocmp8, new_slc, win2, new_win, *([cache5] * n_sel))


def _pad_to(a, axis, size):
    pad = [(0, 0)] * a.ndim
    pad[axis] = (0, size - a.shape[axis])
    return jnp.pad(a, pad)


def _prep_weights(w_in_a, w_in_b, b_gate, w_kv, cmp_pe, cmp_w1, cmp_w2, w_mem_kv, w_out, w_up, conv_w, conv_b,
                  w_down, hgrn_lb):
    depth = w_out.shape[0]
    n_b = w_in_b.shape[0]
    n_gate = N_MIX_HEADS * N_BRANCH
    per_kv = GROUP * N_BRANCH
    half = CMP_STRIDE * HEAD_DIM
    w1 = cmp_w1.astype(BF16)
    lb = jnp.cumsum(jax.nn.softmax(hgrn_lb.astype(F32), axis=0), axis=0)
    lb = lb - lb[0]
    layers = lambda f, n=depth: [f(l) for l in range(n)]
    gate_w = lambda j: w_in_b[j][:, MIX_DIM:MIX_DIM + n_gate].reshape(D_MODEL, N_KV, per_kv)
    return dict(
        in_a=cast_bf16(w_in_a),
        in_b=layers(lambda j: jnp.concatenate(
            [w_in_b[j][:, :MIX_DIM], w_in_b[j][:, MIX_DIM + n_gate:],
             _pad_to(gate_w(j), 2, HEAD_DIM).reshape(D_MODEL, N_KV * HEAD_DIM)], axis=-1).astype(BF16), n_b),
        b_in_b=layers(lambda j: jnp.concatenate(
            [jnp.zeros((MIX_DIM + MEM_DIM,), F32),
             _pad_to(b_gate[j].reshape(N_KV, per_kv), 1, HEAD_DIM).reshape(N_KV * HEAD_DIM)]), n_b),
        kv=cast_bf16(w_kv[None]),
        mem_kv=cast_bf16(w_mem_kv),
        out=cast_bf16(w_out),
        up=cast_split_halves(w_up, D_FF_PAD),
        conv_w=layers(lambda l: _pad_to(_pad_to(conv_w[l], 1, D_FF_PAD), 0, 8)),
        conv_b=layers(lambda l: _pad_to(conv_b[l], 0, D_FF_PAD).reshape(1, D_FF_PAD)),
        down=cast_pad_rows(w_down, D_FF_PAD),
        cmp_w1=w1,
        cmp_w1ab=jnp.concatenate([w1[:, :half], w1[:, half:]], axis=2),
        cmp_w2=cmp_w2.astype(BF16),
        cmp_pe=cmp_pe.reshape(2, 1, 2 * half),
        log_lb=jnp.log(lb),
        log1m_lb=jnp.log1p(-lb),
    )


def kernel(x_prompt, x_sample, mem_prompt, cache_nsa_kv, page_table, cache_win_kv, state_hgrn, state_conv,
           cache_mem_kv, norm_mix, norm_ffn, norm_mem, norm_kv, norm_final, w_in_a, hgrn_lb, hgrn_onorm,
           w_in_b, b_gate, w_kv, cmp_pe, cmp_w1, cmp_w2, w_mem_kv, w_out, w_up, conv_w, conv_b, w_down):
    B, L, _ = x_prompt.shape
    Bs = x_sample.shape[0]
    depth = w_out.shape[0]
    n_a = w_in_a.shape[0]
    n_mem = mem_prompt.shape[1]
    n_pool, page, _, _, _ = cache_nsa_kv.shape
    n_pages = page_table.shape[1]
    past_len = n_pages * page
    w_buf = cache_win_kv.shape[1]
    assert x_sample.shape[1] == 1 and page == PAGE_SIZE and L % 128 == 0 and L // SLC_BLOCK <= HEAD_DIM
    W = _prep_weights(w_in_a, w_in_b, b_gate, w_kv, cmp_pe, cmp_w1, cmp_w2, w_mem_kv, w_out, w_up, conv_w,
                      conv_b, w_down, hgrn_lb)
    pe_h = pe_proj(W["cmp_pe"], W["cmp_w1"])
    onorm = hgrn_onorm.reshape(n_a, 1, MIX_DIM)
    log_lb = W["log_lb"].reshape(n_a, 1, MIX_DIM)
    log1m_lb = W["log1m_lb"].reshape(n_a, 1, MIX_DIM)

    def ffn(x, l, **kw):
        return conv_ffn(x, norm_ffn[l], W["up"], W["conv_w"][l], W["conv_b"][l], W["down"], l, **kw)

    M = B * L
    mem_flat = mem_prompt.reshape(B * n_mem, D_MODEL)
    mem_kv_both = [rms_matmul_heads(mem_flat, norm_mem[l], W["mem_kv"], (l,), 2 * MEM_DIM) for l in range(depth)]
    mem_kv_p = [by_head.reshape(B, n_mem * 2 * N_MEM_HEADS, HEAD_DIM) for _, by_head in mem_kv_both]
    mem_cache = cache_mem_kv.reshape(depth, Bs, n_mem * 2 * N_MEM_HEADS, HEAD_DIM)
    x = x_prompt.reshape(M, D_MODEL)
    hgrn_p, conv_p = [], []
    for l in range(depth):
        if l < n_a:
            z3 = rms_matmul(x, norm_mix[l], W["in_a"], lead=(l,)).reshape(B, L, -1)
            s0 = jnp.zeros((B, N_MIX_HEADS, HEAD_DIM, HEAD_DIM), F32)
            o_mix, s_new = hgrn_prompt(z3, log_lb[l], log1m_lb[l], onorm[l], s0)
            hgrn_p.append(s_new)
            o_mem = mem_attention(z3, 4 * MIX_DIM // MEM_DIM, mem_kv_p[l])
        else:
            j = l - n_a
            if j == 0:
                n_row4 = 4 * N_KV * HEAD_DIM
                rows_p, rows_by_head = rms_matmul_heads(x, norm_kv, W["kv"], (0,), n_row4)
                rows3 = rows_p.reshape(B, L, -1)
                cmp_p = cmp_mlp(rows3, W["cmp_w1"], W["cmp_w2"], pe_h)
            zq3 = rms_matmul(x, norm_mix[l], W["in_b"][j], W["b_in_b"][j]).reshape(B, L, -1)
            o_mix = nsa_prompt_t(zq3, cmp_p[0], cmp_p[1], rows3)
            o_mem = mem_attention(zq3, MIX_DIM // MEM_DIM, mem_kv_p[l])
        x = out_proj(o_mix.reshape(M, MIX_DIM), o_mem.reshape(M, MEM_DIM), W["out"], l, x)
        x, a_tail = ffn(x, l, seq_len=L)
        conv_p.append(a_tail.reshape(B, -1, 8, D_FF_PAD)[:, -1, 8 - (CONV_W - 1):, :D_FF])
    y_prompt = rmsnorm_rows(x, norm_final).reshape(B, L, D_MODEL)
    w_keep = min(WINDOW, L)
    nsa_rows_prompt = rows_by_head.reshape(B, L, 4, N_KV, HEAD_DIM)
    win_prompt = rows3[:, L - w_keep:, n_row4:].reshape(B, w_keep, 2, N_KV, HEAD_DIM)
    mem_kv_prompt = jnp.stack([by_head for _, by_head in mem_kv_both]).reshape(
        depth, B, n_mem, 2, N_MEM_HEADS, HEAD_DIM)

    Ms = 16
    pad_rows = lambda a: _pad_to(a, 0, Ms)
    xs = pad_rows(x_sample.reshape(Bs, D_MODEL))
    hgrn_s, conv_s = [], []
    q_pos = past_len
    t_pad = -(-(past_len + 1) // SLC_BLOCK) * SLC_BLOCK
    n_blk = t_pad // SLC_BLOCK
    nc = t_pad // CMP_STRIDE - 1
    n_sel = min(N_SELECT, n_blk)
    for l in range(depth):
        if l < n_a:
            z = rms_matmul(xs, norm_mix[l], W["in_a"], lead=(l,))
            z3 = z[:Bs].reshape(Bs, 1, -1)
            o_mix, s_new = hgrn_step(z3, log_lb[l], log1m_lb[l], onorm[l], state_hgrn[l])
            hgrn_s.append(s_new)
            o_mix = o_mix.reshape(Bs, MIX_DIM)
            o_mem = mem_attention(z3, 4 * MIX_DIM // MEM_DIM, mem_cache, lead=(l,))
        else:
            j = l - n_a
            if j == 0:
                rows_s = rms_matmul(xs, norm_kv, W["kv"], lead=(0,))[:Bs]
                new_cmp = rows_s[:, 0:1024].reshape(Bs, 8, HEAD_DIM)
                new_slc = rows_s[:, 1024:2048].reshape(Bs, 8, HEAD_DIM)
                new_win = rows_s[:, 2048:3072].reshape(Bs, 8, HEAD_DIM)
                cache4 = cache_nsa_kv.reshape(n_pool, PAGE_SIZE, 4 * N_KV, HEAD_DIM)
                cache5 = cache_nsa_kv.reshape(n_pool, PAGE_SIZE // SLC_BLOCK, SLC_BLOCK, 4 * N_KV, HEAD_DIM)
                win2 = cache_win_kv.reshape(Bs, w_buf * 2 * N_KV, HEAD_DIM)
                k_cmp_s, v_cmp_s = dec_cmp_tokens(cache4, page_table, new_cmp, W["cmp_w1ab"], W["cmp_w2"], pe_h)
                assert k_cmp_s.shape[2] > nc + 1
                ratio = SLC_BLOCK // CMP_STRIDE
                c_of_row = jnp.arange(k_cmp_s.shape[2])[:, None] - 1
                j_of_col = jnp.arange(-(-n_blk // HEAD_DIM) * HEAD_DIM)[None, :]
                member = ((c_of_row >= ratio * j_of_col - 1) & (c_of_row <= ratio * j_of_col + ratio - 1)).astype(BF16)
            zq = rms_matmul(xs, norm_mix[l], W["in_b"][j], W["b_in_b"][j])
            gt = zq[:Bs, MIX_DIM + MEM_DIM:]
            z3 = zq[:Bs].reshape(Bs, 1, -1)
            per_head = lambda a: _pad_to(a.reshape(Bs, N_KV, GROUP, -1), 2, 8)
            q8 = per_head(zq[:Bs, :MIX_DIM])
            o_cmp, idx = dec_select(q8, k_cmp_s, v_cmp_s, member, q_pos=q_pos, nc=nc, n_blk=n_blk, n_sel=n_sel)
            idx_flat = idx[:, :N_KV, :n_sel].reshape(-1)
            gate9 = gt.reshape(Bs, N_KV, HEAD_DIM)[:, :, :GROUP * N_BRANCH]
            o_all = dec_attend(idx_flat, page_table, q8, _pad_to(per_head(gate9), 3, HEAD_DIM), o_cmp,
                               new_slc, win2, new_win, cache5, q_pos=q_pos, past_len=past_len, n_sel=n_sel)
            o_mix = o_all[:, :, :GROUP].reshape(Bs, MIX_DIM).astype(BF16)
            o_mem = mem_attention(z3, MIX_DIM // MEM_DIM, mem_cache, lead=(l,))
        xs = out_proj(pad_rows(o_mix), pad_rows(o_mem.reshape(Bs, MEM_DIM)), W["out"], l, xs)
        prev = (pad_rows(_pad_to(state_conv[l][:, 1], 1, D_FF_PAD)), pad_rows(_pad_to(state_conv[l][:, 0], 1, D_FF_PAD)))
        xs, a_new = ffn(xs, l, prev=prev)
        conv_s.append(jnp.stack([state_conv[l][:, 1], a_new[0, :Bs, :D_FF]], axis=1))
    y_sample = rmsnorm_rows(xs, norm_final)[:Bs].reshape(Bs, 1, D_MODEL)
    nsa_rows_sample = rows_s[:, :n_row4].reshape(Bs, 1, 4, N_KV, HEAD_DIM)
    win_new = rows_s[:, n_row4:].reshape(Bs, 1, 2, N_KV, HEAD_DIM).astype(cache_win_kv.dtype)
    win_sample = jnp.concatenate([cache_win_kv, win_new], axis=1)[:, 1:]

    return (y_prompt, y_sample, nsa_rows_prompt, nsa_rows_sample, win_prompt, win_sample,
            jnp.stack(hgrn_p), jnp.stack(hgrn_s), jnp.stack(conv_p), jnp.stack(conv_s), mem_kv_prompt)
```

```python
import functools

import jax
import jax.numpy as jnp
from jax import lax
from jax.experimental import pallas as pl
from jax.experimental.pallas import tpu as pltpu

F32 = jnp.float32
BF16 = jnp.bfloat16

D_MODEL = 2048
HEAD_DIM = 128
MIX_DIM = 1536
N_MIX_HEADS = 12
MEM_DIM = 512
N_MEM_HEADS = 4
N_KV = 4
GROUP = 3
N_BRANCH = 3
CMP_STRIDE = 16
CMP_HID = 256
SLC_BLOCK = 64
N_SELECT = 16
WINDOW = 512
PAGE_SIZE = 128
D_FF = 5504
D_FF_PAD = 5632
CONV_W = 3
RMS_EPS = 1e-6
SEL_BONUS = 1e9
NEG_BIG = -1e30
LOG2E = 1.4426950408889634

HGRN_T = 64
HGRN_SUB = 8
HGRN_HEADS_PER_STEP = 12
FFN_TF = 512
V7X_VMEM_BYTES = 64 * 1024 * 1024
VMEM_LIMIT = V7X_VMEM_BYTES - 8 * 1024 * 1024


def _cp(sem, vmem=VMEM_LIMIT):
    return pltpu.CompilerParams(dimension_semantics=sem, vmem_limit_bytes=vmem)


def _dot(a, b):
    return jnp.dot(a, b, preferred_element_type=F32)


def _dot_nt(a, b):
    return lax.dot_general(a, b, (((1,), (1,)), ((), ())), preferred_element_type=F32)


def _dot_tn(a, b):
    return lax.dot_general(a, b, (((0,), (0,)), ((), ())), preferred_element_type=F32)


def _pick(n, cands):
    for c in cands:
        if n % c == 0:
            return c
    return n


def _rms_matmul_kernel(x_ref, g_ref, w_ref, b_ref, o_ref, xn_ref):
    @pl.when(pl.program_id(1) == 0)
    def _():
        x = x_ref[...]
        y = x * lax.rsqrt(jnp.mean(x * x, axis=-1, keepdims=True) + RMS_EPS)
        xn_ref[...] = (y * g_ref[...]).astype(BF16)

    o_ref[...] = _dot(xn_ref[...], w_ref[...]) + b_ref[...]


def _layer_spec(lead, block, imap):
    return pl.BlockSpec((None,) * len(lead) + tuple(block), lambda *a: tuple(lead) + tuple(imap(*a)))


def rms_matmul(x, g, w, bias=None, lead=()):
    M, K = x.shape
    N = w.shape[-1]
    tm = _pick(M, (1024, 512, 256, 128))
    tn = _pick(N, (512, 256, 128))
    if bias is None:
        bias = jnp.zeros((N,), F32)
    return pl.pallas_call(
        _rms_matmul_kernel,
        grid=(M // tm, N // tn),
        in_specs=[
            pl.BlockSpec((tm, K), lambda i, j: (i, 0)),
            pl.BlockSpec((1, K), lambda i, j: (0, 0)),
            _layer_spec(lead, (K, tn), lambda i, j: (0, j)),
            pl.BlockSpec((1, tn), lambda i, j: (0, j)),
        ],
        out_specs=pl.BlockSpec((tm, tn), lambda i, j: (i, j)),
        out_shape=jax.ShapeDtypeStruct((M, N), F32),
        scratch_shapes=[pltpu.VMEM((tm, K), BF16)],
        compiler_params=_cp(("parallel", "arbitrary")),
        name="rms_matmul",
    )(x, g.reshape(1, K), w, bias.reshape(1, N))


def _rms_matmul_heads_kernel(x_ref, g_ref, w_ref, o_ref, oh_ref, xn_ref, *, n_head_steps):
    j = pl.program_id(1)

    @pl.when(j == 0)
    def _():
        x = x_ref[...]
        y = x * lax.rsqrt(jnp.mean(x * x, axis=-1, keepdims=True) + RMS_EPS)
        xn_ref[...] = (y * g_ref[...]).astype(BF16)

    y = _dot(xn_ref[...], w_ref[...])
    o_ref[...] = y

    @pl.when(j < n_head_steps)
    def _():
        oh_ref[...] = y.reshape(oh_ref.shape)


def rms_matmul_heads(x, g, w, lead, head_cols):
    M, K = x.shape
    N = w.shape[-1]
    tm = _pick(M, (512, 256, 128))
    tn = 8 * HEAD_DIM
    assert N % tn == 0 and head_cols % tn == 0
    n_head_steps = head_cols // tn
    return pl.pallas_call(
        functools.partial(_rms_matmul_heads_kernel, n_head_steps=n_head_steps),
        grid=(M // tm, N // tn),
        in_specs=[
            pl.BlockSpec((tm, K), lambda i, j: (i, 0)),
            pl.BlockSpec((1, K), lambda i, j: (0, 0)),
            _layer_spec(lead, (K, tn), lambda i, j: (0, j)),
        ],
        out_specs=[
            pl.BlockSpec((tm, tn), lambda i, j: (i, j)),
            pl.BlockSpec((tm, 8, HEAD_DIM), lambda i, j: (i, jnp.minimum(j, n_head_steps - 1), 0)),
        ],
        out_shape=[
            jax.ShapeDtypeStruct((M, N), F32),
            jax.ShapeDtypeStruct((M, head_cols // HEAD_DIM, HEAD_DIM), F32),
        ],
        scratch_shapes=[pltpu.VMEM((tm, K), BF16)],
        compiler_params=_cp(("parallel", "arbitrary")),
        name="rms_matmul_heads",
    )(x, g.reshape(1, K), w)


def _rmsnorm_kernel(x_ref, g_ref, o_ref):
    x = x_ref[...]
    y = x * lax.rsqrt(jnp.mean(x * x, axis=-1, keepdims=True) + RMS_EPS)
    o_ref[...] = y * g_ref[...]


def rmsnorm_rows(x, g):
    M, K = x.shape
    tm = _pick(M, (512, 256, 128))
    return pl.pallas_call(
        _rmsnorm_kernel,
        grid=(M // tm,),
        in_specs=[pl.BlockSpec((tm, K), lambda i: (i, 0)), pl.BlockSpec((1, K), lambda i: (0, 0))],
        out_specs=pl.BlockSpec((tm, K), lambda i: (i, 0)),
        out_shape=jax.ShapeDtypeStruct((M, K), F32),
        compiler_params=_cp(("parallel",)),
        name="final_rmsnorm",
    )(x, g.reshape(1, K))


def _cast_kernel(x_ref, o_ref):
    o_ref[...] = x_ref[...].astype(o_ref.dtype)


def cast_bf16(w):
    G, R, C = w.shape
    tr = _pick(R, (256, 128))
    return pl.pallas_call(
        _cast_kernel,
        grid=(G, R // tr),
        in_specs=[pl.BlockSpec((1, tr, C), lambda g, r: (g, r, 0))],
        out_specs=pl.BlockSpec((1, tr, C), lambda g, r: (g, r, 0)),
        out_shape=jax.ShapeDtypeStruct((G, R, C), BF16),
        compiler_params=_cp(("parallel", "parallel")),
        name="cast_bf16",
    )(w)


def _cast_halves_kernel(x_ref, o_ref):
    c = x_ref.shape[2]
    o_ref[0, 0, :, 0:c] = x_ref[0].astype(o_ref.dtype)
    o_ref[0, 0, :, c:] = jnp.zeros((o_ref.shape[2], o_ref.shape[3] - c), o_ref.dtype)


def cast_split_halves(w, cols_pad):
    G, R, C2 = w.shape
    C = C2 // 2
    tr = _pick(R, (256, 128))
    return pl.pallas_call(
        _cast_halves_kernel,
        grid=(G, 2, R // tr),
        in_specs=[pl.BlockSpec((1, tr, C), lambda g, h, r: (g, r, h))],
        out_specs=pl.BlockSpec((1, 1, tr, cols_pad), lambda g, h, r: (g, h, r, 0)),
        out_shape=jax.ShapeDtypeStruct((G, 2, R, cols_pad), BF16),
        compiler_params=_cp(("parallel", "parallel", "parallel")),
        name="cast_split_halves",
    )(w)


def _cast_pad_rows_kernel(x_ref, o_ref):
    r = x_ref.shape[1]
    o_ref[0, 0:r, :] = x_ref[0].astype(o_ref.dtype)
    o_ref[0, r:, :] = jnp.zeros((o_ref.shape[1] - r, o_ref.shape[2]), o_ref.dtype)


def cast_pad_rows(w, rows_pad):
    G, R, C = w.shape
    tc = _pick(C, (256, 128))
    assert R % 16 == 0 and rows_pad % 16 == 0
    return pl.pallas_call(
        _cast_pad_rows_kernel,
        grid=(G, C // tc),
        in_specs=[pl.BlockSpec((1, R, tc), lambda g, c: (g, 0, c))],
        out_specs=pl.BlockSpec((1, rows_pad, tc), lambda g, c: (g, 0, c)),
        out_shape=jax.ShapeDtypeStruct((G, rows_pad, C), BF16),
        compiler_params=_cp(("parallel", "parallel")),
        name="cast_pad_rows",
    )(w)


def _outproj_kernel(om_ref, oe_ref, w1_ref, w2_ref, x_ref, y_ref):
    y_ref[...] = x_ref[...] + _dot(om_ref[...], w1_ref[...]) + _dot(oe_ref[...], w2_ref[...])


def out_proj(o_mix, o_mem, w_out, layer, x):
    M = x.shape[0]
    tm = _pick(M, (512, 256, 128))
    tn = D_MODEL
    assert MIX_DIM % MEM_DIM == 0
    return pl.pallas_call(
        _outproj_kernel,
        grid=(M // tm, D_MODEL // tn),
        in_specs=[
            pl.BlockSpec((tm, MIX_DIM), lambda i, j: (i, 0)),
            pl.BlockSpec((tm, MEM_DIM), lambda i, j: (i, 0)),
            _layer_spec((layer,), (MIX_DIM, tn), lambda i, j: (0, j)),
            _layer_spec((layer,), (MEM_DIM, tn), lambda i, j: (MIX_DIM // MEM_DIM, j)),
            pl.BlockSpec((tm, tn), lambda i, j: (i, j)),
        ],
        out_specs=pl.BlockSpec((tm, tn), lambda i, j: (i, j)),
        out_shape=jax.ShapeDtypeStruct((M, D_MODEL), F32),
        compiler_params=_cp(("parallel", "parallel")),
        name="out_proj",
    )(o_mix, o_mem, w_out, w_out, x)


def _mem_attn_kernel(q_ref, kv_ref, o_ref):
    scale = HEAD_DIM ** -0.5
    slots = 2 * N_MEM_HEADS
    n_mem = kv_ref.shape[1] // slots
    for h in range(N_MEM_HEADS):
        q = q_ref[0, :, h * HEAD_DIM:(h + 1) * HEAD_DIM].astype(BF16)
        k = kv_ref[0, pl.ds(h, n_mem, stride=slots), :].astype(BF16)
        v = kv_ref[0, pl.ds(N_MEM_HEADS + h, n_mem, stride=slots), :].astype(BF16)
        s = _dot_nt(q, k) * scale
        m = jnp.max(s, axis=-1, keepdims=True)
        p = jnp.exp(s - m)
        p = p / jnp.sum(p, axis=-1, keepdims=True)
        o_ref[0, :, h * HEAD_DIM:(h + 1) * HEAD_DIM] = _dot(p.astype(BF16), v).astype(o_ref.dtype)


def mem_attention(z3, col_block, kv, lead=()):
    B, L, _ = z3.shape
    rows = kv.shape[-2]
    tm = _pick(L, (1024, 512, 256, 128))
    return pl.pallas_call(
        _mem_attn_kernel,
        grid=(B, L // tm),
        in_specs=[
            pl.BlockSpec((1, tm, MEM_DIM), lambda b, i: (b, i, col_block)),
            _layer_spec(lead, (1, rows, HEAD_DIM), lambda b, i: (b, 0, 0)),
        ],
        out_specs=pl.BlockSpec((1, tm, MEM_DIM), lambda b, i: (b, i, 0)),
        out_shape=jax.ShapeDtypeStruct((B, L, MEM_DIM), BF16),
        compiler_params=_cp(("parallel", "parallel")),
        name="mem_attention",
    )(z3, kv)


def _ffn_kernel(*refs, blocks_per_seq, decode):
    if decode:
        (x_ref, g_ref, wa_ref, wu_ref, cw_ref, cb_ref, wd_ref, p1_ref, p2_ref,
         y_ref, at_ref, xn_ref) = refs
    else:
        (x_ref, g_ref, wa_ref, wu_ref, cw_ref, cb_ref, wd_ref,
         y_ref, at_ref, xn_ref, carry_ref) = refs
    i = pl.program_id(0)
    j = pl.program_id(1)

    @pl.when(j == 0)
    def _():
        x = x_ref[...]
        y = x * lax.rsqrt(jnp.mean(x * x, axis=-1, keepdims=True) + RMS_EPS)
        xn_ref[...] = (y * g_ref[...]).astype(BF16)
        y_ref[...] = x

    if not decode:
        @pl.when((i % blocks_per_seq) == 0)
        def _():
            carry_ref[j] = jnp.zeros(carry_ref.shape[1:], F32)

    xn = xn_ref[...]
    tm = xn.shape[0]
    n_tail = at_ref.shape[1]
    half = wa_ref.shape[1] // 2
    down = None
    for cs in (slice(0, half), slice(half, 2 * half)):
        a = _dot(xn, wa_ref[:, cs])
        u = _dot(xn, wu_ref[:, cs])
        if decode:
            a1 = p1_ref[:, cs]
            a2 = p2_ref[:, cs]
        else:
            prev = carry_ref[j, :, cs]
            row = lax.broadcasted_iota(jnp.int32, a.shape, 0)
            a1 = jnp.where(row == 0, prev[7:8], pltpu.roll(a, 1, 0))
            a2 = jnp.where(row == 0, prev[6:7], jnp.where(row == 1, prev[7:8], pltpu.roll(a, 2, 0)))
            carry_ref[j, :, cs] = a[tm - 8:tm]
        at_ref[0, :, cs] = a[tm - n_tail:tm]
        c = cb_ref[:, cs] + cw_ref[2:3, cs] * a
        c = c + cw_ref[0:1, cs] * a2
        c = c + cw_ref[1:2, cs] * a1
        h = (jax.nn.gelu(c) * u).astype(BF16)
        d = _dot(h, wd_ref[cs, :])
        down = d if down is None else down + d
    y_ref[...] += down


def conv_ffn(x, g, w_up, cw, cb, w_down, layer, *, seq_len=None, prev=None):
    M = x.shape[0]
    decode = prev is not None
    tm = M if decode else _pick(seq_len, (1024, 512, 256, 128))
    n_tail = tm if decode else 8
    nf = D_FF_PAD // FFN_TF
    in_specs = [
        pl.BlockSpec((tm, D_MODEL), lambda i, j: (i, 0)),
        pl.BlockSpec((1, D_MODEL), lambda i, j: (0, 0)),
        _layer_spec((layer, 0), (D_MODEL, FFN_TF), lambda i, j: (0, j)),
        _layer_spec((layer, 1), (D_MODEL, FFN_TF), lambda i, j: (0, j)),
        pl.BlockSpec((8, FFN_TF), lambda i, j: (0, j)),
        pl.BlockSpec((1, FFN_TF), lambda i, j: (0, j)),
        _layer_spec((layer,), (FFN_TF, D_MODEL), lambda i, j: (j, 0)),
    ]
    args = [x, g.reshape(1, D_MODEL), w_up, w_up, cw, cb, w_down]
    scratch = [pltpu.VMEM((tm, D_MODEL), BF16)]
    if decode:
        in_specs += [pl.BlockSpec((tm, FFN_TF), lambda i, j: (i, j))] * 2
        args += list(prev)
    else:
        scratch.append(pltpu.VMEM((nf, 8, FFN_TF), F32))
    return pl.pallas_call(
        functools.partial(_ffn_kernel, blocks_per_seq=(1 if decode else seq_len // tm), decode=decode),
        grid=(M // tm, nf),
        in_specs=in_specs,
        out_specs=[
            pl.BlockSpec((tm, D_MODEL), lambda i, j: (i, 0)),
            pl.BlockSpec((1, n_tail, FFN_TF), lambda i, j: (i, 0, j)),
        ],
        out_shape=[
            jax.ShapeDtypeStruct((M, D_MODEL), F32),
            jax.ShapeDtypeStruct((M // tm, n_tail, D_FF_PAD), F32),
        ],
        scratch_shapes=scratch,
        compiler_params=_cp(("arbitrary", "arbitrary")),
        name="conv_ffn",
    )(*args)


def _hgrn_gates(q, f, log_lb, log1m_lb):
    qs = jax.nn.silu(q)
    log_f = jnp.logaddexp(log_lb, log1m_lb + jax.nn.log_sigmoid(f))
    k = 1.0 - jnp.exp(log_f)
    return qs, k, log_f


def _hgrn_out(o, g, onorm):
    y = o * lax.rsqrt(jnp.mean(o * o, axis=-1, keepdims=True) + RMS_EPS)
    return (y * onorm) * jax.nn.silu(g)


def _hgrn_kernel(q_ref, f_ref, i_ref, g_ref, llb_ref, l1m_ref, on_ref, s0_ref, o_ref, s_ref, st_ref,
                 *, n_chunks):
    c = pl.program_id(2)
    T = HGRN_T
    n_heads = st_ref.shape[0]

    @pl.when(c == 0)
    def _():
        for h in range(n_heads):
            st_ref[h] = s0_ref[0, h].T

    assert HGRN_SUB == 8
    rl1 = lax.broadcasted_iota(jnp.int32, (T, 1), 0) % HGRN_SUB
    t_i = lax.broadcasted_iota(jnp.int32, (T, T), 0)
    s_i = lax.broadcasted_iota(jnp.int32, (T, T), 1)
    tril = jnp.where(s_i <= t_i, 1.0, 0.0).astype(BF16)
    pair_masks = []
    size = 2 * HGRN_SUB
    while size <= T:
        same = (t_i & -size) == (s_i & -size)
        pair_masks.append((size, same & ((t_i & (size - 1)) >= size // 2) & ((s_i & (size - 1)) < size // 2)))
        size *= 2

    def shift_rows(x, d):
        return pltpu.roll(x.reshape(T // HGRN_SUB, HGRN_SUB, HEAD_DIM), d, 1).reshape(T, HEAD_DIM)

    def one_head(q, k, b, v, h, sl):
        hs = slice(h * HEAD_DIM, (h + 1) * HEAD_DIM)
        st = st_ref[h]
        o = _dot_nt((q * jnp.exp2(b)).astype(BF16), st.astype(BF16))
        o = o + jnp.sum(q * k, axis=-1, keepdims=True) * v
        for d in range(1, HGRN_SUB):
            w = jnp.exp2(b - shift_rows(b, d))
            a = jnp.sum(q * shift_rows(k, d) * w, axis=-1, keepdims=True)
            o = o + jnp.where(rl1 >= d, a, 0.0) * shift_rows(v, d)
        att = jnp.zeros((T, T), F32)
        for size, keep in pair_masks:
            half = size // 2
            refs = [jnp.broadcast_to(b[j * size + half - 1:j * size + half], (size, HEAD_DIM))
                    for j in range(T // size)]
            r = jnp.concatenate(refs, axis=0) if len(refs) > 1 else refs[0]
            qt = q * jnp.exp2(jnp.minimum(b - r, 0.0))
            kt = k * jnp.exp2(jnp.minimum(r - b, 0.0))
            att = att + jnp.where(keep, _dot_nt(qt.astype(BF16), kt.astype(BF16)), 0.0)
        o = o + _dot(att.astype(BF16), v.astype(BF16))
        bl = b[T - 1:T]
        kt = k * jnp.exp2(bl - b)
        st_ref[h] = st * jnp.exp2(bl) + _dot_tn(v.astype(BF16), kt.astype(BF16))
        o_ref[0, sl, hs] = _hgrn_out(o, g_ref[0, sl, hs], on_ref[:, hs]).astype(o_ref.dtype)

    def chunk(ci, carry):
        sl = pl.ds(pl.multiple_of(ci * T, T), T)
        q, k, lf = _hgrn_gates(q_ref[0, sl, :], f_ref[0, sl, :], llb_ref[...], l1m_ref[...])
        b = sum(_dot(tril, part) for part in _split3(lf)) * LOG2E
        for h in range(n_heads):
            hs = slice(h * HEAD_DIM, (h + 1) * HEAD_DIM)
            one_head(q[:, hs], k[:, hs], b[:, hs], i_ref[0, sl, hs], h, sl)
        return carry

    lax.fori_loop(0, n_chunks, chunk, 0)

    @pl.when(c == pl.num_programs(2) - 1)
    def _():
        for h in range(n_heads):
            s_ref[0, h] = st_ref[h].T


def hgrn_prompt(z3, log_lb, log1m_lb, onorm, s0):
    B, L, _ = z3.shape
    H = N_MIX_HEADS
    hb = HGRN_HEADS_PER_STEP
    wb = hb * HEAD_DIM
    tc = _pick(L, (512, 256, 128, 64))
    zspec = lambda k: pl.BlockSpec((1, tc, wb), lambda b, h, c: (b, c, k * (H // hb) + h))
    vspec = pl.BlockSpec((1, wb), lambda b, h, c: (0, h))
    sspec = pl.BlockSpec((1, hb, HEAD_DIM, HEAD_DIM), lambda b, h, c: (b, h, 0, 0))
    return pl.pallas_call(
        functools.partial(_hgrn_kernel, n_chunks=tc // HGRN_T),
        grid=(B, H // hb, L // tc),
        in_specs=[zspec(0), zspec(1), zspec(2), zspec(3), vspec, vspec, vspec, sspec],
        out_specs=[pl.BlockSpec((1, tc, wb), lambda b, h, c: (b, c, h)), sspec],
        out_shape=[
            jax.ShapeDtypeStruct((B, L, MIX_DIM), BF16),
            jax.ShapeDtypeStruct((B, H, HEAD_DIM, HEAD_DIM), F32),
        ],
        scratch_shapes=[pltpu.VMEM((hb, HEAD_DIM, HEAD_DIM), F32)],
        compiler_params=_cp(("parallel", "parallel", "arbitrary")),
        name="hgrn_chunked",
    )(z3, z3, z3, z3, log_lb, log1m_lb, onorm, s0)


def _hgrn_step_kernel(z_ref, llb_ref, l1m_ref, on_ref, s0_ref, o_ref, s_ref):
    for h in range(N_MIX_HEADS):
        col = lambda k: slice((k * N_MIX_HEADS + h) * HEAD_DIM, (k * N_MIX_HEADS + h + 1) * HEAD_DIM)
        hs = slice(h * HEAD_DIM, (h + 1) * HEAD_DIM)
        q, k, lf = _hgrn_gates(z_ref[0, :, col(0)], z_ref[0, :, col(1)], llb_ref[:, hs], l1m_ref[:, hs])
        v = z_ref[0, :, col(2)]
        g = z_ref[0, :, col(3)]
        rows = jnp.concatenate([q, k, jnp.exp(lf), jnp.zeros((5, HEAD_DIM), F32)], axis=0)
        cols = rows.T
        s_new = cols[:, 2:3] * s0_ref[0, h] + cols[:, 1:2] * v
        s_ref[0, h] = s_new
        o = jnp.sum(cols[:, 0:1] * s_new, axis=0, keepdims=True)
        o_ref[0, :, hs] = _hgrn_out(o, g, on_ref[:, hs]).astype(o_ref.dtype)


def hgrn_step(z3, log_lb, log1m_lb, onorm, s0):
    B = z3.shape[0]
    W = z3.shape[2]
    H = N_MIX_HEADS
    vspec = pl.BlockSpec((1, MIX_DIM), lambda b: (0, 0))
    sspec = pl.BlockSpec((1, H, HEAD_DIM, HEAD_DIM), lambda b: (b, 0, 0, 0))
    return pl.pallas_call(
        _hgrn_step_kernel,
        grid=(B,),
        in_specs=[pl.BlockSpec((1, 1, W), lambda b: (b, 0, 0)), vspec, vspec, vspec, sspec],
        out_specs=[pl.BlockSpec((1, 1, MIX_DIM), lambda b: (b, 0, 0)), sspec],
        out_shape=[
            jax.ShapeDtypeStruct((B, 1, MIX_DIM), BF16),
            jax.ShapeDtypeStruct((B, H, HEAD_DIM, HEAD_DIM), F32),
        ],
        compiler_params=_cp(("parallel",)),
        name="hgrn_step",
    )(z3, log_lb, log1m_lb, onorm, s0)


def _pe_proj_kernel(pe_ref, w_ref, o_ref):
    pe = jnp.broadcast_to(pe_ref[0], (8, pe_ref.shape[2])).astype(BF16)
    o_ref[0] = _dot(pe, w_ref[0])


def pe_proj(pe, w1):
    K = pe.shape[2]
    return pl.pallas_call(
        _pe_proj_kernel,
        grid=(2,),
        in_specs=[pl.BlockSpec((1, 1, K), lambda i: (i, 0, 0)), pl.BlockSpec((1, K, CMP_HID), lambda i: (i, 0, 0))],
        out_specs=pl.BlockSpec((1, 8, CMP_HID), lambda i: (i, 0, 0)),
        out_shape=jax.ShapeDtypeStruct((2, 8, CMP_HID), F32),
        compiler_params=_cp(("parallel",)),
        name="cmp_pe_proj",
    )(pe, w1)


def _cmp_mlp_kernel(x_ref, w1_ref, w2_ref, pe_ref, o_ref):
    n_sub = x_ref.shape[1] // CMP_STRIDE
    half = CMP_STRIDE * HEAD_DIM
    x = jnp.concatenate([x_ref[0, pl.ds(p, n_sub, stride=CMP_STRIDE), :] for p in range(CMP_STRIDE)],
                        axis=1).astype(BF16)
    a = _dot(x, w1_ref[0, 0:half])
    bm = _dot(x, w1_ref[0, half:2 * half])
    n = a.shape[0]
    h = a + pltpu.roll(bm, n - 1, 0) + pe_ref[0, 0:1]
    o_ref[0, 0] = _dot(jax.nn.gelu(h).astype(BF16), w2_ref[0])


def cmp_mlp(rows3, w1, w2, pe_h):
    B, L, _ = rows3.shape
    n_sub = L // CMP_STRIDE
    half = CMP_STRIDE * HEAD_DIM
    return pl.pallas_call(
        _cmp_mlp_kernel,
        grid=(2, B, N_KV),
        in_specs=[
            pl.BlockSpec((1, L, HEAD_DIM), lambda k, b, n: (b, 0, k * N_KV + n)),
            pl.BlockSpec((1, 2 * half, CMP_HID), lambda k, b, n: (k, 0, 0)),
            pl.BlockSpec((1, CMP_HID, HEAD_DIM), lambda k, b, n: (k, 0, 0)),
            pl.BlockSpec((1, 8, CMP_HID), lambda k, b, n: (k, 0, 0)),
        ],
        out_specs=pl.BlockSpec((1, 1, n_sub, HEAD_DIM), lambda k, b, n: (k, b * N_KV + n, 0, 0)),
        out_shape=jax.ShapeDtypeStruct((2, B * N_KV, n_sub, HEAD_DIM), F32),
        compiler_params=_cp(("parallel", "parallel", "parallel")),
        name="cmp_mlp",
    )(rows3, w1, w2, pe_h)


def _masked_softmax_rows(s, mask, exp_fn=jnp.exp):
    s = jnp.where(mask, s, -jnp.inf)
    m = jnp.max(s, axis=-1, keepdims=True)
    m = jnp.where(m == -jnp.inf, 0.0, m)
    p = exp_fn(s - m)
    return p / jnp.maximum(jnp.sum(p, axis=-1, keepdims=True), 1e-30)


def _split3(p):
    hi = p.astype(BF16)
    r = p - hi.astype(F32)
    mid = r.astype(BF16)
    lo = (r - mid.astype(F32)).astype(BF16)
    return hi, mid, lo


def _masked_softmax_cols(s, mask, exp_fn):
    s = jnp.where(mask, s, -jnp.inf)
    m = jnp.max(s, axis=0, keepdims=True)
    m = jnp.where(m == -jnp.inf, 0.0, m)
    p = exp_fn(s - m)
    return p / jnp.maximum(jnp.sum(p, axis=0, keepdims=True), 1e-30)


def _nsa_prompt_t_kernel(q_ref, gt_ref, kc_ref, vc_ref, ks_ref, vs_ref, kw_ref, vw_ref, o_ref,
                         kaug_ref, vst_ref, kwb_ref, vwt_ref, vct_ref, score_ref, *, tq, tk, tw, seq_len):
    qi = pl.program_id(2)
    n_blk = seq_len // SLC_BLOCK
    scale = HEAD_DIM ** -0.5 * LOG2E
    cols = GROUP * tq
    nc = kc_ref.shape[1]

    @pl.when(qi == 0)
    def _():
        r = lax.broadcasted_iota(jnp.int32, (seq_len, HEAD_DIM), 0) // SLC_BLOCK
        col = lax.broadcasted_iota(jnp.int32, (seq_len, HEAD_DIM), 1)
        kaug_ref[:, 0:HEAD_DIM] = ks_ref[0].astype(BF16)
        kaug_ref[:, HEAD_DIM:2 * HEAD_DIM] = jnp.where(r == col, 1.0, 0.0).astype(BF16)
        kwb_ref[...] = kw_ref[0].astype(BF16)
        for t in range(seq_len // tk):
            vst_ref[t] = vs_ref[0, t * tk:(t + 1) * tk, :].T.astype(BF16)
        for t in range(seq_len // tw):
            vwt_ref[t] = vw_ref[0, t * tw:(t + 1) * tw, :].T.astype(BF16)
        vct_ref[...] = vc_ref[0].T.astype(BF16)

    q0 = qi * tq
    qs = q_ref[0] * scale
    qb = jnp.concatenate([qs[:, g * HEAD_DIM:(g + 1) * HEAD_DIM] for g in range(GROUP)], axis=0).astype(BF16)

    def qpos(n_keys):
        return q0 + lax.broadcasted_iota(jnp.int32, (n_keys, cols), 1) % tq

    def kidx(n_keys):
        return lax.broadcasted_iota(jnp.int32, (n_keys, cols), 0)

    cend = CMP_STRIDE * kidx(nc) + (2 * CMP_STRIDE - 1)
    p_c = _masked_softmax_cols(_dot_nt(kc_ref[0].astype(BF16), qb), cend <= qpos(nc), jnp.exp2)
    o_cmp = _dot(vct_ref[...], p_c.astype(BF16))

    psum = p_c[:, 0:tq]
    for g in range(1, GROUP):
        psum = psum + p_c[:, g * tq:(g + 1) * tq]
    ji = lax.broadcasted_iota(jnp.int32, (HEAD_DIM, nc), 0)
    ci = lax.broadcasted_iota(jnp.int32, (HEAD_DIM, nc), 1)
    ratio = SLC_BLOCK // CMP_STRIDE
    member = ((ci >= ratio * ji - 1) & (ci <= ratio * ji + ratio - 1)).astype(BF16)
    imp = sum(_dot(member, part) for part in _split3(psum))
    blk = lax.broadcasted_iota(jnp.int32, (HEAD_DIM, tq), 0)
    cur = (q0 + lax.broadcasted_iota(jnp.int32, (HEAD_DIM, tq), 1)) // SLC_BLOCK
    valid = (blk <= cur) & (blk < n_blk)
    forced = valid & ((blk == 0) | (blk == cur) | (blk == cur - 1))
    score = jnp.where(forced, SEL_BONUS, jnp.where(valid, imp, -jnp.inf))
    score = score[0:n_blk]
    score_ref[...] = score
    blk = lax.broadcasted_iota(jnp.int32, (n_blk, tq), 0)
    per_tile = tq // SLC_BLOCK

    def rank_body(it, rank):
        for u in range(per_tile):
            i = it * per_tile + u
            si = score_ref[pl.ds(i, 1), :]
            later = jnp.where(blk > i, 1, 0)
            rank = rank + jnp.where(si > score, 1, jnp.where(si == score, later, 0))
        return rank

    rank = lax.fori_loop(0, qi + 1, rank_body, jnp.zeros((n_blk, tq), jnp.int32))
    bias_t = jnp.where(rank < N_SELECT, 0.0, NEG_BIG)
    if n_blk < HEAD_DIM:
        bias_t = jnp.concatenate([bias_t, jnp.zeros((HEAD_DIM - n_blk, tq), F32)], axis=0)
    bias = bias_t.T.astype(BF16)
    qaug = jnp.concatenate([qb, jnp.concatenate([bias] * GROUP, axis=0)], axis=1)

    qpos_k = qpos(tk)
    krow = kidx(tk)

    def slc_body(kt, carry, causal):
        m, l, acc = carry
        s = _dot_nt(kaug_ref[pl.ds(pl.multiple_of(kt * tk, tk), tk), :], qaug)
        if causal:
            s = jnp.where(kt * tk + krow <= qpos_k, s, -jnp.inf)
        m_new = jnp.maximum(m, jnp.max(s, axis=0, keepdims=True))
        alpha = jnp.exp2(m - m_new)
        p = jnp.exp2(s - m_new)
        l = alpha * l + jnp.sum(p, axis=0, keepdims=True)
        acc = alpha * acc + _dot(vst_ref[kt], p.astype(BF16))
        return m_new, l, acc

    init = (jnp.full((1, cols), -jnp.inf, F32), jnp.zeros((1, cols), F32), jnp.zeros((HEAD_DIM, cols), F32))
    n_full = q0 // tk

    def two_tiles(i, carry):
        return slc_body(2 * i + 1, slc_body(2 * i, carry, False), False)

    carry = lax.fori_loop(0, n_full // 2, two_tiles, init)
    carry = lax.fori_loop(n_full - n_full % 2, n_full, functools.partial(slc_body, causal=False), carry)
    _, l_s, acc_s = slc_body(n_full, carry, True)
    o_slc = acc_s / jnp.maximum(l_s, 1e-30)

    wk = min(WINDOW + tw, seq_len)
    d_rel = (lax.broadcasted_iota(jnp.int32, (wk, GROUP * tw), 1) % tw
             - lax.broadcasted_iota(jnp.int32, (wk, GROUP * tw), 0))
    win_parts = []
    for h in range(tq // tw):
        qh = q0 + h * tw
        k0 = jnp.minimum(jnp.maximum(qh - WINDOW, 0), seq_len - wk)
        qb_h = jnp.concatenate([qb[g * tq + h * tw:g * tq + (h + 1) * tw] for g in range(GROUP)], axis=0)
        d = d_rel + (qh - k0)
        s_w = _dot_nt(kwb_ref[pl.ds(pl.multiple_of(k0, tw), wk), :], qb_h)
        s_w = jnp.where((d >= 0) & (d <= WINDOW), s_w, -jnp.inf)
        p_w = jnp.exp2(s_w - jnp.max(s_w, axis=0, keepdims=True))
        l_w = jnp.sum(p_w, axis=0, keepdims=True)
        p_wb = p_w.astype(BF16)
        t0 = k0 // tw
        o_h = sum(_dot(vwt_ref[t0 + i], p_wb[i * tw:(i + 1) * tw]) for i in range(wk // tw))
        win_parts.append(o_h / jnp.maximum(l_w, 1e-30))
    o_win = jnp.concatenate([win_parts[h][:, g * tw:(g + 1) * tw]
                             for g in range(GROUP) for h in range(tq // tw)], axis=1)

    gates = jax.nn.sigmoid(gt_ref[0]).T
    for g in range(GROUP):
        cs = slice(g * tq, (g + 1) * tq)
        c0 = g * N_BRANCH
        o = (gates[c0:c0 + 1] * o_cmp[:, cs] + gates[c0 + 1:c0 + 2] * o_slc[:, cs]
             + gates[c0 + 2:c0 + 3] * o_win[:, cs])
        o_ref[0, :, g * HEAD_DIM:(g + 1) * HEAD_DIM] = o.T.astype(o_ref.dtype)


def nsa_prompt_t(zq3, k_cmp, v_cmp, rows3):
    B, L, _ = zq3.shape
    gate_block0 = (MIX_DIM + MEM_DIM) // HEAD_DIM
    tq = _pick(L, (512, 256, 128))
    tk = _pick(L, (512, 256, 128))
    tw = min(tq, 256)
    assert WINDOW % tw == 0 and tk % tq == 0 and tq % tw == 0
    n_sub = k_cmp.shape[1]
    qw = GROUP * HEAD_DIM
    rspec = lambda kind: pl.BlockSpec((1, L, HEAD_DIM), lambda b, n, i: (b, 0, kind * N_KV + n))
    cspec = pl.BlockSpec((1, n_sub, HEAD_DIM), lambda b, n, i: (b * N_KV + n, 0, 0))
    return pl.pallas_call(
        functools.partial(_nsa_prompt_t_kernel, tq=tq, tk=tk, tw=tw, seq_len=L),
        grid=(B, N_KV, L // tq),
        in_specs=[
            pl.BlockSpec((1, tq, qw), lambda b, n, i: (b, i, n)),
            pl.BlockSpec((1, tq, HEAD_DIM), lambda b, n, i: (b, i, gate_block0 + n)),
            cspec, cspec, rspec(2), rspec(3), rspec(4), rspec(5),
        ],
        out_specs=pl.BlockSpec((1, tq, qw), lambda b, n, i: (b, i, n)),
        out_shape=jax.ShapeDtypeStruct((B, L, MIX_DIM), BF16),
        scratch_shapes=[
            pltpu.VMEM((L, 2 * HEAD_DIM), BF16),
            pltpu.VMEM((L // tk, HEAD_DIM, tk), BF16),
            pltpu.VMEM((L, HEAD_DIM), BF16),
            pltpu.VMEM((L // tw, HEAD_DIM, tw), BF16),
            pltpu.VMEM((HEAD_DIM, n_sub), BF16),
            pltpu.VMEM((L // SLC_BLOCK, tq), F32),
        ],
        compiler_params=_cp(("parallel", "parallel", "arbitrary")),
        name="nsa_prompt",
    )(zq3, zq3, k_cmp, v_cmp, rows3, rows3, rows3, rows3)


def _dec_cmp_kernel(pt_ref, *refs, n_pg, n_groups):
    del pt_ref
    pg_refs = refs[:n_pg]
    new_ref, w1_ref, w2_ref, pe_ref, kc_ref, vc_ref, carry_ref = refs[n_pg:]
    g = pl.program_id(1)
    sub_pg = PAGE_SIZE // CMP_STRIDE
    n_sub = n_pg * sub_pg
    pos_per_chunk = 4
    kw = pos_per_chunk * HEAD_DIM

    @pl.when(g == 0)
    def _():
        carry_ref[...] = jnp.zeros_like(carry_ref)

    sub_i = lax.broadcasted_iota(jnp.int32, (8, n_sub, kw), 1)
    new_row = jnp.concatenate([new_ref[0], jnp.zeros((8, kw - HEAD_DIM), F32)], axis=1)[:, None, :]
    ys = [None, None]
    for pc in range(CMP_STRIDE // pos_per_chunk):
        x = jnp.concatenate(
            [jnp.concatenate([jnp.swapaxes(pg_refs[r][0, pl.ds(p, sub_pg, stride=CMP_STRIDE), :, :], 0, 1)
                              for r in range(n_pg)], axis=1)
             for p in range(pc * pos_per_chunk, (pc + 1) * pos_per_chunk)], axis=2)
        x = jnp.where(g < n_groups, x, 0.0)
        if pc == 0:
            x = jnp.where((g == n_groups) & (sub_i == 0), new_row, x)
        xb = x.astype(BF16)
        for kind in range(2):
            part = _dot(xb[kind * N_KV:(kind + 1) * N_KV].reshape(N_KV * n_sub, kw),
                        w1_ref[kind, pc * kw:(pc + 1) * kw, :])
            ys[kind] = part if ys[kind] is None else ys[kind] + part

    row = lax.broadcasted_iota(jnp.int32, (n_sub, CMP_HID), 0)
    for kind, out_ref in enumerate((kc_ref, vc_ref)):
        y = ys[kind]
        for n in range(N_KV):
            slot = kind * N_KV + n
            a = y[n * n_sub:(n + 1) * n_sub, 0:CMP_HID]
            bm = y[n * n_sub:(n + 1) * n_sub, CMP_HID:2 * CMP_HID]
            a_prev = jnp.where(row == 0, carry_ref[slot:slot + 1, :], pltpu.roll(a, 1, 0))
            carry_ref[slot:slot + 1, :] = a[n_sub - 1:n_sub]
            h = a_prev + bm + pe_ref[kind, 0:1]
            out_ref[0, n] = _dot(jax.nn.gelu(h).astype(BF16), w2_ref[kind])


def dec_cmp_tokens(cache4, page_table, new_cmp, w1ab, w2, pe_h):
    B, n_pages = page_table.shape
    n_pg = _pick(n_pages, (16, 8, 4, 2))
    assert n_pages % n_pg == 0
    n_groups = n_pages // n_pg
    sub_pg = PAGE_SIZE // CMP_STRIDE
    n_sub = n_pg * sub_pg
    n_steps = n_groups + 1
    while (n_steps * n_sub) % HEAD_DIM:
        n_steps += 1
    half = CMP_STRIDE * HEAD_DIM

    def page_spec(r):
        def imap(b, g, pt):
            return (pt[b * n_pages + jnp.minimum(g * n_pg + r, n_pages - 1)], 0, 0, 0)
        return pl.BlockSpec((1, PAGE_SIZE, 8, HEAD_DIM), imap)

    out_spec = pl.BlockSpec((1, N_KV, n_sub, HEAD_DIM), lambda b, g, pt: (b, 0, g, 0))
    grid_spec = pltpu.PrefetchScalarGridSpec(
        num_scalar_prefetch=1,
        grid=(B, n_steps),
        in_specs=[page_spec(r) for r in range(n_pg)] + [
            pl.BlockSpec((1, 8, HEAD_DIM), lambda b, g, pt: (b, 0, 0)),
            pl.BlockSpec((2, half, 2 * CMP_HID), lambda b, g, pt: (0, 0, 0)),
            pl.BlockSpec((2, CMP_HID, HEAD_DIM), lambda b, g, pt: (0, 0, 0)),
            pl.BlockSpec((2, 8, CMP_HID), lambda b, g, pt: (0, 0, 0)),
        ],
        out_specs=[out_spec, out_spec],
        scratch_shapes=[pltpu.VMEM((8, CMP_HID), F32)],
    )
    tok = jax.ShapeDtypeStruct((B, N_KV, n_steps * n_sub, HEAD_DIM), F32)
    return pl.pallas_call(
        functools.partial(_dec_cmp_kernel, n_pg=n_pg, n_groups=n_groups),
        grid_spec=grid_spec,
        out_shape=[tok, tok],
        compiler_params=_cp(("parallel", "arbitrary")),
        name="dec_cmp_tokens",
    )(page_table.reshape(-1), *([cache4] * n_pg), new_cmp, w1ab, w2, pe_h)


def _dec_select_kernel(q_ref, kc_ref, vc_ref, mem_ref, ocmp_ref, idx_ref, *, q_pos, nc, n_blk, n_sel):
    C = kc_ref.shape[2]
    JB = mem_ref.shape[1]
    scale = HEAD_DIM ** -0.5
    c_i = lax.broadcasted_iota(jnp.int32, (8, C), 1) - 1
    g_i = lax.broadcasted_iota(jnp.int32, (8, C), 0)
    ok = (c_i >= 0) & (c_i < nc) & (CMP_STRIDE * c_i + (2 * CMP_STRIDE - 1) <= q_pos)
    psums = []
    for n in range(N_KV):
        qb = (q_ref[0, n] * scale).astype(BF16)
        p = _masked_softmax_rows(_dot_nt(qb, kc_ref[0, n].astype(BF16)), ok)
        ocmp_ref[0, n] = _dot(p.astype(BF16), vc_ref[0, n].astype(BF16))
        psums.append(jnp.sum(jnp.where(g_i < GROUP, p, 0.0), axis=0, keepdims=True))
    psum = jnp.concatenate(psums + [jnp.zeros((8 - N_KV, C), F32)], axis=0)
    imp = sum(_dot(part, mem_ref[...]) for part in _split3(psum))
    j_i = lax.broadcasted_iota(jnp.int32, (8, JB), 1)
    cur = q_pos // SLC_BLOCK
    valid = (j_i <= cur) & (j_i < n_blk)
    forced = valid & ((j_i == 0) | (j_i == cur) | (j_i == cur - 1))
    score = jnp.where(forced, SEL_BONUS, jnp.where(valid, imp, -jnp.inf))
    j_f = j_i.astype(F32)
    taken = j_i >= n_blk
    lane_o = lax.broadcasted_iota(jnp.int32, (8, HEAD_DIM), 1)
    out = jnp.zeros((8, HEAD_DIM), F32)
    for r in range(n_sel):
        m = jnp.max(jnp.where(taken, -jnp.inf, score), axis=-1, keepdims=True)
        cand = jnp.logical_not(taken) & (score == m)
        idx = jnp.min(jnp.where(cand, j_f, float(JB)), axis=-1, keepdims=True)
        out = jnp.where(lane_o == r, idx, out)
        taken = taken | (j_f == idx)
    idx_ref[0] = out.astype(jnp.int32)


def dec_select(q8, k_cmp, v_cmp, member, *, q_pos, nc, n_blk, n_sel):
    B, _, C, _ = k_cmp.shape
    JB = member.shape[1]
    head_spec = pl.BlockSpec((1, N_KV, 8, HEAD_DIM), lambda b: (b, 0, 0, 0))
    tok_spec = pl.BlockSpec((1, N_KV, C, HEAD_DIM), lambda b: (b, 0, 0, 0))
    return pl.pallas_call(
        functools.partial(_dec_select_kernel, q_pos=q_pos, nc=nc, n_blk=n_blk, n_sel=n_sel),
        grid=(B,),
        in_specs=[head_spec, tok_spec, tok_spec, pl.BlockSpec((C, JB), lambda b: (0, 0))],
        out_specs=[head_spec, pl.BlockSpec((1, 8, HEAD_DIM), lambda b: (b, 0, 0))],
        out_shape=[
            jax.ShapeDtypeStruct((B, N_KV, 8, HEAD_DIM), F32),
            jax.ShapeDtypeStruct((B, 8, HEAD_DIM), jnp.int32),
        ],
        compiler_params=_cp(("parallel",)),
        name="dec_select",
    )(q8, k_cmp, v_cmp, member)


def _dec_attend_kernel(idx_ref, pt_ref, q_ref, gl_ref, ocmp_ref, nslc_ref, win_ref, nwin_ref, *rest,
                       q_pos, past_len, n_sel):
    del pt_ref
    blk_refs = rest[:n_sel]
    o_ref = rest[n_sel]
    b = pl.program_id(0)
    n = pl.program_id(1)
    base = (b * N_KV + n) * n_sel
    scale = HEAD_DIM ** -0.5
    past_blocks = past_len // SLC_BLOCK
    rb = SLC_BLOCK * 8
    rw = win_ref.shape[1]
    w_buf = rw // 8
    qf = q_ref[0, 0] * scale
    qb = qf.astype(BF16)
    js = [idx_ref[base + s] for s in range(n_sel)]
    has_new = js[0] == past_blocks
    for s in range(1, n_sel):
        has_new = has_new | (js[s] == past_blocks)
    new_ok = has_new & (past_len <= q_pos)

    def head_rows(x):
        slot = lax.broadcasted_iota(jnp.int32, (8, HEAD_DIM), 0)
        k = jnp.sum(jnp.where(slot == n, x, 0.0), axis=0, keepdims=True)
        v = jnp.sum(jnp.where(slot == n + N_KV, x, 0.0), axis=0, keepdims=True)
        return k, v

    def finish(m):
        return jnp.where(m == -jnp.inf, 0.0, m)

    r_s = lax.broadcasted_iota(jnp.int32, (8, rb), 1)
    t_s = r_s >> 3
    mine_s = (r_s & 7) == n

    def slc_scores(s):
        blk = blk_refs[s][0, 0].reshape(rb, HEAD_DIM)
        sc = _dot_nt(qb, blk.astype(BF16))
        ok = mine_s & (js[s] * SLC_BLOCK + t_s <= q_pos) & (js[s] < past_blocks)
        return jnp.where(ok, sc, -jnp.inf)

    k_new, v_new = head_rows(nslc_ref[0])
    s_new = jnp.where(new_ok, jnp.sum(qf * k_new, axis=-1, keepdims=True), -jnp.inf)
    scores = [slc_scores(s) for s in range(n_sel)]
    m = s_new
    for sc in scores:
        m = jnp.maximum(m, jnp.max(sc, axis=-1, keepdims=True))
    m = finish(m)
    p_new = jnp.exp(s_new - m)
    l = p_new
    acc = p_new * v_new
    for s, sc in enumerate(scores):
        p = jnp.exp(sc - m)
        l = l + jnp.sum(p, axis=-1, keepdims=True)
        acc = acc + _dot(pltpu.roll(p, N_KV, 1).astype(BF16), blk_refs[s][0, 0].reshape(rb, HEAD_DIM).astype(BF16))
    o_slc = acc / jnp.maximum(l, 1e-30)

    r_w = lax.broadcasted_iota(jnp.int32, (8, rw), 1)
    w_pos = past_len - w_buf + (r_w >> 3)
    w_ok = ((r_w & 7) == n) & (q_pos - w_pos >= 0) & (q_pos - w_pos <= WINDOW) & (w_pos >= 0)
    win = win_ref[0].astype(BF16)
    s_w = jnp.where(w_ok, _dot_nt(qb, win), -jnp.inf)
    k_nw, v_nw = head_rows(nwin_ref[0])
    nw_ok = (q_pos - past_len >= 0) and (q_pos - past_len <= WINDOW)
    s_nw = jnp.sum(qf * k_nw, axis=-1, keepdims=True) if nw_ok else jnp.full((8, 1), -jnp.inf, F32)
    m = finish(jnp.maximum(s_nw, jnp.max(s_w, axis=-1, keepdims=True)))
    p_w = jnp.exp(s_w - m)
    p_nw = jnp.exp(s_nw - m)
    l = p_nw + jnp.sum(p_w, axis=-1, keepdims=True)
    acc = p_nw * v_nw + _dot(pltpu.roll(p_w, N_KV, 1).astype(BF16), win)
    o_win = acc / jnp.maximum(l, 1e-30)

    gates = jax.nn.sigmoid(gl_ref[0, 0])
    o_ref[0, 0] = gates[:, 0:1] * ocmp_ref[0, 0] + gates[:, 1:2] * o_slc + gates[:, 2:3] * o_win


def dec_attend(idx_flat, page_table, q8, gate8, ocmp8, new_slc, win2, new_win, cache5, *, q_pos, past_len, n_sel):
    B, n_pages = page_table.shape
    rw = win2.shape[1]
    past_blocks = past_len // SLC_BLOCK
    per_page = PAGE_SIZE // SLC_BLOCK

    def blk_spec(s):
        def imap(b, n, idx, pt):
            j = jnp.minimum(idx[(b * N_KV + n) * n_sel + s], past_blocks - 1)
            return (pt[b * n_pages + j // per_page], j % per_page, 0, 1, 0)
        return pl.BlockSpec((1, 1, SLC_BLOCK, 8, HEAD_DIM), imap)

    head_spec = pl.BlockSpec((1, 1, 8, HEAD_DIM), lambda b, n, idx, pt: (b, n, 0, 0))
    tok_spec = pl.BlockSpec((1, 8, HEAD_DIM), lambda b, n, idx, pt: (b, 0, 0))
    grid_spec = pltpu.PrefetchScalarGridSpec(
        num_scalar_prefetch=2,
        grid=(B, N_KV),
        in_specs=[head_spec, head_spec, head_spec, tok_spec,
                  pl.BlockSpec((1, rw, HEAD_DIM), lambda b, n, idx, pt: (b, 0, 0)), tok_spec]
        + [blk_spec(s) for s in range(n_sel)],
        out_specs=head_spec,
    )
    return pl.pallas_call(
        functools.partial(_dec_attend_kernel, q_pos=q_pos, past_len=past_len, n_sel=n_sel),
        grid_spec=grid_spec,
        out_shape=jax.ShapeDtypeStruct((B, N_KV, 8, HEAD_DIM), F32),
        compiler_params=_cp(("parallel", "arbitrary")),
        name="dec_attend",
    )(idx_flat, page_table.reshape(-1), q8, gate8, ocmp8, new_slc, win2, new_win, *([cache5] * n_sel))


def _pad_to(a, axis, size):
    pad = [(0, 0)] * a.ndim
    pad[axis] = (0, size - a.shape[axis])
    return jnp.pad(a, pad)


def _prep_weights(w_in_a, w_in_b, b_gate, w_kv, cmp_pe, cmp_w1, cmp_w2, w_mem_kv, w_out, w_up, conv_w, conv_b,
                  w_down, hgrn_lb):
    depth = w_out.shape[0]
    n_b = w_in_b.shape[0]
    n_gate = N_MIX_HEADS * N_BRANCH
    per_kv = GROUP * N_BRANCH
    half = CMP_STRIDE * HEAD_DIM
    w1 = cmp_w1.astype(BF16)
    lb = jnp.cumsum(jax.nn.softmax(hgrn_lb.astype(F32), axis=0), axis=0)
    lb = lb - lb[0]
    layers = lambda f, n=depth: [f(l) for l in range(n)]
    gate_w = lambda j: w_in_b[j][:, MIX_DIM:MIX_DIM + n_gate].reshape(D_MODEL, N_KV, per_kv)
    return dict(
        in_a=cast_bf16(w_in_a),
        in_b=layers(lambda j: jnp.concatenate(
            [w_in_b[j][:, :MIX_DIM], w_in_b[j][:, MIX_DIM + n_gate:],
             _pad_to(gate_w(j), 2, HEAD_DIM).reshape(D_MODEL, N_KV * HEAD_DIM)], axis=-1).astype(BF16), n_b),
        b_in_b=layers(lambda j: jnp.concatenate(
            [jnp.zeros((MIX_DIM + MEM_DIM,), F32),
             _pad_to(b_gate[j].reshape(N_KV, per_kv), 1, HEAD_DIM).reshape(N_KV * HEAD_DIM)]), n_b),
        kv=cast_bf16(w_kv[None]),
        mem_kv=cast_bf16(w_mem_kv),
        out=cast_bf16(w_out),
        up=cast_split_halves(w_up, D_FF_PAD),
        conv_w=layers(lambda l: _pad_to(_pad_to(conv_w[l], 1, D_FF_PAD), 0, 8)),
        conv_b=layers(lambda l: _pad_to(conv_b[l], 0, D_FF_PAD).reshape(1, D_FF_PAD)),
        down=cast_pad_rows(w_down, D_FF_PAD),
        cmp_w1=w1,
        cmp_w1ab=jnp.concatenate([w1[:, :half], w1[:, half:]], axis=2),
        cmp_w2=cmp_w2.astype(BF16),
        cmp_pe=cmp_pe.reshape(2, 1, 2 * half),
        log_lb=jnp.log(lb),
        log1m_lb=jnp.log1p(-lb),
    )


def kernel(x_prompt, x_sample, mem_prompt, cache_nsa_kv, page_table, cache_win_kv, state_hgrn, state_conv,
           cache_mem_kv, norm_mix, norm_ffn, norm_mem, norm_kv, norm_final, w_in_a, hgrn_lb, hgrn_onorm,
           w_in_b, b_gate, w_kv, cmp_pe, cmp_w1, cmp_w2, w_mem_kv, w_out, w_up, conv_w, conv_b, w_down):
    B, L, _ = x_prompt.shape
    Bs = x_sample.shape[0]
    depth = w_out.shape[0]
    n_a = w_in_a.shape[0]
    n_mem = mem_prompt.shape[1]
    n_pool, page, _, _, _ = cache_nsa_kv.shape
    n_pages = page_table.shape[1]
    past_len = n_pages * page
    w_buf = cache_win_kv.shape[1]
    assert x_sample.shape[1] == 1 and page == PAGE_SIZE and L % 128 == 0 and L // SLC_BLOCK <= HEAD_DIM
    W = _prep_weights(w_in_a, w_in_b, b_gate, w_kv, cmp_pe, cmp_w1, cmp_w2, w_mem_kv, w_out, w_up, conv_w,
                      conv_b, w_down, hgrn_lb)
    pe_h = pe_proj(W["cmp_pe"], W["cmp_w1"])
    onorm = hgrn_onorm.reshape(n_a, 1, MIX_DIM)
    log_lb = W["log_lb"].reshape(n_a, 1, MIX_DIM)
    log1m_lb = W["log1m_lb"].reshape(n_a, 1, MIX_DIM)

    def ffn(x, l, **kw):
        return conv_ffn(x, norm_ffn[l], W["up"], W["conv_w"][l], W["conv_b"][l], W["down"], l, **kw)

    M = B * L
    mem_flat = mem_prompt.reshape(B * n_mem, D_MODEL)
    mem_kv_both = [rms_matmul_heads(mem_flat, norm_mem[l], W["mem_kv"], (l,), 2 * MEM_DIM) for l in range(depth)]
    mem_kv_p = [by_head.reshape(B, n_mem * 2 * N_MEM_HEADS, HEAD_DIM) for _, by_head in mem_kv_both]
    mem_cache = cache_mem_kv.reshape(depth, Bs, n_mem * 2 * N_MEM_HEADS, HEAD_DIM)
    x = x_prompt.reshape(M, D_MODEL)
    hgrn_p, conv_p = [], []
    for l in range(depth):
        if l < n_a:
            z3 = rms_matmul(x, norm_mix[l], W["in_a"], lead=(l,)).reshape(B, L, -1)
            s0 = jnp.zeros((B, N_MIX_HEADS, HEAD_DIM, HEAD_DIM), F32)
            o_mix, s_new = hgrn_prompt(z3, log_lb[l], log1m_lb[l], onorm[l], s0)
            hgrn_p.append(s_new)
            o_mem = mem_attention(z3, 4 * MIX_DIM // MEM_DIM, mem_kv_p[l])
        else:
            j = l - n_a
            if j == 0:
                n_row4 = 4 * N_KV * HEAD_DIM
                rows_p, rows_by_head = rms_matmul_heads(x, norm_kv, W["kv"], (0,), n_row4)
                rows3 = rows_p.reshape(B, L, -1)
                cmp_p = cmp_mlp(rows3, W["cmp_w1"], W["cmp_w2"], pe_h)
            zq3 = rms_matmul(x, norm_mix[l], W["in_b"][j], W["b_in_b"][j]).reshape(B, L, -1)
            o_mix = nsa_prompt_t(zq3, cmp_p[0], cmp_p[1], rows3)
            o_mem = mem_attention(zq3, MIX_DIM // MEM_DIM, mem_kv_p[l])
        x = out_proj(o_mix.reshape(M, MIX_DIM), o_mem.reshape(M, MEM_DIM), W["out"], l, x)
        x, a_tail = ffn(x, l, seq_len=L)
        conv_p.append(a_tail.reshape(B, -1, 8, D_FF_PAD)[:, -1, 8 - (CONV_W - 1):, :D_FF])
    y_prompt = rmsnorm_rows(x, norm_final).reshape(B, L, D_MODEL)
    w_keep = min(WINDOW, L)
    nsa_rows_prompt = rows_by_head.reshape(B, L, 4, N_KV, HEAD_DIM)
    win_prompt = rows3[:, L - w_keep:, n_row4:].reshape(B, w_keep, 2, N_KV, HEAD_DIM)
    mem_kv_prompt = jnp.stack([by_head for _, by_head in mem_kv_both]).reshape(
        depth, B, n_mem, 2, N_MEM_HEADS, HEAD_DIM)

    Ms = 16
    pad_rows = lambda a: _pad_to(a, 0, Ms)
    xs = pad_rows(x_sample.reshape(Bs, D_MODEL))
    hgrn_s, conv_s = [], []
    q_pos = past_len
    t_pad = -(-(past_len + 1) // SLC_BLOCK) * SLC_BLOCK
    n_blk = t_pad // SLC_BLOCK
    nc = t_pad // CMP_STRIDE - 1
    n_sel = min(N_SELECT, n_blk)
    for l in range(depth):
        if l < n_a:
            z = rms_matmul(xs, norm_mix[l], W["in_a"], lead=(l,))
            z3 = z[:Bs].reshape(Bs, 1, -1)
            o_mix, s_new = hgrn_step(z3, log_lb[l], log1m_lb[l], onorm[l], state_hgrn[l])
            hgrn_s.append(s_new)
            o_mix = o_mix.reshape(Bs, MIX_DIM)
            o_mem = mem_attention(z3, 4 * MIX_DIM // MEM_DIM, mem_cache, lead=(l,))
        else:
            j = l - n_a
            if j == 0:
                rows_s = rms_matmul(xs, norm_kv, W["kv"], lead=(0,))[:Bs]
                new_cmp = rows_s[:, 0:1024].reshape(Bs, 8, HEAD_DIM)
                new_slc = rows_s[:, 1024:2048].reshape(Bs, 8, HEAD_DIM)
                new_win = rows_s[:, 2048:3072].reshape(Bs, 8, HEAD_DIM)
                cache4 = cache_nsa_kv.reshape(n_pool, PAGE_SIZE, 4 * N_KV, HEAD_DIM)
                cache5 = cache_nsa_kv.reshape(n_pool, PAGE_SIZE // SLC_BLOCK, SLC_BLOCK, 4 * N_KV, HEAD_DIM)
                win2 = cache_win_kv.reshape(Bs, w_buf * 2 * N_KV, HEAD_DIM)
                k_cmp_s, v_cmp_s = dec_cmp_tokens(cache4, page_table, new_cmp, W["cmp_w1ab"], W["cmp_w2"], pe_h)
                assert k_cmp_s.shape[2] > nc + 1
                ratio = SLC_BLOCK // CMP_STRIDE
                c_of_row = jnp.arange(k_cmp_s.shape[2])[:, None] - 1
                j_of_col = jnp.arange(-(-n_blk // HEAD_DIM) * HEAD_DIM)[None, :]
                member = ((c_of_row >= ratio * j_of_col - 1) & (c_of_row <= ratio * j_of_col + ratio - 1)).astype(BF16)
            zq = rms_matmul(xs, norm_mix[l], W["in_b"][j], W["b_in_b"][j])
            gt = zq[:Bs, MIX_DIM + MEM_DIM:]
            z3 = zq[:Bs].reshape(Bs, 1, -1)
            per_head = lambda a: _pad_to(a.reshape(Bs, N_KV, GROUP, -1), 2, 8)
            q8 = per_head(zq[:Bs, :MIX_DIM])
            o_cmp, idx = dec_select(q8, k_cmp_s, v_cmp_s, member, q_pos=q_pos, nc=nc, n_blk=n_blk, n_sel=n_sel)
            idx_flat = idx[:, :N_KV, :n_sel].reshape(-1)
            gate9 = gt.reshape(Bs, N_KV, HEAD_DIM)[:, :, :GROUP * N_BRANCH]
            o_all = dec_attend(idx_flat, page_table, q8, _pad_to(per_head(gate9), 3, HEAD_DIM), o_cmp,
                               new_slc, win2, new_win, cache5, q_pos=q_pos, past_len=past_len, n_sel=n_sel)
            o_mix = o_all[:, :, :GROUP].reshape(Bs, MIX_DIM).astype(BF16)
            o_mem = mem_attention(z3, MIX_DIM // MEM_DIM, mem_cache, lead=(l,))
        xs = out_proj(pad_rows(o_mix), pad_rows(o_mem.reshape(Bs, MEM_DIM)), W["out"], l, xs)
        prev = (pad_rows(_pad_to(state_conv[l][:, 1], 1, D_FF_PAD)), pad_rows(_pad_to(state_conv[l][:, 0], 1, D_FF_PAD)))
        xs, a_new = ffn(xs, l, prev=prev)
        conv_s.append(jnp.stack([state_conv[l][:, 1], a_new[0, :Bs, :D_FF]], axis=1))
    y_sample = rmsnorm_rows(xs, norm_final)[:Bs].reshape(Bs, 1, D_MODEL)
    nsa_rows_sample = rows_s[:, :n_row4].reshape(Bs, 1, 4, N_KV, HEAD_DIM)
    win_new = rows_s[:, n_row4:].reshape(Bs, 1, 2, N_KV, HEAD_DIM).astype(cache_win_kv.dtype)
    win_sample = jnp.concatenate([cache_win_kv, win_new], axis=1)[:, 1:]

    return (y_prompt, y_sample, nsa_rows_prompt, nsa_rows_sample, win_prompt, win_sample,
            jnp.stack(hgrn_p), jnp.stack(hgrn_s), jnp.stack(conv_p), jnp.stack(conv_s), mem_kv_prompt)
```
